```python
import math
import jax
import jax.numpy as jnp
from jax import lax
import numpy as np

D_MODEL = 1024
BATCH = 32
SEQ = 256
DEPTH = 2
DEC_BATCH = 4
DEC_SEQ = 2048
PAST_LEN = 512

GRID_W = 64
N_AB = (DEPTH + 1) // 2
N_C = DEPTH // 2
MIX_WIDTH = D_MODEL
M_HEADS = 4
M_DIM = MIX_WIDTH // 2 // M_HEADS
M_W = M_HEADS * M_DIM
M_CHUNK = 128
CONV_W = 3
A_HEADS = 4
A_VDIM = MIX_WIDTH // 2 // A_HEADS
A_DIM = A_VDIM // 2
A_W = A_HEADS * A_VDIM
Q_BLOCK = 128
ROPE_THETA = 10000.0
IN_AB = 4 * M_W + 4 * M_HEADS + 3 * A_W
C_CHUNK = 128
C_GROUPS = 4
FF_DENSE = 2816
N_EXPERTS = 8
TOP_K = 2
FF_EXPERT = 3584
EPS = 1e-6

kernel_name = 'hybrid_mlstm_diffattn_gmlp_moe_step'

f32 = jnp.float32


def rms_norm(x, gain=None):
    xf = x.astype(f32)
    y = xf * lax.rsqrt(jnp.mean(xf * xf, axis=-1, keepdims=True) + EPS)
    if gain is not None:
        y = y * gain.astype(f32)
    return y.astype(x.dtype)


def layer_norm(x, gain, bias):
    xf = x.astype(f32)
    mu = jnp.mean(xf, axis=-1, keepdims=True)
    xc = xf - mu
    var = jnp.mean(xc * xc, axis=-1, keepdims=True)
    return (xc * lax.rsqrt(var + EPS) * gain.astype(f32) + bias.astype(f32)).astype(x.dtype)


def adaln(cv, w, b):
    mod = (jax.nn.silu(cv) @ w + b)[:, None, :]
    return jnp.split(mod, 6, axis=-1)


def modulate(x, shift, scale):
    return rms_norm(x) * (1.0 + scale) + shift


def swiglu(h, w1, w3, w2):
    return (jax.nn.silu(h @ w1) * (h @ w3)) @ w2


def conv_silu(x, w, b):
    t = x.shape[1]
    pad = CONV_W // 2
    xp = jnp.pad(x, ((0, 0), (pad, pad), (0, 0)))
    y = b
    for j in range(CONV_W):
        y = y + xp[:, j:j + t] * w[j]
    return jax.nn.silu(y)


def split_heads(x, n_heads):
    b, t, _ = x.shape
    return x.reshape(b, t, n_heads, -1).transpose(0, 2, 1, 3)


def axial_rope(x):
    t = x.shape[-2]
    rows = t // GRID_W
    row = jnp.repeat(jnp.arange(rows, dtype=f32), GRID_W)
    col = (jnp.arange(rows * GRID_W) % GRID_W).astype(f32)
    half = A_DIM // 2
    nf = half // 2
    inv = ROPE_THETA ** (-jnp.arange(nf, dtype=f32) / nf)
    xf = x.astype(f32)

    def rot(xa, pos):
        ang = pos[:, None] * inv
        cos, sin = jnp.cos(ang), jnp.sin(ang)
        x1, x2 = xa[..., :nf], xa[..., nf:]
        return jnp.concatenate([x1 * cos - x2 * sin, x2 * cos + x1 * sin], axis=-1)

    return jnp.concatenate([rot(xf[..., :half], row), rot(xf[..., half:], col)], axis=-1).astype(x.dtype)


def mlstm_scan(q, k, v, log_i, log_f, c0, n0, m0):
    b, h, t, _ = q.shape
    nc = t // M_CHUNK

    def chunks(a):
        a = a.astype(f32).reshape(b, h, nc, M_CHUNK, *a.shape[3:])
        return jnp.moveaxis(a, 2, 0)

    lower = jnp.tril(jnp.ones((M_CHUNK, M_CHUNK), dtype=bool))

    def step(carry, inp):
        cm, nm, mm = carry
        qc, kc, vc, li, lf = inp
        bcum = jnp.cumsum(lf, axis=-1)
        dmat = jnp.where(lower, bcum[..., :, None] - bcum[..., None, :] + li[..., None, :], -jnp.inf)
        inter = mm[..., None] + bcum
        mt = jnp.maximum(inter, jnp.max(dmat, axis=-1))
        w_inter = jnp.exp(inter - mt)
        s = jnp.einsum('bhtd,bhsd->bhts', qc, kc) * jnp.exp(dmat - mt[..., None])
        num = w_inter[..., None] * jnp.einsum('bhtd,bhde->bhte', qc, cm) + jnp.einsum('bhts,bhse->bhte', s, vc)
        den = w_inter * jnp.einsum('bhtd,bhd->bht', qc, nm) + jnp.sum(s, axis=-1)
        hc = num / jnp.maximum(jnp.abs(den), jnp.exp(-mt))[..., None]
        b_last = bcum[..., -1]
        g = b_last[..., None] - bcum + li
        m_new = jnp.maximum(mm + b_last, jnp.max(g, axis=-1))
        decay = jnp.exp(mm + b_last - m_new)
        kw = kc * jnp.exp(g - m_new[..., None])[..., None]
        c_new = decay[..., None, None] * cm + jnp.einsum('bhsd,bhse->bhde', kw, vc)
        n_new = decay[..., None] * nm + jnp.sum(kw, axis=2)
        return (c_new, n_new, m_new), hc

    carry0 = (c0.astype(f32), n0.astype(f32), m0.astype(f32))
    final, hs = lax.scan(step, carry0, (chunks(q), chunks(k), chunks(v), chunks(log_i), chunks(log_f)))
    hs = jnp.moveaxis(hs, 0, 2).reshape(b, h, t, v.shape[-1])
    return hs, final


def mlstm_bidir(q, k, v, log_i, log_f, c0, n0, m0):
    flip = lambda a: jnp.flip(a, axis=2)
    h_f, s_f = mlstm_scan(q, k, v, log_i[0], log_f[0], c0[:, 0], n0[:, 0], m0[:, 0])
    h_b, s_b = mlstm_scan(flip(q), flip(k), flip(v), flip(log_i[1]), flip(log_f[1]), c0[:, 1], n0[:, 1], m0[:, 1])
    h = (h_f + flip(h_b)).astype(v.dtype)
    state = tuple(jnp.stack([a, bb], axis=1) for a, bb in zip(s_f, s_b))
    return h, state


def diff_attend(q, k, v, lam):
    b, h, _, t, d = q.shape
    nb = t // Q_BLOCK
    qb = jnp.moveaxis(q.reshape(b, h, 2, nb, Q_BLOCK, d), 3, 0)
    scale = d ** -0.5

    def block(qi):
        s = jnp.einsum('bhiqd,bhikd->bhiqk', qi, k, preferred_element_type=f32) * scale
        p = jax.nn.softmax(s, axis=-1)
        a = p[:, :, 0] - lam * p[:, :, 1]
        return jnp.einsum('bhqk,bhkv->bhqv', a.astype(v.dtype), v)

    o = lax.map(block, qb)
    return jnp.moveaxis(o, 0, 2).reshape(b, h, t, v.shape[-1])


def mixer_ab(h, w_in, conv_w, conv_b, gate_b, qn_g, kn_g, m_norm_g, a_norm_g, w_out,
             lam, lam_init, ctx_k, ctx_v, c0, n0, m0):
    b, t, _ = h.shape
    z = h @ w_in
    o1, o2, o3 = 2 * M_W, 3 * M_W, 4 * M_W
    o4 = o3 + 4 * M_HEADS
    o5, o6 = o4 + A_W, o4 + 2 * A_W
    qk = conv_silu(z[..., :o1], conv_w, conv_b)
    mq = split_heads(qk[..., :M_W], M_HEADS)
    mk = split_heads(qk[..., M_W:], M_HEADS) * (M_DIM ** -0.5)
    mv = split_heads(z[..., o1:o2], M_HEADS)
    mo = jax.nn.sigmoid(z[..., o2:o3])
    g = (z[..., o3:o4] + gate_b).astype(f32).reshape(b, t, 4, M_HEADS).transpose(2, 0, 3, 1)
    log_i = g[0::2]
    log_f = jax.nn.log_sigmoid(g[1::2])
    aq = rms_norm(z[..., o4:o5].reshape(b, t, A_HEADS, 2, A_DIM), qn_g).transpose(0, 2, 3, 1, 4)
    ak = rms_norm(z[..., o5:o6].reshape(b, t, A_HEADS, 2, A_DIM), kn_g).transpose(0, 2, 3, 1, 4)
    av = split_heads(z[..., o6:], A_HEADS)
    if ctx_k is None:
        k_all, v_all = ak, av
    else:
        aq = axial_rope(aq)
        k_all = jnp.concatenate([ctx_k, axial_rope(ak)], axis=3)
        v_all = jnp.concatenate([ctx_v, av], axis=2)
    hm, state = mlstm_bidir(mq, mk, mv, log_i, log_f, c0, n0, m0)
    hm = rms_norm(hm.transpose(0, 2, 1, 3), m_norm_g).reshape(b, t, M_W) * mo
    ha = diff_attend(aq, k_all, v_all, lam)
    ha = (rms_norm(ha.transpose(0, 2, 1, 3), a_norm_g) * (1.0 - lam_init)).reshape(b, t, A_W)
    out = jnp.concatenate([hm, ha], axis=-1) @ w_out
    return out, (ak, av) + state


def chunk_mlp(h, w_in, b_in, ln_g, ln_b, ws, bs, w_out):
    z = jax.nn.gelu(h @ w_in + b_in)
    u, v = jnp.split(z, 2, axis=-1)
    v = layer_norm(v, ln_g, ln_b)
    b, t, e = v.shape
    vr = v.reshape(b, t // C_CHUNK, C_CHUNK, C_GROUPS, e // C_GROUPS)
    s = jnp.einsum('gpq,bnqgc->bnpgc', ws, vr) + bs.T[:, :, None]
    return (u * s.reshape(b, t, e)) @ w_out


def moe(h, w_router, w1, w3, w2):
    b, t, d = h.shape
    hf = h.reshape(b * t, d)
    logits = (hf @ w_router).astype(f32)
    top_v, top_i = lax.top_k(logits, TOP_K)
    wts = jax.nn.softmax(top_v, axis=-1)
    gates = jnp.sum(jax.nn.one_hot(top_i, N_EXPERTS, dtype=f32) * wts[..., None], axis=1)
    out = jnp.zeros_like(hf)
    for e in range(N_EXPERTS):
        out = out + gates[:, e:e + 1].astype(h.dtype) * swiglu(hf, w1[e], w3[e], w2[e])
    return out.reshape(b, t, d)


def setup_inputs(seed: int = 0) -> dict:
    key = jax.random.key(seed)
    ks = iter(jax.random.split(key, 48))

    def nrm(shape, scale=1.0):
        return scale * jax.random.normal(next(ks), shape, f32)

    E = MIX_WIDTH
    gate_i = nrm((N_AB, 2, M_HEADS), 0.1)
    gate_f = 3.0 + nrm((N_AB, 2, M_HEADS), 0.5)
    return {
        'x_prompt': nrm((BATCH, SEQ, D_MODEL)),
        'x_sample': nrm((DEC_BATCH, DEC_SEQ, D_MODEL)),
        'c': nrm((DEC_BATCH, D_MODEL)),
        'cache_dattn_k': nrm((DEC_BATCH, N_AB, A_HEADS, 2, PAST_LEN, A_DIM)),
        'cache_dattn_v': nrm((DEC_BATCH, N_AB, A_HEADS, PAST_LEN, A_VDIM)),
        'state_mlstm_c': nrm((DEC_BATCH, N_AB, 2, M_HEADS, M_DIM, M_DIM), M_DIM ** -0.5),
        'state_mlstm_n': nrm((DEC_BATCH, N_AB, 2, M_HEADS, M_DIM), M_DIM ** -0.5),
        'state_mlstm_m': nrm((DEC_BATCH, N_AB, 2, M_HEADS)),
        'c_ctx': nrm((D_MODEL,)),
        'w_ada': nrm((DEPTH, D_MODEL, 6 * D_MODEL), 0.5 * D_MODEL ** -0.5),
        'b_ada': nrm((DEPTH, 6 * D_MODEL), 0.02),
        'w_in_ab': nrm((N_AB, D_MODEL, IN_AB), D_MODEL ** -0.5),
        'conv_w': nrm((N_AB, CONV_W, 2 * M_W), 0.5),
        'conv_b': nrm((N_AB, 2 * M_W), 0.02),
        'gate_b': jnp.stack([gate_i, gate_f], axis=2).reshape(N_AB, 4 * M_HEADS),
        'qn_g': 1.0 + nrm((N_AB, A_DIM), 0.05),
        'kn_g': 1.0 + nrm((N_AB, A_DIM), 0.05),
        'lam_q1': nrm((N_AB, A_DIM), 0.1),
        'lam_k1': nrm((N_AB, A_DIM), 0.1),
        'lam_q2': nrm((N_AB, A_DIM), 0.1),
        'lam_k2': nrm((N_AB, A_DIM), 0.1),
        'm_norm_g': 1.0 + nrm((N_AB, M_HEADS, M_DIM), 0.05),
        'a_norm_g': 1.0 + nrm((N_AB, A_VDIM), 0.05),
        'w_out_ab': nrm((N_AB, E, D_MODEL), E ** -0.5),
        'ff_w1': nrm((N_AB, D_MODEL, FF_DENSE), D_MODEL ** -0.5),
        'ff_w3': nrm((N_AB, D_MODEL, FF_DENSE), D_MODEL ** -0.5),
        'ff_w2': nrm((N_AB, FF_DENSE, D_MODEL), FF_DENSE ** -0.5),
        'w_in_c': nrm((N_C, D_MODEL, 2 * E), D_MODEL ** -0.5),
        'b_in_c': nrm((N_C, 2 * E), 0.02),
        'c_ln_g': 1.0 + nrm((N_C, E), 0.05),
        'c_ln_b': nrm((N_C, E), 0.02),
        'c_ws': nrm((N_C, C_GROUPS, C_CHUNK, C_CHUNK), C_CHUNK ** -0.5),
        'c_bs': 1.0 + nrm((N_C, C_GROUPS, C_CHUNK), 0.1),
        'w_out_c': nrm((N_C, E, D_MODEL), E ** -0.5),
        'w_router': nrm((N_C, D_MODEL, N_EXPERTS), D_MODEL ** -0.5),
        'ex_w1': nrm((N_C, N_EXPERTS, D_MODEL, FF_EXPERT), D_MODEL ** -0.5),
        'ex_w3': nrm((N_C, N_EXPERTS, D_MODEL, FF_EXPERT), D_MODEL ** -0.5),
        'ex_w2': nrm((N_C, N_EXPERTS, FF_EXPERT, D_MODEL), FF_EXPERT ** -0.5),
    }


def reference(x_prompt, x_sample, c, cache_dattn_k, cache_dattn_v, state_mlstm_c, state_mlstm_n,
              state_mlstm_m, c_ctx, w_ada, b_ada, w_in_ab, conv_w, conv_b, gate_b, qn_g, kn_g,
              lam_q1, lam_k1, lam_q2, lam_k2, m_norm_g, a_norm_g, w_out_ab, ff_w1, ff_w3, ff_w2,
              w_in_c, b_in_c, c_ln_g, c_ln_b, c_ws, c_bs, w_out_c, w_router, ex_w1, ex_w3, ex_w2):
    xp, xs = x_prompt, x_sample
    bp = xp.shape[0]
    new_k, new_v, new_c, new_n, new_m = [], [], [], [], []
    for l in range(DEPTH):
        j = l // 2
        p_sh1, p_sc1, p_g1, p_sh2, p_sc2, p_g2 = adaln(c_ctx[None], w_ada[l], b_ada[l])
        s_sh1, s_sc1, s_g1, s_sh2, s_sc2, s_g2 = adaln(c, w_ada[l], b_ada[l])
        hp = modulate(xp, p_sh1, p_sc1)
        hs = modulate(xs, s_sh1, s_sc1)
        if l % 2 == 0:
            lam_init = 0.8 - 0.6 * math.exp(-0.3 * l)
            lam = (jnp.exp(jnp.sum((lam_q1[j] * lam_k1[j]).astype(f32)))
                   - jnp.exp(jnp.sum((lam_q2[j] * lam_k2[j]).astype(f32))) + lam_init)
            ab_w = (w_in_ab[j], conv_w[j], conv_b[j], gate_b[j], qn_g[j], kn_g[j],
                    m_norm_g[j], a_norm_g[j], w_out_ab[j])
            zc = jnp.zeros((bp, 2, M_HEADS, M_DIM, M_DIM), f32)
            zn = jnp.zeros((bp, 2, M_HEADS, M_DIM), f32)
            zm = jnp.zeros((bp, 2, M_HEADS), f32)
            op, (k_ctx, v_ctx, c_f, n_f, m_f) = mixer_ab(hp, *ab_w, lam, lam_init, None, None, zc, zn, zm)
            new_k.append(k_ctx)
            new_v.append(v_ctx)
            new_c.append(c_f)
            new_n.append(n_f)
            new_m.append(m_f)
            os_, _ = mixer_ab(hs, *ab_w, lam, lam_init, cache_dattn_k[:, j], cache_dattn_v[:, j],
                              state_mlstm_c[:, j], state_mlstm_n[:, j], state_mlstm_m[:, j])
        else:
            c_w = (w_in_c[j], b_in_c[j], c_ln_g[j], c_ln_b[j], c_ws[j], c_bs[j], w_out_c[j])
            op = chunk_mlp(hp, *c_w)
            os_ = chunk_mlp(hs, *c_w)
        xp = xp + p_g1 * op
        xs = xs + s_g1 * os_
        hp = modulate(xp, p_sh2, p_sc2)
        hs = modulate(xs, s_sh2, s_sc2)
        if l % 2 == 0:
            fp = swiglu(hp, ff_w1[j], ff_w3[j], ff_w2[j])
            fs = swiglu(hs, ff_w1[j], ff_w3[j], ff_w2[j])
        else:
            fp = moe(hp, w_router[j], ex_w1[j], ex_w3[j], ex_w2[j])
            fs = moe(hs, w_router[j], ex_w1[j], ex_w3[j], ex_w2[j])
        xp = xp + p_g2 * fp
        xs = xs + s_g2 * fs
    new_dattn_k = jnp.stack(new_k, axis=1)
    new_dattn_v = jnp.stack(new_v, axis=1)
    new_mlstm_c = jnp.stack(new_c, axis=1)
    new_mlstm_n = jnp.stack(new_n, axis=1)
    new_mlstm_m = jnp.stack(new_m, axis=1)
    return (xp, xs, new_dattn_k, new_dattn_v, new_mlstm_c, new_mlstm_n, new_mlstm_m)
```

```python
import functools
import math

import jax
import jax.numpy as jnp
import numpy as np
from jax import lax
from jax.experimental import pallas as pl
from jax.experimental.pallas import tpu as pltpu

f32 = jnp.float32
bf16 = jnp.bfloat16

D_MODEL = 1024
M_HEADS = 4
M_DIM = 128
M_W = M_HEADS * M_DIM
M_CHUNK = 128
A_HEADS = 4
A_VDIM = 128
A_DIM = 64
A_W = A_HEADS * A_VDIM
GRID_W = 64
ROPE_THETA = 10000.0
C_CHUNK = 128
C_GROUPS = 4
N_EXPERTS = 8
EPS = 1e-6

LANES = 128
ROW_TILE = 512
VMEM_LIMIT = 56 * 1024 * 1024


def _cparams(*sem):
    return pltpu.CompilerParams(dimension_semantics=tuple(sem), vmem_limit_bytes=VMEM_LIMIT)


def _const_spec(shape):
    nd = len(shape)
    return pl.BlockSpec(shape, lambda *_: (0,) * nd, pipeline_mode=pl.Buffered(1))


def _sigmoid(x):
    return 1.0 / (1.0 + jnp.exp(-x))


def _silu(x):
    return x * _sigmoid(x)


def _log_sigmoid(x):
    return jnp.minimum(x, 0.0) - jnp.log(1.0 + jnp.exp(-jnp.abs(x)))


def _rms(x):
    return x * lax.rsqrt(jnp.mean(x * x, axis=-1, keepdims=True) + EPS)


def _modulate(x, mod_ref, first):
    shift = mod_ref[first:first + 1, :]
    scale = mod_ref[first + 1:first + 2, :]
    return _rms(x) * (1.0 + scale) + shift


def _dot(a, b):
    return jnp.dot(a, b, preferred_element_type=f32)


def _dot_nt(a, b):
    return lax.dot_general(a, b, (((1,), (1,)), ((), ())), preferred_element_type=f32)


def _dot_tn(a, b):
    return lax.dot_general(a, b, (((0,), (0,)), ((), ())), preferred_element_type=f32)


def _split3(x):
    hi = x.astype(bf16)
    r1 = x - hi.astype(f32)
    mid = r1.astype(bf16)
    lo = (r1 - mid.astype(f32)).astype(bf16)
    return hi, mid, lo


def _group_of_tile(i, tm, n_prompt_rows, dec_seq):
    pt = n_prompt_rows // tm
    return jnp.where(i < pt, 0, 1 + (i - pt) // (dec_seq // tm))


def _ada_kernel(cv_ref, w_ref, b_ref, o_ref):
    a = _silu(cv_ref[...]).astype(bf16)
    o_ref[...] = _dot(a, w_ref[...].astype(bf16)) + b_ref[...]


def _ada_table(cv, w_ada, b_ada):
    depth, d, n = w_ada.shape
    g = cv.shape[0]
    gp = 8 * ((g + 7) // 8)
    cvp = jnp.zeros((gp, d), f32).at[:g].set(cv)
    tn = 1536
    out = pl.pallas_call(
        _ada_kernel,
        out_shape=jax.ShapeDtypeStruct((depth, gp, n), f32),
        grid=(depth, n // tn),
        in_specs=[
            pl.BlockSpec((gp, d), lambda l, j: (0, 0)),
            pl.BlockSpec((None, d, tn), lambda l, j: (l, 0, j)),
            pl.BlockSpec((None, 1, tn), lambda l, j: (l, 0, j)),
        ],
        out_specs=pl.BlockSpec((None, gp, tn), lambda l, j: (l, 0, j)),
        compiler_params=_cparams("arbitrary", "arbitrary"),
        name="ada_table",
    )(cvp, w_ada, b_ada.reshape(depth, 1, n))
    return out[:, :g].reshape(depth, g, 6, d)


def _inproj_kernel(x_ref, mod_ref, w_ref, wg_ref, z_ref, g_ref, *, n_main):
    h = _modulate(x_ref[...], mod_ref, 0).astype(bf16)
    step = 512
    for j in range(n_main // step):
        z_ref[:, j * step:(j + 1) * step] = _dot(h, w_ref[:, j * step:(j + 1) * step]).astype(bf16)
    g_ref[...] = _dot(h, wg_ref[...])


def _inproj(x, mod, w_main, w_gate, n_prompt_rows, dec_seq):
    r, d = x.shape
    tm = ROW_TILE
    n_main = w_main.shape[1]
    grp = functools.partial(_group_of_tile, tm=tm, n_prompt_rows=n_prompt_rows, dec_seq=dec_seq)
    return pl.pallas_call(
        functools.partial(_inproj_kernel, n_main=n_main),
        out_shape=(jax.ShapeDtypeStruct((r, n_main), bf16), jax.ShapeDtypeStruct((r, LANES), f32)),
        grid=(r // tm,),
        in_specs=[
            pl.BlockSpec((tm, d), lambda i: (i, 0)),
            pl.BlockSpec((None, 6, d), lambda i: (grp(i), 0, 0)),
            _const_spec(w_main.shape),
            _const_spec(w_gate.shape),
        ],
        out_specs=(pl.BlockSpec((tm, n_main), lambda i: (i, 0)), pl.BlockSpec((tm, LANES), lambda i: (i, 0))),
        compiler_params=_cparams("arbitrary"),
        name="inproj_ab",
    )(x, mod, w_main, w_gate)


def _conv_silu(x, w_ref, b_ref, t):
    row = lax.broadcasted_iota(jnp.int32, (t, 1), 0)
    prev = jnp.where(row == 0, 0.0, pltpu.roll(x, 1, 0))
    nxt = jnp.where(row == t - 1, 0.0, pltpu.roll(x, t - 1, 0))
    y = b_ref[...] + prev * w_ref[0:1, :] + x * w_ref[1:2, :] + nxt * w_ref[2:3, :]
    return _silu(y)


def _mlstm_kernel(q_ref, k_ref, v_ref, o_ref, g_ref, sel_ref, cwq_ref, cwk_ref, cbq_ref, cbk_ref, ng_ref,
                  c0_ref, n0_ref, m0_ref,
                  hm_ref, c_out, n_out, m_out,
                  q_s, k_s, gs_s, hf_s, *, t):
    nc = t // M_CHUNK
    lc = M_CHUNK
    q_s[...] = _conv_silu(q_ref[...].astype(f32), cwq_ref, cbq_ref, t)
    k_s[...] = _conv_silu(k_ref[...].astype(f32), cwk_ref, cbk_ref, t) * (M_DIM ** -0.5)

    lane = lax.broadcasted_iota(jnp.int32, (1, LANES), 1)
    g = g_ref[...]
    cols = [jnp.sum(g * sel_ref[j:j + 1, :], axis=-1, keepdims=True) for j in range(4)]
    cols[1] = _log_sigmoid(cols[1])
    cols[3] = _log_sigmoid(cols[3])
    compact = jnp.zeros((t, LANES), f32)
    for j in range(4):
        compact = jnp.where(lane == j, cols[j], compact)
    gs_s[...] = compact

    rr = lax.broadcasted_iota(jnp.int32, (lc, lc), 0)
    cc = lax.broadcasted_iota(jnp.int32, (lc, lc), 1)
    tri_incl = jnp.where(rr >= cc, 1.0, 0.0).astype(bf16)

    def cumsum_body(c, _):
        off = pl.multiple_of(c * lc, lc)
        tile = gs_s[pl.ds(off, lc), :]
        lf = jnp.where(lane == 4, tile[:, 1:2], jnp.where(lane == 5, tile[:, 3:4], 0.0))
        hi, mid, lo = _split3(lf)
        cs = _dot(tri_incl, hi) + _dot(tri_incl, mid) + _dot(tri_incl, lo)
        gs_s[pl.ds(off, lc), :] = jnp.where(lane >= 4, cs, tile)
        return 0

    lax.fori_loop(0, nc, cumsum_body, 0)

    ng = ng_ref[...]

    for d in range(2):
        keep = (rr >= cc) if d == 0 else (cc >= rr)

        def body(i, carry, d=d, keep=keep):
            cm, nm, mm = carry
            c = i if d == 0 else nc - 1 - i
            off = pl.multiple_of(c * lc, lc)
            qc = q_s[pl.ds(off, lc), :]
            kc = k_s[pl.ds(off, lc), :]
            qb = qc.astype(bf16)
            kb = kc.astype(bf16)
            vb = v_ref[pl.ds(off, lc), :]
            tile = gs_s[pl.ds(off, lc), :]
            tile_t = tile.T
            li_col, lf_col, a_col = tile[:, 2 * d:2 * d + 1], tile[:, 2 * d + 1:2 * d + 2], tile[:, 4 + d:5 + d]
            li_row, lf_row, a_row = tile_t[2 * d:2 * d + 1, :], tile_t[2 * d + 1:2 * d + 2, :], tile_t[4 + d:5 + d, :]
            total = a_col[lc - 1:lc, :]
            if d == 0:
                b_col, b_row = a_col, a_row
            else:
                b_col, b_row = total - a_col + lf_col, total - a_row + lf_row
            dmat = jnp.where(keep, b_col - b_row + li_row, -jnp.inf)
            inter = mm + b_col
            mt = jnp.maximum(inter, jnp.max(dmat, axis=-1, keepdims=True))
            w_inter = jnp.exp(inter - mt)
            s = _dot_nt(qb, kb) * jnp.exp(dmat - mt)
            num = w_inter * _dot(qb, cm.astype(bf16)) + _dot(s.astype(bf16), vb)
            den = w_inter * jnp.sum(qc * nm, axis=-1, keepdims=True) + jnp.sum(s, axis=-1, keepdims=True)
            hc = num / jnp.maximum(jnp.abs(den), jnp.exp(-mt))
            gcol = total - b_col + li_col
            m_new = jnp.maximum(mm + total, jnp.max(gcol, axis=0, keepdims=True))
            decay = jnp.exp(mm + total - m_new)
            kw = kc * jnp.exp(gcol - m_new)
            c_new = decay * cm + _dot_tn(kw.astype(bf16), vb)
            n_new = decay * nm + jnp.sum(kw, axis=0, keepdims=True)
            if d == 0:
                hf_s[pl.ds(off, lc), :] = hc
            else:
                hsum = hf_s[pl.ds(off, lc), :] + hc
                mo = _sigmoid(o_ref[pl.ds(off, lc), :].astype(f32))
                hm_ref[pl.ds(off, lc), :] = (_rms(hsum) * ng * mo).astype(hm_ref.dtype)
            return c_new, n_new, m_new

        cm, nm, mm = lax.fori_loop(0, nc, body, (c0_ref[d], n0_ref[d], m0_ref[d]))
        c_out[d] = cm
        n_out[d] = nm
        m_out[d] = mm


def _mlstm(z, gates, sel, conv_w, conv_b, m_norm_g, c0, n0, m0, *, batch, t, row_block0):
    nh = M_HEADS
    grid = (batch, nh)
    seq = lambda col0: pl.BlockSpec((t, LANES), lambda b, h: (row_block0 + b, col0 + h))
    per_head_vec = lambda col0: pl.BlockSpec((1, LANES), lambda b, h: (0, col0 + h))
    state = lambda *tail: pl.BlockSpec((None, 2, None) + tail, lambda b, h: (b, 0, h) + (0,) * len(tail))
    return pl.pallas_call(
        functools.partial(_mlstm_kernel, t=t),
        out_shape=(
            jax.ShapeDtypeStruct((batch * t, M_W), bf16),
            jax.ShapeDtypeStruct((batch, 2, nh, M_DIM, M_DIM), f32),
            jax.ShapeDtypeStruct((batch, 2, nh, 1, M_DIM), f32),
            jax.ShapeDtypeStruct((batch, 2, nh, 1, 1), f32),
        ),
        grid=grid,
        in_specs=[
            seq(0), seq(nh), seq(2 * nh), seq(3 * nh),
            pl.BlockSpec((t, LANES), lambda b, h: (row_block0 + b, 0)),
            pl.BlockSpec((None, 4, LANES), lambda b, h: (h, 0, 0)),
            pl.BlockSpec((3, LANES), lambda b, h: (0, h)),
            pl.BlockSpec((3, LANES), lambda b, h: (0, nh + h)),
            per_head_vec(0), per_head_vec(nh),
            pl.BlockSpec((None, 1, LANES), lambda b, h: (h, 0, 0)),
            state(M_DIM, M_DIM), state(1, M_DIM), state(1, 1),
        ],
        out_specs=(
            pl.BlockSpec((t, LANES), lambda b, h: (b, h)),
            state(M_DIM, M_DIM), state(1, M_DIM), state(1, 1),
        ),
        scratch_shapes=[pltpu.VMEM((t, LANES), f32)] * 4,
        compiler_params=_cparams("arbitrary", "arbitrary"),
        name=f"mlstm_t{t}",
    )(z, z, z, z, gates, sel, conv_w, conv_w, conv_b, conv_b, m_norm_g, c0, n0, m0)


def _pair_norm(x, gain):
    lane = lax.broadcasted_iota(jnp.int32, (1, LANES), 1)
    first = lane < A_DIM
    sq = x * x
    s_all = jnp.sum(sq, axis=-1, keepdims=True)
    s0 = jnp.sum(jnp.where(first, sq, 0.0), axis=-1, keepdims=True)
    inv0 = lax.rsqrt(s0 * (1.0 / A_DIM) + EPS)
    inv1 = lax.rsqrt((s_all - s0) * (1.0 / A_DIM) + EPS)
    return x * jnp.where(first, inv0, inv1) * gain


def _rope(x, cos, sin_signed):
    lane = lax.broadcasted_iota(jnp.int32, (1, LANES), 1)
    nf = A_DIM // 4
    partner = jnp.where((lane % (2 * nf)) < nf, pltpu.roll(x, LANES - nf, 1), pltpu.roll(x, nf, 1))
    return x * cos + partner * sin_signed


def _attn_kernel(*refs, t, tq, rope, ctx, out_scale):
    it = iter(refs)
    lam_ref = next(it)
    q_ref, k_ref, v_ref = next(it), next(it), next(it)
    qg_ref, kg_ref, ag_ref = next(it), next(it), next(it)
    if rope:
        cosq_ref, sinq_ref, cosk_ref, sink_ref = next(it), next(it), next(it), next(it)
    if ctx:
        kc_ref, vc_ref = next(it), next(it)
    ha_ref = next(it)
    if not ctx:
        newk_ref, newv_ref = next(it), next(it)
    kn_s = next(it)

    @pl.when(pl.program_id(2) == 0)
    def _():
        kn = _pair_norm(k_ref[...].astype(f32), kg_ref[...])
        if not ctx:
            newk_ref[0] = kn[:, :A_DIM]
            newk_ref[1] = kn[:, A_DIM:]
            newv_ref[...] = v_ref[...].astype(f32)
        if rope:
            kn = _rope(kn, cosk_ref[...], sink_ref[...])
        kn_s[...] = kn.astype(bf16)

    lane = lax.broadcasted_iota(jnp.int32, (1, LANES), 1)
    q = _pair_norm(q_ref[...].astype(f32), qg_ref[...])
    if rope:
        q = _rope(q, cosq_ref[...], sinq_ref[...])
    q = q * (A_DIM ** -0.5)
    qs = [jnp.where(lane < A_DIM, q, 0.0).astype(bf16), jnp.where(lane >= A_DIM, q, 0.0).astype(bf16)]
    kn = kn_s[...]
    if ctx:
        kcb = kc_ref[...].astype(bf16)
    probs = []
    for i in range(2):
        sn = _dot_nt(qs[i], kn)
        mx = jnp.max(sn, axis=-1, keepdims=True)
        if ctx:
            sc = _dot_nt(qs[i], kcb)
            mx = jnp.maximum(mx, jnp.max(sc, axis=-1, keepdims=True))
        pn = jnp.exp(sn - mx)
        den = jnp.sum(pn, axis=-1, keepdims=True)
        if ctx:
            pc = jnp.exp(sc - mx)
            den = den + jnp.sum(pc, axis=-1, keepdims=True)
        inv = 1.0 / den
        probs.append((pn * inv, pc * inv if ctx else None))
    lam = lam_ref[0]
    o = _dot((probs[0][0] - lam * probs[1][0]).astype(bf16), v_ref[...])
    if ctx:
        o = o + _dot((probs[0][1] - lam * probs[1][1]).astype(bf16), vc_ref[...].astype(bf16))
    ha_ref[...] = (_rms(o) * ag_ref[...] * out_scale).astype(ha_ref.dtype)


def _attn(z, lam, qg2, kg2, a_norm_g, *, batch, t, row_block0, rope_tabs=None, ctx_kv=None, out_scale):
    nh = A_HEADS
    tq = min(t, 256)
    nq = t // tq
    rope = rope_tabs is not None
    ctx = ctx_kv is not None
    qcol, kcol, vcol = 4 * M_HEADS, 4 * M_HEADS + nh, 4 * M_HEADS + 2 * nh
    vec = pl.BlockSpec((1, LANES), lambda b, h, i: (0, 0))
    in_specs = [
        pl.BlockSpec(memory_space=pltpu.SMEM),
        pl.BlockSpec((tq, LANES), lambda b, h, i: ((row_block0 + b) * nq + i, qcol + h)),
        pl.BlockSpec((t, LANES), lambda b, h, i: (row_block0 + b, kcol + h)),
        pl.BlockSpec((t, LANES), lambda b, h, i: (row_block0 + b, vcol + h)),
        vec, vec, vec,
    ]
    args = [lam, z, z, z, qg2, kg2, a_norm_g]
    if rope:
        cos, sin = rope_tabs
        in_specs += [pl.BlockSpec((tq, LANES), lambda b, h, i: (i, 0))] * 2
        in_specs += [pl.BlockSpec((t, LANES), lambda b, h, i: (0, 0))] * 2
        args += [cos, sin, cos, sin]
    if ctx:
        kc, vc = ctx_kv
        p = kc.shape[2]
        in_specs += [pl.BlockSpec((None, None, p, LANES), lambda b, h, i: (b, h, 0, 0))] * 2
        args += [kc, vc]
    out_shape = [jax.ShapeDtypeStruct((batch * t, A_W), bf16)]
    out_specs = [pl.BlockSpec((tq, LANES), lambda b, h, i: (b * nq + i, h))]
    if not ctx:
        out_shape += [jax.ShapeDtypeStruct((batch, nh, 2, t, A_DIM), f32),
                      jax.ShapeDtypeStruct((batch, nh, t, A_VDIM), f32)]
        out_specs += [pl.BlockSpec((None, None, 2, t, A_DIM), lambda b, h, i: (b, h, 0, 0, 0)),
                      pl.BlockSpec((None, None, t, A_VDIM), lambda b, h, i: (b, h, 0, 0))]
    return pl.pallas_call(
        functools.partial(_attn_kernel, t=t, tq=tq, rope=rope, ctx=ctx, out_scale=out_scale),
        out_shape=tuple(out_shape),
        grid=(batch, nh, nq),
        in_specs=in_specs,
        out_specs=tuple(out_specs),
        scratch_shapes=[pltpu.VMEM((t, LANES), bf16)],
        compiler_params=_cparams("arbitrary", "arbitrary", "arbitrary"),
        name=f"diff_attn_t{t}",
    )(*args)


def _rope_tables(t):
    rows = t // GRID_W
    pos_row = np.repeat(np.arange(rows, dtype=np.float32), GRID_W)
    pos_col = (np.arange(rows * GRID_W) % GRID_W).astype(np.float32)
    nf = A_DIM // 4
    inv = (ROPE_THETA ** (-jnp.arange(nf, dtype=f32) / nf))
    lane = np.arange(LANES)
    j = lane % (2 * nf)
    use_col = (lane % A_DIM) >= (A_DIM // 2)
    pos = jnp.where(jnp.asarray(use_col)[None, :], jnp.asarray(pos_col)[:, None], jnp.asarray(pos_row)[:, None])
    ang = pos * inv[jnp.asarray(j % nf)][None, :]
    sign = jnp.asarray(np.where(j < nf, -1.0, 1.0).astype(np.float32))[None, :]
    return jnp.cos(ang), jnp.sin(ang) * sign


def _outproj_kernel(x_ref, hm_ref, ha_ref, mod_ref, w_ref, o_ref):
    half = hm_ref.shape[1]
    y = _dot(hm_ref[...], w_ref[:half, :]) + _dot(ha_ref[...], w_ref[half:, :])
    o_ref[...] = x_ref[...] + mod_ref[2:3, :] * y


def _outproj(x, hm, ha, mod, w, n_prompt_rows, dec_seq):
    r, d = x.shape
    tm = ROW_TILE
    grp = functools.partial(_group_of_tile, tm=tm, n_prompt_rows=n_prompt_rows, dec_seq=dec_seq)
    return pl.pallas_call(
        _outproj_kernel,
        out_shape=jax.ShapeDtypeStruct((r, d), f32),
        grid=(r // tm,),
        in_specs=[
            pl.BlockSpec((tm, d), lambda i: (i, 0)),
            pl.BlockSpec((tm, hm.shape[1]), lambda i: (i, 0)),
            pl.BlockSpec((tm, ha.shape[1]), lambda i: (i, 0)),
            pl.BlockSpec((None, 6, d), lambda i: (grp(i), 0, 0)),
            _const_spec(w.shape),
        ],
        out_specs=pl.BlockSpec((tm, d), lambda i: (i, 0)),
        compiler_params=_cparams("arbitrary"),
        name="outproj_ab",
    )(x, hm, ha, mod, w)


def _ffn_kernel(x_ref, mod_ref, w1_ref, w3_ref, w2_ref, o_ref, *, chunk):
    x = x_ref[...]
    h = _modulate(x, mod_ref, 3).astype(bf16)
    ff = w1_ref.shape[1]
    acc = jnp.zeros(x.shape, f32)
    for j in range(ff // chunk):
        sl = slice(j * chunk, (j + 1) * chunk)
        a = _silu(_dot(h, w1_ref[:, sl])) * _dot(h, w3_ref[:, sl])
        acc = acc + _dot(a.astype(bf16), w2_ref[sl, :])
    o_ref[...] = x + mod_ref[5:6, :] * acc


def _ffn(x, mod, w1, w3, w2, n_prompt_rows, dec_seq):
    r, d = x.shape
    tm = ROW_TILE
    grp = functools.partial(_group_of_tile, tm=tm, n_prompt_rows=n_prompt_rows, dec_seq=dec_seq)
    ff = w1.shape[1]
    chunk = ff // 2 if (ff // 2) % LANES == 0 else ff
    return pl.pallas_call(
        functools.partial(_ffn_kernel, chunk=chunk),
        out_shape=jax.ShapeDtypeStruct((r, d), f32),
        grid=(r // tm,),
        in_specs=[
            pl.BlockSpec((tm, d), lambda i: (i, 0)),
            pl.BlockSpec((None, 6, d), lambda i: (grp(i), 0, 0)),
            _const_spec(w1.shape), _const_spec(w3.shape), _const_spec(w2.shape),
        ],
        out_specs=pl.BlockSpec((tm, d), lambda i: (i, 0)),
        compiler_params=_cparams("arbitrary"),
        name="ffn_dense",
    )(x, mod, w1, w3, w2)


def _gelu_tanh(x):
    return 0.5 * x * (1.0 + jnp.tanh(math.sqrt(2.0 / math.pi) * (x + 0.044715 * (x * x * x))))


def _gmlp_kernel(x_ref, mod_ref, win_ref, bin_ref, lng_ref, lnb_ref, ws_ref, bs_ref, wout_ref, o_ref, us_s):
    x = x_ref[...]
    tm = x.shape[0]
    e = wout_ref.shape[0]
    ge = e // C_GROUPS
    h = _modulate(x, mod_ref, 0).astype(bf16)
    u = _gelu_tanh(_dot(h, win_ref[:, :e]) + bin_ref[:, :e])
    v = _gelu_tanh(_dot(h, win_ref[:, e:]) + bin_ref[:, e:])
    mu = jnp.mean(v, axis=-1, keepdims=True)
    vc = v - mu
    var = jnp.mean(vc * vc, axis=-1, keepdims=True)
    vn = (vc * lax.rsqrt(var + EPS) * lng_ref[...] + lnb_ref[...]).astype(bf16)
    for n in range(tm // C_CHUNK):
        rows = slice(n * C_CHUNK, (n + 1) * C_CHUNK)
        for g in range(C_GROUPS):
            cols = slice(g * ge, (g + 1) * ge)
            s = _dot(ws_ref[g], vn[rows, cols]) + bs_ref[:, cols]
            us_s[rows, cols] = (u[rows, cols] * s).astype(bf16)
    o_ref[...] = x + mod_ref[2:3, :] * _dot(us_s[...], wout_ref[...])


def _gmlp(x, mod, w_in, b_in, ln_g, ln_b, ws, bs_full, w_out, n_prompt_rows, dec_seq):
    r, d = x.shape
    tm = ROW_TILE
    e = w_out.shape[0]
    grp = functools.partial(_group_of_tile, tm=tm, n_prompt_rows=n_prompt_rows, dec_seq=dec_seq)
    return pl.pallas_call(
        _gmlp_kernel,
        out_shape=jax.ShapeDtypeStruct((r, d), f32),
        grid=(r // tm,),
        in_specs=[
            pl.BlockSpec((tm, d), lambda i: (i, 0)),
            pl.BlockSpec((None, 6, d), lambda i: (grp(i), 0, 0)),
            _const_spec(w_in.shape), _const_spec(b_in.shape), _const_spec(ln_g.shape), _const_spec(ln_b.shape),
            _const_spec(ws.shape), _const_spec(bs_full.shape), _const_spec(w_out.shape),
        ],
        out_specs=pl.BlockSpec((tm, d), lambda i: (i, 0)),
        scratch_shapes=[pltpu.VMEM((tm, e), bf16)],
        compiler_params=_cparams("arbitrary"),
        name="gmlp",
    )(x, mod, w_in, b_in, ln_g, ln_b, ws, bs_full, w_out)


def _router_kernel(x_ref, mod_ref, wr_hi_ref, wr_lo_ref, h_ref, gates_ref):
    hf = _modulate(x_ref[...], mod_ref, 3)
    hb = hf.astype(bf16)
    h_ref[...] = hb
    h_lo = (hf - hb.astype(f32)).astype(bf16)
    logits = _dot(hb, wr_hi_ref[...]) + (_dot(hb, wr_lo_ref[...]) + _dot(h_lo, wr_hi_ref[...]))
    lane = lax.broadcasted_iota(jnp.int32, logits.shape, 1)
    logits = jnp.where(lane < N_EXPERTS, logits, -jnp.inf)
    m1 = jnp.max(logits, axis=-1, keepdims=True)
    i1 = jnp.min(jnp.where(logits == m1, lane, LANES), axis=-1, keepdims=True)
    rest = jnp.where(lane == i1, -jnp.inf, logits)
    m2 = jnp.max(rest, axis=-1, keepdims=True)
    i2 = jnp.min(jnp.where(rest == m2, lane, LANES), axis=-1, keepdims=True)
    e2 = jnp.exp(m2 - m1)
    w1 = 1.0 / (1.0 + e2)
    w2 = e2 * w1
    gates_ref[...] = jnp.where(lane == i1, w1, 0.0) + jnp.where(lane == i2, w2, 0.0)


def _router(x, mod, wr_hi, wr_lo, n_prompt_rows, dec_seq):
    r, d = x.shape
    tm = ROW_TILE
    grp = functools.partial(_group_of_tile, tm=tm, n_prompt_rows=n_prompt_rows, dec_seq=dec_seq)
    return pl.pallas_call(
        _router_kernel,
        out_shape=(jax.ShapeDtypeStruct((r, d), bf16), jax.ShapeDtypeStruct((r, LANES), f32)),
        grid=(r // tm,),
        in_specs=[
            pl.BlockSpec((tm, d), lambda i: (i, 0)),
            pl.BlockSpec((None, 6, d), lambda i: (grp(i), 0, 0)),
            _const_spec(wr_hi.shape), _const_spec(wr_lo.shape),
        ],
        out_specs=(pl.BlockSpec((tm, d), lambda i: (i, 0)), pl.BlockSpec((tm, LANES), lambda i: (i, 0))),
        compiler_params=_cparams("arbitrary"),
        name="router",
    )(x, mod, wr_hi, wr_lo)


def _moe_dense_kernel(x_ref, h_ref, gates_ref, mod_ref, w1_ref, w3_ref, w2_ref, o_ref, acc_s):
    e = pl.program_id(1)
    f = pl.program_id(2)

    @pl.when((e == 0) & (f == 0))
    def _():
        acc_s[...] = jnp.zeros_like(acc_s)

    h = h_ref[...]
    lane = lax.broadcasted_iota(jnp.int32, (1, LANES), 1)
    gate = jnp.sum(jnp.where(lane == e, gates_ref[...], 0.0), axis=-1, keepdims=True)
    a = _silu(_dot(h, w1_ref[...])) * _dot(h, w3_ref[...])
    acc_s[...] += gate * _dot(a.astype(bf16), w2_ref[...])

    @pl.when((e == pl.num_programs(1) - 1) & (f == pl.num_programs(2) - 1))
    def _():
        o_ref[...] = x_ref[...] + mod_ref[5:6, :] * acc_s[...]


def _moe_dense(x, h, gates, mod, w1, w3, w2, n_prompt_rows, dec_seq):
    r, d = x.shape
    tm = 1024
    ne, _, ff = w1.shape
    tf = 512
    grp = functools.partial(_group_of_tile, tm=tm, n_prompt_rows=n_prompt_rows, dec_seq=dec_seq)
    return pl.pallas_call(
        _moe_dense_kernel,
        out_shape=jax.ShapeDtypeStruct((r, d), f32),
        grid=(r // tm, ne, ff // tf),
        in_specs=[
            pl.BlockSpec((tm, d), lambda i, e, f: (i, 0)),
            pl.BlockSpec((tm, d), lambda i, e, f: (i, 0)),
            pl.BlockSpec((tm, LANES), lambda i, e, f: (i, 0)),
            pl.BlockSpec((None, 6, d), lambda i, e, f: (grp(i), 0, 0)),
            pl.BlockSpec((None, d, tf), lambda i, e, f: (e, 0, f)),
            pl.BlockSpec((None, d, tf), lambda i, e, f: (e, 0, f)),
            pl.BlockSpec((None, tf, d), lambda i, e, f: (e, f, 0)),
        ],
        out_specs=pl.BlockSpec((tm, d), lambda i, e, f: (i, 0)),
        scratch_shapes=[pltpu.VMEM((tm, d), f32)],
        compiler_params=_cparams("arbitrary", "arbitrary", "arbitrary"),
        name="moe_dense",
    )(x, h, gates, mod, w1, w3, w2)


def kernel(x_prompt, x_sample, c, cache_dattn_k, cache_dattn_v, state_mlstm_c, state_mlstm_n, state_mlstm_m,
           c_ctx, w_ada, b_ada, w_in_ab, conv_w, conv_b, gate_b, qn_g, kn_g, lam_q1, lam_k1, lam_q2, lam_k2,
           m_norm_g, a_norm_g, w_out_ab, ff_w1, ff_w3, ff_w2, w_in_c, b_in_c, c_ln_g, c_ln_b, c_ws, c_bs,
           w_out_c, w_router, ex_w1, ex_w3, ex_w2):
    bp, seq, d = x_prompt.shape
    bs, dec_seq, _ = x_sample.shape
    depth = w_ada.shape[0]
    n_prompt_rows = bp * seq
    assert n_prompt_rows % dec_seq == 0 and seq % M_CHUNK == 0 and dec_seq % 1024 == 0
    nh = M_HEADS

    mods = _ada_table(jnp.concatenate([c_ctx[None], c], axis=0), w_ada, b_ada)
    x = jnp.concatenate([x_prompt.reshape(n_prompt_rows, d), x_sample.reshape(bs * dec_seq, d)], axis=0)
    rows = (n_prompt_rows, dec_seq)

    new_k, new_v, new_c, new_n, new_m = [], [], [], [], []
    for l in range(depth):
        j = l // 2
        mod = mods[l]
        if l % 2 == 0:
            lam_init = 0.8 - 0.6 * math.exp(-0.3 * l)
            lam = (jnp.exp(jnp.sum((lam_q1[j] * lam_k1[j]).astype(f32)))
                   - jnp.exp(jnp.sum((lam_q2[j] * lam_k2[j]).astype(f32))) + lam_init).reshape(1)
            o3 = 4 * M_W
            o4 = o3 + 4 * nh
            w = w_in_ab[j]
            w_main = jnp.concatenate([w[:, :o3], w[:, o4:]], axis=1).astype(bf16)
            w_gate = jnp.zeros((d, LANES), f32).at[:, :4 * nh].set(w[:, o3:o4]).astype(bf16)
            z, gates = _inproj(x, mod, w_main, w_gate, *rows)
            gates = gates + jnp.zeros((1, LANES), f32).at[0, :4 * nh].set(gate_b[j])
            sel = np.zeros((nh, 4, LANES), np.float32)
            for h in range(nh):
                for gi in range(4):
                    sel[h, gi, gi * nh + h] = 1.0
            sel = jnp.asarray(sel)
            mng = m_norm_g[j].reshape(nh, 1, M_DIM)
            cb = conv_b[j].reshape(1, 2 * M_W)
            zc = jnp.zeros((bp, 2, nh, M_DIM, M_DIM), f32)
            zn = jnp.zeros((bp, 2, nh, 1, M_DIM), f32)
            zm = jnp.zeros((bp, 2, nh, 1, 1), f32)
            hm_p, c_f, n_f, m_f = _mlstm(z, gates, sel, conv_w[j], cb, mng, zc, zn, zm,
                                         batch=bp, t=seq, row_block0=0)
            hm_s, _, _, _ = _mlstm(z, gates, sel, conv_w[j], cb, mng, state_mlstm_c[:, j],
                                   state_mlstm_n[:, j].reshape(bs, 2, nh, 1, M_DIM),
                                   state_mlstm_m[:, j].reshape(bs, 2, nh, 1, 1),
                                   batch=bs, t=dec_seq, row_block0=n_prompt_rows // dec_seq)
            new_c.append(c_f)
            new_n.append(n_f.reshape(bp, 2, nh, M_DIM))
            new_m.append(m_f.reshape(bp, 2, nh))

            qg2 = jnp.tile(qn_g[j], 2).reshape(1, LANES)
            kg2 = jnp.tile(kn_g[j], 2).reshape(1, LANES)
            ag = a_norm_g[j].reshape(1, LANES)
            ha_p, k_ctx, v_ctx = _attn(z, lam, qg2, kg2, ag, batch=bp, t=seq, row_block0=0,
                                       out_scale=1.0 - lam_init)
            kc = cache_dattn_k[:, j].transpose(0, 1, 3, 2, 4).reshape(bs, A_HEADS, -1, LANES)
            (ha_s,) = _attn(z, lam, qg2, kg2, ag, batch=bs, t=dec_seq, row_block0=n_prompt_rows // dec_seq,
                            rope_tabs=_rope_tables(dec_seq), ctx_kv=(kc, cache_dattn_v[:, j]),
                            out_scale=1.0 - lam_init)
            new_k.append(k_ctx)
            new_v.append(v_ctx)
            hm = jnp.concatenate([hm_p, hm_s], axis=0)
            ha = jnp.concatenate([ha_p, ha_s], axis=0)
            x = _outproj(x, hm, ha, mod, w_out_ab[j].astype(bf16), *rows)
            x = _ffn(x, mod, ff_w1[j].astype(bf16), ff_w3[j].astype(bf16), ff_w2[j].astype(bf16), *rows)
        else:
            e = w_out_c.shape[1]
            bs_full = jnp.repeat(c_bs[j].T, e // C_GROUPS, axis=1)
            x = _gmlp(x, mod, w_in_c[j].astype(bf16), b_in_c[j].reshape(1, -1), c_ln_g[j].reshape(1, -1),
                      c_ln_b[j].reshape(1, -1), c_ws[j].astype(bf16), bs_full, w_out_c[j].astype(bf16), *rows)
            wr = jnp.zeros((d, LANES), f32).at[:, :N_EXPERTS].set(w_router[j])
            wr_hi = wr.astype(bf16)
            wr_lo = (wr - wr_hi.astype(f32)).astype(bf16)
            h, gates = _router(x, mod, wr_hi, wr_lo, *rows)
            x = _moe_dense(x, h, gates, mod, ex_w1[j].astype(bf16), ex_w3[j].astype(bf16),
                           ex_w2[j].astype(bf16), *rows)

    y_prompt = x[:n_prompt_rows].reshape(bp, seq, d)
    y_sample = x[n_prompt_rows:].reshape(bs, dec_seq, d)
    return (y_prompt, y_sample, jnp.stack(new_k, axis=1), jnp.stack(new_v, axis=1),
            jnp.stack(new_c, axis=1), jnp.stack(new_n, axis=1), jnp.stack(new_m, axis=1))
```

```python
import functools
import math

import jax
import jax.numpy as jnp
import numpy as np
from jax import lax
from jax.experimental import pallas as pl
from jax.experimental.pallas import tpu as pltpu

f32 = jnp.float32
bf16 = jnp.bfloat16

D_MODEL = 1024
M_HEADS = 4
M_DIM = 128
M_W = M_HEADS * M_DIM
M_CHUNK = 128
A_HEADS = 4
A_VDIM = 128
A_DIM = 64
A_W = A_HEADS * A_VDIM
GRID_W = 64
ROPE_THETA = 10000.0
C_CHUNK = 128
C_GROUPS = 4
N_EXPERTS = 8
EPS = 1e-6

LANES = 128
ROW_TILE = 512
PIECE_ALIGN = 16
MOE_SUPER_BLOCK = 2048
MOE_SORT_BLOCK = 512
MOE_CHUNK = 256
MOE_FF_TILE = 512
VMEM_LIMIT = 56 * 1024 * 1024


def _cparams(*sem):
    return pltpu.CompilerParams(dimension_semantics=tuple(sem), vmem_limit_bytes=VMEM_LIMIT)


def _const_spec(shape):
    nd = len(shape)
    return pl.BlockSpec(shape, lambda *_: (0,) * nd, pipeline_mode=pl.Buffered(1))


def _sigmoid(x):
    return 1.0 / (1.0 + jnp.exp(-x))


def _silu(x):
    return x * _sigmoid(x)


def _log_sigmoid(x):
    return jnp.minimum(x, 0.0) - jnp.log(1.0 + jnp.exp(-jnp.abs(x)))


def _rms(x):
    return x * lax.rsqrt(jnp.mean(x * x, axis=-1, keepdims=True) + EPS)


def _modulate(x, mod_ref, first):
    shift = mod_ref[first:first + 1, :]
    scale = mod_ref[first + 1:first + 2, :]
    return _rms(x) * (1.0 + scale) + shift


def _dot(a, b):
    return jnp.dot(a, b, preferred_element_type=f32)


def _dot_nt(a, b):
    return lax.dot_general(a, b, (((1,), (1,)), ((), ())), preferred_element_type=f32)


def _dot_tn(a, b):
    return lax.dot_general(a, b, (((0,), (0,)), ((), ())), preferred_element_type=f32)


def _split3(x):
    hi = x.astype(bf16)
    r1 = x - hi.astype(f32)
    mid = r1.astype(bf16)
    lo = (r1 - mid.astype(f32)).astype(bf16)
    return hi, mid, lo


def _group_of_tile(i, tm, n_prompt_rows, dec_seq):
    pt = n_prompt_rows // tm
    return jnp.where(i < pt, 0, 1 + (i - pt) // (dec_seq // tm))


def _ada_kernel(cv_ref, w_ref, b_ref, o_ref):
    a = _silu(cv_ref[...]).astype(bf16)
    o_ref[...] = _dot(a, w_ref[...].astype(bf16)) + b_ref[...]


def _ada_table(cv, w_ada, b_ada):
    depth, d, n = w_ada.shape
    g = cv.shape[0]
    gp = 8 * ((g + 7) // 8)
    cvp = jnp.zeros((gp, d), f32).at[:g].set(cv)
    tn = 1536
    out = pl.pallas_call(
        _ada_kernel,
        out_shape=jax.ShapeDtypeStruct((depth, gp, n), f32),
        grid=(depth, n // tn),
        in_specs=[
            pl.BlockSpec((gp, d), lambda l, j: (0, 0)),
            pl.BlockSpec((None, d, tn), lambda l, j: (l, 0, j)),
            pl.BlockSpec((None, 1, tn), lambda l, j: (l, 0, j)),
        ],
        out_specs=pl.BlockSpec((None, gp, tn), lambda l, j: (l, 0, j)),
        compiler_params=_cparams("arbitrary", "arbitrary"),
        name="ada_table",
    )(cvp, w_ada, b_ada.reshape(depth, 1, n))
    return out[:, :g].reshape(depth, g, 6, d)


def _inproj_kernel(x_ref, mod_ref, w_ref, wg_ref, z_ref, g_ref, *, n_main):
    h = _modulate(x_ref[...], mod_ref, 0).astype(bf16)
    step = 512
    for j in range(n_main // step):
        z_ref[:, j * step:(j + 1) * step] = _dot(h, w_ref[:, j * step:(j + 1) * step]).astype(bf16)
    g_ref[...] = _dot(h, wg_ref[...])


def _inproj(x, mod, w_main, w_gate, n_prompt_rows, dec_seq):
    r, d = x.shape
    tm = ROW_TILE
    n_main = w_main.shape[1]
    grp = functools.partial(_group_of_tile, tm=tm, n_prompt_rows=n_prompt_rows, dec_seq=dec_seq)
    return pl.pallas_call(
        functools.partial(_inproj_kernel, n_main=n_main),
        out_shape=(jax.ShapeDtypeStruct((r, n_main), bf16), jax.ShapeDtypeStruct((r, LANES), f32)),
        grid=(r // tm,),
        in_specs=[
            pl.BlockSpec((tm, d), lambda i: (i, 0)),
            pl.BlockSpec((None, 6, d), lambda i: (grp(i), 0, 0)),
            _const_spec(w_main.shape),
            _const_spec(w_gate.shape),
        ],
        out_specs=(pl.BlockSpec((tm, n_main), lambda i: (i, 0)), pl.BlockSpec((tm, LANES), lambda i: (i, 0))),
        compiler_params=_cparams("arbitrary"),
        name="inproj_ab",
    )(x, mod, w_main, w_gate)


def _conv_silu(x, w_ref, b_ref, t):
    row = lax.broadcasted_iota(jnp.int32, (t, 1), 0)
    prev = jnp.where(row == 0, 0.0, pltpu.roll(x, 1, 0))
    nxt = jnp.where(row == t - 1, 0.0, pltpu.roll(x, t - 1, 0))
    y = b_ref[...] + prev * w_ref[0:1, :] + x * w_ref[1:2, :] + nxt * w_ref[2:3, :]
    return _silu(y)


def _mlstm_kernel(q_ref, k_ref, v_ref, o_ref, g_ref, sel_ref, cwq_ref, cwk_ref, cbq_ref, cbk_ref, ng_ref,
                  c0_ref, n0_ref, m0_ref,
                  hm_ref, c_out, n_out, m_out,
                  q_s, k_s, gs_s, hf_s, *, t):
    nc = t // M_CHUNK
    lc = M_CHUNK
    q_s[...] = _conv_silu(q_ref[...].astype(f32), cwq_ref, cbq_ref, t)
    k_s[...] = _conv_silu(k_ref[...].astype(f32), cwk_ref, cbk_ref, t) * (M_DIM ** -0.5)

    lane = lax.broadcasted_iota(jnp.int32, (1, LANES), 1)
    g = g_ref[...]
    cols = [jnp.sum(g * sel_ref[j:j + 1, :], axis=-1, keepdims=True) for j in range(4)]
    cols[1] = _log_sigmoid(cols[1])
    cols[3] = _log_sigmoid(cols[3])
    compact = jnp.zeros((t, LANES), f32)
    for j in range(4):
        compact = jnp.where(lane == j, cols[j], compact)
    gs_s[...] = compact

    rr = lax.broadcasted_iota(jnp.int32, (lc, lc), 0)
    cc = lax.broadcasted_iota(jnp.int32, (lc, lc), 1)
    tri_incl = jnp.where(rr >= cc, 1.0, 0.0).astype(bf16)

    def cumsum_body(c, _):
        off = pl.multiple_of(c * lc, lc)
        tile = gs_s[pl.ds(off, lc), :]
        lf = jnp.where(lane == 4, tile[:, 1:2], jnp.where(lane == 5, tile[:, 3:4], 0.0))
        hi, mid, lo = _split3(lf)
        cs = _dot(tri_incl, hi) + _dot(tri_incl, mid) + _dot(tri_incl, lo)
        gs_s[pl.ds(off, lc), :] = jnp.where(lane >= 4, cs, tile)
        return 0

    lax.fori_loop(0, nc, cumsum_body, 0)

    ng = ng_ref[...]

    for d in range(2):
        keep = (rr >= cc) if d == 0 else (cc >= rr)

        def body(i, carry, d=d, keep=keep):
            cm, nm, mm = carry
            c = i if d == 0 else nc - 1 - i
            off = pl.multiple_of(c * lc, lc)
            qc = q_s[pl.ds(off, lc), :]
            kc = k_s[pl.ds(off, lc), :]
            qb = qc.astype(bf16)
            kb = kc.astype(bf16)
            vb = v_ref[pl.ds(off, lc), :]
            tile = gs_s[pl.ds(off, lc), :]
            tile_t = tile.T
            li_col, lf_col, a_col = tile[:, 2 * d:2 * d + 1], tile[:, 2 * d + 1:2 * d + 2], tile[:, 4 + d:5 + d]
            li_row, lf_row, a_row = tile_t[2 * d:2 * d + 1, :], tile_t[2 * d + 1:2 * d + 2, :], tile_t[4 + d:5 + d, :]
            total = a_col[lc - 1:lc, :]
            if d == 0:
                b_col, b_row = a_col, a_row
            else:
                b_col, b_row = total - a_col + lf_col, total - a_row + lf_row
            dmat = jnp.where(keep, b_col - b_row + li_row, -jnp.inf)
            inter = mm + b_col
            mt = jnp.maximum(inter, jnp.max(dmat, axis=-1, keepdims=True))
            w_inter = jnp.exp(inter - mt)
            s = _dot_nt(qb, kb) * jnp.exp(dmat - mt)
            num = w_inter * _dot(qb, cm.astype(bf16)) + _dot(s.astype(bf16), vb)
            den = w_inter * jnp.sum(qc * nm, axis=-1, keepdims=True) + jnp.sum(s, axis=-1, keepdims=True)
            hc = num / jnp.maximum(jnp.abs(den), jnp.exp(-mt))
            gcol = total - b_col + li_col
            m_new = jnp.maximum(mm + total, jnp.max(gcol, axis=0, keepdims=True))
            decay = jnp.exp(mm + total - m_new)
            kw = kc * jnp.exp(gcol - m_new)
            c_new = decay * cm + _dot_tn(kw.astype(bf16), vb)
            n_new = decay * nm + jnp.sum(kw, axis=0, keepdims=True)
            if d == 0:
                hf_s[pl.ds(off, lc), :] = hc
            else:
                hsum = hf_s[pl.ds(off, lc), :] + hc
                mo = _sigmoid(o_ref[pl.ds(off, lc), :].astype(f32))
                hm_ref[pl.ds(off, lc), :] = (_rms(hsum) * ng * mo).astype(hm_ref.dtype)
            return c_new, n_new, m_new

        cm, nm, mm = lax.fori_loop(0, nc, body, (c0_ref[d], n0_ref[d], m0_ref[d]))
        c_out[d] = cm
        n_out[d] = nm
        m_out[d] = mm


def _mlstm(z, gates, sel, conv_w, conv_b, m_norm_g, c0, n0, m0, *, batch, t, row_block0):
    nh = M_HEADS
    grid = (batch, nh)
    seq = lambda col0: pl.BlockSpec((t, LANES), lambda b, h: (row_block0 + b, col0 + h))
    per_head_vec = lambda col0: pl.BlockSpec((1, LANES), lambda b, h: (0, col0 + h))
    state = lambda *tail: pl.BlockSpec((None, 2, None) + tail, lambda b, h: (b, 0, h) + (0,) * len(tail))
    return pl.pallas_call(
        functools.partial(_mlstm_kernel, t=t),
        out_shape=(
            jax.ShapeDtypeStruct((batch * t, M_W), bf16),
            jax.ShapeDtypeStruct((batch, 2, nh, M_DIM, M_DIM), f32),
            jax.ShapeDtypeStruct((batch, 2, nh, 1, M_DIM), f32),
            jax.ShapeDtypeStruct((batch, 2, nh, 1, 1), f32),
        ),
        grid=grid,
        in_specs=[
            seq(0), seq(nh), seq(2 * nh), seq(3 * nh),
            pl.BlockSpec((t, LANES), lambda b, h: (row_block0 + b, 0)),
            pl.BlockSpec((None, 4, LANES), lambda b, h: (h, 0, 0)),
            pl.BlockSpec((3, LANES), lambda b, h: (0, h)),
            pl.BlockSpec((3, LANES), lambda b, h: (0, nh + h)),
            per_head_vec(0), per_head_vec(nh),
            pl.BlockSpec((None, 1, LANES), lambda b, h: (h, 0, 0)),
            state(M_DIM, M_DIM), state(1, M_DIM), state(1, 1),
        ],
        out_specs=(
            pl.BlockSpec((t, LANES), lambda b, h: (b, h)),
            state(M_DIM, M_DIM), state(1, M_DIM), state(1, 1),
        ),
        scratch_shapes=[pltpu.VMEM((t, LANES), f32)] * 4,
        compiler_params=_cparams("arbitrary", "arbitrary"),
        name=f"mlstm_t{t}",
    )(z, z, z, z, gates, sel, conv_w, conv_w, conv_b, conv_b, m_norm_g, c0, n0, m0)


def _pair_norm(x, gain):
    lane = lax.broadcasted_iota(jnp.int32, (1, LANES), 1)
    first = lane < A_DIM
    sq = x * x
    s_all = jnp.sum(sq, axis=-1, keepdims=True)
    s0 = jnp.sum(jnp.where(first, sq, 0.0), axis=-1, keepdims=True)
    inv0 = lax.rsqrt(s0 * (1.0 / A_DIM) + EPS)
    inv1 = lax.rsqrt((s_all - s0) * (1.0 / A_DIM) + EPS)
    return x * jnp.where(first, inv0, inv1) * gain


def _rope(x, cos, sin_signed):
    lane = lax.broadcasted_iota(jnp.int32, (1, LANES), 1)
    nf = A_DIM // 4
    partner = jnp.where((lane % (2 * nf)) < nf, pltpu.roll(x, LANES - nf, 1), pltpu.roll(x, nf, 1))
    return x * cos + partner * sin_signed


def _attn_kernel(*refs, t, tq, rope, ctx, out_scale):
    it = iter(refs)
    lam_ref = next(it)
    q_ref, k_ref, v_ref = next(it), next(it), next(it)
    qg_ref, kg_ref, ag_ref = next(it), next(it), next(it)
    if rope:
        cosq_ref, sinq_ref, cosk_ref, sink_ref = next(it), next(it), next(it), next(it)
    if ctx:
        kc_ref, vc_ref = next(it), next(it)
    ha_ref = next(it)
    if not ctx:
        newk_ref, newv_ref = next(it), next(it)
    kn_s = next(it)

    @pl.when(pl.program_id(2) == 0)
    def _():
        kn = _pair_norm(k_ref[...].astype(f32), kg_ref[...])
        if not ctx:
            newk_ref[0] = kn[:, :A_DIM]
            newk_ref[1] = kn[:, A_DIM:]
            newv_ref[...] = v_ref[...].astype(f32)
        if rope:
            kn = _rope(kn, cosk_ref[...], sink_ref[...])
        kn_s[...] = kn.astype(bf16)

    lane = lax.broadcasted_iota(jnp.int32, (1, LANES), 1)
    q = _pair_norm(q_ref[...].astype(f32), qg_ref[...])
    if rope:
        q = _rope(q, cosq_ref[...], sinq_ref[...])
    q = q * (A_DIM ** -0.5)
    qs = [jnp.where(lane < A_DIM, q, 0.0).astype(bf16), jnp.where(lane >= A_DIM, q, 0.0).astype(bf16)]
    kn = kn_s[...]
    if ctx:
        kcb = kc_ref[...].astype(bf16)
    probs = []
    for i in range(2):
        sn = _dot_nt(qs[i], kn)
        mx = jnp.max(sn, axis=-1, keepdims=True)
        if ctx:
            sc = _dot_nt(qs[i], kcb)
            mx = jnp.maximum(mx, jnp.max(sc, axis=-1, keepdims=True))
        pn = jnp.exp(sn - mx)
        den = jnp.sum(pn, axis=-1, keepdims=True)
        if ctx:
            pc = jnp.exp(sc - mx)
            den = den + jnp.sum(pc, axis=-1, keepdims=True)
        inv = 1.0 / den
        probs.append((pn * inv, pc * inv if ctx else None))
    lam = lam_ref[0]
    o = _dot((probs[0][0] - lam * probs[1][0]).astype(bf16), v_ref[...])
    if ctx:
        o = o + _dot((probs[0][1] - lam * probs[1][1]).astype(bf16), vc_ref[...].astype(bf16))
    ha_ref[...] = (_rms(o) * ag_ref[...] * out_scale).astype(ha_ref.dtype)


def _attn(z, lam, qg2, kg2, a_norm_g, *, batch, t, row_block0, rope_tabs=None, ctx_kv=None, out_scale):
    nh = A_HEADS
    tq = min(t, 256)
    nq = t // tq
    rope = rope_tabs is not None
    ctx = ctx_kv is not None
    qcol, kcol, vcol = 4 * M_HEADS, 4 * M_HEADS + nh, 4 * M_HEADS + 2 * nh
    vec = pl.BlockSpec((1, LANES), lambda b, h, i: (0, 0))
    in_specs = [
        pl.BlockSpec(memory_space=pltpu.SMEM),
        pl.BlockSpec((tq, LANES), lambda b, h, i: ((row_block0 + b) * nq + i, qcol + h)),
        pl.BlockSpec((t, LANES), lambda b, h, i: (row_block0 + b, kcol + h)),
        pl.BlockSpec((t, LANES), lambda b, h, i: (row_block0 + b, vcol + h)),
        vec, vec, vec,
    ]
    args = [lam, z, z, z, qg2, kg2, a_norm_g]
    if rope:
        cos, sin = rope_tabs
        in_specs += [pl.BlockSpec((tq, LANES), lambda b, h, i: (i, 0))] * 2
        in_specs += [pl.BlockSpec((t, LANES), lambda b, h, i: (0, 0))] * 2
        args += [cos, sin, cos, sin]
    if ctx:
        kc, vc = ctx_kv
        p = kc.shape[2]
        in_specs += [pl.BlockSpec((None, None, p, LANES), lambda b, h, i: (b, h, 0, 0))] * 2
        args += [kc, vc]
    out_shape = [jax.ShapeDtypeStruct((batch * t, A_W), bf16)]
    out_specs = [pl.BlockSpec((tq, LANES), lambda b, h, i: (b * nq + i, h))]
    if not ctx:
        out_shape += [jax.ShapeDtypeStruct((batch, nh, 2, t, A_DIM), f32),
                      jax.ShapeDtypeStruct((batch, nh, t, A_VDIM), f32)]
        out_specs += [pl.BlockSpec((None, None, 2, t, A_DIM), lambda b, h, i: (b, h, 0, 0, 0)),
                      pl.BlockSpec((None, None, t, A_VDIM), lambda b, h, i: (b, h, 0, 0))]
    return pl.pallas_call(
        functools.partial(_attn_kernel, t=t, tq=tq, rope=rope, ctx=ctx, out_scale=out_scale),
        out_shape=tuple(out_shape),
        grid=(batch, nh, nq),
        in_specs=in_specs,
        out_specs=tuple(out_specs),
        scratch_shapes=[pltpu.VMEM((t, LANES), bf16)],
        compiler_params=_cparams("arbitrary", "arbitrary", "arbitrary"),
        name=f"diff_attn_t{t}",
    )(*args)


def _rope_tables(t):
    rows = t // GRID_W
    pos_row = np.repeat(np.arange(rows, dtype=np.float32), GRID_W)
    pos_col = (np.arange(rows * GRID_W) % GRID_W).astype(np.float32)
    nf = A_DIM // 4
    inv = (ROPE_THETA ** (-jnp.arange(nf, dtype=f32) / nf))
    lane = np.arange(LANES)
    j = lane % (2 * nf)
    use_col = (lane % A_DIM) >= (A_DIM // 2)
    pos = jnp.where(jnp.asarray(use_col)[None, :], jnp.asarray(pos_col)[:, None], jnp.asarray(pos_row)[:, None])
    ang = pos * inv[jnp.asarray(j % nf)][None, :]
    sign = jnp.asarray(np.where(j < nf, -1.0, 1.0).astype(np.float32))[None, :]
    return jnp.cos(ang), jnp.sin(ang) * sign


def _outproj_kernel(x_ref, hm_ref, ha_ref, mod_ref, w_ref, o_ref):
    half = hm_ref.shape[1]
    y = _dot(hm_ref[...], w_ref[:half, :]) + _dot(ha_ref[...], w_ref[half:, :])
    o_ref[...] = x_ref[...] + mod_ref[2:3, :] * y


def _outproj(x, hm, ha, mod, w, n_prompt_rows, dec_seq):
    r, d = x.shape
    tm = ROW_TILE
    grp = functools.partial(_group_of_tile, tm=tm, n_prompt_rows=n_prompt_rows, dec_seq=dec_seq)
    return pl.pallas_call(
        _outproj_kernel,
        out_shape=jax.ShapeDtypeStruct((r, d), f32),
        grid=(r // tm,),
        in_specs=[
            pl.BlockSpec((tm, d), lambda i: (i, 0)),
            pl.BlockSpec((tm, hm.shape[1]), lambda i: (i, 0)),
            pl.BlockSpec((tm, ha.shape[1]), lambda i: (i, 0)),
            pl.BlockSpec((None, 6, d), lambda i: (grp(i), 0, 0)),
            _const_spec(w.shape),
        ],
        out_specs=pl.BlockSpec((tm, d), lambda i: (i, 0)),
        compiler_params=_cparams("arbitrary"),
        name="outproj_ab",
    )(x, hm, ha, mod, w)


def _ffn_kernel(x_ref, mod_ref, w1_ref, w3_ref, w2_ref, o_ref, *, chunk):
    x = x_ref[...]
    h = _modulate(x, mod_ref, 3).astype(bf16)
    ff = w1_ref.shape[1]
    acc = jnp.zeros(x.shape, f32)
    for j in range(ff // chunk):
        sl = slice(j * chunk, (j + 1) * chunk)
        a = _silu(_dot(h, w1_ref[:, sl])) * _dot(h, w3_ref[:, sl])
        acc = acc + _dot(a.astype(bf16), w2_ref[sl, :])
    o_ref[...] = x + mod_ref[5:6, :] * acc


def _ffn(x, mod, w1, w3, w2, n_prompt_rows, dec_seq):
    r, d = x.shape
    tm = ROW_TILE
    grp = functools.partial(_group_of_tile, tm=tm, n_prompt_rows=n_prompt_rows, dec_seq=dec_seq)
    ff = w1.shape[1]
    chunk = ff // 2 if (ff // 2) % LANES == 0 else ff
    return pl.pallas_call(
        functools.partial(_ffn_kernel, chunk=chunk),
        out_shape=jax.ShapeDtypeStruct((r, d), f32),
        grid=(r // tm,),
        in_specs=[
            pl.BlockSpec((tm, d), lambda i: (i, 0)),
            pl.BlockSpec((None, 6, d), lambda i: (grp(i), 0, 0)),
            _const_spec(w1.shape), _const_spec(w3.shape), _const_spec(w2.shape),
        ],
        out_specs=pl.BlockSpec((tm, d), lambda i: (i, 0)),
        compiler_params=_cparams("arbitrary"),
        name="ffn_dense",
    )(x, mod, w1, w3, w2)


def _gelu_tanh(x):
    return 0.5 * x * (1.0 + jnp.tanh(math.sqrt(2.0 / math.pi) * (x + 0.044715 * (x * x * x))))


def _gmlp_kernel(x_ref, mod_ref, win_ref, bin_ref, lng_ref, lnb_ref, ws_ref, bs_ref, wout_ref, o_ref, us_s):
    x = x_ref[...]
    tm = x.shape[0]
    e = wout_ref.shape[0]
    ge = e // C_GROUPS
    h = _modulate(x, mod_ref, 0).astype(bf16)
    u = _gelu_tanh(_dot(h, win_ref[:, :e]) + bin_ref[:, :e])
    v = _gelu_tanh(_dot(h, win_ref[:, e:]) + bin_ref[:, e:])
    mu = jnp.mean(v, axis=-1, keepdims=True)
    vc = v - mu
    var = jnp.mean(vc * vc, axis=-1, keepdims=True)
    vn = (vc * lax.rsqrt(var + EPS) * lng_ref[...] + lnb_ref[...]).astype(bf16)
    for n in range(tm // C_CHUNK):
        rows = slice(n * C_CHUNK, (n + 1) * C_CHUNK)
        for g in range(C_GROUPS):
            cols = slice(g * ge, (g + 1) * ge)
            s = _dot(ws_ref[g], vn[rows, cols]) + bs_ref[:, cols]
            us_s[rows, cols] = (u[rows, cols] * s).astype(bf16)
    o_ref[...] = x + mod_ref[2:3, :] * _dot(us_s[...], wout_ref[...])


def _gmlp(x, mod, w_in, b_in, ln_g, ln_b, ws, bs_full, w_out, n_prompt_rows, dec_seq):
    r, d = x.shape
    tm = ROW_TILE
    e = w_out.shape[0]
    grp = functools.partial(_group_of_tile, tm=tm, n_prompt_rows=n_prompt_rows, dec_seq=dec_seq)
    return pl.pallas_call(
        _gmlp_kernel,
        out_shape=jax.ShapeDtypeStruct((r, d), f32),
        grid=(r // tm,),
        in_specs=[
            pl.BlockSpec((tm, d), lambda i: (i, 0)),
            pl.BlockSpec((None, 6, d), lambda i: (grp(i), 0, 0)),
            _const_spec(w_in.shape), _const_spec(b_in.shape), _const_spec(ln_g.shape), _const_spec(ln_b.shape),
            _const_spec(ws.shape), _const_spec(bs_full.shape), _const_spec(w_out.shape),
        ],
        out_specs=pl.BlockSpec((tm, d), lambda i: (i, 0)),
        scratch_shapes=[pltpu.VMEM((tm, e), bf16)],
        compiler_params=_cparams("arbitrary"),
        name="gmlp",
    )(x, mod, w_in, b_in, ln_g, ln_b, ws, bs_full, w_out)


def _router_kernel(x_ref, mod_ref, wr_hi_ref, wr_lo_ref, h_ref, meta_ref, counts_ref):
    hf = _modulate(x_ref[...], mod_ref, 3)
    hb = hf.astype(bf16)
    h_ref[...] = hb
    h_lo = (hf - hb.astype(f32)).astype(bf16)
    logits = _dot(hb, wr_hi_ref[...]) + (_dot(hb, wr_lo_ref[...]) + _dot(h_lo, wr_hi_ref[...]))
    lane = lax.broadcasted_iota(jnp.int32, logits.shape, 1).astype(f32)
    logits = jnp.where(lane < N_EXPERTS, logits, -jnp.inf)
    m1 = jnp.max(logits, axis=-1, keepdims=True)
    i1 = jnp.min(jnp.where(logits == m1, lane, float(LANES)), axis=-1, keepdims=True)
    rest = jnp.where(lane == i1, -jnp.inf, logits)
    m2 = jnp.max(rest, axis=-1, keepdims=True)
    i2 = jnp.min(jnp.where(rest == m2, lane, float(LANES)), axis=-1, keepdims=True)
    e2 = jnp.exp(m2 - m1)
    w1 = 1.0 / (1.0 + e2)
    w2 = e2 * w1

    tm = logits.shape[0]
    cnt = jnp.where(lane == i1, 1.0, jnp.where(lane == i2, 1.0, 0.0))
    rr = lax.broadcasted_iota(jnp.int32, (tm, tm), 0)
    cc = lax.broadcasted_iota(jnp.int32, (tm, tm), 1)
    before = jnp.where(rr > cc, 1.0, 0.0).astype(bf16)
    rank = _dot(before, cnt.astype(bf16))
    counts = jnp.sum(cnt, axis=0, keepdims=True)
    padded = jnp.floor((counts + (PIECE_ALIGN - 1)) * (1.0 / PIECE_ALIGN)) * PIECE_ALIGN
    lane1 = lane[0:1, :]
    piece_off = jnp.zeros((1, LANES), f32)
    off = jnp.zeros((1, 1), f32)
    for e in range(N_EXPERTS):
        piece_off = jnp.where(lane1 == e, off, piece_off)
        off = off + padded[:, e:e + 1]
    local = piece_off + rank
    pos1 = jnp.sum(jnp.where(lane == i1, local, 0.0), axis=-1, keepdims=True)
    pos2 = jnp.sum(jnp.where(lane == i2, local, 0.0), axis=-1, keepdims=True)
    meta_ref[...] = jnp.where(lane == 0, pos1, jnp.where(lane == 1, pos2, jnp.where(lane == 2, w1,
                              jnp.where(lane == 3, w2, 0.0))))
    counts_ref[...] = counts


def _router(x, mod, wr_hi, wr_lo, n_prompt_rows, dec_seq):
    r, d = x.shape
    tm = MOE_SORT_BLOCK
    grp = functools.partial(_group_of_tile, tm=tm, n_prompt_rows=n_prompt_rows, dec_seq=dec_seq)
    return pl.pallas_call(
        _router_kernel,
        out_shape=(jax.ShapeDtypeStruct((r, d), bf16), jax.ShapeDtypeStruct((r, LANES), f32),
                   jax.ShapeDtypeStruct((r // tm, 1, LANES), f32)),
        grid=(r // tm,),
        in_specs=[
            pl.BlockSpec((tm, d), lambda i: (i, 0)),
            pl.BlockSpec((None, 6, d), lambda i: (grp(i), 0, 0)),
            _const_spec(wr_hi.shape), _const_spec(wr_lo.shape),
        ],
        out_specs=(pl.BlockSpec((tm, d), lambda i: (i, 0)), pl.BlockSpec((tm, LANES), lambda i: (i, 0)),
                   pl.BlockSpec((None, 1, LANES), lambda i: (i, 0, 0))),
        compiler_params=_cparams("arbitrary"),
        name="router",
    )(x, mod, wr_hi, wr_lo)


def _moe_kernel(n16_ref, loc_ref, dst_ref, seg_off_ref, seg_len_ref,
                h_ref, meta_ref, x_ref, mod_ref, w1_ref, w3_ref, w2_ref, o_ref,
                hs_s, ys_s, loc_s, *, nsub, ne, nf, ch):
    sb = pl.program_id(0)
    p = pl.program_id(1)
    n_exp = ne * nf
    loc_rows = loc_s.shape[0]
    pa = PIECE_ALIGN

    def one_hot_cols(meta, v1, v2):
        lane = lax.broadcasted_iota(jnp.int32, (1, loc_rows), 1).astype(f32)
        return jnp.where(lane == meta[:, 0:1], v1, jnp.where(lane == meta[:, 1:2], v2, 0.0)).astype(bf16)

    def copy_pieces(blk, to_sorted):
        for e in range(ne):
            n = n16_ref[blk * ne + e]
            src = loc_ref[blk * ne + e]
            dst = dst_ref[blk * ne + e]

            def cp(i, _, src=src, dst=dst):
                a = pl.ds(pl.multiple_of(src + pa * i, pa), pa)
                b = pl.ds(pl.multiple_of(dst + pa * i, pa), pa)
                if to_sorted:
                    hs_s[b, :] = loc_s[a, :]
                else:
                    loc_s[a, :] = ys_s[b, :].astype(bf16)
                return 0

            lax.fori_loop(0, n, cp, 0)

    @pl.when(p < nsub)
    def _():
        @pl.when(p == 0)
        def _():
            hs_s[...] = jnp.zeros_like(hs_s)
            ys_s[...] = jnp.zeros_like(ys_s)

        pt = one_hot_cols(meta_ref[...], 1.0, 1.0)
        loc_s[...] = _dot_tn(pt, h_ref[...]).astype(bf16)
        copy_pieces(sb * nsub + p, True)

    @pl.when((p >= nsub) & (p < nsub + n_exp))
    def _():
        e = lax.div(p - nsub, jnp.int32(nf))
        start = seg_off_ref[sb * ne + e]
        ln = seg_len_ref[sb * ne + e]
        half = ch // 2
        nfull = lax.div(ln, jnp.int32(ch))
        rem = ln - nfull * ch

        def chunk(r0, valid, size):
            rows = pl.ds(pl.multiple_of(r0, pa), size)
            xc = hs_s[rows, :]
            a = _silu(_dot(xc, w1_ref[...])) * _dot(xc, w3_ref[...])
            y = _dot(a.astype(bf16), w2_ref[...])
            ri = lax.broadcasted_iota(jnp.int32, (size, 1), 0)
            ys_s[rows, :] += jnp.where(ri < valid, y, 0.0)

        def body(j, _):
            chunk(start + j * ch, ln - j * ch, ch)
            return 0

        lax.fori_loop(0, nfull + jnp.where(rem > half, 1, 0), body, 0)

        @pl.when((rem > 0) & (rem <= half))
        def _():
            chunk(start + nfull * ch, rem, half)

    @pl.when(p >= nsub + n_exp)
    def _():
        copy_pieces(sb * nsub + (p - nsub - n_exp), False)
        meta = meta_ref[...]
        a = one_hot_cols(meta, meta[:, 2:3], meta[:, 3:4])
        o_ref[...] = x_ref[...] + mod_ref[5:6, :] * _dot(a, loc_s[...])


def _moe(x, h, meta, counts, mod, w1, w3, w2, n_prompt_rows, dec_seq):
    r, d = x.shape
    ne, _, ff = w1.shape
    t_super, tb, ch, tf, pa = MOE_SUPER_BLOCK, MOE_SORT_BLOCK, MOE_CHUNK, MOE_FF_TILE, PIECE_ALIGN
    nsub = t_super // tb
    nsb = r // t_super
    nf = ff // tf
    n_exp = ne * nf
    loc_rows = 2 * tb + LANES
    assert loc_rows >= 2 * tb + ne * (pa - 1)
    max_rows = 2 * t_super + nsub * ne * (pa - 1) + ch
    sort_rows = LANES * (-(-max_rows // LANES))

    cnt = counts[:, 0, :ne].astype(jnp.int32)
    n16 = (cnt + (pa - 1)) // pa
    loc = pa * (jnp.cumsum(n16, axis=1) - n16)
    n16_sb = n16.reshape(nsb, nsub, ne)
    seg_len = pa * jnp.sum(n16_sb, axis=1)
    seg_off = jnp.cumsum(seg_len, axis=1) - seg_len
    dst = seg_off[:, None, :] + pa * (jnp.cumsum(n16_sb, axis=1) - n16_sb)
    scalars = [a.reshape(-1).astype(jnp.int32) for a in (n16, loc, dst, seg_off, seg_len)]

    grp = functools.partial(_group_of_tile, tm=t_super, n_prompt_rows=n_prompt_rows, dec_seq=dec_seq)

    def exp_step(p):
        return jnp.clip(p - nsub, 0, n_exp - 1)

    def tok_blk(sb, s):
        return sb * nsub + jnp.clip(s, 0, nsub - 1)

    grid_spec = pltpu.PrefetchScalarGridSpec(
        num_scalar_prefetch=len(scalars),
        grid=(nsb, nsub + n_exp + nsub),
        in_specs=[
            pl.BlockSpec((tb, d), lambda sb, p, *_: (tok_blk(sb, p), 0)),
            pl.BlockSpec((tb, LANES), lambda sb, p, *_: (sb * nsub + jnp.where(p < nsub, p, jnp.clip(p - nsub - n_exp, 0, nsub - 1)), 0)),
            pl.BlockSpec((tb, d), lambda sb, p, *_: (tok_blk(sb, p - nsub - n_exp), 0)),
            pl.BlockSpec((None, 6, d), lambda sb, p, *_: (grp(sb), 0, 0)),
            pl.BlockSpec((None, d, tf), lambda sb, p, *_: (exp_step(p) // nf, 0, exp_step(p) % nf)),
            pl.BlockSpec((None, d, tf), lambda sb, p, *_: (exp_step(p) // nf, 0, exp_step(p) % nf)),
            pl.BlockSpec((None, tf, d), lambda sb, p, *_: (exp_step(p) // nf, exp_step(p) % nf, 0)),
        ],
        out_specs=pl.BlockSpec((tb, d), lambda sb, p, *_: (tok_blk(sb, p - nsub - n_exp), 0)),
        scratch_shapes=[
            pltpu.VMEM((sort_rows, d), bf16),
            pltpu.VMEM((sort_rows, d), f32),
            pltpu.VMEM((loc_rows, d), bf16),
        ],
    )
    return pl.pallas_call(
        functools.partial(_moe_kernel, nsub=nsub, ne=ne, nf=nf, ch=ch),
        out_shape=jax.ShapeDtypeStruct((r, d), f32),
        grid_spec=grid_spec,
        compiler_params=_cparams("arbitrary", "arbitrary"),
        name="moe_sparse",
    )(*scalars, h, meta, x, mod, w1, w3, w2)


def kernel(x_prompt, x_sample, c, cache_dattn_k, cache_dattn_v, state_mlstm_c, state_mlstm_n, state_mlstm_m,
           c_ctx, w_ada, b_ada, w_in_ab, conv_w, conv_b, gate_b, qn_g, kn_g, lam_q1, lam_k1, lam_q2, lam_k2,
           m_norm_g, a_norm_g, w_out_ab, ff_w1, ff_w3, ff_w2, w_in_c, b_in_c, c_ln_g, c_ln_b, c_ws, c_bs,
           w_out_c, w_router, ex_w1, ex_w3, ex_w2):
    bp, seq, d = x_prompt.shape
    bs, dec_seq, _ = x_sample.shape
    depth = w_ada.shape[0]
    n_prompt_rows = bp * seq
    assert n_prompt_rows % dec_seq == 0 and seq % M_CHUNK == 0 and dec_seq % MOE_SUPER_BLOCK == 0
    nh = M_HEADS

    mods = _ada_table(jnp.concatenate([c_ctx[None], c], axis=0), w_ada, b_ada)
    x = jnp.concatenate([x_prompt.reshape(n_prompt_rows, d), x_sample.reshape(bs * dec_seq, d)], axis=0)
    rows = (n_prompt_rows, dec_seq)

    new_k, new_v, new_c, new_n, new_m = [], [], [], [], []
    for l in range(depth):
        j = l // 2
        mod = mods[l]
        if l % 2 == 0:
            lam_init = 0.8 - 0.6 * math.exp(-0.3 * l)
            lam = (jnp.exp(jnp.sum((lam_q1[j] * lam_k1[j]).astype(f32)))
                   - jnp.exp(jnp.sum((lam_q2[j] * lam_k2[j]).astype(f32))) + lam_init).reshape(1)
            o3 = 4 * M_W
            o4 = o3 + 4 * nh
            w = w_in_ab[j]
            w_main = jnp.concatenate([w[:, :o3], w[:, o4:]], axis=1).astype(bf16)
            w_gate = jnp.zeros((d, LANES), f32).at[:, :4 * nh].set(w[:, o3:o4]).astype(bf16)
            z, gates = _inproj(x, mod, w_main, w_gate, *rows)
            gates = gates + jnp.zeros((1, LANES), f32).at[0, :4 * nh].set(gate_b[j])
            sel = np.zeros((nh, 4, LANES), np.float32)
            for h in range(nh):
                for gi in range(4):
                    sel[h, gi, gi * nh + h] = 1.0
            sel = jnp.asarray(sel)
            mng = m_norm_g[j].reshape(nh, 1, M_DIM)
            cb = conv_b[j].reshape(1, 2 * M_W)
            zc = jnp.zeros((bp, 2, nh, M_DIM, M_DIM), f32)
            zn = jnp.zeros((bp, 2, nh, 1, M_DIM), f32)
            zm = jnp.zeros((bp, 2, nh, 1, 1), f32)
            hm_p, c_f, n_f, m_f = _mlstm(z, gates, sel, conv_w[j], cb, mng, zc, zn, zm,
                                         batch=bp, t=seq, row_block0=0)
            hm_s, _, _, _ = _mlstm(z, gates, sel, conv_w[j], cb, mng, state_mlstm_c[:, j],
                                   state_mlstm_n[:, j].reshape(bs, 2, nh, 1, M_DIM),
                                   state_mlstm_m[:, j].reshape(bs, 2, nh, 1, 1),
                                   batch=bs, t=dec_seq, row_block0=n_prompt_rows // dec_seq)
            new_c.append(c_f)
            new_n.append(n_f.reshape(bp, 2, nh, M_DIM))
            new_m.append(m_f.reshape(bp, 2, nh))

            qg2 = jnp.tile(qn_g[j], 2).reshape(1, LANES)
            kg2 = jnp.tile(kn_g[j], 2).reshape(1, LANES)
            ag = a_norm_g[j].reshape(1, LANES)
            ha_p, k_ctx, v_ctx = _attn(z, lam, qg2, kg2, ag, batch=bp, t=seq, row_block0=0,
                                       out_scale=1.0 - lam_init)
            kc = cache_dattn_k[:, j].transpose(0, 1, 3, 2, 4).reshape(bs, A_HEADS, -1, LANES)
            (ha_s,) = _attn(z, lam, qg2, kg2, ag, batch=bs, t=dec_seq, row_block0=n_prompt_rows // dec_seq,
                            rope_tabs=_rope_tables(dec_seq), ctx_kv=(kc, cache_dattn_v[:, j]),
                            out_scale=1.0 - lam_init)
            new_k.append(k_ctx)
            new_v.append(v_ctx)
            hm = jnp.concatenate([hm_p, hm_s], axis=0)
            ha = jnp.concatenate([ha_p, ha_s], axis=0)
            x = _outproj(x, hm, ha, mod, w_out_ab[j].astype(bf16), *rows)
            x = _ffn(x, mod, ff_w1[j].astype(bf16), ff_w3[j].astype(bf16), ff_w2[j].astype(bf16), *rows)
        else:
            e = w_out_c.shape[1]
            bs_full = jnp.repeat(c_bs[j].T, e // C_GROUPS, axis=1)
            x = _gmlp(x, mod, w_in_c[j].astype(bf16), b_in_c[j].reshape(1, -1), c_ln_g[j].reshape(1, -1),
                      c_ln_b[j].reshape(1, -1), c_ws[j].astype(bf16), bs_full, w_out_c[j].astype(bf16), *rows)
            wr = jnp.zeros((d, LANES), f32).at[:, :N_EXPERTS].set(w_router[j])
            wr_hi = wr.astype(bf16)
            wr_lo = (wr - wr_hi.astype(f32)).astype(bf16)
            h, meta, counts = _router(x, mod, wr_hi, wr_lo, *rows)
            x = _moe(x, h, meta, counts, mod, ex_w1[j].astype(bf16), ex_w3[j].astype(bf16),
                     ex_w2[j].astype(bf16), *rows)

    y_prompt = x[:n_prompt_rows].reshape(bp, seq, d)
    y_sample = x[n_prompt_rows:].reshape(bs, dec_seq, d)
    return (y_prompt, y_sample, jnp.stack(new_k, axis=1), jnp.stack(new_v, axis=1),
            jnp.stack(new_c, axis=1), jnp.stack(new_n, axis=1), jnp.stack(new_m, axis=1))
```

```python
import functools
import math

import jax
import jax.numpy as jnp
import numpy as np
from jax import lax
from jax.experimental import pallas as pl
from jax.experimental.pallas import tpu as pltpu

f32 = jnp.float32
bf16 = jnp.bfloat16

D_MODEL = 1024
M_HEADS = 4
M_DIM = 128
M_W = M_HEADS * M_DIM
M_CHUNK = 128
A_HEADS = 4
A_VDIM = 128
A_DIM = 64
A_W = A_HEADS * A_VDIM
GRID_W = 64
ROPE_THETA = 10000.0
C_CHUNK = 128
C_GROUPS = 4
N_EXPERTS = 8
EPS = 1e-6

LANES = 128
ROW_TILE = 512
PIECE_ALIGN = 16
MOE_SUPER_BLOCK = 2048
MOE_SORT_BLOCK = 512
MOE_CHUNK = 256
MOE_FF_TILE = 896
VMEM_LIMIT = 60 * 1024 * 1024


def _cparams(*sem):
    return pltpu.CompilerParams(dimension_semantics=tuple(sem), vmem_limit_bytes=VMEM_LIMIT)


def _const_spec(shape):
    nd = len(shape)
    return pl.BlockSpec(shape, lambda *_: (0,) * nd, pipeline_mode=pl.Buffered(1))


def _sigmoid(x):
    return 1.0 / (1.0 + jnp.exp(-x))


def _silu(x):
    return x * _sigmoid(x)


def _log_sigmoid(x):
    return jnp.minimum(x, 0.0) - jnp.log(1.0 + jnp.exp(-jnp.abs(x)))


def _rms(x):
    return x * lax.rsqrt(jnp.mean(x * x, axis=-1, keepdims=True) + EPS)


def _modulate(x, mod_ref, first):
    shift = mod_ref[first:first + 1, :]
    scale = mod_ref[first + 1:first + 2, :]
    return _rms(x) * (1.0 + scale) + shift


def _dot(a, b):
    return jnp.dot(a, b, preferred_element_type=f32)


def _dot_nt(a, b):
    return lax.dot_general(a, b, (((1,), (1,)), ((), ())), preferred_element_type=f32)


def _dot_tn(a, b):
    return lax.dot_general(a, b, (((0,), (0,)), ((), ())), preferred_element_type=f32)


def _split3(x):
    hi = x.astype(bf16)
    r1 = x - hi.astype(f32)
    mid = r1.astype(bf16)
    lo = (r1 - mid.astype(f32)).astype(bf16)
    return hi, mid, lo


def _group_of_tile(i, tm, n_prompt_rows, dec_seq):
    pt = n_prompt_rows // tm
    return jnp.where(i < pt, 0, 1 + (i - pt) // (dec_seq // tm))


def _ada_kernel(cv_ref, w_ref, b_ref, o_ref):
    a = _silu(cv_ref[...]).astype(bf16)
    o_ref[...] = _dot(a, w_ref[...].astype(bf16)) + b_ref[...]


def _ada_table(cv, w_ada, b_ada):
    depth, d, n = w_ada.shape
    g = cv.shape[0]
    gp = 8 * ((g + 7) // 8)
    cvp = jnp.zeros((gp, d), f32).at[:g].set(cv)
    tn = 1536
    out = pl.pallas_call(
        _ada_kernel,
        out_shape=jax.ShapeDtypeStruct((depth, gp, n), f32),
        grid=(depth, n // tn),
        in_specs=[
            pl.BlockSpec((gp, d), lambda l, j: (0, 0)),
            pl.BlockSpec((None, d, tn), lambda l, j: (l, 0, j)),
            pl.BlockSpec((None, 1, tn), lambda l, j: (l, 0, j)),
        ],
        out_specs=pl.BlockSpec((None, gp, tn), lambda l, j: (l, 0, j)),
        compiler_params=_cparams("arbitrary", "arbitrary"),
        name="ada_table",
    )(cvp, w_ada, b_ada.reshape(depth, 1, n))
    return out[:, :g].reshape(depth, g, 6, d)


def _part_specs(tm, width, npt):
    first = pl.BlockSpec((tm, width), lambda i: (jnp.minimum(i, npt - 1), 0))
    second = pl.BlockSpec((tm, width), lambda i: (jnp.maximum(i - npt, 0), 0))
    return first, second


def _pick(a_ref, b_ref, npt):
    return jnp.where(pl.program_id(0) < npt, a_ref[...], b_ref[...])


def _inproj_kernel(xp_ref, xs_ref, mod_ref, w_ref, wg_ref, z_ref, g_ref, *, n_main, npt):
    h = _modulate(_pick(xp_ref, xs_ref, npt), mod_ref, 0).astype(bf16)
    step = 512
    for j in range(n_main // step):
        z_ref[:, j * step:(j + 1) * step] = _dot(h, w_ref[:, j * step:(j + 1) * step]).astype(bf16)
    g_ref[...] = _dot(h, wg_ref[...])


def _inproj(xp, xs, mod, w_main, w_gate, n_prompt_rows, dec_seq):
    d = xp.shape[1]
    r = xp.shape[0] + xs.shape[0]
    tm = ROW_TILE
    npt = n_prompt_rows // tm
    n_main = w_main.shape[1]
    grp = functools.partial(_group_of_tile, tm=tm, n_prompt_rows=n_prompt_rows, dec_seq=dec_seq)
    return pl.pallas_call(
        functools.partial(_inproj_kernel, n_main=n_main, npt=npt),
        out_shape=(jax.ShapeDtypeStruct((r, n_main), bf16), jax.ShapeDtypeStruct((r, LANES), f32)),
        grid=(r // tm,),
        in_specs=[
            *_part_specs(tm, d, npt),
            pl.BlockSpec((None, 6, d), lambda i: (grp(i), 0, 0)),
            _const_spec(w_main.shape),
            _const_spec(w_gate.shape),
        ],
        out_specs=(pl.BlockSpec((tm, n_main), lambda i: (i, 0)), pl.BlockSpec((tm, LANES), lambda i: (i, 0))),
        compiler_params=_cparams("arbitrary"),
        name="inproj_ab",
    )(xp, xs, mod, w_main, w_gate)


def _conv_silu(x, w_ref, b_ref, t):
    row = lax.broadcasted_iota(jnp.int32, (t, 1), 0)
    prev = jnp.where(row == 0, 0.0, pltpu.roll(x, 1, 0))
    nxt = jnp.where(row == t - 1, 0.0, pltpu.roll(x, t - 1, 0))
    y = b_ref[...] + prev * w_ref[0:1, :] + x * w_ref[1:2, :] + nxt * w_ref[2:3, :]
    return _silu(y)


def _mlstm_kernel(q_ref, k_ref, v_ref, o_ref, g_ref, cw_ref, cb_ref, ng_ref, *rest, t, has_state):
    if has_state:
        c0_ref, n0_ref, m0_ref, *rest = rest
    hm_ref, c_out, n_out, m_out, qt_s, k_s, gs_s, hf_s, hb_s = rest
    nc = t // M_CHUNK
    lc = M_CHUNK
    nh = M_HEADS
    w = M_W
    hf_s[...] = _conv_silu(q_ref[...].astype(f32), cw_ref[:, :w], cb_ref[:, :w], t)
    k_s[...] = _conv_silu(k_ref[...].astype(f32), cw_ref[:, w:], cb_ref[:, w:], t) * (M_DIM ** -0.5)
    if has_state:
        c_out[...] = c0_ref[...]
        n_out[...] = n0_ref[...]
        m_out[...] = m0_ref[...]
    else:
        c_out[...] = jnp.zeros_like(c_out)
        n_out[...] = jnp.zeros_like(n_out)
        m_out[...] = jnp.zeros_like(m_out)

    lane = lax.broadcasted_iota(jnp.int32, (1, LANES), 1)
    is_lf = ((lane >= nh) & (lane < 2 * nh)) | ((lane >= 3 * nh) & (lane < 4 * nh))
    is_bw_lf = (lane >= 3 * nh) & (lane < 4 * nh)
    g = g_ref[...]
    gs_s[...] = jnp.where(is_lf, _log_sigmoid(g), g)

    rr = lax.broadcasted_iota(jnp.int32, (lc, lc), 0)
    cc = lax.broadcasted_iota(jnp.int32, (lc, lc), 1)
    tri_incl = jnp.where(rr >= cc, 1.0, 0.0).astype(bf16)

    def prep_body(c, _):
        off = pl.multiple_of(c * lc, lc)
        tile = gs_s[pl.ds(off, lc), :]
        hi, mid, lo = _split3(jnp.where(is_lf, tile, 0.0))
        cs = _dot(tri_incl, hi) + _dot(tri_incl, mid) + _dot(tri_incl, lo)
        total = jnp.broadcast_to(cs[lc - 1:lc, :], (lc, LANES))
        b = jnp.where(is_bw_lf, total - cs + tile, cs)
        lmb = tile - pltpu.roll(b, LANES - nh, 1)
        gv = pltpu.roll(total, LANES - nh, 1) + lmb
        low = jnp.where(is_lf, b, lmb)
        high = pltpu.roll(jnp.where(is_lf, total, gv), 4 * nh, 1)
        gs_s[pl.ds(off, lc), :] = jnp.where(lane < 4 * nh, low, jnp.where(lane < 8 * nh, high, 0.0))
        qt_s[pl.ds(pl.multiple_of(c * w, w), w), :] = hf_s[pl.ds(off, lc), :].T
        return 0

    lax.fori_loop(0, nc, prep_body, 0)

    sub8 = lax.broadcasted_iota(jnp.int32, (8, LANES), 0)

    def chain(c, h, d, tile, tile_t):
        off = pl.multiple_of(c * lc, lc)
        col = slice(h * M_DIM, (h + 1) * M_DIM)
        ln = 2 * nh * d + h
        lmb_col = tile[:, ln:ln + 1]
        g_col = tile[:, ln + 4 * nh:ln + 4 * nh + 1]
        total = tile[0:1, ln + 5 * nh:ln + 5 * nh + 1]
        b_row = tile_t[ln + nh:ln + nh + 1, :]
        cm, nm, mm = c_out[d, h], n_out[d, h], m_out[d, h]
        qt = qt_s[pl.ds(pl.multiple_of(c * w + h * M_DIM, M_DIM), M_DIM), :]
        qtb = qt.astype(bf16)
        kc = k_s[pl.ds(off, lc), col]
        vb = v_ref[pl.ds(off, lc), col]
        keep = (cc >= rr) if d == 0 else (rr >= cc)
        dmat = jnp.where(keep, b_row + lmb_col, -jnp.inf)
        inter = mm + b_row
        mt = jnp.maximum(inter, jnp.max(dmat, axis=0, keepdims=True))
        w_inter = jnp.exp(inter - mt)
        s = _dot(kc.astype(bf16), qtb) * jnp.exp(dmat - mt)
        n_hi = nm.astype(bf16).astype(f32)
        n2 = jnp.where(sub8 == 0, n_hi, jnp.where(sub8 == 1, nm - n_hi, 0.0)).astype(bf16)
        qn = _dot(n2, qtb)
        den = w_inter * (qn[0:1, :] + qn[1:2, :]) + jnp.sum(s, axis=0, keepdims=True)
        inv = 1.0 / jnp.maximum(jnp.abs(den), jnp.exp(-mt))
        lhs = jnp.concatenate([qt * (w_inter * inv), s * inv], axis=0).astype(bf16)
        rhs = jnp.concatenate([cm.astype(bf16), vb], axis=0)
        hc = _dot_tn(lhs, rhs)
        m_new = jnp.maximum(mm + total, jnp.max(g_col, axis=0, keepdims=True))
        decay = jnp.exp(mm + total - m_new)
        kw = kc * jnp.exp(g_col - m_new)
        c_out[d, h] = decay * cm + _dot_tn(kw.astype(bf16), vb)
        n_out[d, h] = decay * nm + jnp.sum(kw, axis=0, keepdims=True)
        m_out[d, h] = m_new
        if d == 0:
            hf_s[pl.ds(off, lc), col] = hc
        else:
            hb_s[pl.ds(off, lc), col] = hc

    def body(i, _):
        for d in range(2):
            c = i if d == 0 else nc - 1 - i
            tile = gs_s[pl.ds(pl.multiple_of(c * lc, lc), lc), :]
            tile_t = tile.T
            for h in range(nh):
                chain(c, h, d, tile, tile_t)
        return 0

    lax.fori_loop(0, nc, body, 0)

    def out_body(c, _):
        rows = pl.ds(pl.multiple_of(c * lc, lc), lc)
        for h in range(nh):
            col = slice(h * M_DIM, (h + 1) * M_DIM)
            mo = _sigmoid(o_ref[rows, col].astype(f32))
            hm_ref[rows, col] = (_rms(hf_s[rows, col] + hb_s[rows, col]) * ng_ref[:, col] * mo).astype(hm_ref.dtype)
        return 0

    lax.fori_loop(0, nc, out_body, 0)


def _mlstm(z, gates, conv_w, conv_b, m_norm_g, state0, *, batch, t, row_block0):
    nh = M_HEADS
    w = M_W
    seq = lambda colblk: pl.BlockSpec((t, w), lambda b: (row_block0 + b, colblk))
    state = lambda *tail: pl.BlockSpec((None, 2, nh) + tail, lambda b: (b,) + (0,) * (2 + len(tail)))
    state_specs = [state(M_DIM, M_DIM), state(1, M_DIM), state(1, 1)]
    has_state = state0 is not None
    return pl.pallas_call(
        functools.partial(_mlstm_kernel, t=t, has_state=has_state),
        out_shape=(
            jax.ShapeDtypeStruct((batch * t, w), bf16),
            jax.ShapeDtypeStruct((batch, 2, nh, M_DIM, M_DIM), f32),
            jax.ShapeDtypeStruct((batch, 2, nh, 1, M_DIM), f32),
            jax.ShapeDtypeStruct((batch, 2, nh, 1, 1), f32),
        ),
        grid=(batch,),
        in_specs=[
            seq(0), seq(1), seq(2), seq(3),
            pl.BlockSpec((t, LANES), lambda b: (row_block0 + b, 0)),
            _const_spec(conv_w.shape), _const_spec(conv_b.shape), _const_spec(m_norm_g.shape),
            *(state_specs if has_state else []),
        ],
        out_specs=(pl.BlockSpec((t, w), lambda b: (b, 0)), *state_specs),
        scratch_shapes=[
            pltpu.VMEM((t * nh, M_DIM), f32),
            pltpu.VMEM((t, w), f32),
            pltpu.VMEM((t, LANES), f32),
            pltpu.VMEM((t, w), f32),
            pltpu.VMEM((t, w), f32),
        ],
        compiler_params=_cparams("arbitrary"),
        name=f"mlstm_t{t}",
    )(z, z, z, z, gates, conv_w, conv_b, m_norm_g, *(state0 if has_state else ()))


def _pair_norm(x, gain):
    lane = lax.broadcasted_iota(jnp.int32, (1, LANES), 1)
    first = lane < A_DIM
    sq = x * x
    s_all = jnp.sum(sq, axis=-1, keepdims=True)
    s0 = jnp.sum(jnp.where(first, sq, 0.0), axis=-1, keepdims=True)
    inv0 = lax.rsqrt(s0 * (1.0 / A_DIM) + EPS)
    inv1 = lax.rsqrt((s_all - s0) * (1.0 / A_DIM) + EPS)
    return x * jnp.where(first, inv0, inv1) * gain


def _rope(x, cos, sin_signed):
    lane = lax.broadcasted_iota(jnp.int32, (1, LANES), 1)
    nf = A_DIM // 4
    partner = jnp.where((lane % (2 * nf)) < nf, pltpu.roll(x, LANES - nf, 1), pltpu.roll(x, nf, 1))
    return x * cos + partner * sin_signed


def _attn_kernel(*refs, t, tq, rope, ctx, out_scale):
    it = iter(refs)
    lam_ref = next(it)
    q_ref, k_ref, v_ref = next(it), next(it), next(it)
    qg_ref, kg_ref, ag_ref = next(it), next(it), next(it)
    if rope:
        cosq_ref, sinq_ref, cosk_ref, sink_ref = next(it), next(it), next(it), next(it)
    if ctx:
        kct_ref, vc_ref = next(it), next(it)
    ha_ref = next(it)
    if not ctx:
        newk_ref, newv_ref = next(it), next(it)
    kt_s = next(it)
    nh = A_HEADS

    @pl.when(pl.program_id(1) == 0)
    def _():
        for h in range(nh):
            col = slice(h * LANES, (h + 1) * LANES)
            kn = _pair_norm(k_ref[:, col].astype(f32), kg_ref[...])
            if not ctx:
                newk_ref[h, 0] = kn[:, :A_DIM]
                newk_ref[h, 1] = kn[:, A_DIM:]
                newv_ref[h] = v_ref[:, col].astype(f32)
            if rope:
                kn = _rope(kn, cosk_ref[...], sink_ref[...])
            kt_s[col, :] = kn.T.astype(bf16)

    lane = lax.broadcasted_iota(jnp.int32, (1, LANES), 1)
    lam = lam_ref[0]
    for h in range(nh):
        col = slice(h * LANES, (h + 1) * LANES)
        q = _pair_norm(q_ref[:, col].astype(f32), qg_ref[...])
        if rope:
            q = _rope(q, cosq_ref[...], sinq_ref[...])
        q = q * (A_DIM ** -0.5)
        qs = [jnp.where(lane < A_DIM, q, 0.0).astype(bf16), jnp.where(lane >= A_DIM, q, 0.0).astype(bf16)]
        kt = kt_s[col, :]
        vb = v_ref[:, col]
        if ctx:
            kctb = kct_ref[h].astype(bf16)
            vcb = vc_ref[h].astype(bf16)
        outs = []
        for i in range(2):
            sn = _dot(qs[i], kt)
            mx = jnp.max(sn, axis=-1, keepdims=True)
            if ctx:
                sc = _dot(qs[i], kctb)
                mx = jnp.maximum(mx, jnp.max(sc, axis=-1, keepdims=True))
            en = jnp.exp(sn - mx)
            den = jnp.sum(en, axis=-1, keepdims=True)
            o = _dot(en.astype(bf16), vb)
            if ctx:
                ec = jnp.exp(sc - mx)
                den = den + jnp.sum(ec, axis=-1, keepdims=True)
                o = o + _dot(ec.astype(bf16), vcb)
            outs.append(o * (1.0 / den))
        o = outs[0] - lam * outs[1]
        ha_ref[:, col] = (_rms(o) * ag_ref[...] * out_scale).astype(ha_ref.dtype)


def _attn(z, lam, qg2, kg2, a_norm_g, *, batch, t, row_block0, rope_tabs=None, ctx_kv=None, out_scale):
    nh = A_HEADS
    w = A_W
    tq = min(t, 256)
    nq = t // tq
    rope = rope_tabs is not None
    ctx = ctx_kv is not None
    qblk = 4 * M_W // w
    vec = pl.BlockSpec((1, LANES), lambda b, i: (0, 0))
    in_specs = [
        pl.BlockSpec(memory_space=pltpu.SMEM),
        pl.BlockSpec((tq, w), lambda b, i: ((row_block0 + b) * nq + i, qblk)),
        pl.BlockSpec((t, w), lambda b, i: (row_block0 + b, qblk + 1)),
        pl.BlockSpec((t, w), lambda b, i: (row_block0 + b, qblk + 2)),
        vec, vec, vec,
    ]
    args = [lam, z, z, z, qg2, kg2, a_norm_g]
    if rope:
        cos, sin = rope_tabs
        in_specs += [pl.BlockSpec((tq, LANES), lambda b, i: (i, 0))] * 2
        in_specs += [pl.BlockSpec((t, LANES), lambda b, i: (0, 0))] * 2
        args += [cos, sin, cos, sin]
    if ctx:
        kct, vc = ctx_kv
        in_specs += [pl.BlockSpec((None,) + kct.shape[1:], lambda b, i: (b, 0, 0, 0)),
                     pl.BlockSpec((None,) + vc.shape[1:], lambda b, i: (b, 0, 0, 0))]
        args += [kct, vc]
    out_shape = [jax.ShapeDtypeStruct((batch * t, w), bf16)]
    out_specs = [pl.BlockSpec((tq, w), lambda b, i: (b * nq + i, 0))]
    if not ctx:
        out_shape += [jax.ShapeDtypeStruct((batch, nh, 2, t, A_DIM), f32),
                      jax.ShapeDtypeStruct((batch, nh, t, A_VDIM), f32)]
        out_specs += [pl.BlockSpec((None, nh, 2, t, A_DIM), lambda b, i: (b, 0, 0, 0, 0)),
                      pl.BlockSpec((None, nh, t, A_VDIM), lambda b, i: (b, 0, 0, 0))]
    return pl.pallas_call(
        functools.partial(_attn_kernel, t=t, tq=tq, rope=rope, ctx=ctx, out_scale=out_scale),
        out_shape=tuple(out_shape),
        grid=(batch, nq),
        in_specs=in_specs,
        out_specs=tuple(out_specs),
        scratch_shapes=[pltpu.VMEM((w, t), bf16)],
        compiler_params=_cparams("arbitrary", "arbitrary"),
        name=f"diff_attn_t{t}",
    )(*args)


def _rope_tables(t):
    rows = t // GRID_W
    pos_row = np.repeat(np.arange(rows, dtype=np.float32), GRID_W)
    pos_col = (np.arange(rows * GRID_W) % GRID_W).astype(np.float32)
    nf = A_DIM // 4
    inv = (ROPE_THETA ** (-jnp.arange(nf, dtype=f32) / nf))
    lane = np.arange(LANES)
    j = lane % (2 * nf)
    use_col = (lane % A_DIM) >= (A_DIM // 2)
    pos = jnp.where(jnp.asarray(use_col)[None, :], jnp.asarray(pos_col)[:, None], jnp.asarray(pos_row)[:, None])
    ang = pos * inv[jnp.asarray(j % nf)][None, :]
    sign = jnp.asarray(np.where(j < nf, -1.0, 1.0).astype(np.float32))[None, :]
    return jnp.cos(ang), jnp.sin(ang) * sign


def _outproj_kernel(xp_ref, xs_ref, hmp_ref, hms_ref, hap_ref, has_ref, mod_ref, w_ref, o_ref, *, npt):
    half = hmp_ref.shape[1]
    hm = _pick(hmp_ref, hms_ref, npt)
    ha = _pick(hap_ref, has_ref, npt)
    y = _dot(hm, w_ref[:half, :]) + _dot(ha, w_ref[half:, :])
    o_ref[...] = _pick(xp_ref, xs_ref, npt) + mod_ref[2:3, :] * y


def _outproj(xp, xs, hm_p, hm_s, ha_p, ha_s, mod, w, n_prompt_rows, dec_seq):
    d = xp.shape[1]
    r = xp.shape[0] + xs.shape[0]
    tm = ROW_TILE
    npt = n_prompt_rows // tm
    grp = functools.partial(_group_of_tile, tm=tm, n_prompt_rows=n_prompt_rows, dec_seq=dec_seq)
    return pl.pallas_call(
        functools.partial(_outproj_kernel, npt=npt),
        out_shape=jax.ShapeDtypeStruct((r, d), f32),
        grid=(r // tm,),
        in_specs=[
            *_part_specs(tm, d, npt),
            *_part_specs(tm, hm_p.shape[1], npt),
            *_part_specs(tm, ha_p.shape[1], npt),
            pl.BlockSpec((None, 6, d), lambda i: (grp(i), 0, 0)),
            _const_spec(w.shape),
        ],
        out_specs=pl.BlockSpec((tm, d), lambda i: (i, 0)),
        compiler_params=_cparams("arbitrary"),
        name="outproj_ab",
    )(xp, xs, hm_p, hm_s, ha_p, ha_s, mod, w)


def _ffn_kernel(x_ref, mod_ref, w1_ref, w3_ref, w2_ref, o_ref, *, chunk):
    x = x_ref[...]
    h = _modulate(x, mod_ref, 3).astype(bf16)
    ff = w1_ref.shape[1]
    acc = jnp.zeros(x.shape, f32)
    for j in range(ff // chunk):
        sl = slice(j * chunk, (j + 1) * chunk)
        a = _silu(_dot(h, w1_ref[:, sl])) * _dot(h, w3_ref[:, sl])
        acc = acc + _dot(a.astype(bf16), w2_ref[sl, :])
    o_ref[...] = x + mod_ref[5:6, :] * acc


def _ffn(x, mod, w1, w3, w2, n_prompt_rows, dec_seq):
    r, d = x.shape
    tm = ROW_TILE
    grp = functools.partial(_group_of_tile, tm=tm, n_prompt_rows=n_prompt_rows, dec_seq=dec_seq)
    ff = w1.shape[1]
    chunk = ff // 2 if (ff // 2) % LANES == 0 else ff
    return pl.pallas_call(
        functools.partial(_ffn_kernel, chunk=chunk),
        out_shape=jax.ShapeDtypeStruct((r, d), f32),
        grid=(r // tm,),
        in_specs=[
            pl.BlockSpec((tm, d), lambda i: (i, 0)),
            pl.BlockSpec((None, 6, d), lambda i: (grp(i), 0, 0)),
            _const_spec(w1.shape), _const_spec(w3.shape), _const_spec(w2.shape),
        ],
        out_specs=pl.BlockSpec((tm, d), lambda i: (i, 0)),
        compiler_params=_cparams("arbitrary"),
        name="ffn_dense",
    )(x, mod, w1, w3, w2)


def _gelu_tanh(x):
    return 0.5 * x * (1.0 + jnp.tanh(math.sqrt(2.0 / math.pi) * (x + 0.044715 * (x * x * x))))


def _gmlp_kernel(x_ref, mod_ref, win_ref, bin_ref, lng_ref, lnb_ref, ws_ref, bs_ref, wout_ref, o_ref, us_s):
    x = x_ref[...]
    tm = x.shape[0]
    e = wout_ref.shape[0]
    ge = e // C_GROUPS
    h = _modulate(x, mod_ref, 0).astype(bf16)
    u = _gelu_tanh(_dot(h, win_ref[:, :e]) + bin_ref[:, :e])
    v = _gelu_tanh(_dot(h, win_ref[:, e:]) + bin_ref[:, e:])
    mu = jnp.mean(v, axis=-1, keepdims=True)
    vc = v - mu
    var = jnp.mean(vc * vc, axis=-1, keepdims=True)
    vn = (vc * lax.rsqrt(var + EPS) * lng_ref[...] + lnb_ref[...]).astype(bf16)
    for n in range(tm // C_CHUNK):
        rows = slice(n * C_CHUNK, (n + 1) * C_CHUNK)
        for g in range(C_GROUPS):
            cols = slice(g * ge, (g + 1) * ge)
            s = _dot(ws_ref[g], vn[rows, cols]) + bs_ref[:, cols]
            us_s[rows, cols] = (u[rows, cols] * s).astype(bf16)
    o_ref[...] = x + mod_ref[2:3, :] * _dot(us_s[...], wout_ref[...])


def _gmlp(x, mod, w_in, b_in, ln_g, ln_b, ws, bs_full, w_out, n_prompt_rows, dec_seq):
    r, d = x.shape
    tm = ROW_TILE
    e = w_out.shape[0]
    grp = functools.partial(_group_of_tile, tm=tm, n_prompt_rows=n_prompt_rows, dec_seq=dec_seq)
    return pl.pallas_call(
        _gmlp_kernel,
        out_shape=jax.ShapeDtypeStruct((r, d), f32),
        grid=(r // tm,),
        in_specs=[
            pl.BlockSpec((tm, d), lambda i: (i, 0)),
            pl.BlockSpec((None, 6, d), lambda i: (grp(i), 0, 0)),
            _const_spec(w_in.shape), _const_spec(b_in.shape), _const_spec(ln_g.shape), _const_spec(ln_b.shape),
            _const_spec(ws.shape), _const_spec(bs_full.shape), _const_spec(w_out.shape),
        ],
        out_specs=pl.BlockSpec((tm, d), lambda i: (i, 0)),
        scratch_shapes=[pltpu.VMEM((tm, e), bf16)],
        compiler_params=_cparams("arbitrary"),
        name="gmlp",
    )(x, mod, w_in, b_in, ln_g, ln_b, ws, bs_full, w_out)


def _router_kernel(x_ref, mod_ref, wr_hi_ref, wr_lo_ref, h_ref, meta_ref, counts_ref):
    hf = _modulate(x_ref[...], mod_ref, 3)
    hb = hf.astype(bf16)
    h_ref[...] = hb
    h_lo = (hf - hb.astype(f32)).astype(bf16)
    logits = _dot(hb, wr_hi_ref[...]) + (_dot(hb, wr_lo_ref[...]) + _dot(h_lo, wr_hi_ref[...]))
    lane = lax.broadcasted_iota(jnp.int32, logits.shape, 1).astype(f32)
    logits = jnp.where(lane < N_EXPERTS, logits, -jnp.inf)
    m1 = jnp.max(logits, axis=-1, keepdims=True)
    i1 = jnp.min(jnp.where(logits == m1, lane, float(LANES)), axis=-1, keepdims=True)
    rest = jnp.where(lane == i1, -jnp.inf, logits)
    m2 = jnp.max(rest, axis=-1, keepdims=True)
    i2 = jnp.min(jnp.where(rest == m2, lane, float(LANES)), axis=-1, keepdims=True)
    e2 = jnp.exp(m2 - m1)
    w1 = 1.0 / (1.0 + e2)
    w2 = e2 * w1

    tm = logits.shape[0]
    cnt = jnp.where(lane == i1, 1.0, jnp.where(lane == i2, 1.0, 0.0))
    rr = lax.broadcasted_iota(jnp.int32, (tm, tm), 0)
    cc = lax.broadcasted_iota(jnp.int32, (tm, tm), 1)
    before = jnp.where(rr > cc, 1.0, 0.0).astype(bf16)
    rank = _dot(before, cnt.astype(bf16))
    counts = jnp.sum(cnt, axis=0, keepdims=True)
    padded = jnp.floor((counts + (PIECE_ALIGN - 1)) * (1.0 / PIECE_ALIGN)) * PIECE_ALIGN
    lane1 = lane[0:1, :]
    piece_off = jnp.zeros((1, LANES), f32)
    off = jnp.zeros((1, 1), f32)
    for e in range(N_EXPERTS):
        piece_off = jnp.where(lane1 == e, off, piece_off)
        off = off + padded[:, e:e + 1]
    local = piece_off + rank
    pos1 = jnp.sum(jnp.where(lane == i1, local, 0.0), axis=-1, keepdims=True)
    pos2 = jnp.sum(jnp.where(lane == i2, local, 0.0), axis=-1, keepdims=True)
    meta_ref[...] = jnp.where(lane == 0, pos1, jnp.where(lane == 1, pos2, jnp.where(lane == 2, w1,
                              jnp.where(lane == 3, w2, 0.0))))
    counts_ref[...] = counts


def _router(x, mod, wr_hi, wr_lo, n_prompt_rows, dec_seq):
    r, d = x.shape
    tm = MOE_SORT_BLOCK
    grp = functools.partial(_group_of_tile, tm=tm, n_prompt_rows=n_prompt_rows, dec_seq=dec_seq)
    return pl.pallas_call(
        _router_kernel,
        out_shape=(jax.ShapeDtypeStruct((r, d), bf16), jax.ShapeDtypeStruct((r, LANES), f32),
                   jax.ShapeDtypeStruct((r // tm, 1, LANES), f32)),
        grid=(r // tm,),
        in_specs=[
            pl.BlockSpec((tm, d), lambda i: (i, 0)),
            pl.BlockSpec((None, 6, d), lambda i: (grp(i), 0, 0)),
            _const_spec(wr_hi.shape), _const_spec(wr_lo.shape),
        ],
        out_specs=(pl.BlockSpec((tm, d), lambda i: (i, 0)), pl.BlockSpec((tm, LANES), lambda i: (i, 0)),
                   pl.BlockSpec((None, 1, LANES), lambda i: (i, 0, 0))),
        compiler_params=_cparams("arbitrary"),
        name="router",
    )(x, mod, wr_hi, wr_lo)


def _moe_kernel(n16_ref, loc_ref, dst_ref, seg_off_ref, seg_len_ref,
                h_ref, meta_ref, x_ref, mod_ref, w1_ref, w3_ref, w2_ref, op_ref, os_ref,
                hs_s, ys_s, loc_s, *, nsub, ne, nf, ch, nsb_prompt):
    sb = pl.program_id(0)
    p = pl.program_id(1)
    n_exp = ne * nf
    loc_rows = loc_s.shape[0]
    pa = PIECE_ALIGN

    def one_hot_cols(meta, v1, v2):
        lane = lax.broadcasted_iota(jnp.int32, (1, loc_rows), 1).astype(f32)
        return jnp.where(lane == meta[:, 0:1], v1, jnp.where(lane == meta[:, 1:2], v2, 0.0)).astype(bf16)

    def copy_pieces(blk, to_sorted):
        for e in range(ne):
            n = n16_ref[blk * ne + e]
            src = loc_ref[blk * ne + e]
            dst = dst_ref[blk * ne + e]

            def cp(i, _, src=src, dst=dst):
                a = pl.ds(pl.multiple_of(src + pa * i, pa), pa)
                b = pl.ds(pl.multiple_of(dst + pa * i, pa), pa)
                if to_sorted:
                    hs_s[b, :] = loc_s[a, :]
                else:
                    loc_s[a, :] = ys_s[b, :].astype(bf16)
                return 0

            lax.fori_loop(0, n, cp, 0)

    @pl.when(p < nsub)
    def _():
        @pl.when(p == 0)
        def _():
            hs_s[...] = jnp.zeros_like(hs_s)
            ys_s[...] = jnp.zeros_like(ys_s)

        pt = one_hot_cols(meta_ref[...], 1.0, 1.0)
        loc_s[...] = _dot_tn(pt, h_ref[...]).astype(bf16)
        copy_pieces(sb * nsub + p, True)

    @pl.when((p >= nsub) & (p < nsub + n_exp))
    def _():
        e = lax.div(p - nsub, jnp.int32(nf))
        start = seg_off_ref[sb * ne + e]
        ln = seg_len_ref[sb * ne + e]
        half = ch // 2
        nfull = lax.div(ln, jnp.int32(ch))
        rem = ln - nfull * ch

        def chunk(r0, valid, size):
            rows = pl.ds(pl.multiple_of(r0, pa), size)
            xc = hs_s[rows, :]
            a = _silu(_dot(xc, w1_ref[...])) * _dot(xc, w3_ref[...])
            y = _dot(a.astype(bf16), w2_ref[...])
            ri = lax.broadcasted_iota(jnp.int32, (size, 1), 0)
            ys_s[rows, :] += jnp.where(ri < valid, y, 0.0)

        def body(j, _):
            chunk(start + j * ch, ln - j * ch, ch)
            return 0

        lax.fori_loop(0, nfull + jnp.where(rem > half, 1, 0), body, 0)

        @pl.when((rem > 0) & (rem <= half))
        def _():
            chunk(start + nfull * ch, rem, half)

    @pl.when(p >= nsub + n_exp)
    def _():
        copy_pieces(sb * nsub + (p - nsub - n_exp), False)
        meta = meta_ref[...]
        a = one_hot_cols(meta, meta[:, 2:3], meta[:, 3:4])
        y = x_ref[...] + mod_ref[5:6, :] * _dot(a, loc_s[...])

        @pl.when(sb < nsb_prompt)
        def _():
            op_ref[...] = y

        @pl.when(sb >= nsb_prompt)
        def _():
            os_ref[...] = y


def _moe(x, h, meta, counts, mod, w1, w3, w2, n_prompt_rows, dec_seq):
    r, d = x.shape
    ne, _, ff = w1.shape
    t_super, tb, ch, tf, pa = MOE_SUPER_BLOCK, MOE_SORT_BLOCK, MOE_CHUNK, MOE_FF_TILE, PIECE_ALIGN
    nsub = t_super // tb
    nsb = r // t_super
    npb = n_prompt_rows // tb
    nf = ff // tf
    n_exp = ne * nf
    loc_rows = 2 * tb + LANES
    assert loc_rows >= 2 * tb + ne * (pa - 1)
    max_rows = 2 * t_super + nsub * ne * (pa - 1) + ch
    sort_rows = LANES * (-(-max_rows // LANES))

    cnt = counts[:, 0, :ne].astype(jnp.int32)
    n16 = (cnt + (pa - 1)) // pa
    loc = pa * (jnp.cumsum(n16, axis=1) - n16)
    n16_sb = n16.reshape(nsb, nsub, ne)
    seg_len = pa * jnp.sum(n16_sb, axis=1)
    seg_off = jnp.cumsum(seg_len, axis=1) - seg_len
    dst = seg_off[:, None, :] + pa * (jnp.cumsum(n16_sb, axis=1) - n16_sb)
    scalars = [a.reshape(-1).astype(jnp.int32) for a in (n16, loc, dst, seg_off, seg_len)]

    grp = functools.partial(_group_of_tile, tm=t_super, n_prompt_rows=n_prompt_rows, dec_seq=dec_seq)

    def exp_step(p):
        return jnp.clip(p - nsub, 0, n_exp - 1)

    def tok_blk(sb, s):
        return sb * nsub + jnp.clip(s, 0, nsub - 1)

    grid_spec = pltpu.PrefetchScalarGridSpec(
        num_scalar_prefetch=len(scalars),
        grid=(nsb, nsub + n_exp + nsub),
        in_specs=[
            pl.BlockSpec((tb, d), lambda sb, p, *_: (tok_blk(sb, p), 0)),
            pl.BlockSpec((tb, LANES), lambda sb, p, *_: (sb * nsub + jnp.where(p < nsub, p, jnp.clip(p - nsub - n_exp, 0, nsub - 1)), 0)),
            pl.BlockSpec((tb, d), lambda sb, p, *_: (tok_blk(sb, p - nsub - n_exp), 0)),
            pl.BlockSpec((None, 6, d), lambda sb, p, *_: (grp(sb), 0, 0)),
            pl.BlockSpec((None, d, tf), lambda sb, p, *_: (exp_step(p) // nf, 0, exp_step(p) % nf)),
            pl.BlockSpec((None, d, tf), lambda sb, p, *_: (exp_step(p) // nf, 0, exp_step(p) % nf)),
            pl.BlockSpec((None, tf, d), lambda sb, p, *_: (exp_step(p) // nf, exp_step(p) % nf, 0)),
        ],
        out_specs=(
            pl.BlockSpec((tb, d), lambda sb, p, *_: (jnp.minimum(tok_blk(sb, p - nsub - n_exp), npb - 1), 0)),
            pl.BlockSpec((tb, d), lambda sb, p, *_: (jnp.maximum(tok_blk(sb, p - nsub - n_exp) - npb, 0), 0)),
        ),
        scratch_shapes=[
            pltpu.VMEM((sort_rows, d), bf16),
            pltpu.VMEM((sort_rows, d), f32),
            pltpu.VMEM((loc_rows, d), bf16),
        ],
    )
    return pl.pallas_call(
        functools.partial(_moe_kernel, nsub=nsub, ne=ne, nf=nf, ch=ch, nsb_prompt=n_prompt_rows // t_super),
        out_shape=(jax.ShapeDtypeStruct((n_prompt_rows, d), f32), jax.ShapeDtypeStruct((r - n_prompt_rows, d), f32)),
        grid_spec=grid_spec,
        compiler_params=_cparams("arbitrary", "arbitrary"),
        name="moe_sparse",
    )(*scalars, h, meta, x, mod, w1, w3, w2)


def kernel(x_prompt, x_sample, c, cache_dattn_k, cache_dattn_v, state_mlstm_c, state_mlstm_n, state_mlstm_m,
           c_ctx, w_ada, b_ada, w_in_ab, conv_w, conv_b, gate_b, qn_g, kn_g, lam_q1, lam_k1, lam_q2, lam_k2,
           m_norm_g, a_norm_g, w_out_ab, ff_w1, ff_w3, ff_w2, w_in_c, b_in_c, c_ln_g, c_ln_b, c_ws, c_bs,
           w_out_c, w_router, ex_w1, ex_w3, ex_w2):
    bp, seq, d = x_prompt.shape
    bs, dec_seq, _ = x_sample.shape
    depth = w_ada.shape[0]
    n_prompt_rows = bp * seq
    assert n_prompt_rows % dec_seq == 0 and seq % M_CHUNK == 0 and dec_seq % MOE_SUPER_BLOCK == 0
    nh = M_HEADS

    mods = _ada_table(jnp.concatenate([c_ctx[None], c], axis=0), w_ada, b_ada)
    x = (x_prompt.reshape(n_prompt_rows, d), x_sample.reshape(bs * dec_seq, d))
    rows = (n_prompt_rows, dec_seq)

    def joined(v):
        return jnp.concatenate(v, axis=0) if isinstance(v, tuple) else v

    def parts(v):
        return v if isinstance(v, tuple) else (v[:n_prompt_rows], v[n_prompt_rows:])

    new_k, new_v, new_c, new_n, new_m = [], [], [], [], []
    for l in range(depth):
        j = l // 2
        mod = mods[l]
        if l % 2 == 0:
            lam_init = 0.8 - 0.6 * math.exp(-0.3 * l)
            lam = (jnp.exp(jnp.sum((lam_q1[j] * lam_k1[j]).astype(f32)))
                   - jnp.exp(jnp.sum((lam_q2[j] * lam_k2[j]).astype(f32))) + lam_init).reshape(1)
            o3 = 4 * M_W
            o4 = o3 + 4 * nh
            w = w_in_ab[j]
            w_main = jnp.concatenate([w[:, :o3], w[:, o4:]], axis=1).astype(bf16)
            w_gate = jnp.zeros((d, LANES), f32).at[:, :4 * nh].set(w[:, o3:o4]).astype(bf16)
            x = parts(x)
            z, gates = _inproj(*x, mod, w_main, w_gate, *rows)
            gates = gates + jnp.zeros((1, LANES), f32).at[0, :4 * nh].set(gate_b[j])
            mng = m_norm_g[j].reshape(1, M_W)
            cb = conv_b[j].reshape(1, 2 * M_W)
            hm_p, c_f, n_f, m_f = _mlstm(z, gates, conv_w[j], cb, mng, None, batch=bp, t=seq, row_block0=0)
            state0 = (state_mlstm_c[:, j], state_mlstm_n[:, j].reshape(bs, 2, nh, 1, M_DIM),
                      state_mlstm_m[:, j].reshape(bs, 2, nh, 1, 1))
            hm_s, _, _, _ = _mlstm(z, gates, conv_w[j], cb, mng, state0,
                                   batch=bs, t=dec_seq, row_block0=n_prompt_rows // dec_seq)
            new_c.append(c_f)
            new_n.append(n_f.reshape(bp, 2, nh, M_DIM))
            new_m.append(m_f.reshape(bp, 2, nh))

            qg2 = jnp.tile(qn_g[j], 2).reshape(1, LANES)
            kg2 = jnp.tile(kn_g[j], 2).reshape(1, LANES)
            ag = a_norm_g[j].reshape(1, LANES)
            ha_p, k_ctx, v_ctx = _attn(z, lam, qg2, kg2, ag, batch=bp, t=seq, row_block0=0,
                                       out_scale=1.0 - lam_init)
            kct = cache_dattn_k[:, j].transpose(0, 1, 2, 4, 3).reshape(bs, A_HEADS, LANES, -1)
            (ha_s,) = _attn(z, lam, qg2, kg2, ag, batch=bs, t=dec_seq, row_block0=n_prompt_rows // dec_seq,
                            rope_tabs=_rope_tables(dec_seq), ctx_kv=(kct, cache_dattn_v[:, j]),
                            out_scale=1.0 - lam_init)
            new_k.append(k_ctx)
            new_v.append(v_ctx)
            x = _outproj(*x, hm_p, hm_s, ha_p, ha_s, mod, w_out_ab[j].astype(bf16), *rows)
            x = _ffn(x, mod, ff_w1[j].astype(bf16), ff_w3[j].astype(bf16), ff_w2[j].astype(bf16), *rows)
        else:
            e = w_out_c.shape[1]
            bs_full = jnp.repeat(c_bs[j].T, e // C_GROUPS, axis=1)
            x = _gmlp(joined(x), mod, w_in_c[j].astype(bf16), b_in_c[j].reshape(1, -1), c_ln_g[j].reshape(1, -1),
                      c_ln_b[j].reshape(1, -1), c_ws[j].astype(bf16), bs_full, w_out_c[j].astype(bf16), *rows)
            wr = jnp.zeros((d, LANES), f32).at[:, :N_EXPERTS].set(w_router[j])
            wr_hi = wr.astype(bf16)
            wr_lo = (wr - wr_hi.astype(f32)).astype(bf16)
            h, meta, counts = _router(x, mod, wr_hi, wr_lo, *rows)
            x = _moe(x, h, meta, counts, mod, ex_w1[j].astype(bf16), ex_w3[j].astype(bf16),
                     ex_w2[j].astype(bf16), *rows)

    y_prompt, y_sample = parts(x)
    y_prompt = y_prompt.reshape(bp, seq, d)
    y_sample = y_sample.reshape(bs, dec_seq, d)
    return (y_prompt, y_sample, jnp.stack(new_k, axis=1), jnp.stack(new_v, axis=1),
            jnp.stack(new_c, axis=1), jnp.stack(new_n, axis=1), jnp.stack(new_m, axis=1))
```

```python
import functools
import math

import jax
import jax.numpy as jnp
import numpy as np
from jax import lax
from jax.experimental import pallas as pl
from jax.experimental.pallas import tpu as pltpu

f32 = jnp.float32
bf16 = jnp.bfloat16

D_MODEL = 1024
M_HEADS = 4
M_DIM = 128
M_W = M_HEADS * M_DIM
M_CHUNK = 128
A_HEADS = 4
A_VDIM = 128
A_DIM = 64
A_W = A_HEADS * A_VDIM
GRID_W = 64
ROPE_THETA = 10000.0
C_CHUNK = 128
C_GROUPS = 4
N_EXPERTS = 8
EPS = 1e-6

LANES = 128
ROW_TILE = 512
PIECE_ALIGN = 16
MOE_SUPER_BLOCK = 2048
MOE_SORT_BLOCK = 512
MOE_CHUNKS = (128, 192, 224, 256)
MOE_FF_TILE = 896
VMEM_LIMIT = 60 * 1024 * 1024


def _cparams(*sem):
    return pltpu.CompilerParams(dimension_semantics=tuple(sem), vmem_limit_bytes=VMEM_LIMIT)


def _const_spec(shape):
    nd = len(shape)
    return pl.BlockSpec(shape, lambda *_: (0,) * nd, pipeline_mode=pl.Buffered(1))


def _sigmoid(x):
    return 1.0 / (1.0 + jnp.exp(-x))


def _silu(x):
    return x * _sigmoid(x)


def _log_sigmoid(x):
    return jnp.minimum(x, 0.0) - jnp.log(1.0 + jnp.exp(-jnp.abs(x)))


def _rms(x):
    return x * lax.rsqrt(jnp.mean(x * x, axis=-1, keepdims=True) + EPS)


def _modulate(x, mod_ref, first):
    shift = mod_ref[first:first + 1, :]
    scale = mod_ref[first + 1:first + 2, :]
    return _rms(x) * (1.0 + scale) + shift


def _dot(a, b):
    return jnp.dot(a, b, preferred_element_type=f32)


def _dot_nt(a, b):
    return lax.dot_general(a, b, (((1,), (1,)), ((), ())), preferred_element_type=f32)


def _dot_tn(a, b):
    return lax.dot_general(a, b, (((0,), (0,)), ((), ())), preferred_element_type=f32)


def _split3(x):
    hi = x.astype(bf16)
    r1 = x - hi.astype(f32)
    mid = r1.astype(bf16)
    lo = (r1 - mid.astype(f32)).astype(bf16)
    return hi, mid, lo


def _group_of_tile(i, tm, n_prompt_rows, dec_seq):
    pt = n_prompt_rows // tm
    return jnp.where(i < pt, 0, 1 + (i - pt) // (dec_seq // tm))


def _ada_kernel(cv_ref, w_ref, b_ref, o_ref):
    a = _silu(cv_ref[...]).astype(bf16)
    o_ref[...] = _dot(a, w_ref[...].astype(bf16)) + b_ref[...]


def _ada_table(cv, w_ada, b_ada):
    depth, d, n = w_ada.shape
    g = cv.shape[0]
    gp = 8 * ((g + 7) // 8)
    cvp = jnp.zeros((gp, d), f32).at[:g].set(cv)
    tn = 1536
    out = pl.pallas_call(
        _ada_kernel,
        out_shape=jax.ShapeDtypeStruct((depth, gp, n), f32),
        grid=(depth, n // tn),
        in_specs=[
            pl.BlockSpec((gp, d), lambda l, j: (0, 0)),
            pl.BlockSpec((None, d, tn), lambda l, j: (l, 0, j)),
            pl.BlockSpec((None, 1, tn), lambda l, j: (l, 0, j)),
        ],
        out_specs=pl.BlockSpec((None, gp, tn), lambda l, j: (l, 0, j)),
        compiler_params=_cparams("arbitrary", "arbitrary"),
        name="ada_table",
    )(cvp, w_ada, b_ada.reshape(depth, 1, n))
    return out[:, :g].reshape(depth, g, 6, d)


def _part_specs(tm, width, npt):
    first = pl.BlockSpec((tm, width), lambda i: (jnp.minimum(i, npt - 1), 0))
    second = pl.BlockSpec((tm, width), lambda i: (jnp.maximum(i - npt, 0), 0))
    return first, second


def _pick(a_ref, b_ref, npt):
    return jnp.where(pl.program_id(0) < npt, a_ref[...], b_ref[...])


def _inproj_kernel(xp_ref, xs_ref, mod_ref, w_ref, wg_ref, z_ref, g_ref, *, n_main, npt):
    h = _modulate(_pick(xp_ref, xs_ref, npt), mod_ref, 0).astype(bf16)
    step = 512
    for j in range(n_main // step):
        z_ref[:, j * step:(j + 1) * step] = _dot(h, w_ref[:, j * step:(j + 1) * step]).astype(bf16)
    g_ref[...] = _dot(h, wg_ref[...])


def _inproj(xp, xs, mod, w_main, w_gate, n_prompt_rows, dec_seq):
    d = xp.shape[1]
    r = xp.shape[0] + xs.shape[0]
    tm = ROW_TILE
    npt = n_prompt_rows // tm
    n_main = w_main.shape[1]
    grp = functools.partial(_group_of_tile, tm=tm, n_prompt_rows=n_prompt_rows, dec_seq=dec_seq)
    return pl.pallas_call(
        functools.partial(_inproj_kernel, n_main=n_main, npt=npt),
        out_shape=(jax.ShapeDtypeStruct((r, n_main), bf16), jax.ShapeDtypeStruct((r, LANES), f32)),
        grid=(r // tm,),
        in_specs=[
            *_part_specs(tm, d, npt),
            pl.BlockSpec((None, 6, d), lambda i: (grp(i), 0, 0)),
            _const_spec(w_main.shape),
            _const_spec(w_gate.shape),
        ],
        out_specs=(pl.BlockSpec((tm, n_main), lambda i: (i, 0)), pl.BlockSpec((tm, LANES), lambda i: (i, 0))),
        compiler_params=_cparams("arbitrary"),
        name="inproj_ab",
    )(xp, xs, mod, w_main, w_gate)


def _conv_silu(x, w_ref, b_ref, t):
    row = lax.broadcasted_iota(jnp.int32, (t, 1), 0)
    prev = jnp.where(row == 0, 0.0, pltpu.roll(x, 1, 0))
    nxt = jnp.where(row == t - 1, 0.0, pltpu.roll(x, t - 1, 0))
    y = b_ref[...] + prev * w_ref[0:1, :] + x * w_ref[1:2, :] + nxt * w_ref[2:3, :]
    return _silu(y)


def _mlstm_kernel(q_ref, k_ref, v_ref, o_ref, g_ref, cw_ref, cb_ref, ng_ref, *rest, t, has_state):
    if has_state:
        c0_ref, n0_ref, m0_ref, *rest = rest
    hm_ref, c_out, n_out, m_out, q_s, qt_s, kh_s, kl_s, vt_s, gs_s, hft_s, hbt_s, ct_s, sel_s = rest
    nc = t // M_CHUNK
    lc = M_CHUNK
    nh = M_HEADS
    w = M_W
    q_s[...] = _conv_silu(q_ref[...].astype(f32), cw_ref[:, :w], cb_ref[:, :w], t)
    kf = _conv_silu(k_ref[...].astype(f32), cw_ref[:, w:], cb_ref[:, w:], t) * (M_DIM ** -0.5)
    kh = kf.astype(bf16)
    kh_s[...] = kh
    kl_s[...] = (kf - kh.astype(f32)).astype(bf16)
    sel_row = lax.broadcasted_iota(jnp.int32, (3 * LANES, LANES), 0)
    for d in range(2):
        for h in range(nh):
            ci, ln = d * nh + h, 2 * nh * d + h
            ct_s[ci] = c0_ref[d, h].T if has_state else jnp.zeros((M_DIM, M_DIM), f32)
            sel_s[ci] = jnp.where(sel_row == ln, 1.0, jnp.where(sel_row == LANES + ln, 1.0, jnp.where(
                sel_row == 2 * LANES + ln, 1.0, 0.0))).astype(bf16)
    if has_state:
        n_out[...] = n0_ref[...]
        m_out[...] = m0_ref[...]
    else:
        n_out[...] = jnp.zeros_like(n_out)
        m_out[...] = jnp.zeros_like(m_out)

    lane = lax.broadcasted_iota(jnp.int32, (1, LANES), 1)
    is_lf = ((lane >= nh) & (lane < 2 * nh)) | ((lane >= 3 * nh) & (lane < 4 * nh))
    is_bw_lf = (lane >= 3 * nh) & (lane < 4 * nh)
    g = g_ref[...]
    gs_s[...] = jnp.where(is_lf, _log_sigmoid(g), g)

    rr = lax.broadcasted_iota(jnp.int32, (lc, lc), 0)
    cc = lax.broadcasted_iota(jnp.int32, (lc, lc), 1)
    tri_incl = jnp.where(rr >= cc, 1.0, 0.0).astype(bf16)

    def prep_body(c, _):
        off = pl.multiple_of(c * lc, lc)
        tile = gs_s[pl.ds(off, lc), :]
        hi, mid, lo = _split3(jnp.where(is_lf, tile, 0.0))
        cs = _dot(tri_incl, hi) + _dot(tri_incl, mid) + _dot(tri_incl, lo)
        total = jnp.broadcast_to(cs[lc - 1:lc, :], (lc, LANES))
        b = jnp.where(is_bw_lf, total - cs + tile, cs)
        lmb = tile - pltpu.roll(b, LANES - nh, 1)
        gv = pltpu.roll(total, LANES - nh, 1) + lmb
        low = jnp.where(is_lf, b, lmb)
        high = pltpu.roll(jnp.where(is_lf, total, gv), 4 * nh, 1)
        gs_s[pl.ds(off, lc), :] = jnp.where(lane < 4 * nh, low, jnp.where(lane < 8 * nh, high, 0.0))
        wide = pl.ds(pl.multiple_of(c * w, w), w)
        qt_s[wide, :] = q_s[pl.ds(off, lc), :].T
        vt_s[wide, :] = v_ref[pl.ds(off, lc), :].astype(f32).T.astype(bf16)
        return 0

    lax.fori_loop(0, nc, prep_body, 0)

    sub8 = lax.broadcasted_iota(jnp.int32, (8, LANES), 0)

    def two_rows(x):
        hi = x.astype(bf16).astype(f32)
        return jnp.where(sub8 == 0, hi, jnp.where(sub8 == 1, x - hi, 0.0)).astype(bf16)

    def issue(c, h, d, tile3, tile_t):
        off = pl.multiple_of(c * lc, lc)
        col = slice(h * M_DIM, (h + 1) * M_DIM)
        ci, ln = d * nh + h, 2 * nh * d + h
        head = pl.ds(pl.multiple_of(c * w + h * M_DIM, M_DIM), M_DIM)
        b_row = tile_t[ln + nh:ln + nh + 1, :]
        g_row = tile_t[ln + 4 * nh:ln + 4 * nh + 1, :]
        total = tile_t[ln + 5 * nh:ln + 5 * nh + 1, 0:1]
        ct, nm, mm = ct_s[ci], n_out[d, h], m_out[d, h]
        qt = qt_s[head, :]
        qtb = qt.astype(bf16)
        khb = kh_s[pl.ds(off, lc), col]
        vtb = vt_s[head, :]
        lmb = _dot(tile3, sel_s[ci])
        sraw = _dot(khb, qtb)
        qn = _dot(two_rows(nm), qtb)
        lhs = jnp.concatenate([ct.astype(bf16), vtb], axis=1)
        m_new = jnp.maximum(mm + total, jnp.max(g_row, axis=1, keepdims=True))
        decay = jnp.exp(mm + total - m_new)
        ew = jnp.exp(g_row - m_new)
        ct_s[ci] = decay * ct + _dot(vtb * ew.astype(bf16), khb)
        ew2 = two_rows(ew)
        nk_h = _dot(ew2, khb)
        nk_l = _dot(ew2, kl_s[pl.ds(off, lc), col])
        n_out[d, h] = decay * nm + (nk_h[0:1, :] + nk_h[1:2, :] + nk_l[0:1, :])
        m_out[d, h] = m_new
        return dict(d=d, head=head, b_row=b_row, mm=mm, qt=qt, lmb=lmb, sraw=sraw, qn=qn, lhs=lhs)

    def weigh(st):
        keep = (cc >= rr) if st["d"] == 0 else (rr >= cc)
        dmat = jnp.where(keep, st["lmb"] + st["b_row"], -jnp.inf)
        inter = st["mm"] + st["b_row"]
        mt = jnp.maximum(inter, jnp.max(dmat, axis=0, keepdims=True))
        w_inter = jnp.exp(inter - mt)
        s = st["sraw"] * jnp.exp(dmat - mt)
        qn = st["qn"]
        den = w_inter * (qn[0:1, :] + qn[1:2, :]) + jnp.sum(s, axis=0, keepdims=True)
        inv = 1.0 / jnp.maximum(jnp.abs(den), jnp.exp(-mt))
        return jnp.concatenate([st["qt"] * (w_inter * inv), s * inv], axis=0).astype(bf16)

    def body(i, _):
        states = []
        for d in range(2):
            c = i if d == 0 else nc - 1 - i
            tile = gs_s[pl.ds(pl.multiple_of(c * lc, lc), lc), :]
            tile_t = tile.T
            hi, mid, lo = _split3(tile)
            tile3 = jnp.concatenate([hi, mid, lo], axis=1)
            states += [issue(c, h, d, tile3, tile_t) for h in range(nh)]
        rhss = [weigh(st) for st in states]
        for st, rhs in zip(states, rhss):
            hct = _dot(st["lhs"], rhs)
            if st["d"] == 0:
                hft_s[st["head"], :] = hct
            else:
                hbt_s[st["head"], :] = hct
        return 0

    lax.fori_loop(0, nc, body, 0)

    def out_body(c, _):
        rows = pl.ds(pl.multiple_of(c * lc, lc), lc)
        wide = pl.ds(pl.multiple_of(c * w, w), w)
        hsum = (hft_s[wide, :] + hbt_s[wide, :]).T
        for h in range(nh):
            col = slice(h * M_DIM, (h + 1) * M_DIM)
            mo = _sigmoid(o_ref[rows, col].astype(f32))
            hm_ref[rows, col] = (_rms(hsum[:, col]) * ng_ref[:, col] * mo).astype(hm_ref.dtype)
        return 0

    lax.fori_loop(0, nc, out_body, 0)
    for d in range(2):
        for h in range(nh):
            c_out[d, h] = ct_s[d * nh + h].T


def _mlstm(z, gates, conv_w, conv_b, m_norm_g, state0, *, batch, t, row_block0):
    nh = M_HEADS
    w = M_W
    mode = dict(pipeline_mode=pl.Buffered(1)) if t * w * 2 > (1 << 20) else {}
    seq = lambda colblk: pl.BlockSpec((t, w), lambda b: (row_block0 + b, colblk), **mode)
    state = lambda *tail: pl.BlockSpec((None, 2, nh) + tail, lambda b: (b,) + (0,) * (2 + len(tail)))
    state_specs = [state(M_DIM, M_DIM), state(1, M_DIM), state(1, 1)]
    has_state = state0 is not None
    return pl.pallas_call(
        functools.partial(_mlstm_kernel, t=t, has_state=has_state),
        out_shape=(
            jax.ShapeDtypeStruct((batch * t, w), bf16),
            jax.ShapeDtypeStruct((batch, 2, nh, M_DIM, M_DIM), f32),
            jax.ShapeDtypeStruct((batch, 2, nh, 1, M_DIM), f32),
            jax.ShapeDtypeStruct((batch, 2, nh, 1, 1), f32),
        ),
        grid=(batch,),
        in_specs=[
            seq(0), seq(1), seq(2), seq(3),
            pl.BlockSpec((t, LANES), lambda b: (row_block0 + b, 0)),
            _const_spec(conv_w.shape), _const_spec(conv_b.shape), _const_spec(m_norm_g.shape),
            *(state_specs if has_state else []),
        ],
        out_specs=(pl.BlockSpec((t, w), lambda b: (b, 0)), *state_specs),
        scratch_shapes=[
            pltpu.VMEM((t, w), f32),
            pltpu.VMEM((t * nh, M_DIM), f32),
            pltpu.VMEM((t, w), bf16),
            pltpu.VMEM((t, w), bf16),
            pltpu.VMEM((t * nh, M_DIM), bf16),
            pltpu.VMEM((t, LANES), f32),
            pltpu.VMEM((t * nh, M_DIM), f32),
            pltpu.VMEM((t * nh, M_DIM), f32),
            pltpu.VMEM((2 * nh, M_DIM, M_DIM), f32),
            pltpu.VMEM((2 * nh, 3 * LANES, LANES), bf16),
        ],
        compiler_params=_cparams("arbitrary"),
        name=f"mlstm_t{t}",
    )(z, z, z, z, gates, conv_w, conv_b, m_norm_g, *(state0 if has_state else ()))


def _pair_norm(x, gain):
    lane = lax.broadcasted_iota(jnp.int32, (1, LANES), 1)
    first = lane < A_DIM
    sq = x * x
    s_all = jnp.sum(sq, axis=-1, keepdims=True)
    s0 = jnp.sum(jnp.where(first, sq, 0.0), axis=-1, keepdims=True)
    inv0 = lax.rsqrt(s0 * (1.0 / A_DIM) + EPS)
    inv1 = lax.rsqrt((s_all - s0) * (1.0 / A_DIM) + EPS)
    return x * jnp.where(first, inv0, inv1) * gain


def _rope(x, cos, sin_signed):
    lane = lax.broadcasted_iota(jnp.int32, (1, LANES), 1)
    nf = A_DIM // 4
    partner = jnp.where((lane % (2 * nf)) < nf, pltpu.roll(x, LANES - nf, 1), pltpu.roll(x, nf, 1))
    return x * cos + partner * sin_signed


def _attn_kernel(*refs, t, tq, rope, ctx, out_scale):
    it = iter(refs)
    lam_ref = next(it)
    q_ref, k_ref, v_ref = next(it), next(it), next(it)
    qg_ref, kg_ref, ag_ref = next(it), next(it), next(it)
    if rope:
        cosq_ref, sinq_ref, cosk_ref, sink_ref = next(it), next(it), next(it), next(it)
    if ctx:
        kct_ref, vc_ref = next(it), next(it)
    ha_ref = next(it)
    if not ctx:
        newk_ref, newv_ref = next(it), next(it)
    kt_s = next(it)
    nh = A_HEADS

    @pl.when(pl.program_id(1) == 0)
    def _():
        for h in range(nh):
            col = slice(h * LANES, (h + 1) * LANES)
            kn = _pair_norm(k_ref[:, col].astype(f32), kg_ref[...])
            if not ctx:
                newk_ref[h, 0] = kn[:, :A_DIM]
                newk_ref[h, 1] = kn[:, A_DIM:]
                newv_ref[h] = v_ref[:, col].astype(f32)
            if rope:
                kn = _rope(kn, cosk_ref[...], sink_ref[...])
            kt_s[col, :] = kn.T.astype(bf16)

    lane = lax.broadcasted_iota(jnp.int32, (1, LANES), 1)
    lam = lam_ref[0]
    for h in range(nh):
        col = slice(h * LANES, (h + 1) * LANES)
        q = _pair_norm(q_ref[:, col].astype(f32), qg_ref[...])
        if rope:
            q = _rope(q, cosq_ref[...], sinq_ref[...])
        q = q * (A_DIM ** -0.5)
        qs = [jnp.where(lane < A_DIM, q, 0.0).astype(bf16), jnp.where(lane >= A_DIM, q, 0.0).astype(bf16)]
        kt = kt_s[col, :]
        vb = v_ref[:, col]
        if ctx:
            kctb = kct_ref[h].astype(bf16)
            vcb = vc_ref[h].astype(bf16)
        outs = []
        for i in range(2):
            sn = _dot(qs[i], kt)
            mx = jnp.max(sn, axis=-1, keepdims=True)
            if ctx:
                sc = _dot(qs[i], kctb)
                mx = jnp.maximum(mx, jnp.max(sc, axis=-1, keepdims=True))
            en = jnp.exp(sn - mx)
            den = jnp.sum(en, axis=-1, keepdims=True)
            o = _dot(en.astype(bf16), vb)
            if ctx:
                ec = jnp.exp(sc - mx)
                den = den + jnp.sum(ec, axis=-1, keepdims=True)
                o = o + _dot(ec.astype(bf16), vcb)
            outs.append(o * (1.0 / den))
        o = outs[0] - lam * outs[1]
        ha_ref[:, col] = (_rms(o) * ag_ref[...] * out_scale).astype(ha_ref.dtype)


def _attn(z, lam, qg2, kg2, a_norm_g, *, batch, t, row_block0, rope_tabs=None, ctx_kv=None, out_scale):
    nh = A_HEADS
    w = A_W
    tq = min(t, 256)
    nq = t // tq
    rope = rope_tabs is not None
    ctx = ctx_kv is not None
    qblk = 4 * M_W // w
    vec = pl.BlockSpec((1, LANES), lambda b, i: (0, 0))
    in_specs = [
        pl.BlockSpec(memory_space=pltpu.SMEM),
        pl.BlockSpec((tq, w), lambda b, i: ((row_block0 + b) * nq + i, qblk)),
        pl.BlockSpec((t, w), lambda b, i: (row_block0 + b, qblk + 1)),
        pl.BlockSpec((t, w), lambda b, i: (row_block0 + b, qblk + 2)),
        vec, vec, vec,
    ]
    args = [lam, z, z, z, qg2, kg2, a_norm_g]
    if rope:
        cos, sin = rope_tabs
        in_specs += [pl.BlockSpec((tq, LANES), lambda b, i: (i, 0))] * 2
        in_specs += [pl.BlockSpec((t, LANES), lambda b, i: (0, 0))] * 2
        args += [cos, sin, cos, sin]
    if ctx:
        kct, vc = ctx_kv
        in_specs += [pl.BlockSpec((None,) + kct.shape[1:], lambda b, i: (b, 0, 0, 0)),
                     pl.BlockSpec((None,) + vc.shape[1:], lambda b, i: (b, 0, 0, 0))]
        args += [kct, vc]
    out_shape = [jax.ShapeDtypeStruct((batch * t, w), bf16)]
    out_specs = [pl.BlockSpec((tq, w), lambda b, i: (b * nq + i, 0))]
    if not ctx:
        out_shape += [jax.ShapeDtypeStruct((batch, nh, 2, t, A_DIM), f32),
                      jax.ShapeDtypeStruct((batch, nh, t, A_VDIM), f32)]
        out_specs += [pl.BlockSpec((None, nh, 2, t, A_DIM), lambda b, i: (b, 0, 0, 0, 0)),
                      pl.BlockSpec((None, nh, t, A_VDIM), lambda b, i: (b, 0, 0, 0))]
    return pl.pallas_call(
        functools.partial(_attn_kernel, t=t, tq=tq, rope=rope, ctx=ctx, out_scale=out_scale),
        out_shape=tuple(out_shape),
        grid=(batch, nq),
        in_specs=in_specs,
        out_specs=tuple(out_specs),
        scratch_shapes=[pltpu.VMEM((w, t), bf16)],
        compiler_params=_cparams("arbitrary", "arbitrary"),
        name=f"diff_attn_t{t}",
    )(*args)


def _rope_tables(t):
    rows = t // GRID_W
    pos_row = np.repeat(np.arange(rows, dtype=np.float32), GRID_W)
    pos_col = (np.arange(rows * GRID_W) % GRID_W).astype(np.float32)
    nf = A_DIM // 4
    inv = (ROPE_THETA ** (-jnp.arange(nf, dtype=f32) / nf))
    lane = np.arange(LANES)
    j = lane % (2 * nf)
    use_col = (lane % A_DIM) >= (A_DIM // 2)
    pos = jnp.where(jnp.asarray(use_col)[None, :], jnp.asarray(pos_col)[:, None], jnp.asarray(pos_row)[:, None])
    ang = pos * inv[jnp.asarray(j % nf)][None, :]
    sign = jnp.asarray(np.where(j < nf, -1.0, 1.0).astype(np.float32))[None, :]
    return jnp.cos(ang), jnp.sin(ang) * sign


def _outproj_kernel(xp_ref, xs_ref, hmp_ref, hms_ref, hap_ref, has_ref, mod_ref, w_ref, o_ref, *, npt):
    half = hmp_ref.shape[1]
    hm = _pick(hmp_ref, hms_ref, npt)
    ha = _pick(hap_ref, has_ref, npt)
    y = _dot(hm, w_ref[:half, :]) + _dot(ha, w_ref[half:, :])
    o_ref[...] = _pick(xp_ref, xs_ref, npt) + mod_ref[2:3, :] * y


def _outproj(xp, xs, hm_p, hm_s, ha_p, ha_s, mod, w, n_prompt_rows, dec_seq):
    d = xp.shape[1]
    r = xp.shape[0] + xs.shape[0]
    tm = ROW_TILE
    npt = n_prompt_rows // tm
    grp = functools.partial(_group_of_tile, tm=tm, n_prompt_rows=n_prompt_rows, dec_seq=dec_seq)
    return pl.pallas_call(
        functools.partial(_outproj_kernel, npt=npt),
        out_shape=jax.ShapeDtypeStruct((r, d), f32),
        grid=(r // tm,),
        in_specs=[
            *_part_specs(tm, d, npt),
            *_part_specs(tm, hm_p.shape[1], npt),
            *_part_specs(tm, ha_p.shape[1], npt),
            pl.BlockSpec((None, 6, d), lambda i: (grp(i), 0, 0)),
            _const_spec(w.shape),
        ],
        out_specs=pl.BlockSpec((tm, d), lambda i: (i, 0)),
        compiler_params=_cparams("arbitrary"),
        name="outproj_ab",
    )(xp, xs, hm_p, hm_s, ha_p, ha_s, mod, w)


def _ffn_kernel(x_ref, mod_ref, w1_ref, w3_ref, w2_ref, o_ref, *, chunk):
    x = x_ref[...]
    h = _modulate(x, mod_ref, 3).astype(bf16)
    ff = w1_ref.shape[1]
    acc = jnp.zeros(x.shape, f32)
    for j in range(ff // chunk):
        sl = slice(j * chunk, (j + 1) * chunk)
        a = _silu(_dot(h, w1_ref[:, sl])) * _dot(h, w3_ref[:, sl])
        acc = acc + _dot(a.astype(bf16), w2_ref[sl, :])
    o_ref[...] = x + mod_ref[5:6, :] * acc


def _ffn(x, mod, w1, w3, w2, n_prompt_rows, dec_seq):
    r, d = x.shape
    tm = ROW_TILE
    grp = functools.partial(_group_of_tile, tm=tm, n_prompt_rows=n_prompt_rows, dec_seq=dec_seq)
    ff = w1.shape[1]
    chunk = ff // 2 if (ff // 2) % LANES == 0 else ff
    return pl.pallas_call(
        functools.partial(_ffn_kernel, chunk=chunk),
        out_shape=jax.ShapeDtypeStruct((r, d), f32),
        grid=(r // tm,),
        in_specs=[
            pl.BlockSpec((tm, d), lambda i: (i, 0)),
            pl.BlockSpec((None, 6, d), lambda i: (grp(i), 0, 0)),
            _const_spec(w1.shape), _const_spec(w3.shape), _const_spec(w2.shape),
        ],
        out_specs=pl.BlockSpec((tm, d), lambda i: (i, 0)),
        compiler_params=_cparams("arbitrary"),
        name="ffn_dense",
    )(x, mod, w1, w3, w2)


def _gelu_tanh(x):
    return 0.5 * x * (1.0 + jnp.tanh(math.sqrt(2.0 / math.pi) * (x + 0.044715 * (x * x * x))))


def _gmlp_kernel(x_ref, mod_ref, win_ref, bin_ref, lng_ref, lnb_ref, ws_ref, bs_ref, wout_ref, o_ref, us_s):
    x = x_ref[...]
    tm = x.shape[0]
    e = wout_ref.shape[0]
    ge = e // C_GROUPS
    h = _modulate(x, mod_ref, 0).astype(bf16)
    u = _gelu_tanh(_dot(h, win_ref[:, :e]) + bin_ref[:, :e])
    v = _gelu_tanh(_dot(h, win_ref[:, e:]) + bin_ref[:, e:])
    mu = jnp.mean(v, axis=-1, keepdims=True)
    vc = v - mu
    var = jnp.mean(vc * vc, axis=-1, keepdims=True)
    vn = (vc * lax.rsqrt(var + EPS) * lng_ref[...] + lnb_ref[...]).astype(bf16)
    for n in range(tm // C_CHUNK):
        rows = slice(n * C_CHUNK, (n + 1) * C_CHUNK)
        for g in range(C_GROUPS):
            cols = slice(g * ge, (g + 1) * ge)
            s = _dot(ws_ref[g], vn[rows, cols]) + bs_ref[:, cols]
            us_s[rows, cols] = (u[rows, cols] * s).astype(bf16)
    o_ref[...] = x + mod_ref[2:3, :] * _dot(us_s[...], wout_ref[...])


def _gmlp(x, mod, w_in, b_in, ln_g, ln_b, ws, bs_full, w_out, n_prompt_rows, dec_seq):
    r, d = x.shape
    tm = ROW_TILE
    e = w_out.shape[0]
    grp = functools.partial(_group_of_tile, tm=tm, n_prompt_rows=n_prompt_rows, dec_seq=dec_seq)
    return pl.pallas_call(
        _gmlp_kernel,
        out_shape=jax.ShapeDtypeStruct((r, d), f32),
        grid=(r // tm,),
        in_specs=[
            pl.BlockSpec((tm, d), lambda i: (i, 0)),
            pl.BlockSpec((None, 6, d), lambda i: (grp(i), 0, 0)),
            _const_spec(w_in.shape), _const_spec(b_in.shape), _const_spec(ln_g.shape), _const_spec(ln_b.shape),
            _const_spec(ws.shape), _const_spec(bs_full.shape), _const_spec(w_out.shape),
        ],
        out_specs=pl.BlockSpec((tm, d), lambda i: (i, 0)),
        scratch_shapes=[pltpu.VMEM((tm, e), bf16)],
        compiler_params=_cparams("arbitrary"),
        name="gmlp",
    )(x, mod, w_in, b_in, ln_g, ln_b, ws, bs_full, w_out)


def _router_kernel(x_ref, mod_ref, wr_hi_ref, wr_lo_ref, h_ref, meta_ref, counts_ref):
    hf = _modulate(x_ref[...], mod_ref, 3)
    hb = hf.astype(bf16)
    h_ref[...] = hb
    h_lo = (hf - hb.astype(f32)).astype(bf16)
    logits = _dot(hb, wr_hi_ref[...]) + (_dot(hb, wr_lo_ref[...]) + _dot(h_lo, wr_hi_ref[...]))
    lane = lax.broadcasted_iota(jnp.int32, logits.shape, 1).astype(f32)
    logits = jnp.where(lane < N_EXPERTS, logits, -jnp.inf)
    m1 = jnp.max(logits, axis=-1, keepdims=True)
    i1 = jnp.min(jnp.where(logits == m1, lane, float(LANES)), axis=-1, keepdims=True)
    rest = jnp.where(lane == i1, -jnp.inf, logits)
    m2 = jnp.max(rest, axis=-1, keepdims=True)
    i2 = jnp.min(jnp.where(rest == m2, lane, float(LANES)), axis=-1, keepdims=True)
    e2 = jnp.exp(m2 - m1)
    w1 = 1.0 / (1.0 + e2)
    w2 = e2 * w1

    tm = logits.shape[0]
    cnt = jnp.where(lane == i1, 1.0, jnp.where(lane == i2, 1.0, 0.0))
    rr = lax.broadcasted_iota(jnp.int32, (tm, tm), 0)
    cc = lax.broadcasted_iota(jnp.int32, (tm, tm), 1)
    before = jnp.where(rr > cc, 1.0, 0.0).astype(bf16)
    rank = _dot(before, cnt.astype(bf16))
    counts = jnp.sum(cnt, axis=0, keepdims=True)
    padded = jnp.floor((counts + (PIECE_ALIGN - 1)) * (1.0 / PIECE_ALIGN)) * PIECE_ALIGN
    lane1 = lane[0:1, :]
    piece_off = jnp.zeros((1, LANES), f32)
    off = jnp.zeros((1, 1), f32)
    for e in range(N_EXPERTS):
        piece_off = jnp.where(lane1 == e, off, piece_off)
        off = off + padded[:, e:e + 1]
    local = piece_off + rank
    pos1 = jnp.sum(jnp.where(lane == i1, local, 0.0), axis=-1, keepdims=True)
    pos2 = jnp.sum(jnp.where(lane == i2, local, 0.0), axis=-1, keepdims=True)
    meta_ref[...] = jnp.where(lane == 0, pos1, jnp.where(lane == 1, pos2, jnp.where(lane == 2, w1,
                              jnp.where(lane == 3, w2, 0.0))))
    counts_ref[...] = counts


def _router(x, mod, wr_hi, wr_lo, n_prompt_rows, dec_seq):
    r, d = x.shape
    tm = MOE_SORT_BLOCK
    grp = functools.partial(_group_of_tile, tm=tm, n_prompt_rows=n_prompt_rows, dec_seq=dec_seq)
    return pl.pallas_call(
        _router_kernel,
        out_shape=(jax.ShapeDtypeStruct((r, d), bf16), jax.ShapeDtypeStruct((r, LANES), f32),
                   jax.ShapeDtypeStruct((r // tm, 1, LANES), f32)),
        grid=(r // tm,),
        in_specs=[
            pl.BlockSpec((tm, d), lambda i: (i, 0)),
            pl.BlockSpec((None, 6, d), lambda i: (grp(i), 0, 0)),
            _const_spec(wr_hi.shape), _const_spec(wr_lo.shape),
        ],
        out_specs=(pl.BlockSpec((tm, d), lambda i: (i, 0)), pl.BlockSpec((tm, LANES), lambda i: (i, 0)),
                   pl.BlockSpec((None, 1, LANES), lambda i: (i, 0, 0))),
        compiler_params=_cparams("arbitrary"),
        name="router",
    )(x, mod, wr_hi, wr_lo)


def _moe_kernel(n16_ref, loc_ref, dst_ref, seg_off_ref, seg_len_ref, csel_ref, cnum_ref,
                h_ref, meta_ref, x_ref, mod_ref, w1_ref, w3_ref, w2_ref, op_ref, os_ref,
                hs_s, ys_s, loc_s, *, nsub, ne, nf, chunks, nsb_prompt):
    sb = pl.program_id(0)
    p = pl.program_id(1)
    n_exp = ne * nf
    loc_rows = loc_s.shape[0]
    pa = PIECE_ALIGN

    def one_hot_cols(meta, v1, v2):
        lane = lax.broadcasted_iota(jnp.int32, (1, loc_rows), 1).astype(f32)
        return jnp.where(lane == meta[:, 0:1], v1, jnp.where(lane == meta[:, 1:2], v2, 0.0)).astype(bf16)

    def copy_pieces(blk, to_sorted):
        for e in range(ne):
            n = n16_ref[blk * ne + e]
            src = loc_ref[blk * ne + e]
            dst = dst_ref[blk * ne + e]

            def cp(i, _, src=src, dst=dst):
                a = pl.ds(pl.multiple_of(src + pa * i, pa), pa)
                b = pl.ds(pl.multiple_of(dst + pa * i, pa), pa)
                if to_sorted:
                    hs_s[b, :] = loc_s[a, :]
                else:
                    loc_s[a, :] = ys_s[b, :].astype(bf16)
                return 0

            lax.fori_loop(0, n, cp, 0)

    @pl.when(p < nsub)
    def _():
        @pl.when(p == 0)
        def _():
            hs_s[...] = jnp.zeros_like(hs_s)
            ys_s[...] = jnp.zeros_like(ys_s)

        pt = one_hot_cols(meta_ref[...], 1.0, 1.0)
        loc_s[...] = _dot_tn(pt, h_ref[...]).astype(bf16)
        copy_pieces(sb * nsub + p, True)

    @pl.when((p >= nsub) & (p < nsub + n_exp))
    def _():
        e = lax.div(p - nsub, jnp.int32(nf))
        start = seg_off_ref[sb * ne + e]
        ln = seg_len_ref[sb * ne + e]
        which = csel_ref[sb * ne + e]
        count = cnum_ref[sb * ne + e]

        def chunk(r0, valid, size):
            rows = pl.ds(pl.multiple_of(r0, pa), size)
            xc = hs_s[rows, :]
            a = _silu(_dot(xc, w1_ref[...])) * _dot(xc, w3_ref[...])
            y = _dot(a.astype(bf16), w2_ref[...])
            ri = lax.broadcasted_iota(jnp.int32, (size, 1), 0)
            ys_s[rows, :] += jnp.where(ri < valid, y, 0.0)

        for k, size in enumerate(chunks):
            @pl.when(which == k)
            def _(size=size):
                def body(j, _):
                    chunk(start + j * size, ln - j * size, size)
                    return 0

                lax.fori_loop(0, count, body, 0)

    @pl.when(p >= nsub + n_exp)
    def _():
        copy_pieces(sb * nsub + (p - nsub - n_exp), False)
        meta = meta_ref[...]
        a = one_hot_cols(meta, meta[:, 2:3], meta[:, 3:4])
        y = x_ref[...] + mod_ref[5:6, :] * _dot(a, loc_s[...])

        @pl.when(sb < nsb_prompt)
        def _():
            op_ref[...] = y

        @pl.when(sb >= nsb_prompt)
        def _():
            os_ref[...] = y


def _moe(x, h, meta, counts, mod, w1, w3, w2, n_prompt_rows, dec_seq):
    r, d = x.shape
    ne, _, ff = w1.shape
    t_super, tb, chunks, tf, pa = MOE_SUPER_BLOCK, MOE_SORT_BLOCK, MOE_CHUNKS, MOE_FF_TILE, PIECE_ALIGN
    ch = max(chunks)
    nsub = t_super // tb
    nsb = r // t_super
    npb = n_prompt_rows // tb
    nf = ff // tf
    n_exp = ne * nf
    loc_rows = 2 * tb + LANES
    assert loc_rows >= 2 * tb + ne * (pa - 1)
    max_rows = 2 * t_super + nsub * ne * (pa - 1) + ch
    sort_rows = LANES * (-(-max_rows // LANES))

    cnt = counts[:, 0, :ne].astype(jnp.int32)
    n16 = (cnt + (pa - 1)) // pa
    loc = pa * (jnp.cumsum(n16, axis=1) - n16)
    n16_sb = n16.reshape(nsb, nsub, ne)
    seg_len = pa * jnp.sum(n16_sb, axis=1)
    seg_off = jnp.cumsum(seg_len, axis=1) - seg_len
    dst = seg_off[:, None, :] + pa * (jnp.cumsum(n16_sb, axis=1) - n16_sb)
    cnum = (seg_len + ch - 1) // ch
    need = pa * ((seg_len // pa + jnp.maximum(cnum, 1) - 1) // jnp.maximum(cnum, 1))
    csel = sum((need > size).astype(jnp.int32) for size in chunks[:-1])
    scalars = [a.reshape(-1).astype(jnp.int32) for a in (n16, loc, dst, seg_off, seg_len, csel, cnum)]

    grp = functools.partial(_group_of_tile, tm=t_super, n_prompt_rows=n_prompt_rows, dec_seq=dec_seq)

    def exp_step(p):
        return jnp.clip(p - nsub, 0, n_exp - 1)

    def tok_blk(sb, s):
        return sb * nsub + jnp.clip(s, 0, nsub - 1)

    grid_spec = pltpu.PrefetchScalarGridSpec(
        num_scalar_prefetch=len(scalars),
        grid=(nsb, nsub + n_exp + nsub),
        in_specs=[
            pl.BlockSpec((tb, d), lambda sb, p, *_: (tok_blk(sb, p), 0)),
            pl.BlockSpec((tb, LANES), lambda sb, p, *_: (sb * nsub + jnp.where(p < nsub, p, jnp.clip(p - nsub - n_exp, 0, nsub - 1)), 0)),
            pl.BlockSpec((tb, d), lambda sb, p, *_: (tok_blk(sb, p - nsub - n_exp), 0)),
            pl.BlockSpec((None, 6, d), lambda sb, p, *_: (grp(sb), 0, 0)),
            pl.BlockSpec((None, d, tf), lambda sb, p, *_: (exp_step(p) // nf, 0, exp_step(p) % nf)),
            pl.BlockSpec((None, d, tf), lambda sb, p, *_: (exp_step(p) // nf, 0, exp_step(p) % nf)),
            pl.BlockSpec((None, tf, d), lambda sb, p, *_: (exp_step(p) // nf, exp_step(p) % nf, 0)),
        ],
        out_specs=(
            pl.BlockSpec((tb, d), lambda sb, p, *_: (jnp.minimum(tok_blk(sb, p - nsub - n_exp), npb - 1), 0)),
            pl.BlockSpec((tb, d), lambda sb, p, *_: (jnp.maximum(tok_blk(sb, p - nsub - n_exp) - npb, 0), 0)),
        ),
        scratch_shapes=[
            pltpu.VMEM((sort_rows, d), bf16),
            pltpu.VMEM((sort_rows, d), f32),
            pltpu.VMEM((loc_rows, d), bf16),
        ],
    )
    return pl.pallas_call(
        functools.partial(_moe_kernel, nsub=nsub, ne=ne, nf=nf, chunks=chunks,
                          nsb_prompt=n_prompt_rows // t_super),
        out_shape=(jax.ShapeDtypeStruct((n_prompt_rows, d), f32), jax.ShapeDtypeStruct((r - n_prompt_rows, d), f32)),
        grid_spec=grid_spec,
        compiler_params=_cparams("arbitrary", "arbitrary"),
        name="moe_sparse",
    )(*scalars, h, meta, x, mod, w1, w3, w2)


def kernel(x_prompt, x_sample, c, cache_dattn_k, cache_dattn_v, state_mlstm_c, state_mlstm_n, state_mlstm_m,
           c_ctx, w_ada, b_ada, w_in_ab, conv_w, conv_b, gate_b, qn_g, kn_g, lam_q1, lam_k1, lam_q2, lam_k2,
           m_norm_g, a_norm_g, w_out_ab, ff_w1, ff_w3, ff_w2, w_in_c, b_in_c, c_ln_g, c_ln_b, c_ws, c_bs,
           w_out_c, w_router, ex_w1, ex_w3, ex_w2):
    bp, seq, d = x_prompt.shape
    bs, dec_seq, _ = x_sample.shape
    depth = w_ada.shape[0]
    n_prompt_rows = bp * seq
    assert n_prompt_rows % dec_seq == 0 and seq % M_CHUNK == 0 and dec_seq % MOE_SUPER_BLOCK == 0
    nh = M_HEADS

    mods = _ada_table(jnp.concatenate([c_ctx[None], c], axis=0), w_ada, b_ada)
    x = (x_prompt.reshape(n_prompt_rows, d), x_sample.reshape(bs * dec_seq, d))
    rows = (n_prompt_rows, dec_seq)

    def joined(v):
        return jnp.concatenate(v, axis=0) if isinstance(v, tuple) else v

    def parts(v):
        return v if isinstance(v, tuple) else (v[:n_prompt_rows], v[n_prompt_rows:])

    new_k, new_v, new_c, new_n, new_m = [], [], [], [], []
    for l in range(depth):
        j = l // 2
        mod = mods[l]
        if l % 2 == 0:
            lam_init = 0.8 - 0.6 * math.exp(-0.3 * l)
            lam = (jnp.exp(jnp.sum((lam_q1[j] * lam_k1[j]).astype(f32)))
                   - jnp.exp(jnp.sum((lam_q2[j] * lam_k2[j]).astype(f32))) + lam_init).reshape(1)
            o3 = 4 * M_W
            o4 = o3 + 4 * nh
            w = w_in_ab[j]
            w_main = jnp.concatenate([w[:, :o3], w[:, o4:]], axis=1).astype(bf16)
            w_gate = jnp.zeros((d, LANES), f32).at[:, :4 * nh].set(w[:, o3:o4]).astype(bf16)
            x = parts(x)
            z, gates = _inproj(*x, mod, w_main, w_gate, *rows)
            gates = gates + jnp.zeros((1, LANES), f32).at[0, :4 * nh].set(gate_b[j])
            mng = m_norm_g[j].reshape(1, M_W)
            cb = conv_b[j].reshape(1, 2 * M_W)
            hm_p, c_f, n_f, m_f = _mlstm(z, gates, conv_w[j], cb, mng, None, batch=bp, t=seq, row_block0=0)
            state0 = (state_mlstm_c[:, j], state_mlstm_n[:, j].reshape(bs, 2, nh, 1, M_DIM),
                      state_mlstm_m[:, j].reshape(bs, 2, nh, 1, 1))
            hm_s, _, _, _ = _mlstm(z, gates, conv_w[j], cb, mng, state0,
                                   batch=bs, t=dec_seq, row_block0=n_prompt_rows // dec_seq)
            new_c.append(c_f)
            new_n.append(n_f.reshape(bp, 2, nh, M_DIM))
            new_m.append(m_f.reshape(bp, 2, nh))

            qg2 = jnp.tile(qn_g[j], 2).reshape(1, LANES)
            kg2 = jnp.tile(kn_g[j], 2).reshape(1, LANES)
            ag = a_norm_g[j].reshape(1, LANES)
            ha_p, k_ctx, v_ctx = _attn(z, lam, qg2, kg2, ag, batch=bp, t=seq, row_block0=0,
                                       out_scale=1.0 - lam_init)
            kct = cache_dattn_k[:, j].transpose(0, 1, 2, 4, 3).reshape(bs, A_HEADS, LANES, -1)
            (ha_s,) = _attn(z, lam, qg2, kg2, ag, batch=bs, t=dec_seq, row_block0=n_prompt_rows // dec_seq,
                            rope_tabs=_rope_tables(dec_seq), ctx_kv=(kct, cache_dattn_v[:, j]),
                            out_scale=1.0 - lam_init)
            new_k.append(k_ctx)
            new_v.append(v_ctx)
            x = _outproj(*x, hm_p, hm_s, ha_p, ha_s, mod, w_out_ab[j].astype(bf16), *rows)
            x = _ffn(x, mod, ff_w1[j].astype(bf16), ff_w3[j].astype(bf16), ff_w2[j].astype(bf16), *rows)
        else:
            e = w_out_c.shape[1]
            bs_full = jnp.repeat(c_bs[j].T, e // C_GROUPS, axis=1)
            x = _gmlp(joined(x), mod, w_in_c[j].astype(bf16), b_in_c[j].reshape(1, -1), c_ln_g[j].reshape(1, -1),
                      c_ln_b[j].reshape(1, -1), c_ws[j].astype(bf16), bs_full, w_out_c[j].astype(bf16), *rows)
            wr = jnp.zeros((d, LANES), f32).at[:, :N_EXPERTS].set(w_router[j])
            wr_hi = wr.astype(bf16)
            wr_lo = (wr - wr_hi.astype(f32)).astype(bf16)
            h, meta, counts = _router(x, mod, wr_hi, wr_lo, *rows)
            x = _moe(x, h, meta, counts, mod, ex_w1[j].astype(bf16), ex_w3[j].astype(bf16),
                     ex_w2[j].astype(bf16), *rows)

    y_prompt, y_sample = parts(x)
    y_prompt = y_prompt.reshape(bp, seq, d)
    y_sample = y_sample.reshape(bs, dec_seq, d)
    return (y_prompt, y_sample, jnp.stack(new_k, axis=1), jnp.stack(new_v, axis=1),
            jnp.stack(new_c, axis=1), jnp.stack(new_n, axis=1), jnp.stack(new_m, axis=1))
```

```python
import functools
import math
from typing import Callable, NamedTuple

import jax
import jax.numpy as jnp
import numpy as np
from jax import lax
from jax.experimental import pallas as pl
from jax.experimental.pallas import tpu as pltpu

f32 = jnp.float32
bf16 = jnp.bfloat16

D_MODEL = 1024
M_HEADS = 4
M_DIM = 128
M_W = M_HEADS * M_DIM
M_CHUNK = 128
A_HEADS = 4
A_VDIM = 128
A_DIM = 64
A_W = A_HEADS * A_VDIM
GRID_W = 64
ROPE_THETA = 10000.0
C_CHUNK = 128
C_GROUPS = 4
N_EXPERTS = 8
EPS = 1e-6

LANES = 128
ROW_TILE = 512
PIECE_ALIGN = 16
MOE_SUPER_BLOCK = 2048
MOE_SORT_BLOCK = 512
MOE_CHUNKS = (128, 192, 224, 256)
MOE_FF_TILE = 896
VMEM_LIMIT = 60 * 1024 * 1024


def _cparams(*sem):
    return pltpu.CompilerParams(dimension_semantics=tuple(sem), vmem_limit_bytes=VMEM_LIMIT)


def _const_spec(shape):
    nd = len(shape)
    return pl.BlockSpec(shape, lambda *_: (0,) * nd, pipeline_mode=pl.Buffered(1))


def _sigmoid(x):
    return 1.0 / (1.0 + jnp.exp(-x))


def _silu(x):
    return x * _sigmoid(x)


def _log_sigmoid(x):
    return jnp.minimum(x, 0.0) - jnp.log(1.0 + jnp.exp(-jnp.abs(x)))


def _rms(x):
    return x * lax.rsqrt(jnp.mean(x * x, axis=-1, keepdims=True) + EPS)


def _modulate(x, mod_ref, first):
    shift = mod_ref[first:first + 1, :]
    scale = mod_ref[first + 1:first + 2, :]
    return _rms(x) * (1.0 + scale) + shift


def _dot(a, b):
    return jnp.dot(a, b, preferred_element_type=f32)


def _dot_nt(a, b):
    return lax.dot_general(a, b, (((1,), (1,)), ((), ())), preferred_element_type=f32)


def _dot_tn(a, b):
    return lax.dot_general(a, b, (((0,), (0,)), ((), ())), preferred_element_type=f32)


def _split3(x):
    hi = x.astype(bf16)
    r1 = x - hi.astype(f32)
    mid = r1.astype(bf16)
    lo = (r1 - mid.astype(f32)).astype(bf16)
    return hi, mid, lo


def _group_of_tile(i, tm, n_prompt_rows, dec_seq):
    pt = n_prompt_rows // tm
    return jnp.where(i < pt, 0, 1 + (i - pt) // (dec_seq // tm))


def _ada_kernel(cv_ref, w_ref, b_ref, o_ref):
    a = _silu(cv_ref[...]).astype(bf16)
    o_ref[...] = _dot(a, w_ref[...].astype(bf16)) + b_ref[...]


def _ada_table(cv, w_ada, b_ada):
    depth, d, n = w_ada.shape
    g = cv.shape[0]
    gp = 8 * ((g + 7) // 8)
    cvp = jnp.zeros((gp, d), f32).at[:g].set(cv)
    tn = 1536
    out = pl.pallas_call(
        _ada_kernel,
        out_shape=jax.ShapeDtypeStruct((depth, gp, n), f32),
        grid=(depth, n // tn),
        in_specs=[
            pl.BlockSpec((gp, d), lambda l, j: (0, 0)),
            pl.BlockSpec((None, d, tn), lambda l, j: (l, 0, j)),
            pl.BlockSpec((None, 1, tn), lambda l, j: (l, 0, j)),
        ],
        out_specs=pl.BlockSpec((None, gp, tn), lambda l, j: (l, 0, j)),
        compiler_params=_cparams("arbitrary", "arbitrary"),
        name="ada_table",
    )(cvp, w_ada, b_ada.reshape(depth, 1, n))
    return out[:, :g].reshape(depth, g, 6, d)


def _part_specs(tm, width, npt):
    first = pl.BlockSpec((tm, width), lambda i: (jnp.minimum(i, npt - 1), 0))
    second = pl.BlockSpec((tm, width), lambda i: (jnp.maximum(i - npt, 0), 0))
    return first, second


def _pick(a_ref, b_ref, npt):
    return jnp.where(pl.program_id(0) < npt, a_ref[...], b_ref[...])


class _CastJob(NamedTuple):
    src: jax.Array
    in_spec: pl.BlockSpec
    out_spec: pl.BlockSpec
    out_shape: jax.ShapeDtypeStruct
    body: Callable


def _cast_job_tiled(src, steps, tf):
    ne, d, ff = src.shape
    bands = steps // ne
    rows = d // bands
    nf = ff // tf

    def body(i_ref, o_ref):
        for f in range(nf):
            o_ref[f] = i_ref[:, f * tf:(f + 1) * tf].astype(bf16)

    return _CastJob(src, pl.BlockSpec((None, rows, ff), lambda i: (i // bands, i % bands, 0)),
                    pl.BlockSpec((None, nf, rows, tf), lambda i: (i // bands, 0, i % bands, 0)),
                    jax.ShapeDtypeStruct((ne, nf, d, tf), bf16), body)


def _cast_job_rows(src, steps):
    ne, ff, d = src.shape
    rows = ne * ff // steps

    def body(i_ref, o_ref):
        o_ref[...] = i_ref[...].astype(bf16)

    return _CastJob(src.reshape(ne * ff, d), pl.BlockSpec((rows, d), lambda i: (i, 0)),
                    pl.BlockSpec((rows, d), lambda i: (i, 0)), jax.ShapeDtypeStruct((ne * ff, d), bf16), body)


def _call_with_job(kernel_fn, job, *, out_shape, in_specs, out_specs, args, **kw):
    if job is None:
        return pl.pallas_call(kernel_fn, out_shape=out_shape, in_specs=in_specs, out_specs=out_specs, **kw)(*args)
    n_in, n_out = len(in_specs), len(out_shape)

    def with_job(*refs):
        kernel_fn(*refs[:n_in], *refs[n_in + 1:n_in + 1 + n_out], *refs[n_in + 2 + n_out:])
        job.body(refs[n_in], refs[n_in + 1 + n_out])

    return pl.pallas_call(with_job, out_shape=(*out_shape, job.out_shape), in_specs=[*in_specs, job.in_spec],
                          out_specs=(*out_specs, job.out_spec), **kw)(*args, job.src)


def _inproj_kernel(xp_ref, xs_ref, mod_ref, w_ref, wg_ref, z_ref, g_ref, *, n_main, npt):
    h = _modulate(_pick(xp_ref, xs_ref, npt), mod_ref, 0).astype(bf16)
    step = 512
    for j in range(n_main // step):
        z_ref[:, j * step:(j + 1) * step] = _dot(h, w_ref[:, j * step:(j + 1) * step]).astype(bf16)
    g_ref[...] = _dot(h, wg_ref[...])


def _inproj(xp, xs, mod, w_main, w_gate, n_prompt_rows, dec_seq, job=None):
    d = xp.shape[1]
    r = xp.shape[0] + xs.shape[0]
    tm = ROW_TILE
    npt = n_prompt_rows // tm
    n_main = w_main.shape[1]
    grp = functools.partial(_group_of_tile, tm=tm, n_prompt_rows=n_prompt_rows, dec_seq=dec_seq)
    return _call_with_job(
        functools.partial(_inproj_kernel, n_main=n_main, npt=npt), job(r // tm) if job else None,
        out_shape=(jax.ShapeDtypeStruct((r, n_main), bf16), jax.ShapeDtypeStruct((r, LANES), f32)),
        grid=(r // tm,),
        in_specs=[
            *_part_specs(tm, d, npt),
            pl.BlockSpec((None, 6, d), lambda i: (grp(i), 0, 0)),
            _const_spec(w_main.shape),
            _const_spec(w_gate.shape),
        ],
        out_specs=(pl.BlockSpec((tm, n_main), lambda i: (i, 0)), pl.BlockSpec((tm, LANES), lambda i: (i, 0))),
        args=(xp, xs, mod, w_main, w_gate),
        compiler_params=_cparams("arbitrary"),
        name="inproj_ab",
    )


def _conv_silu(x, w_ref, b_ref, t):
    row = lax.broadcasted_iota(jnp.int32, (t, 1), 0)
    prev = jnp.where(row == 0, 0.0, pltpu.roll(x, 1, 0))
    nxt = jnp.where(row == t - 1, 0.0, pltpu.roll(x, t - 1, 0))
    y = b_ref[...] + prev * w_ref[0:1, :] + x * w_ref[1:2, :] + nxt * w_ref[2:3, :]
    return _silu(y)


def _mlstm_kernel(q_ref, k_ref, v_ref, o_ref, g_ref, cw_ref, cb_ref, ng_ref, *rest, t, has_state):
    if has_state:
        c0_ref, n0_ref, m0_ref, *rest = rest
    hm_ref, c_out, n_out, m_out, q_s, qt_s, kh_s, kl_s, vt_s, gs_s, hft_s, hbt_s, ct_s, sel_s = rest
    nc = t // M_CHUNK
    lc = M_CHUNK
    nh = M_HEADS
    w = M_W
    q_s[...] = _conv_silu(q_ref[...].astype(f32), cw_ref[:, :w], cb_ref[:, :w], t)
    kf = _conv_silu(k_ref[...].astype(f32), cw_ref[:, w:], cb_ref[:, w:], t) * (M_DIM ** -0.5)
    kh = kf.astype(bf16)
    kh_s[...] = kh
    kl_s[...] = (kf - kh.astype(f32)).astype(bf16)
    sel_row = lax.broadcasted_iota(jnp.int32, (3 * LANES, LANES), 0)
    for d in range(2):
        for h in range(nh):
            ci, ln = d * nh + h, 2 * nh * d + h
            ct_s[ci] = c0_ref[d, h].T if has_state else jnp.zeros((M_DIM, M_DIM), f32)
            sel_s[ci] = jnp.where(sel_row == ln, 1.0, jnp.where(sel_row == LANES + ln, 1.0, jnp.where(
                sel_row == 2 * LANES + ln, 1.0, 0.0))).astype(bf16)
    if has_state:
        n_out[...] = n0_ref[...]
        m_out[...] = m0_ref[...]
    else:
        n_out[...] = jnp.zeros_like(n_out)
        m_out[...] = jnp.zeros_like(m_out)

    lane = lax.broadcasted_iota(jnp.int32, (1, LANES), 1)
    is_lf = ((lane >= nh) & (lane < 2 * nh)) | ((lane >= 3 * nh) & (lane < 4 * nh))
    is_bw_lf = (lane >= 3 * nh) & (lane < 4 * nh)
    g = g_ref[...]
    gs_s[...] = jnp.where(is_lf, _log_sigmoid(g), g)

    rr = lax.broadcasted_iota(jnp.int32, (lc, lc), 0)
    cc = lax.broadcasted_iota(jnp.int32, (lc, lc), 1)
    tri_incl = jnp.where(rr >= cc, 1.0, 0.0).astype(bf16)

    def prep_body(c, _):
        off = pl.multiple_of(c * lc, lc)
        tile = gs_s[pl.ds(off, lc), :]
        hi, mid, lo = _split3(jnp.where(is_lf, tile, 0.0))
        cs = _dot(tri_incl, hi) + _dot(tri_incl, mid) + _dot(tri_incl, lo)
        total = jnp.broadcast_to(cs[lc - 1:lc, :], (lc, LANES))
        b = jnp.where(is_bw_lf, total - cs + tile, cs)
        lmb = tile - pltpu.roll(b, LANES - nh, 1)
        gv = pltpu.roll(total, LANES - nh, 1) + lmb
        low = jnp.where(is_lf, b, lmb)
        high = pltpu.roll(jnp.where(is_lf, total, gv), 4 * nh, 1)
        gs_s[pl.ds(off, lc), :] = jnp.where(lane < 4 * nh, low, jnp.where(lane < 8 * nh, high, 0.0))
        wide = pl.ds(pl.multiple_of(c * w, w), w)
        qt_s[wide, :] = q_s[pl.ds(off, lc), :].T
        vt_s[wide, :] = v_ref[pl.ds(off, lc), :].astype(f32).T.astype(bf16)
        return 0

    lax.fori_loop(0, nc, prep_body, 0)

    sub8 = lax.broadcasted_iota(jnp.int32, (8, LANES), 0)

    def two_rows(x):
        hi = x.astype(bf16).astype(f32)
        return jnp.where(sub8 == 0, hi, jnp.where(sub8 == 1, x - hi, 0.0)).astype(bf16)

    def issue(c, h, d, tile3, tile_t):
        off = pl.multiple_of(c * lc, lc)
        col = slice(h * M_DIM, (h + 1) * M_DIM)
        ci, ln = d * nh + h, 2 * nh * d + h
        head = pl.ds(pl.multiple_of(c * w + h * M_DIM, M_DIM), M_DIM)
        b_row = tile_t[ln + nh:ln + nh + 1, :]
        g_row = tile_t[ln + 4 * nh:ln + 4 * nh + 1, :]
        total = tile_t[ln + 5 * nh:ln + 5 * nh + 1, 0:1]
        ct, nm, mm = ct_s[ci], n_out[d, h], m_out[d, h]
        qt = qt_s[head, :]
        qtb = qt.astype(bf16)
        khb = kh_s[pl.ds(off, lc), col]
        vtb = vt_s[head, :]
        lmb = _dot(tile3, sel_s[ci])
        sraw = _dot(khb, qtb)
        qn = _dot(two_rows(nm), qtb)
        lhs = jnp.concatenate([ct.astype(bf16), vtb], axis=1)
        m_new = jnp.maximum(mm + total, jnp.max(g_row, axis=1, keepdims=True))
        decay = jnp.exp(mm + total - m_new)
        ew = jnp.exp(g_row - m_new)
        ct_s[ci] = decay * ct + _dot(vtb * ew.astype(bf16), khb)
        ew2 = two_rows(ew)
        nk_h = _dot(ew2, khb)
        nk_l = _dot(ew2, kl_s[pl.ds(off, lc), col])
        n_out[d, h] = decay * nm + (nk_h[0:1, :] + nk_h[1:2, :] + nk_l[0:1, :])
        m_out[d, h] = m_new
        return dict(d=d, head=head, b_row=b_row, mm=mm, qt=qt, lmb=lmb, sraw=sraw, qn=qn, lhs=lhs)

    def weigh(st):
        keep = (cc >= rr) if st["d"] == 0 else (rr >= cc)
        dmat = jnp.where(keep, st["lmb"] + st["b_row"], -jnp.inf)
        inter = st["mm"] + st["b_row"]
        mt = jnp.maximum(inter, jnp.max(dmat, axis=0, keepdims=True))
        w_inter = jnp.exp(inter - mt)
        s = st["sraw"] * jnp.exp(dmat - mt)
        qn = st["qn"]
        den = w_inter * (qn[0:1, :] + qn[1:2, :]) + jnp.sum(s, axis=0, keepdims=True)
        inv = 1.0 / jnp.maximum(jnp.abs(den), jnp.exp(-mt))
        return jnp.concatenate([st["qt"] * (w_inter * inv), s * inv], axis=0).astype(bf16)

    def body(i, _):
        states = []
        for d in range(2):
            c = i if d == 0 else nc - 1 - i
            tile = gs_s[pl.ds(pl.multiple_of(c * lc, lc), lc), :]
            tile_t = tile.T
            hi, mid, lo = _split3(tile)
            tile3 = jnp.concatenate([hi, mid, lo], axis=1)
            states += [issue(c, h, d, tile3, tile_t) for h in range(nh)]
        rhss = [weigh(st) for st in states]
        for st, rhs in zip(states, rhss):
            hct = _dot(st["lhs"], rhs)
            if st["d"] == 0:
                hft_s[st["head"], :] = hct
            else:
                hbt_s[st["head"], :] = hct
        return 0

    lax.fori_loop(0, nc, body, 0)

    def out_body(c, _):
        rows = pl.ds(pl.multiple_of(c * lc, lc), lc)
        wide = pl.ds(pl.multiple_of(c * w, w), w)
        hsum = (hft_s[wide, :] + hbt_s[wide, :]).T
        for h in range(nh):
            col = slice(h * M_DIM, (h + 1) * M_DIM)
            mo = _sigmoid(o_ref[rows, col].astype(f32))
            hm_ref[rows, col] = (_rms(hsum[:, col]) * ng_ref[:, col] * mo).astype(hm_ref.dtype)
        return 0

    lax.fori_loop(0, nc, out_body, 0)
    for d in range(2):
        for h in range(nh):
            c_out[d, h] = ct_s[d * nh + h].T


def _mlstm(z, gates, conv_w, conv_b, m_norm_g, state0, *, batch, t, row_block0):
    nh = M_HEADS
    w = M_W
    mode = dict(pipeline_mode=pl.Buffered(1)) if t * w * 2 > (1 << 20) else {}
    seq = lambda colblk: pl.BlockSpec((t, w), lambda b: (row_block0 + b, colblk), **mode)
    state = lambda *tail: pl.BlockSpec((None, 2, nh) + tail, lambda b: (b,) + (0,) * (2 + len(tail)))
    state_specs = [state(M_DIM, M_DIM), state(1, M_DIM), state(1, 1)]
    has_state = state0 is not None
    return pl.pallas_call(
        functools.partial(_mlstm_kernel, t=t, has_state=has_state),
        out_shape=(
            jax.ShapeDtypeStruct((batch * t, w), bf16),
            jax.ShapeDtypeStruct((batch, 2, nh, M_DIM, M_DIM), f32),
            jax.ShapeDtypeStruct((batch, 2, nh, 1, M_DIM), f32),
            jax.ShapeDtypeStruct((batch, 2, nh, 1, 1), f32),
        ),
        grid=(batch,),
        in_specs=[
            seq(0), seq(1), seq(2), seq(3),
            pl.BlockSpec((t, LANES), lambda b: (row_block0 + b, 0)),
            _const_spec(conv_w.shape), _const_spec(conv_b.shape), _const_spec(m_norm_g.shape),
            *(state_specs if has_state else []),
        ],
        out_specs=(pl.BlockSpec((t, w), lambda b: (b, 0)), *state_specs),
        scratch_shapes=[
            pltpu.VMEM((t, w), f32),
            pltpu.VMEM((t * nh, M_DIM), f32),
            pltpu.VMEM((t, w), bf16),
            pltpu.VMEM((t, w), bf16),
            pltpu.VMEM((t * nh, M_DIM), bf16),
            pltpu.VMEM((t, LANES), f32),
            pltpu.VMEM((t * nh, M_DIM), f32),
            pltpu.VMEM((t * nh, M_DIM), f32),
            pltpu.VMEM((2 * nh, M_DIM, M_DIM), f32),
            pltpu.VMEM((2 * nh, 3 * LANES, LANES), bf16),
        ],
        compiler_params=_cparams("arbitrary"),
        name=f"mlstm_t{t}",
    )(z, z, z, z, gates, conv_w, conv_b, m_norm_g, *(state0 if has_state else ()))


def _pair_norm(x, gain):
    lane = lax.broadcasted_iota(jnp.int32, (1, LANES), 1)
    first = lane < A_DIM
    sq = x * x
    s_all = jnp.sum(sq, axis=-1, keepdims=True)
    s0 = jnp.sum(jnp.where(first, sq, 0.0), axis=-1, keepdims=True)
    inv0 = lax.rsqrt(s0 * (1.0 / A_DIM) + EPS)
    inv1 = lax.rsqrt((s_all - s0) * (1.0 / A_DIM) + EPS)
    return x * jnp.where(first, inv0, inv1) * gain


def _rope(x, cos, sin_signed):
    lane = lax.broadcasted_iota(jnp.int32, (1, LANES), 1)
    nf = A_DIM // 4
    partner = jnp.where((lane % (2 * nf)) < nf, pltpu.roll(x, LANES - nf, 1), pltpu.roll(x, nf, 1))
    return x * cos + partner * sin_signed


def _attn_kernel(*refs, t, tq, rope, ctx, out_scale):
    it = iter(refs)
    lam_ref = next(it)
    q_ref, k_ref, v_ref = next(it), next(it), next(it)
    qg_ref, kg_ref, ag_ref = next(it), next(it), next(it)
    if rope:
        cosq_ref, sinq_ref, cosk_ref, sink_ref = next(it), next(it), next(it), next(it)
    if ctx:
        kct_ref, vc_ref = next(it), next(it)
    ha_ref = next(it)
    if not ctx:
        newk_ref, newv_ref = next(it), next(it)
    kt_s = next(it)
    nh = A_HEADS

    @pl.when(pl.program_id(1) == 0)
    def _():
        for h in range(nh):
            col = slice(h * LANES, (h + 1) * LANES)
            kn = _pair_norm(k_ref[:, col].astype(f32), kg_ref[...])
            if not ctx:
                newk_ref[h, 0] = kn[:, :A_DIM]
                newk_ref[h, 1] = kn[:, A_DIM:]
                newv_ref[h] = v_ref[:, col].astype(f32)
            if rope:
                kn = _rope(kn, cosk_ref[...], sink_ref[...])
            kt_s[col, :] = kn.T.astype(bf16)

    lane = lax.broadcasted_iota(jnp.int32, (1, LANES), 1)
    lam = lam_ref[0]
    for h in range(nh):
        col = slice(h * LANES, (h + 1) * LANES)
        q = _pair_norm(q_ref[:, col].astype(f32), qg_ref[...])
        if rope:
            q = _rope(q, cosq_ref[...], sinq_ref[...])
        q = q * (A_DIM ** -0.5)
        qs = [jnp.where(lane < A_DIM, q, 0.0).astype(bf16), jnp.where(lane >= A_DIM, q, 0.0).astype(bf16)]
        kt = kt_s[col, :]
        vb = v_ref[:, col]
        if ctx:
            kctb = kct_ref[h].astype(bf16)
            vcb = vc_ref[h].astype(bf16)
        outs = []
        for i in range(2):
            sn = _dot(qs[i], kt)
            mx = jnp.max(sn, axis=-1, keepdims=True)
            if ctx:
                sc = _dot(qs[i], kctb)
                mx = jnp.maximum(mx, jnp.max(sc, axis=-1, keepdims=True))
            en = jnp.exp(sn - mx)
            den = jnp.sum(en, axis=-1, keepdims=True)
            o = _dot(en.astype(bf16), vb)
            if ctx:
                ec = jnp.exp(sc - mx)
                den = den + jnp.sum(ec, axis=-1, keepdims=True)
                o = o + _dot(ec.astype(bf16), vcb)
            outs.append(o * (1.0 / den))
        o = outs[0] - lam * outs[1]
        ha_ref[:, col] = (_rms(o) * ag_ref[...] * out_scale).astype(ha_ref.dtype)


def _attn(z, lam, qg2, kg2, a_norm_g, *, batch, t, row_block0, rope_tabs=None, ctx_kv=None, out_scale):
    nh = A_HEADS
    w = A_W
    tq = min(t, 256)
    nq = t // tq
    rope = rope_tabs is not None
    ctx = ctx_kv is not None
    qblk = 4 * M_W // w
    vec = pl.BlockSpec((1, LANES), lambda b, i: (0, 0))
    in_specs = [
        pl.BlockSpec(memory_space=pltpu.SMEM),
        pl.BlockSpec((tq, w), lambda b, i: ((row_block0 + b) * nq + i, qblk)),
        pl.BlockSpec((t, w), lambda b, i: (row_block0 + b, qblk + 1)),
        pl.BlockSpec((t, w), lambda b, i: (row_block0 + b, qblk + 2)),
        vec, vec, vec,
    ]
    args = [lam, z, z, z, qg2, kg2, a_norm_g]
    if rope:
        cos, sin = rope_tabs
        in_specs += [pl.BlockSpec((tq, LANES), lambda b, i: (i, 0))] * 2
        in_specs += [pl.BlockSpec((t, LANES), lambda b, i: (0, 0))] * 2
        args += [cos, sin, cos, sin]
    if ctx:
        kct, vc = ctx_kv
        in_specs += [pl.BlockSpec((None,) + kct.shape[1:], lambda b, i: (b, 0, 0, 0)),
                     pl.BlockSpec((None,) + vc.shape[1:], lambda b, i: (b, 0, 0, 0))]
        args += [kct, vc]
    out_shape = [jax.ShapeDtypeStruct((batch * t, w), bf16)]
    out_specs = [pl.BlockSpec((tq, w), lambda b, i: (b * nq + i, 0))]
    if not ctx:
        out_shape += [jax.ShapeDtypeStruct((batch, nh, 2, t, A_DIM), f32),
                      jax.ShapeDtypeStruct((batch, nh, t, A_VDIM), f32)]
        out_specs += [pl.BlockSpec((None, nh, 2, t, A_DIM), lambda b, i: (b, 0, 0, 0, 0)),
                      pl.BlockSpec((None, nh, t, A_VDIM), lambda b, i: (b, 0, 0, 0))]
    return pl.pallas_call(
        functools.partial(_attn_kernel, t=t, tq=tq, rope=rope, ctx=ctx, out_scale=out_scale),
        out_shape=tuple(out_shape),
        grid=(batch, nq),
        in_specs=in_specs,
        out_specs=tuple(out_specs),
        scratch_shapes=[pltpu.VMEM((w, t), bf16)],
        compiler_params=_cparams("arbitrary", "arbitrary"),
        name=f"diff_attn_t{t}",
    )(*args)


def _rope_tables(t):
    rows = t // GRID_W
    pos_row = np.repeat(np.arange(rows, dtype=np.float32), GRID_W)
    pos_col = (np.arange(rows * GRID_W) % GRID_W).astype(np.float32)
    nf = A_DIM // 4
    inv = (ROPE_THETA ** (-jnp.arange(nf, dtype=f32) / nf))
    lane = np.arange(LANES)
    j = lane % (2 * nf)
    use_col = (lane % A_DIM) >= (A_DIM // 2)
    pos = jnp.where(jnp.asarray(use_col)[None, :], jnp.asarray(pos_col)[:, None], jnp.asarray(pos_row)[:, None])
    ang = pos * inv[jnp.asarray(j % nf)][None, :]
    sign = jnp.asarray(np.where(j < nf, -1.0, 1.0).astype(np.float32))[None, :]
    return jnp.cos(ang), jnp.sin(ang) * sign


def _outproj_kernel(xp_ref, xs_ref, hmp_ref, hms_ref, hap_ref, has_ref, mod_ref, w_ref, o_ref, *, npt):
    half = hmp_ref.shape[1]
    hm = _pick(hmp_ref, hms_ref, npt)
    ha = _pick(hap_ref, has_ref, npt)
    y = _dot(hm, w_ref[:half, :]) + _dot(ha, w_ref[half:, :])
    o_ref[...] = _pick(xp_ref, xs_ref, npt) + mod_ref[2:3, :] * y


def _outproj(xp, xs, hm_p, hm_s, ha_p, ha_s, mod, w, n_prompt_rows, dec_seq):
    d = xp.shape[1]
    r = xp.shape[0] + xs.shape[0]
    tm = ROW_TILE
    npt = n_prompt_rows // tm
    grp = functools.partial(_group_of_tile, tm=tm, n_prompt_rows=n_prompt_rows, dec_seq=dec_seq)
    return pl.pallas_call(
        functools.partial(_outproj_kernel, npt=npt),
        out_shape=jax.ShapeDtypeStruct((r, d), f32),
        grid=(r // tm,),
        in_specs=[
            *_part_specs(tm, d, npt),
            *_part_specs(tm, hm_p.shape[1], npt),
            *_part_specs(tm, ha_p.shape[1], npt),
            pl.BlockSpec((None, 6, d), lambda i: (grp(i), 0, 0)),
            _const_spec(w.shape),
        ],
        out_specs=pl.BlockSpec((tm, d), lambda i: (i, 0)),
        compiler_params=_cparams("arbitrary"),
        name="outproj_ab",
    )(xp, xs, hm_p, hm_s, ha_p, ha_s, mod, w)


def _ffn_kernel(x_ref, mod_ref, w1_ref, w3_ref, w2_ref, o_ref, *, chunk):
    x = x_ref[...]
    h = _modulate(x, mod_ref, 3).astype(bf16)
    ff = w1_ref.shape[1]
    acc = jnp.zeros(x.shape, f32)
    for j in range(ff // chunk):
        sl = slice(j * chunk, (j + 1) * chunk)
        a = _silu(_dot(h, w1_ref[:, sl])) * _dot(h, w3_ref[:, sl])
        acc = acc + _dot(a.astype(bf16), w2_ref[sl, :])
    o_ref[...] = x + mod_ref[5:6, :] * acc


def _ffn(x, mod, w1, w3, w2, n_prompt_rows, dec_seq, job=None):
    r, d = x.shape
    tm = ROW_TILE
    grp = functools.partial(_group_of_tile, tm=tm, n_prompt_rows=n_prompt_rows, dec_seq=dec_seq)
    ff = w1.shape[1]
    chunk = ff // 2 if (ff // 2) % LANES == 0 else ff
    return _call_with_job(
        functools.partial(_ffn_kernel, chunk=chunk), job(r // tm) if job else None,
        out_shape=(jax.ShapeDtypeStruct((r, d), f32),),
        grid=(r // tm,),
        in_specs=[
            pl.BlockSpec((tm, d), lambda i: (i, 0)),
            pl.BlockSpec((None, 6, d), lambda i: (grp(i), 0, 0)),
            _const_spec(w1.shape), _const_spec(w3.shape), _const_spec(w2.shape),
        ],
        out_specs=(pl.BlockSpec((tm, d), lambda i: (i, 0)),),
        args=(x, mod, w1, w3, w2),
        compiler_params=_cparams("arbitrary"),
        name="ffn_dense",
    )


def _gelu_tanh(x):
    return 0.5 * x * (1.0 + jnp.tanh(math.sqrt(2.0 / math.pi) * (x + 0.044715 * (x * x * x))))


def _gmlp_kernel(x_ref, mod_ref, win_ref, bin_ref, lng_ref, lnb_ref, ws_ref, bs_ref, wout_ref, o_ref, us_s):
    x = x_ref[...]
    tm = x.shape[0]
    e = wout_ref.shape[0]
    ge = e // C_GROUPS
    h = _modulate(x, mod_ref, 0).astype(bf16)
    u = _gelu_tanh(_dot(h, win_ref[:, :e]) + bin_ref[:, :e])
    v = _gelu_tanh(_dot(h, win_ref[:, e:]) + bin_ref[:, e:])
    mu = jnp.mean(v, axis=-1, keepdims=True)
    vc = v - mu
    var = jnp.mean(vc * vc, axis=-1, keepdims=True)
    vn = (vc * lax.rsqrt(var + EPS) * lng_ref[...] + lnb_ref[...]).astype(bf16)
    for n in range(tm // C_CHUNK):
        rows = slice(n * C_CHUNK, (n + 1) * C_CHUNK)
        for g in range(C_GROUPS):
            cols = slice(g * ge, (g + 1) * ge)
            s = _dot(ws_ref[g], vn[rows, cols]) + bs_ref[:, cols]
            us_s[rows, cols] = (u[rows, cols] * s).astype(bf16)
    o_ref[...] = x + mod_ref[2:3, :] * _dot(us_s[...], wout_ref[...])


def _gmlp(x, mod, w_in, b_in, ln_g, ln_b, ws, bs_full, w_out, n_prompt_rows, dec_seq, job=None):
    r, d = x.shape
    tm = ROW_TILE
    e = w_out.shape[0]
    grp = functools.partial(_group_of_tile, tm=tm, n_prompt_rows=n_prompt_rows, dec_seq=dec_seq)
    return _call_with_job(
        _gmlp_kernel, job(r // tm) if job else None,
        out_shape=(jax.ShapeDtypeStruct((r, d), f32),),
        grid=(r // tm,),
        in_specs=[
            pl.BlockSpec((tm, d), lambda i: (i, 0)),
            pl.BlockSpec((None, 6, d), lambda i: (grp(i), 0, 0)),
            _const_spec(w_in.shape), _const_spec(b_in.shape), _const_spec(ln_g.shape), _const_spec(ln_b.shape),
            _const_spec(ws.shape), _const_spec(bs_full.shape), _const_spec(w_out.shape),
        ],
        out_specs=(pl.BlockSpec((tm, d), lambda i: (i, 0)),),
        args=(x, mod, w_in, b_in, ln_g, ln_b, ws, bs_full, w_out),
        scratch_shapes=[pltpu.VMEM((tm, e), bf16)],
        compiler_params=_cparams("arbitrary"),
        name="gmlp",
    )


def _router_kernel(x_ref, mod_ref, wr_hi_ref, wr_lo_ref, h_ref, meta_ref, counts_ref):
    hf = _modulate(x_ref[...], mod_ref, 3)
    hb = hf.astype(bf16)
    h_ref[...] = hb
    h_lo = (hf - hb.astype(f32)).astype(bf16)
    logits = _dot(hb, wr_hi_ref[...]) + (_dot(hb, wr_lo_ref[...]) + _dot(h_lo, wr_hi_ref[...]))
    lane = lax.broadcasted_iota(jnp.int32, logits.shape, 1).astype(f32)
    logits = jnp.where(lane < N_EXPERTS, logits, -jnp.inf)
    m1 = jnp.max(logits, axis=-1, keepdims=True)
    i1 = jnp.min(jnp.where(logits == m1, lane, float(LANES)), axis=-1, keepdims=True)
    rest = jnp.where(lane == i1, -jnp.inf, logits)
    m2 = jnp.max(rest, axis=-1, keepdims=True)
    i2 = jnp.min(jnp.where(rest == m2, lane, float(LANES)), axis=-1, keepdims=True)
    e2 = jnp.exp(m2 - m1)
    w1 = 1.0 / (1.0 + e2)
    w2 = e2 * w1

    tm = logits.shape[0]
    cnt = jnp.where(lane == i1, 1.0, jnp.where(lane == i2, 1.0, 0.0))
    rr = lax.broadcasted_iota(jnp.int32, (tm, tm), 0)
    cc = lax.broadcasted_iota(jnp.int32, (tm, tm), 1)
    before = jnp.where(rr > cc, 1.0, 0.0).astype(bf16)
    rank = _dot(before, cnt.astype(bf16))
    counts = jnp.sum(cnt, axis=0, keepdims=True)
    padded = jnp.floor((counts + (PIECE_ALIGN - 1)) * (1.0 / PIECE_ALIGN)) * PIECE_ALIGN
    lane1 = lane[0:1, :]
    piece_off = jnp.zeros((1, LANES), f32)
    off = jnp.zeros((1, 1), f32)
    for e in range(N_EXPERTS):
        piece_off = jnp.where(lane1 == e, off, piece_off)
        off = off + padded[:, e:e + 1]
    local = piece_off + rank
    pos1 = jnp.sum(jnp.where(lane == i1, local, 0.0), axis=-1, keepdims=True)
    pos2 = jnp.sum(jnp.where(lane == i2, local, 0.0), axis=-1, keepdims=True)
    meta_ref[...] = jnp.where(lane == 0, pos1, jnp.where(lane == 1, pos2, jnp.where(lane == 2, w1,
                              jnp.where(lane == 3, w2, 0.0))))
    counts_ref[...] = counts


def _router(x, mod, wr_hi, wr_lo, n_prompt_rows, dec_seq):
    r, d = x.shape
    tm = MOE_SORT_BLOCK
    grp = functools.partial(_group_of_tile, tm=tm, n_prompt_rows=n_prompt_rows, dec_seq=dec_seq)
    return pl.pallas_call(
        _router_kernel,
        out_shape=(jax.ShapeDtypeStruct((r, d), bf16), jax.ShapeDtypeStruct((r, LANES), f32),
                   jax.ShapeDtypeStruct((r // tm, 1, LANES), f32)),
        grid=(r // tm,),
        in_specs=[
            pl.BlockSpec((tm, d), lambda i: (i, 0)),
            pl.BlockSpec((None, 6, d), lambda i: (grp(i), 0, 0)),
            _const_spec(wr_hi.shape), _const_spec(wr_lo.shape),
        ],
        out_specs=(pl.BlockSpec((tm, d), lambda i: (i, 0)), pl.BlockSpec((tm, LANES), lambda i: (i, 0)),
                   pl.BlockSpec((None, 1, LANES), lambda i: (i, 0, 0))),
        compiler_params=_cparams("arbitrary"),
        name="router",
    )(x, mod, wr_hi, wr_lo)


def _moe_kernel(n16_ref, loc_ref, dst_ref, seg_off_ref, seg_len_ref, csel_ref, cnum_ref,
                h_ref, meta_ref, x_ref, mod_ref, w1_ref, w3_ref, w2_ref, op_ref, os_ref,
                hs_s, ys_s, loc_s, *, nsub, ne, nf, chunks, nsb_prompt):
    sb = pl.program_id(0)
    p = pl.program_id(1)
    n_exp = ne * nf
    loc_rows = loc_s.shape[0]
    pa = PIECE_ALIGN

    def one_hot_cols(meta, v1, v2):
        lane = lax.broadcasted_iota(jnp.int32, (1, loc_rows), 1).astype(f32)
        return jnp.where(lane == meta[:, 0:1], v1, jnp.where(lane == meta[:, 1:2], v2, 0.0)).astype(bf16)

    def copy_pieces(blk, to_sorted):
        for e in range(ne):
            n = n16_ref[blk * ne + e]
            src = loc_ref[blk * ne + e]
            dst = dst_ref[blk * ne + e]

            def cp(i, _, src=src, dst=dst):
                a = pl.ds(pl.multiple_of(src + pa * i, pa), pa)
                b = pl.ds(pl.multiple_of(dst + pa * i, pa), pa)
                if to_sorted:
                    hs_s[b, :] = loc_s[a, :]
                else:
                    loc_s[a, :] = ys_s[b, :].astype(bf16)
                return 0

            lax.fori_loop(0, n, cp, 0)

    @pl.when(p < nsub)
    def _():
        @pl.when(p == 0)
        def _():
            hs_s[...] = jnp.zeros_like(hs_s)
            ys_s[...] = jnp.zeros_like(ys_s)

        pt = one_hot_cols(meta_ref[...], 1.0, 1.0)
        loc_s[...] = _dot_tn(pt, h_ref[...]).astype(bf16)
        copy_pieces(sb * nsub + p, True)

    @pl.when((p >= nsub) & (p < nsub + n_exp))
    def _():
        e = lax.div(p - nsub, jnp.int32(nf))
        start = seg_off_ref[sb * ne + e]
        ln = seg_len_ref[sb * ne + e]
        which = csel_ref[sb * ne + e]
        count = cnum_ref[sb * ne + e]

        def chunk(r0, valid, size):
            rows = pl.ds(pl.multiple_of(r0, pa), size)
            xc = hs_s[rows, :]
            a = _silu(_dot(xc, w1_ref[...])) * _dot(xc, w3_ref[...])
            y = _dot(a.astype(bf16), w2_ref[...])
            ri = lax.broadcasted_iota(jnp.int32, (size, 1), 0)
            ys_s[rows, :] += jnp.where(ri < valid, y, 0.0)

        for k, size in enumerate(chunks):
            @pl.when(which == k)
            def _(size=size):
                def body(j, _):
                    chunk(start + j * size, ln - j * size, size)
                    return 0

                lax.fori_loop(0, count, body, 0)

    @pl.when(p >= nsub + n_exp)
    def _():
        copy_pieces(sb * nsub + (p - nsub - n_exp), False)
        meta = meta_ref[...]
        a = one_hot_cols(meta, meta[:, 2:3], meta[:, 3:4])
        y = x_ref[...] + mod_ref[5:6, :] * _dot(a, loc_s[...])

        @pl.when(sb < nsb_prompt)
        def _():
            op_ref[...] = y

        @pl.when(sb >= nsb_prompt)
        def _():
            os_ref[...] = y


def _moe(x, h, meta, counts, mod, w1, w3, w2, n_prompt_rows, dec_seq):
    r, d = x.shape
    ne, ff, _ = w2.shape
    assert w1.shape == (ne, ff // MOE_FF_TILE, d, MOE_FF_TILE)
    t_super, tb, chunks, tf, pa = MOE_SUPER_BLOCK, MOE_SORT_BLOCK, MOE_CHUNKS, MOE_FF_TILE, PIECE_ALIGN
    ch = max(chunks)
    nsub = t_super // tb
    nsb = r // t_super
    npb = n_prompt_rows // tb
    nf = ff // tf
    n_exp = ne * nf
    loc_rows = 2 * tb + LANES
    assert loc_rows >= 2 * tb + ne * (pa - 1)
    max_rows = 2 * t_super + nsub * ne * (pa - 1) + ch
    sort_rows = LANES * (-(-max_rows // LANES))

    cnt = counts[:, 0, :ne].astype(jnp.int32)
    n16 = (cnt + (pa - 1)) // pa
    loc = pa * (jnp.cumsum(n16, axis=1) - n16)
    n16_sb = n16.reshape(nsb, nsub, ne)
    seg_len = pa * jnp.sum(n16_sb, axis=1)
    seg_off = jnp.cumsum(seg_len, axis=1) - seg_len
    dst = seg_off[:, None, :] + pa * (jnp.cumsum(n16_sb, axis=1) - n16_sb)
    cnum = (seg_len + ch - 1) // ch
    need = pa * ((seg_len // pa + jnp.maximum(cnum, 1) - 1) // jnp.maximum(cnum, 1))
    csel = sum((need > size).astype(jnp.int32) for size in chunks[:-1])
    scalars = [a.reshape(-1).astype(jnp.int32) for a in (n16, loc, dst, seg_off, seg_len, csel, cnum)]

    grp = functools.partial(_group_of_tile, tm=t_super, n_prompt_rows=n_prompt_rows, dec_seq=dec_seq)

    def exp_step(p):
        return jnp.clip(p - nsub, 0, n_exp - 1)

    def tok_blk(sb, s):
        return sb * nsub + jnp.clip(s, 0, nsub - 1)

    grid_spec = pltpu.PrefetchScalarGridSpec(
        num_scalar_prefetch=len(scalars),
        grid=(nsb, nsub + n_exp + nsub),
        in_specs=[
            pl.BlockSpec((tb, d), lambda sb, p, *_: (tok_blk(sb, p), 0)),
            pl.BlockSpec((tb, LANES), lambda sb, p, *_: (sb * nsub + jnp.where(p < nsub, p, jnp.clip(p - nsub - n_exp, 0, nsub - 1)), 0)),
            pl.BlockSpec((tb, d), lambda sb, p, *_: (tok_blk(sb, p - nsub - n_exp), 0)),
            pl.BlockSpec((None, 6, d), lambda sb, p, *_: (grp(sb), 0, 0)),
            pl.BlockSpec((None, None, d, tf), lambda sb, p, *_: (exp_step(p) // nf, exp_step(p) % nf, 0, 0)),
            pl.BlockSpec((None, None, d, tf), lambda sb, p, *_: (exp_step(p) // nf, exp_step(p) % nf, 0, 0)),
            pl.BlockSpec((None, tf, d), lambda sb, p, *_: (exp_step(p) // nf, exp_step(p) % nf, 0)),
        ],
        out_specs=(
            pl.BlockSpec((tb, d), lambda sb, p, *_: (jnp.minimum(tok_blk(sb, p - nsub - n_exp), npb - 1), 0)),
            pl.BlockSpec((tb, d), lambda sb, p, *_: (jnp.maximum(tok_blk(sb, p - nsub - n_exp) - npb, 0), 0)),
        ),
        scratch_shapes=[
            pltpu.VMEM((sort_rows, d), bf16),
            pltpu.VMEM((sort_rows, d), f32),
            pltpu.VMEM((loc_rows, d), bf16),
        ],
    )
    return pl.pallas_call(
        functools.partial(_moe_kernel, nsub=nsub, ne=ne, nf=nf, chunks=chunks,
                          nsb_prompt=n_prompt_rows // t_super),
        out_shape=(jax.ShapeDtypeStruct((n_prompt_rows, d), f32), jax.ShapeDtypeStruct((r - n_prompt_rows, d), f32)),
        grid_spec=grid_spec,
        compiler_params=_cparams("arbitrary", "arbitrary"),
        name="moe_sparse",
    )(*scalars, h, meta, x, mod, w1, w3, w2)


def kernel(x_prompt, x_sample, c, cache_dattn_k, cache_dattn_v, state_mlstm_c, state_mlstm_n, state_mlstm_m,
           c_ctx, w_ada, b_ada, w_in_ab, conv_w, conv_b, gate_b, qn_g, kn_g, lam_q1, lam_k1, lam_q2, lam_k2,
           m_norm_g, a_norm_g, w_out_ab, ff_w1, ff_w3, ff_w2, w_in_c, b_in_c, c_ln_g, c_ln_b, c_ws, c_bs,
           w_out_c, w_router, ex_w1, ex_w3, ex_w2):
    bp, seq, d = x_prompt.shape
    bs, dec_seq, _ = x_sample.shape
    depth = w_ada.shape[0]
    n_prompt_rows = bp * seq
    assert n_prompt_rows % dec_seq == 0 and seq % M_CHUNK == 0 and dec_seq % MOE_SUPER_BLOCK == 0
    nh = M_HEADS

    mods = _ada_table(jnp.concatenate([c_ctx[None], c], axis=0), w_ada, b_ada)
    x = (x_prompt.reshape(n_prompt_rows, d), x_sample.reshape(bs * dec_seq, d))
    rows = (n_prompt_rows, dec_seq)

    def joined(v):
        return jnp.concatenate(v, axis=0) if isinstance(v, tuple) else v

    def parts(v):
        return v if isinstance(v, tuple) else (v[:n_prompt_rows], v[n_prompt_rows:])

    new_k, new_v, new_c, new_n, new_m = [], [], [], [], []
    for l in range(depth):
        j = l // 2
        mod = mods[l]
        if l % 2 == 0:
            lam_init = 0.8 - 0.6 * math.exp(-0.3 * l)
            lam = (jnp.exp(jnp.sum((lam_q1[j] * lam_k1[j]).astype(f32)))
                   - jnp.exp(jnp.sum((lam_q2[j] * lam_k2[j]).astype(f32))) + lam_init).reshape(1)
            o3 = 4 * M_W
            o4 = o3 + 4 * nh
            w = w_in_ab[j]
            w_main = jnp.concatenate([w[:, :o3], w[:, o4:]], axis=1).astype(bf16)
            w_gate = jnp.zeros((d, LANES), f32).at[:, :4 * nh].set(w[:, o3:o4]).astype(bf16)
            x = parts(x)
            nxt = j if l + 1 < depth else None
            z, gates, *ex2 = _inproj(*x, mod, w_main, w_gate, *rows,
                                     job=None if nxt is None else functools.partial(_cast_job_rows, ex_w2[nxt]))
            gates = gates + jnp.zeros((1, LANES), f32).at[0, :4 * nh].set(gate_b[j])
            mng = m_norm_g[j].reshape(1, M_W)
            cb = conv_b[j].reshape(1, 2 * M_W)
            hm_p, c_f, n_f, m_f = _mlstm(z, gates, conv_w[j], cb, mng, None, batch=bp, t=seq, row_block0=0)
            state0 = (state_mlstm_c[:, j], state_mlstm_n[:, j].reshape(bs, 2, nh, 1, M_DIM),
                      state_mlstm_m[:, j].reshape(bs, 2, nh, 1, 1))
            hm_s, _, _, _ = _mlstm(z, gates, conv_w[j], cb, mng, state0,
                                   batch=bs, t=dec_seq, row_block0=n_prompt_rows // dec_seq)
            new_c.append(c_f)
            new_n.append(n_f.reshape(bp, 2, nh, M_DIM))
            new_m.append(m_f.reshape(bp, 2, nh))

            qg2 = jnp.tile(qn_g[j], 2).reshape(1, LANES)
            kg2 = jnp.tile(kn_g[j], 2).reshape(1, LANES)
            ag = a_norm_g[j].reshape(1, LANES)
            ha_p, k_ctx, v_ctx = _attn(z, lam, qg2, kg2, ag, batch=bp, t=seq, row_block0=0,
                                       out_scale=1.0 - lam_init)
            kct = cache_dattn_k[:, j].transpose(0, 1, 2, 4, 3).reshape(bs, A_HEADS, LANES, -1)
            (ha_s,) = _attn(z, lam, qg2, kg2, ag, batch=bs, t=dec_seq, row_block0=n_prompt_rows // dec_seq,
                            rope_tabs=_rope_tables(dec_seq), ctx_kv=(kct, cache_dattn_v[:, j]),
                            out_scale=1.0 - lam_init)
            new_k.append(k_ctx)
            new_v.append(v_ctx)
            x = _outproj(*x, hm_p, hm_s, ha_p, ha_s, mod, w_out_ab[j].astype(bf16), *rows)
            x, *ex1 = _ffn(x, mod, ff_w1[j].astype(bf16), ff_w3[j].astype(bf16), ff_w2[j].astype(bf16), *rows,
                           job=None if nxt is None else functools.partial(_cast_job_tiled, ex_w1[nxt], tf=MOE_FF_TILE))
        else:
            e = w_out_c.shape[1]
            bs_full = jnp.repeat(c_bs[j].T, e // C_GROUPS, axis=1)
            x, ex3 = _gmlp(joined(x), mod, w_in_c[j].astype(bf16), b_in_c[j].reshape(1, -1),
                           c_ln_g[j].reshape(1, -1), c_ln_b[j].reshape(1, -1), c_ws[j].astype(bf16), bs_full,
                           w_out_c[j].astype(bf16), *rows,
                           job=functools.partial(_cast_job_tiled, ex_w3[j], tf=MOE_FF_TILE))
            wr = jnp.zeros((d, LANES), f32).at[:, :N_EXPERTS].set(w_router[j])
            wr_hi = wr.astype(bf16)
            wr_lo = (wr - wr_hi.astype(f32)).astype(bf16)
            h, meta, counts = _router(x, mod, wr_hi, wr_lo, *rows)
            x = _moe(x, h, meta, counts, mod, ex1[0], ex3, ex2[0].reshape(ex_w2[j].shape), *rows)

    y_prompt, y_sample = parts(x)
    y_prompt = y_prompt.reshape(bp, seq, d)
    y_sample = y_sample.reshape(bs, dec_seq, d)
    return (y_prompt, y_sample, jnp.stack(new_k, axis=1), jnp.stack(new_v, axis=1),
            jnp.stack(new_c, axis=1), jnp.stack(new_n, axis=1), jnp.stack(new_m, axis=1))
```

```python
import functools
import math
from typing import Callable, NamedTuple

import jax
import jax.numpy as jnp
import numpy as np
from jax import lax
from jax.experimental import pallas as pl
from jax.experimental.pallas import tpu as pltpu

f32 = jnp.float32
bf16 = jnp.bfloat16

D_MODEL = 1024
M_HEADS = 4
M_DIM = 128
M_W = M_HEADS * M_DIM
M_CHUNK = 128
A_HEADS = 4
A_VDIM = 128
A_DIM = 64
A_W = A_HEADS * A_VDIM
GRID_W = 64
ROPE_THETA = 10000.0
C_CHUNK = 128
C_GROUPS = 4
N_EXPERTS = 8
EPS = 1e-6

LANES = 128
MXU_COLS = 256
ROW_TILE = 512
PIECE_ALIGN = 16
MOE_SUPER_BLOCK = 2048
MOE_SORT_BLOCK = 512
MOE_CHUNKS = (128, 192, 224, 256)
MOE_FF_TILE = 896
VMEM_LIMIT = 60 * 1024 * 1024


def _cparams(*sem):
    return pltpu.CompilerParams(dimension_semantics=tuple(sem), vmem_limit_bytes=VMEM_LIMIT)


def _const_spec(shape):
    nd = len(shape)
    return pl.BlockSpec(shape, lambda *_: (0,) * nd, pipeline_mode=pl.Buffered(1))


def _sigmoid(x):
    return 1.0 / (1.0 + jnp.exp(-x))


def _silu(x):
    return x * _sigmoid(x)


def _log_sigmoid(x):
    return jnp.minimum(x, 0.0) - jnp.log(1.0 + jnp.exp(-jnp.abs(x)))


def _rms(x):
    return x * lax.rsqrt(jnp.mean(x * x, axis=-1, keepdims=True) + EPS)


def _modulate(x, mod_ref, first):
    shift = mod_ref[first:first + 1, :]
    scale = mod_ref[first + 1:first + 2, :]
    return _rms(x) * (1.0 + scale) + shift


def _dot(a, b):
    return jnp.dot(a, b, preferred_element_type=f32)


def _dot_nt(a, b):
    return lax.dot_general(a, b, (((1,), (1,)), ((), ())), preferred_element_type=f32)


def _dot_tn(a, b):
    return lax.dot_general(a, b, (((0,), (0,)), ((), ())), preferred_element_type=f32)


def _split3(x):
    hi = x.astype(bf16)
    r1 = x - hi.astype(f32)
    mid = r1.astype(bf16)
    lo = (r1 - mid.astype(f32)).astype(bf16)
    return hi, mid, lo


def _group_of_tile(i, tm, n_prompt_rows, dec_seq):
    pt = n_prompt_rows // tm
    return jnp.where(i < pt, 0, 1 + (i - pt) // (dec_seq // tm))


def _ada_kernel(cv_ref, w_ref, b_ref, o_ref):
    a = _silu(cv_ref[...]).astype(bf16)
    o_ref[...] = _dot(a, w_ref[...].astype(bf16)) + b_ref[...]


def _ada_table(cv, w_ada, b_ada):
    depth, d, n = w_ada.shape
    g = cv.shape[0]
    gp = 8 * ((g + 7) // 8)
    cvp = jnp.zeros((gp, d), f32).at[:g].set(cv)
    tn = 1536
    out = pl.pallas_call(
        _ada_kernel,
        out_shape=jax.ShapeDtypeStruct((depth, gp, n), f32),
        grid=(depth, n // tn),
        in_specs=[
            pl.BlockSpec((gp, d), lambda l, j: (0, 0)),
            pl.BlockSpec((None, d, tn), lambda l, j: (l, 0, j)),
            pl.BlockSpec((None, 1, tn), lambda l, j: (l, 0, j)),
        ],
        out_specs=pl.BlockSpec((None, gp, tn), lambda l, j: (l, 0, j)),
        compiler_params=_cparams("arbitrary", "arbitrary"),
        name="ada_table",
    )(cvp, w_ada, b_ada.reshape(depth, 1, n))
    return out[:, :g].reshape(depth, g, 6, d)


def _part_specs(tm, width, npt):
    first = pl.BlockSpec((tm, width), lambda i: (jnp.minimum(i, npt - 1), 0))
    second = pl.BlockSpec((tm, width), lambda i: (jnp.maximum(i - npt, 0), 0))
    return first, second


def _pick(a_ref, b_ref, npt):
    return jnp.where(pl.program_id(0) < npt, a_ref[...], b_ref[...])


class _CastJob(NamedTuple):
    src: jax.Array
    in_spec: pl.BlockSpec
    out_spec: pl.BlockSpec
    out_shape: jax.ShapeDtypeStruct
    body: Callable


def _cast_job_tiled(src, steps, tf):
    ne, d, ff = src.shape
    bands = steps // ne
    rows = d // bands
    nf = ff // tf

    def body(i_ref, o_ref):
        for f in range(nf):
            o_ref[f] = i_ref[:, f * tf:(f + 1) * tf].astype(bf16)

    return _CastJob(src, pl.BlockSpec((None, rows, ff), lambda i: (i // bands, i % bands, 0)),
                    pl.BlockSpec((None, nf, rows, tf), lambda i: (i // bands, 0, i % bands, 0)),
                    jax.ShapeDtypeStruct((ne, nf, d, tf), bf16), body)


def _cast_job_rows(src, steps):
    ne, ff, d = src.shape
    rows = ne * ff // steps

    def body(i_ref, o_ref):
        o_ref[...] = i_ref[...].astype(bf16)

    return _CastJob(src.reshape(ne * ff, d), pl.BlockSpec((rows, d), lambda i: (i, 0)),
                    pl.BlockSpec((rows, d), lambda i: (i, 0)), jax.ShapeDtypeStruct((ne * ff, d), bf16), body)


def _call_with_job(kernel_fn, job, *, out_shape, in_specs, out_specs, args, **kw):
    if job is None:
        return pl.pallas_call(kernel_fn, out_shape=out_shape, in_specs=in_specs, out_specs=out_specs, **kw)(*args)
    n_in, n_out = len(in_specs), len(out_shape)

    def with_job(*refs):
        kernel_fn(*refs[:n_in], *refs[n_in + 1:n_in + 1 + n_out], *refs[n_in + 2 + n_out:])
        job.body(refs[n_in], refs[n_in + 1 + n_out])

    return pl.pallas_call(with_job, out_shape=(*out_shape, job.out_shape), in_specs=[*in_specs, job.in_spec],
                          out_specs=(*out_specs, job.out_spec), **kw)(*args, job.src)


def _inproj_kernel(xp_ref, xs_ref, mod_ref, w_ref, wg_ref, z_ref, g_ref, *, n_main, npt):
    h = _modulate(_pick(xp_ref, xs_ref, npt), mod_ref, 0).astype(bf16)
    step = 512
    for j in range(n_main // step):
        z_ref[:, j * step:(j + 1) * step] = _dot(h, w_ref[:, j * step:(j + 1) * step]).astype(bf16)
    g_ref[...] = _dot(h, wg_ref[...])


def _inproj(xp, xs, mod, w_main, w_gate, n_prompt_rows, dec_seq, job=None):
    d = xp.shape[1]
    r = xp.shape[0] + xs.shape[0]
    tm = ROW_TILE
    npt = n_prompt_rows // tm
    n_main = w_main.shape[1]
    grp = functools.partial(_group_of_tile, tm=tm, n_prompt_rows=n_prompt_rows, dec_seq=dec_seq)
    return _call_with_job(
        functools.partial(_inproj_kernel, n_main=n_main, npt=npt), job(r // tm) if job else None,
        out_shape=(jax.ShapeDtypeStruct((r, n_main), bf16), jax.ShapeDtypeStruct((r, LANES), f32)),
        grid=(r // tm,),
        in_specs=[
            *_part_specs(tm, d, npt),
            pl.BlockSpec((None, 6, d), lambda i: (grp(i), 0, 0)),
            _const_spec(w_main.shape),
            _const_spec(w_gate.shape),
        ],
        out_specs=(pl.BlockSpec((tm, n_main), lambda i: (i, 0)), pl.BlockSpec((tm, LANES), lambda i: (i, 0))),
        args=(xp, xs, mod, w_main, w_gate),
        compiler_params=_cparams("arbitrary"),
        name="inproj_ab",
    )


def _conv_silu(x, w_ref, b_ref, t):
    row = lax.broadcasted_iota(jnp.int32, (t, 1), 0)
    prev = jnp.where(row == 0, 0.0, pltpu.roll(x, 1, 0))
    nxt = jnp.where(row == t - 1, 0.0, pltpu.roll(x, t - 1, 0))
    y = b_ref[...] + prev * w_ref[0:1, :] + x * w_ref[1:2, :] + nxt * w_ref[2:3, :]
    return _silu(y)


def _mlstm_kernel(q_ref, k_ref, v_ref, o_ref, g_ref, gb_ref, cw_ref, cb_ref, ng_ref, *rest, t, has_state):
    if has_state:
        c0_ref, n0_ref, m0_ref, *rest = rest
    hm_ref, c_out, n_out, m_out, q_s, qt_s, kh_s, kl_s, vt_s, gs_s, hft_s, hbt_s, ct_s, sel_s = rest
    nc = t // M_CHUNK
    lc = M_CHUNK
    nh = M_HEADS
    w = M_W
    q_s[...] = _conv_silu(q_ref[...].astype(f32), cw_ref[:, :w], cb_ref[:, :w], t)
    kf = _conv_silu(k_ref[...].astype(f32), cw_ref[:, w:], cb_ref[:, w:], t) * (M_DIM ** -0.5)
    kh = kf.astype(bf16)
    kh_s[...] = kh
    kl_s[...] = (kf - kh.astype(f32)).astype(bf16)
    sel_row = lax.broadcasted_iota(jnp.int32, (3 * LANES, LANES), 0)
    for d in range(2):
        for h in range(nh):
            ci, ln = d * nh + h, 2 * nh * d + h
            ct_s[ci] = c0_ref[d, h].T if has_state else jnp.zeros((M_DIM, M_DIM), f32)
            sel_s[ci] = jnp.where(sel_row == ln, 1.0, jnp.where(sel_row == LANES + ln, 1.0, jnp.where(
                sel_row == 2 * LANES + ln, 1.0, 0.0))).astype(bf16)
    if has_state:
        n_out[...] = n0_ref[...]
        m_out[...] = m0_ref[...]
    else:
        n_out[...] = jnp.zeros_like(n_out)
        m_out[...] = jnp.zeros_like(m_out)

    lane = lax.broadcasted_iota(jnp.int32, (1, LANES), 1)
    is_lf = ((lane >= nh) & (lane < 2 * nh)) | ((lane >= 3 * nh) & (lane < 4 * nh))
    is_bw_lf = (lane >= 3 * nh) & (lane < 4 * nh)
    g = g_ref[...] + gb_ref[...]
    gs_s[...] = jnp.where(is_lf, _log_sigmoid(g), g)

    rr = lax.broadcasted_iota(jnp.int32, (lc, lc), 0)
    cc = lax.broadcasted_iota(jnp.int32, (lc, lc), 1)
    tri_incl = jnp.where(rr >= cc, 1.0, 0.0).astype(bf16)

    def prep_body(c, _):
        off = pl.multiple_of(c * lc, lc)
        tile = gs_s[pl.ds(off, lc), :]
        hi, mid, lo = _split3(jnp.where(is_lf, tile, 0.0))
        cs = _dot(tri_incl, hi) + _dot(tri_incl, mid) + _dot(tri_incl, lo)
        total = jnp.broadcast_to(cs[lc - 1:lc, :], (lc, LANES))
        b = jnp.where(is_bw_lf, total - cs + tile, cs)
        lmb = tile - pltpu.roll(b, LANES - nh, 1)
        gv = pltpu.roll(total, LANES - nh, 1) + lmb
        low = jnp.where(is_lf, b, lmb)
        high = pltpu.roll(jnp.where(is_lf, total, gv), 4 * nh, 1)
        gs_s[pl.ds(off, lc), :] = jnp.where(lane < 4 * nh, low, jnp.where(lane < 8 * nh, high, 0.0))
        wide = pl.ds(pl.multiple_of(c * w, w), w)
        qt_s[wide, :] = q_s[pl.ds(off, lc), :].T
        vt_s[wide, :] = v_ref[pl.ds(off, lc), :].astype(f32).T.astype(bf16)
        return 0

    lax.fori_loop(0, nc, prep_body, 0)

    sub8 = lax.broadcasted_iota(jnp.int32, (8, LANES), 0)

    def two_rows(x):
        hi = x.astype(bf16).astype(f32)
        return jnp.where(sub8 == 0, hi, jnp.where(sub8 == 1, x - hi, 0.0)).astype(bf16)

    def issue(c, h, d, tile3, tile_t):
        off = pl.multiple_of(c * lc, lc)
        col = slice(h * M_DIM, (h + 1) * M_DIM)
        ci, ln = d * nh + h, 2 * nh * d + h
        head = pl.ds(pl.multiple_of(c * w + h * M_DIM, M_DIM), M_DIM)
        b_row = tile_t[ln + nh:ln + nh + 1, :]
        g_row = tile_t[ln + 4 * nh:ln + 4 * nh + 1, :]
        total = tile_t[ln + 5 * nh:ln + 5 * nh + 1, 0:1]
        ct, nm, mm = ct_s[ci], n_out[d, h], m_out[d, h]
        qt = qt_s[head, :]
        qtb = qt.astype(bf16)
        khb = kh_s[pl.ds(off, lc), col]
        vtb = vt_s[head, :]
        lmb = _dot(tile3, sel_s[ci])
        sraw = _dot(khb, qtb)
        qn = _dot(two_rows(nm), qtb)
        lhs = jnp.concatenate([ct.astype(bf16), vtb], axis=1)
        m_new = jnp.maximum(mm + total, jnp.max(g_row, axis=1, keepdims=True))
        decay = jnp.exp(mm + total - m_new)
        ew = jnp.exp(g_row - m_new)
        ct_s[ci] = decay * ct + _dot(vtb * ew.astype(bf16), khb)
        ew2 = two_rows(ew)
        nk_h = _dot(ew2, khb)
        nk_l = _dot(ew2, kl_s[pl.ds(off, lc), col])
        n_out[d, h] = decay * nm + (nk_h[0:1, :] + nk_h[1:2, :] + nk_l[0:1, :])
        m_out[d, h] = m_new
        return dict(d=d, head=head, b_row=b_row, mm=mm, qt=qt, lmb=lmb, sraw=sraw, qn=qn, lhs=lhs)

    def weigh(st):
        keep = (cc >= rr) if st["d"] == 0 else (rr >= cc)
        dmat = jnp.where(keep, st["lmb"] + st["b_row"], -jnp.inf)
        inter = st["mm"] + st["b_row"]
        mt = jnp.maximum(inter, jnp.max(dmat, axis=0, keepdims=True))
        w_inter = jnp.exp(inter - mt)
        s = st["sraw"] * jnp.exp(dmat - mt)
        qn = st["qn"]
        den = w_inter * (qn[0:1, :] + qn[1:2, :]) + jnp.sum(s, axis=0, keepdims=True)
        inv = 1.0 / jnp.maximum(jnp.abs(den), jnp.exp(-mt))
        return jnp.concatenate([st["qt"] * (w_inter * inv), s * inv], axis=0).astype(bf16)

    def body(i, _):
        states = []
        for d in range(2):
            c = i if d == 0 else nc - 1 - i
            tile = gs_s[pl.ds(pl.multiple_of(c * lc, lc), lc), :]
            tile_t = tile.T
            hi, mid, lo = _split3(tile)
            tile3 = jnp.concatenate([hi, mid, lo], axis=1)
            states += [issue(c, h, d, tile3, tile_t) for h in range(nh)]
        rhss = [weigh(st) for st in states]
        for st, rhs in zip(states, rhss):
            hct = _dot(st["lhs"], rhs)
            if st["d"] == 0:
                hft_s[st["head"], :] = hct
            else:
                hbt_s[st["head"], :] = hct
        return 0

    lax.fori_loop(0, nc, body, 0)

    def out_body(c, _):
        rows = pl.ds(pl.multiple_of(c * lc, lc), lc)
        wide = pl.ds(pl.multiple_of(c * w, w), w)
        hsum = (hft_s[wide, :] + hbt_s[wide, :]).T
        for h in range(nh):
            col = slice(h * M_DIM, (h + 1) * M_DIM)
            mo = _sigmoid(o_ref[rows, col].astype(f32))
            hm_ref[rows, col] = (_rms(hsum[:, col]) * ng_ref[:, col] * mo).astype(hm_ref.dtype)
        return 0

    lax.fori_loop(0, nc, out_body, 0)
    for d in range(2):
        for h in range(nh):
            c_out[d, h] = ct_s[d * nh + h].T


def _mlstm(z, gates, gate_b, conv_w, conv_b, m_norm_g, state0, *, batch, t, row_block0):
    nh = M_HEADS
    w = M_W
    mode = dict(pipeline_mode=pl.Buffered(1)) if t * w * 2 > (1 << 20) else {}
    seq = lambda colblk: pl.BlockSpec((t, w), lambda b: (row_block0 + b, colblk), **mode)
    state = lambda *tail: pl.BlockSpec((None, 2, nh) + tail, lambda b: (b,) + (0,) * (2 + len(tail)))
    state_specs = [state(M_DIM, M_DIM), state(1, M_DIM), state(1, 1)]
    has_state = state0 is not None
    return pl.pallas_call(
        functools.partial(_mlstm_kernel, t=t, has_state=has_state),
        out_shape=(
            jax.ShapeDtypeStruct((batch * t, w), bf16),
            jax.ShapeDtypeStruct((batch, 2, nh, M_DIM, M_DIM), f32),
            jax.ShapeDtypeStruct((batch, 2, nh, 1, M_DIM), f32),
            jax.ShapeDtypeStruct((batch, 2, nh, 1, 1), f32),
        ),
        grid=(batch,),
        in_specs=[
            seq(0), seq(1), seq(2), seq(3),
            pl.BlockSpec((t, LANES), lambda b: (row_block0 + b, 0)),
            _const_spec(gate_b.shape),
            _const_spec(conv_w.shape), _const_spec(conv_b.shape), _const_spec(m_norm_g.shape),
            *(state_specs if has_state else []),
        ],
        out_specs=(pl.BlockSpec((t, w), lambda b: (b, 0)), *state_specs),
        scratch_shapes=[
            pltpu.VMEM((t, w), f32),
            pltpu.VMEM((t * nh, M_DIM), f32),
            pltpu.VMEM((t, w), bf16),
            pltpu.VMEM((t, w), bf16),
            pltpu.VMEM((t * nh, M_DIM), bf16),
            pltpu.VMEM((t, LANES), f32),
            pltpu.VMEM((t * nh, M_DIM), f32),
            pltpu.VMEM((t * nh, M_DIM), f32),
            pltpu.VMEM((2 * nh, M_DIM, M_DIM), f32),
            pltpu.VMEM((2 * nh, 3 * LANES, LANES), bf16),
        ],
        compiler_params=_cparams("arbitrary"),
        name=f"mlstm_t{t}",
    )(z, z, z, z, gates, gate_b, conv_w, conv_b, m_norm_g, *(state0 if has_state else ()))


def _pair_norm(x, gain):
    lane = lax.broadcasted_iota(jnp.int32, (1, LANES), 1)
    first = lane < A_DIM
    sq = x * x
    s_all = jnp.sum(sq, axis=-1, keepdims=True)
    s0 = jnp.sum(jnp.where(first, sq, 0.0), axis=-1, keepdims=True)
    inv0 = lax.rsqrt(s0 * (1.0 / A_DIM) + EPS)
    inv1 = lax.rsqrt((s_all - s0) * (1.0 / A_DIM) + EPS)
    return x * jnp.where(first, inv0, inv1) * gain


def _rope(x, cos, sin_signed):
    lane = lax.broadcasted_iota(jnp.int32, (1, LANES), 1)
    nf = A_DIM // 4
    partner = jnp.where((lane % (2 * nf)) < nf, pltpu.roll(x, LANES - nf, 1), pltpu.roll(x, nf, 1))
    return x * cos + partner * sin_signed


def _attn_kernel(*refs, t, tq, rope, ctx, out_scale):
    it = iter(refs)
    lam_ref = next(it)
    q_ref, k_ref, v_ref = next(it), next(it), next(it)
    qg_ref, kg_ref, ag_ref = next(it), next(it), next(it)
    if rope:
        cosq_ref, sinq_ref, cosk_ref, sink_ref = next(it), next(it), next(it), next(it)
    if ctx:
        kct_ref, vc_ref = next(it), next(it)
    ha_ref = next(it)
    if not ctx:
        newk_ref, newv_ref = next(it), next(it)
    kt_s = next(it)
    nh = A_HEADS

    @pl.when(pl.program_id(1) == 0)
    def _():
        for h in range(nh):
            col = slice(h * LANES, (h + 1) * LANES)
            kn = _pair_norm(k_ref[:, col].astype(f32), kg_ref[...])
            if not ctx:
                newk_ref[h, 0] = kn[:, :A_DIM]
                newk_ref[h, 1] = kn[:, A_DIM:]
                newv_ref[h] = v_ref[:, col].astype(f32)
            if rope:
                kn = _rope(kn, cosk_ref[...], sink_ref[...])
            kt_s[col, :] = kn.T.astype(bf16)

    lane = lax.broadcasted_iota(jnp.int32, (1, LANES), 1)
    lam = lam_ref[0]
    for h in range(nh):
        col = slice(h * LANES, (h + 1) * LANES)
        q = _pair_norm(q_ref[:, col].astype(f32), qg_ref[...])
        if rope:
            q = _rope(q, cosq_ref[...], sinq_ref[...])
        q = q * (A_DIM ** -0.5 * math.log2(math.e))
        qs = [jnp.where(lane < A_DIM, q, 0.0).astype(bf16), jnp.where(lane >= A_DIM, q, 0.0).astype(bf16)]
        kt = kt_s[col, :]
        vb = v_ref[:, col]
        if ctx:
            kctb = kct_ref[h].astype(bf16)
            vcb = vc_ref[h].astype(bf16)
        outs = []
        for i in range(2):
            sn = _dot(qs[i], kt)
            mx = jnp.max(sn, axis=-1, keepdims=True)
            if ctx:
                sc = _dot(qs[i], kctb)
                mx = jnp.maximum(mx, jnp.max(sc, axis=-1, keepdims=True))
            en = jnp.exp2(sn - mx)
            den = jnp.sum(en, axis=-1, keepdims=True)
            o = _dot(en.astype(bf16), vb)
            if ctx:
                ec = jnp.exp2(sc - mx)
                den = den + jnp.sum(ec, axis=-1, keepdims=True)
                o = o + _dot(ec.astype(bf16), vcb)
            outs.append(o * (1.0 / den))
        o = outs[0] - lam * outs[1]
        ha_ref[:, col] = (_rms(o) * ag_ref[...] * out_scale).astype(ha_ref.dtype)


def _attn(z, lam, qg2, kg2, a_norm_g, *, batch, t, row_block0, rope_tabs=None, ctx_kv=None, out_scale):
    nh = A_HEADS
    w = A_W
    tq = min(t, 256)
    nq = t // tq
    rope = rope_tabs is not None
    ctx = ctx_kv is not None
    qblk = 4 * M_W // w
    vec = pl.BlockSpec((1, LANES), lambda b, i: (0, 0))
    in_specs = [
        pl.BlockSpec(memory_space=pltpu.SMEM),
        pl.BlockSpec((tq, w), lambda b, i: ((row_block0 + b) * nq + i, qblk)),
        pl.BlockSpec((t, w), lambda b, i: (row_block0 + b, qblk + 1)),
        pl.BlockSpec((t, w), lambda b, i: (row_block0 + b, qblk + 2)),
        vec, vec, vec,
    ]
    args = [lam, z, z, z, qg2, kg2, a_norm_g]
    if rope:
        cos, sin = rope_tabs
        in_specs += [pl.BlockSpec((tq, LANES), lambda b, i: (i, 0))] * 2
        in_specs += [pl.BlockSpec((t, LANES), lambda b, i: (0, 0))] * 2
        args += [cos, sin, cos, sin]
    if ctx:
        kct, vc = ctx_kv
        in_specs += [pl.BlockSpec((None,) + kct.shape[1:], lambda b, i: (b, 0, 0, 0)),
                     pl.BlockSpec((None,) + vc.shape[1:], lambda b, i: (b, 0, 0, 0))]
        args += [kct, vc]
    out_shape = [jax.ShapeDtypeStruct((batch * t, w), bf16)]
    out_specs = [pl.BlockSpec((tq, w), lambda b, i: (b * nq + i, 0))]
    if not ctx:
        out_shape += [jax.ShapeDtypeStruct((batch, nh, 2, t, A_DIM), f32),
                      jax.ShapeDtypeStruct((batch, nh, t, A_VDIM), f32)]
        out_specs += [pl.BlockSpec((None, nh, 2, t, A_DIM), lambda b, i: (b, 0, 0, 0, 0)),
                      pl.BlockSpec((None, nh, t, A_VDIM), lambda b, i: (b, 0, 0, 0))]
    return pl.pallas_call(
        functools.partial(_attn_kernel, t=t, tq=tq, rope=rope, ctx=ctx, out_scale=out_scale),
        out_shape=tuple(out_shape),
        grid=(batch, nq),
        in_specs=in_specs,
        out_specs=tuple(out_specs),
        scratch_shapes=[pltpu.VMEM((w, t), bf16)],
        compiler_params=_cparams("arbitrary", "arbitrary"),
        name=f"diff_attn_t{t}",
    )(*args)


def _rope_tables(t):
    rows = t // GRID_W
    pos_row = np.repeat(np.arange(rows, dtype=np.float64), GRID_W)
    pos_col = (np.arange(rows * GRID_W) % GRID_W).astype(np.float64)
    nf = A_DIM // 4
    inv = ROPE_THETA ** (-np.arange(nf, dtype=np.float64) / nf)
    lane = np.arange(LANES)
    j = lane % (2 * nf)
    use_col = (lane % A_DIM) >= (A_DIM // 2)
    ang = np.where(use_col[None, :], pos_col[:, None], pos_row[:, None]) * inv[j % nf][None, :]
    sign = np.where(j < nf, -1.0, 1.0)[None, :]
    return jnp.asarray(np.cos(ang), f32), jnp.asarray(np.sin(ang) * sign, f32)


def _outproj_kernel(xp_ref, xs_ref, hmp_ref, hms_ref, hap_ref, has_ref, mod_ref, w_ref, o_ref, *, npt):
    half = hmp_ref.shape[1]
    hm = _pick(hmp_ref, hms_ref, npt)
    ha = _pick(hap_ref, has_ref, npt)
    y = _dot(hm, w_ref[:half, :]) + _dot(ha, w_ref[half:, :])
    o_ref[...] = _pick(xp_ref, xs_ref, npt) + mod_ref[2:3, :] * y


def _outproj(xp, xs, hm_p, hm_s, ha_p, ha_s, mod, w, n_prompt_rows, dec_seq):
    d = xp.shape[1]
    r = xp.shape[0] + xs.shape[0]
    tm = ROW_TILE
    npt = n_prompt_rows // tm
    grp = functools.partial(_group_of_tile, tm=tm, n_prompt_rows=n_prompt_rows, dec_seq=dec_seq)
    return pl.pallas_call(
        functools.partial(_outproj_kernel, npt=npt),
        out_shape=jax.ShapeDtypeStruct((r, d), f32),
        grid=(r // tm,),
        in_specs=[
            *_part_specs(tm, d, npt),
            *_part_specs(tm, hm_p.shape[1], npt),
            *_part_specs(tm, ha_p.shape[1], npt),
            pl.BlockSpec((None, 6, d), lambda i: (grp(i), 0, 0)),
            _const_spec(w.shape),
        ],
        out_specs=pl.BlockSpec((tm, d), lambda i: (i, 0)),
        compiler_params=_cparams("arbitrary"),
        name="outproj_ab",
    )(xp, xs, hm_p, hm_s, ha_p, ha_s, mod, w)


def _ffn_kernel(x_ref, mod_ref, w1_ref, w3_ref, w2_ref, o_ref, *, chunks):
    x = x_ref[...]
    h = _modulate(x, mod_ref, 3).astype(bf16)
    acc = jnp.zeros(x.shape, f32)
    for lo, hi in chunks:
        ab = _dot(h, jnp.concatenate([w1_ref[:, lo:hi], w3_ref[:, lo:hi]], axis=1))
        a = _silu(ab[:, :hi - lo]) * ab[:, hi - lo:]
        acc = acc + _dot(a.astype(bf16), w2_ref[lo:hi, :])
    o_ref[...] = x + mod_ref[5:6, :] * acc


def _ffn(x, mod, w1, w3, w2, n_prompt_rows, dec_seq, job=None):
    r, d = x.shape
    tm = ROW_TILE
    grp = functools.partial(_group_of_tile, tm=tm, n_prompt_rows=n_prompt_rows, dec_seq=dec_seq)
    ff = w1.shape[1]
    cut = MXU_COLS * ((ff // MXU_COLS + 1) // 2) if ff % MXU_COLS == 0 else ff
    chunks = ((0, cut), (cut, ff)) if cut < ff else ((0, ff),)
    return _call_with_job(
        functools.partial(_ffn_kernel, chunks=chunks), job(r // tm) if job else None,
        out_shape=(jax.ShapeDtypeStruct((r, d), f32),),
        grid=(r // tm,),
        in_specs=[
            pl.BlockSpec((tm, d), lambda i: (i, 0)),
            pl.BlockSpec((None, 6, d), lambda i: (grp(i), 0, 0)),
            _const_spec(w1.shape), _const_spec(w3.shape), _const_spec(w2.shape),
        ],
        out_specs=(pl.BlockSpec((tm, d), lambda i: (i, 0)),),
        args=(x, mod, w1, w3, w2),
        compiler_params=_cparams("arbitrary"),
        name="ffn_dense",
    )


def _gelu_tanh(x):
    return 0.5 * x * (1.0 + jnp.tanh(math.sqrt(2.0 / math.pi) * (x + 0.044715 * (x * x * x))))


def _gmlp_kernel(x_ref, mod_ref, win_ref, bin_ref, lng_ref, lnb_ref, ws_ref, bs_ref, wout_ref, o_ref, us_s):
    x = x_ref[...]
    tm = x.shape[0]
    e = wout_ref.shape[0]
    ge = e // C_GROUPS
    h = _modulate(x, mod_ref, 0).astype(bf16)
    u = _gelu_tanh(_dot(h, win_ref[:, :e]) + bin_ref[:, :e])
    v = _gelu_tanh(_dot(h, win_ref[:, e:]) + bin_ref[:, e:])
    mu = jnp.mean(v, axis=-1, keepdims=True)
    vc = v - mu
    var = jnp.mean(vc * vc, axis=-1, keepdims=True)
    vn = (vc * lax.rsqrt(var + EPS) * lng_ref[...] + lnb_ref[...]).astype(bf16)
    for n in range(tm // C_CHUNK):
        rows = slice(n * C_CHUNK, (n + 1) * C_CHUNK)
        for g in range(C_GROUPS):
            cols = slice(g * ge, (g + 1) * ge)
            s = _dot(ws_ref[g], vn[rows, cols]) + bs_ref[:, cols]
            us_s[rows, cols] = (u[rows, cols] * s).astype(bf16)
    o_ref[...] = x + mod_ref[2:3, :] * _dot(us_s[...], wout_ref[...])


def _gmlp(x, mod, w_in, b_in, ln_g, ln_b, ws, bs_full, w_out, n_prompt_rows, dec_seq, job=None):
    r, d = x.shape
    tm = ROW_TILE
    e = w_out.shape[0]
    grp = functools.partial(_group_of_tile, tm=tm, n_prompt_rows=n_prompt_rows, dec_seq=dec_seq)
    return _call_with_job(
        _gmlp_kernel, job(r // tm) if job else None,
        out_shape=(jax.ShapeDtypeStruct((r, d), f32),),
        grid=(r // tm,),
        in_specs=[
            pl.BlockSpec((tm, d), lambda i: (i, 0)),
            pl.BlockSpec((None, 6, d), lambda i: (grp(i), 0, 0)),
            _const_spec(w_in.shape), _const_spec(b_in.shape), _const_spec(ln_g.shape), _const_spec(ln_b.shape),
            _const_spec(ws.shape), _const_spec(bs_full.shape), _const_spec(w_out.shape),
        ],
        out_specs=(pl.BlockSpec((tm, d), lambda i: (i, 0)),),
        args=(x, mod, w_in, b_in, ln_g, ln_b, ws, bs_full, w_out),
        scratch_shapes=[pltpu.VMEM((tm, e), bf16)],
        compiler_params=_cparams("arbitrary"),
        name="gmlp",
    )


def _router_kernel(x_ref, mod_ref, wr_hi_ref, wr_lo_ref, h_ref, meta_ref, counts_ref):
    hf = _modulate(x_ref[...], mod_ref, 3)
    hb = hf.astype(bf16)
    h_ref[...] = hb
    h_lo = (hf - hb.astype(f32)).astype(bf16)
    logits = _dot(hb, wr_hi_ref[...]) + (_dot(hb, wr_lo_ref[...]) + _dot(h_lo, wr_hi_ref[...]))
    lane = lax.broadcasted_iota(jnp.int32, logits.shape, 1).astype(f32)
    logits = jnp.where(lane < N_EXPERTS, logits, -jnp.inf)
    m1 = jnp.max(logits, axis=-1, keepdims=True)
    i1 = jnp.min(jnp.where(logits == m1, lane, float(LANES)), axis=-1, keepdims=True)
    rest = jnp.where(lane == i1, -jnp.inf, logits)
    m2 = jnp.max(rest, axis=-1, keepdims=True)
    i2 = jnp.min(jnp.where(rest == m2, lane, float(LANES)), axis=-1, keepdims=True)
    e2 = jnp.exp(m2 - m1)
    w1 = 1.0 / (1.0 + e2)
    w2 = e2 * w1

    tm = logits.shape[0]
    cnt = jnp.where(lane == i1, 1.0, jnp.where(lane == i2, 1.0, 0.0))
    rr = lax.broadcasted_iota(jnp.int32, (tm, tm), 0)
    cc = lax.broadcasted_iota(jnp.int32, (tm, tm), 1)
    before = jnp.where(rr > cc, 1.0, 0.0).astype(bf16)
    rank = _dot(before, cnt.astype(bf16))
    counts = jnp.sum(cnt, axis=0, keepdims=True)
    padded = jnp.floor((counts + (PIECE_ALIGN - 1)) * (1.0 / PIECE_ALIGN)) * PIECE_ALIGN
    lane1 = lane[0:1, :]
    piece_off = jnp.zeros((1, LANES), f32)
    off = jnp.zeros((1, 1), f32)
    for e in range(N_EXPERTS):
        piece_off = jnp.where(lane1 == e, off, piece_off)
        off = off + padded[:, e:e + 1]
    local = piece_off + rank
    pos1 = jnp.sum(jnp.where(lane == i1, local, 0.0), axis=-1, keepdims=True)
    pos2 = jnp.sum(jnp.where(lane == i2, local, 0.0), axis=-1, keepdims=True)
    meta_ref[...] = jnp.where(lane == 0, pos1, jnp.where(lane == 1, pos2, jnp.where(lane == 2, w1,
                              jnp.where(lane == 3, w2, 0.0))))
    counts_ref[...] = counts


def _router(x, mod, wr_hi, wr_lo, n_prompt_rows, dec_seq):
    r, d = x.shape
    tm = MOE_SORT_BLOCK
    grp = functools.partial(_group_of_tile, tm=tm, n_prompt_rows=n_prompt_rows, dec_seq=dec_seq)
    return pl.pallas_call(
        _router_kernel,
        out_shape=(jax.ShapeDtypeStruct((r, d), bf16), jax.ShapeDtypeStruct((r, LANES), f32),
                   jax.ShapeDtypeStruct((r // tm, 1, LANES), f32)),
        grid=(r // tm,),
        in_specs=[
            pl.BlockSpec((tm, d), lambda i: (i, 0)),
            pl.BlockSpec((None, 6, d), lambda i: (grp(i), 0, 0)),
            _const_spec(wr_hi.shape), _const_spec(wr_lo.shape),
        ],
        out_specs=(pl.BlockSpec((tm, d), lambda i: (i, 0)), pl.BlockSpec((tm, LANES), lambda i: (i, 0)),
                   pl.BlockSpec((None, 1, LANES), lambda i: (i, 0, 0))),
        compiler_params=_cparams("arbitrary"),
        name="router",
    )(x, mod, wr_hi, wr_lo)


def _moe_kernel(n16_ref, loc_ref, dst_ref, seg_off_ref, seg_len_ref, csel_ref, cnum_ref,
                h_ref, meta_ref, x_ref, mod_ref, w1_ref, w3_ref, w2_ref, op_ref, os_ref,
                hs_s, ys_s, loc_s, *, nsub, ne, nf, chunks, nsb_prompt):
    sb = pl.program_id(0)
    p = pl.program_id(1)
    n_exp = ne * nf
    loc_rows = loc_s.shape[0]
    pa = PIECE_ALIGN

    def one_hot_cols(meta, v1, v2):
        lane = lax.broadcasted_iota(jnp.int32, (1, loc_rows), 1).astype(f32)
        return jnp.where(lane == meta[:, 0:1], v1, jnp.where(lane == meta[:, 1:2], v2, 0.0)).astype(bf16)

    def copy_pieces(blk, to_sorted):
        for e in range(ne):
            n = n16_ref[blk * ne + e]
            src = loc_ref[blk * ne + e]
            dst = dst_ref[blk * ne + e]

            def cp(i, _, src=src, dst=dst):
                a = pl.ds(pl.multiple_of(src + pa * i, pa), pa)
                b = pl.ds(pl.multiple_of(dst + pa * i, pa), pa)
                if to_sorted:
                    hs_s[b, :] = loc_s[a, :]
                else:
                    loc_s[a, :] = ys_s[b, :].astype(bf16)
                return 0

            lax.fori_loop(0, n, cp, 0)

    @pl.when(p < nsub)
    def _():
        @pl.when(p == 0)
        def _():
            hs_s[...] = jnp.zeros_like(hs_s)
            ys_s[...] = jnp.zeros_like(ys_s)

        pt = one_hot_cols(meta_ref[...], 1.0, 1.0)
        loc_s[...] = _dot_tn(pt, h_ref[...]).astype(bf16)
        copy_pieces(sb * nsub + p, True)

    @pl.when((p >= nsub) & (p < nsub + n_exp))
    def _():
        e = lax.div(p - nsub, jnp.int32(nf))
        start = seg_off_ref[sb * ne + e]
        ln = seg_len_ref[sb * ne + e]
        which = csel_ref[sb * ne + e]
        count = cnum_ref[sb * ne + e]

        def chunk(r0, valid, size):
            rows = pl.ds(pl.multiple_of(r0, pa), size)
            xc = hs_s[rows, :]
            ab = _dot(xc, jnp.concatenate([w1_ref[...], w3_ref[...]], axis=1))
            tf = w1_ref.shape[1]
            a = _silu(ab[:, :tf]) * ab[:, tf:]
            y = _dot(a.astype(bf16), w2_ref[...])
            ri = lax.broadcasted_iota(jnp.int32, (size, 1), 0)
            ys_s[rows, :] += jnp.where(ri < valid, y, 0.0)

        for k, size in enumerate(chunks):
            @pl.when(which == k)
            def _(size=size):
                def body(j, _):
                    chunk(start + j * size, ln - j * size, size)
                    return 0

                lax.fori_loop(0, count, body, 0)

    @pl.when(p >= nsub + n_exp)
    def _():
        copy_pieces(sb * nsub + (p - nsub - n_exp), False)
        meta = meta_ref[...]
        a = one_hot_cols(meta, meta[:, 2:3], meta[:, 3:4])
        y = x_ref[...] + mod_ref[5:6, :] * _dot(a, loc_s[...])

        @pl.when(sb < nsb_prompt)
        def _():
            op_ref[...] = y

        @pl.when(sb >= nsb_prompt)
        def _():
            os_ref[...] = y


def _moe(x, h, meta, counts, mod, w1, w3, w2, n_prompt_rows, dec_seq):
    r, d = x.shape
    ne, ff, _ = w2.shape
    assert w1.shape == (ne, ff // MOE_FF_TILE, d, MOE_FF_TILE)
    t_super, tb, chunks, tf, pa = MOE_SUPER_BLOCK, MOE_SORT_BLOCK, MOE_CHUNKS, MOE_FF_TILE, PIECE_ALIGN
    ch = max(chunks)
    nsub = t_super // tb
    nsb = r // t_super
    npb = n_prompt_rows // tb
    nf = ff // tf
    n_exp = ne * nf
    loc_rows = 2 * tb + LANES
    assert loc_rows >= 2 * tb + ne * (pa - 1)
    max_rows = 2 * t_super + nsub * ne * (pa - 1) + ch
    sort_rows = LANES * (-(-max_rows // LANES))

    cnt = counts[:, 0, :ne].astype(jnp.int32)
    n16 = (cnt + (pa - 1)) // pa
    loc = pa * (jnp.cumsum(n16, axis=1) - n16)
    n16_sb = n16.reshape(nsb, nsub, ne)
    seg_len = pa * jnp.sum(n16_sb, axis=1)
    seg_off = jnp.cumsum(seg_len, axis=1) - seg_len
    dst = seg_off[:, None, :] + pa * (jnp.cumsum(n16_sb, axis=1) - n16_sb)
    cnum = (seg_len + ch - 1) // ch
    need = pa * ((seg_len // pa + jnp.maximum(cnum, 1) - 1) // jnp.maximum(cnum, 1))
    csel = sum((need > size).astype(jnp.int32) for size in chunks[:-1])
    scalars = [a.reshape(-1).astype(jnp.int32) for a in (n16, loc, dst, seg_off, seg_len, csel, cnum)]

    grp = functools.partial(_group_of_tile, tm=t_super, n_prompt_rows=n_prompt_rows, dec_seq=dec_seq)

    def exp_step(p):
        return jnp.clip(p - nsub, 0, n_exp - 1)

    def tok_blk(sb, s):
        return sb * nsub + jnp.clip(s, 0, nsub - 1)

    grid_spec = pltpu.PrefetchScalarGridSpec(
        num_scalar_prefetch=len(scalars),
        grid=(nsb, nsub + n_exp + nsub),
        in_specs=[
            pl.BlockSpec((tb, d), lambda sb, p, *_: (tok_blk(sb, p), 0)),
            pl.BlockSpec((tb, LANES), lambda sb, p, *_: (sb * nsub + jnp.where(p < nsub, p, jnp.clip(p - nsub - n_exp, 0, nsub - 1)), 0)),
            pl.BlockSpec((tb, d), lambda sb, p, *_: (tok_blk(sb, p - nsub - n_exp), 0)),
            pl.BlockSpec((None, 6, d), lambda sb, p, *_: (grp(sb), 0, 0)),
            pl.BlockSpec((None, None, d, tf), lambda sb, p, *_: (exp_step(p) // nf, exp_step(p) % nf, 0, 0)),
            pl.BlockSpec((None, None, d, tf), lambda sb, p, *_: (exp_step(p) // nf, exp_step(p) % nf, 0, 0)),
            pl.BlockSpec((None, tf, d), lambda sb, p, *_: (exp_step(p) // nf, exp_step(p) % nf, 0)),
        ],
        out_specs=(
            pl.BlockSpec((tb, d), lambda sb, p, *_: (jnp.minimum(tok_blk(sb, p - nsub - n_exp), npb - 1), 0)),
            pl.BlockSpec((tb, d), lambda sb, p, *_: (jnp.maximum(tok_blk(sb, p - nsub - n_exp) - npb, 0), 0)),
        ),
        scratch_shapes=[
            pltpu.VMEM((sort_rows, d), bf16),
            pltpu.VMEM((sort_rows, d), f32),
            pltpu.VMEM((loc_rows, d), bf16),
        ],
    )
    return pl.pallas_call(
        functools.partial(_moe_kernel, nsub=nsub, ne=ne, nf=nf, chunks=chunks,
                          nsb_prompt=n_prompt_rows // t_super),
        out_shape=(jax.ShapeDtypeStruct((n_prompt_rows, d), f32), jax.ShapeDtypeStruct((r - n_prompt_rows, d), f32)),
        grid_spec=grid_spec,
        compiler_params=_cparams("arbitrary", "arbitrary"),
        name="moe_sparse",
    )(*scalars, h, meta, x, mod, w1, w3, w2)


def kernel(x_prompt, x_sample, c, cache_dattn_k, cache_dattn_v, state_mlstm_c, state_mlstm_n, state_mlstm_m,
           c_ctx, w_ada, b_ada, w_in_ab, conv_w, conv_b, gate_b, qn_g, kn_g, lam_q1, lam_k1, lam_q2, lam_k2,
           m_norm_g, a_norm_g, w_out_ab, ff_w1, ff_w3, ff_w2, w_in_c, b_in_c, c_ln_g, c_ln_b, c_ws, c_bs,
           w_out_c, w_router, ex_w1, ex_w3, ex_w2):
    bp, seq, d = x_prompt.shape
    bs, dec_seq, _ = x_sample.shape
    depth = w_ada.shape[0]
    n_prompt_rows = bp * seq
    assert n_prompt_rows % dec_seq == 0 and seq % M_CHUNK == 0 and dec_seq % MOE_SUPER_BLOCK == 0
    nh = M_HEADS

    mods = _ada_table(jnp.concatenate([c_ctx[None], c], axis=0), w_ada, b_ada)
    x = (x_prompt.reshape(n_prompt_rows, d), x_sample.reshape(bs * dec_seq, d))
    rows = (n_prompt_rows, dec_seq)

    def joined(v):
        return jnp.concatenate(v, axis=0) if isinstance(v, tuple) else v

    def parts(v):
        return v if isinstance(v, tuple) else (v[:n_prompt_rows], v[n_prompt_rows:])

    new_k, new_v, new_c, new_n, new_m = [], [], [], [], []
    for l in range(depth):
        j = l // 2
        mod = mods[l]
        if l % 2 == 0:
            lam_init = 0.8 - 0.6 * math.exp(-0.3 * l)
            lam = (jnp.exp(jnp.sum((lam_q1[j] * lam_k1[j]).astype(f32)))
                   - jnp.exp(jnp.sum((lam_q2[j] * lam_k2[j]).astype(f32))) + lam_init).reshape(1)
            o3 = 4 * M_W
            o4 = o3 + 4 * nh
            w = w_in_ab[j]
            w_main = jnp.concatenate([w[:, :o3], w[:, o4:]], axis=1).astype(bf16)
            w_gate = jnp.zeros((d, LANES), f32).at[:, :4 * nh].set(w[:, o3:o4]).astype(bf16)
            x = parts(x)
            nxt = j if l + 1 < depth else None
            z, gates, *ex2 = _inproj(*x, mod, w_main, w_gate, *rows,
                                     job=None if nxt is None else functools.partial(_cast_job_rows, ex_w2[nxt]))
            gb = jnp.zeros((1, LANES), f32).at[0, :4 * nh].set(gate_b[j])
            mng = m_norm_g[j].reshape(1, M_W)
            cb = conv_b[j].reshape(1, 2 * M_W)
            hm_p, c_f, n_f, m_f = _mlstm(z, gates, gb, conv_w[j], cb, mng, None, batch=bp, t=seq, row_block0=0)
            state0 = (state_mlstm_c[:, j], state_mlstm_n[:, j].reshape(bs, 2, nh, 1, M_DIM),
                      state_mlstm_m[:, j].reshape(bs, 2, nh, 1, 1))
            hm_s, _, _, _ = _mlstm(z, gates, gb, conv_w[j], cb, mng, state0,
                                   batch=bs, t=dec_seq, row_block0=n_prompt_rows // dec_seq)
            new_c.append(c_f)
            new_n.append(n_f.reshape(bp, 2, nh, M_DIM))
            new_m.append(m_f.reshape(bp, 2, nh))

            qg2 = jnp.tile(qn_g[j], 2).reshape(1, LANES)
            kg2 = jnp.tile(kn_g[j], 2).reshape(1, LANES)
            ag = a_norm_g[j].reshape(1, LANES)
            ha_p, k_ctx, v_ctx = _attn(z, lam, qg2, kg2, ag, batch=bp, t=seq, row_block0=0,
                                       out_scale=1.0 - lam_init)
            kct = cache_dattn_k[:, j].transpose(0, 1, 2, 4, 3).reshape(bs, A_HEADS, LANES, -1)
            (ha_s,) = _attn(z, lam, qg2, kg2, ag, batch=bs, t=dec_seq, row_block0=n_prompt_rows // dec_seq,
                            rope_tabs=_rope_tables(dec_seq), ctx_kv=(kct, cache_dattn_v[:, j]),
                            out_scale=1.0 - lam_init)
            new_k.append(k_ctx)
            new_v.append(v_ctx)
            x = _outproj(*x, hm_p, hm_s, ha_p, ha_s, mod, w_out_ab[j].astype(bf16), *rows)
            x, *ex1 = _ffn(x, mod, ff_w1[j].astype(bf16), ff_w3[j].astype(bf16), ff_w2[j].astype(bf16), *rows,
                           job=None if nxt is None else functools.partial(_cast_job_tiled, ex_w1[nxt], tf=MOE_FF_TILE))
        else:
            e = w_out_c.shape[1]
            bs_full = jnp.repeat(c_bs[j].T, e // C_GROUPS, axis=1)
            x, ex3 = _gmlp(joined(x), mod, w_in_c[j].astype(bf16), b_in_c[j].reshape(1, -1),
                           c_ln_g[j].reshape(1, -1), c_ln_b[j].reshape(1, -1), c_ws[j].astype(bf16), bs_full,
                           w_out_c[j].astype(bf16), *rows,
                           job=functools.partial(_cast_job_tiled, ex_w3[j], tf=MOE_FF_TILE))
            wr = jnp.zeros((d, LANES), f32).at[:, :N_EXPERTS].set(w_router[j])
            wr_hi = wr.astype(bf16)
            wr_lo = (wr - wr_hi.astype(f32)).astype(bf16)
            h, meta, counts = _router(x, mod, wr_hi, wr_lo, *rows)
            x = _moe(x, h, meta, counts, mod, ex1[0], ex3, ex2[0].reshape(ex_w2[j].shape), *rows)

    y_prompt, y_sample = parts(x)
    y_prompt = y_prompt.reshape(bp, seq, d)
    y_sample = y_sample.reshape(bs, dec_seq, d)
    return (y_prompt, y_sample, jnp.stack(new_k, axis=1), jnp.stack(new_v, axis=1),
            jnp.stack(new_c, axis=1), jnp.stack(new_n, axis=1), jnp.stack(new_m, axis=1))
```

```python
import functools
import math
from typing import Callable, NamedTuple

import jax
import jax.numpy as jnp
import numpy as np
from jax import lax
from jax.experimental import pallas as pl
from jax.experimental.pallas import tpu as pltpu

f32 = jnp.float32
bf16 = jnp.bfloat16

D_MODEL = 1024
M_HEADS = 4
M_DIM = 128
M_W = M_HEADS * M_DIM
M_CHUNK = 128
A_HEADS = 4
A_VDIM = 128
A_DIM = 64
A_W = A_HEADS * A_VDIM
GRID_W = 64
ROPE_THETA = 10000.0
C_CHUNK = 128
C_GROUPS = 4
N_EXPERTS = 8
EPS = 1e-6

LANES = 128
MXU_COLS = 256
ROW_TILE = 512
PIECE_ALIGN = 16
MOE_SUPER_BLOCK = 2048
MOE_SORT_BLOCK = 512
MOE_CHUNKS = (128, 192, 256, 320, 384, 448, 512, 576, 640)
MOE_FF_TILE = 896
VMEM_LIMIT = 60 * 1024 * 1024


def _cparams(*sem):
    return pltpu.CompilerParams(dimension_semantics=tuple(sem), vmem_limit_bytes=VMEM_LIMIT)


def _const_spec(shape):
    nd = len(shape)
    return pl.BlockSpec(shape, lambda *_: (0,) * nd, pipeline_mode=pl.Buffered(1))


def _sigmoid(x):
    return 1.0 / (1.0 + jnp.exp(-x))


def _silu(x):
    return x * _sigmoid(x)


def _log_sigmoid(x):
    return jnp.minimum(x, 0.0) - jnp.log(1.0 + jnp.exp(-jnp.abs(x)))


def _rms(x):
    return x * lax.rsqrt(jnp.mean(x * x, axis=-1, keepdims=True) + EPS)


def _modulate(x, mod_ref, first):
    shift = mod_ref[first:first + 1, :]
    scale = mod_ref[first + 1:first + 2, :]
    return _rms(x) * (1.0 + scale) + shift


def _dot(a, b):
    return jnp.dot(a, b, preferred_element_type=f32)


def _dot_nt(a, b):
    return lax.dot_general(a, b, (((1,), (1,)), ((), ())), preferred_element_type=f32)


def _dot_tn(a, b):
    return lax.dot_general(a, b, (((0,), (0,)), ((), ())), preferred_element_type=f32)


def _split3(x):
    hi = x.astype(bf16)
    r1 = x - hi.astype(f32)
    mid = r1.astype(bf16)
    lo = (r1 - mid.astype(f32)).astype(bf16)
    return hi, mid, lo


def _group_of_tile(i, tm, n_prompt_rows, dec_seq):
    pt = n_prompt_rows // tm
    return jnp.where(i < pt, 0, 1 + (i - pt) // (dec_seq // tm))


def _ada_kernel(cv_ref, w_ref, b_ref, o_ref):
    a = _silu(cv_ref[...]).astype(bf16)
    o_ref[...] = _dot(a, w_ref[...].astype(bf16)) + b_ref[...]


def _ada_table(cv, w_ada, b_ada):
    depth, d, n = w_ada.shape
    g = cv.shape[0]
    gp = 8 * ((g + 7) // 8)
    cvp = jnp.zeros((gp, d), f32).at[:g].set(cv)
    tn = 1536
    out = pl.pallas_call(
        _ada_kernel,
        out_shape=jax.ShapeDtypeStruct((depth, gp, n), f32),
        grid=(depth, n // tn),
        in_specs=[
            pl.BlockSpec((gp, d), lambda l, j: (0, 0)),
            pl.BlockSpec((None, d, tn), lambda l, j: (l, 0, j)),
            pl.BlockSpec((None, 1, tn), lambda l, j: (l, 0, j)),
        ],
        out_specs=pl.BlockSpec((None, gp, tn), lambda l, j: (l, 0, j)),
        compiler_params=_cparams("arbitrary", "arbitrary"),
        name="ada_table",
    )(cvp, w_ada, b_ada.reshape(depth, 1, n))
    return out[:, :g].reshape(depth, g, 6, d)


def _part_specs(tm, width, npt):
    first = pl.BlockSpec((tm, width), lambda i: (jnp.minimum(i, npt - 1), 0))
    second = pl.BlockSpec((tm, width), lambda i: (jnp.maximum(i - npt, 0), 0))
    return first, second


def _pick(a_ref, b_ref, npt):
    return jnp.where(pl.program_id(0) < npt, a_ref[...], b_ref[...])


class _CastJob(NamedTuple):
    src: jax.Array
    in_spec: pl.BlockSpec
    out_spec: pl.BlockSpec
    out_shape: jax.ShapeDtypeStruct
    body: Callable


def _cast_job_tiled(src, steps, tf):
    ne, d, ff = src.shape
    bands = steps // ne
    rows = d // bands
    nf = ff // tf

    def body(i_ref, o_ref):
        for f in range(nf):
            o_ref[f] = i_ref[:, f * tf:(f + 1) * tf].astype(bf16)

    return _CastJob(src, pl.BlockSpec((None, rows, ff), lambda i: (i // bands, i % bands, 0)),
                    pl.BlockSpec((None, nf, rows, tf), lambda i: (i // bands, 0, i % bands, 0)),
                    jax.ShapeDtypeStruct((ne, nf, d, tf), bf16), body)


def _cast_job_rows(src, steps):
    ne, ff, d = src.shape
    rows = ne * ff // steps

    def body(i_ref, o_ref):
        o_ref[...] = i_ref[...].astype(bf16)

    return _CastJob(src.reshape(ne * ff, d), pl.BlockSpec((rows, d), lambda i: (i, 0)),
                    pl.BlockSpec((rows, d), lambda i: (i, 0)), jax.ShapeDtypeStruct((ne * ff, d), bf16), body)


def _call_with_job(kernel_fn, job, *, out_shape, in_specs, out_specs, args, **kw):
    if job is None:
        return pl.pallas_call(kernel_fn, out_shape=out_shape, in_specs=in_specs, out_specs=out_specs, **kw)(*args)
    n_in, n_out = len(in_specs), len(out_shape)

    def with_job(*refs):
        kernel_fn(*refs[:n_in], *refs[n_in + 1:n_in + 1 + n_out], *refs[n_in + 2 + n_out:])
        job.body(refs[n_in], refs[n_in + 1 + n_out])

    return pl.pallas_call(with_job, out_shape=(*out_shape, job.out_shape), in_specs=[*in_specs, job.in_spec],
                          out_specs=(*out_specs, job.out_spec), **kw)(*args, job.src)


def _inproj_kernel(xp_ref, xs_ref, mod_ref, w_ref, wg_ref, z_ref, g_ref, *, n_main, npt):
    h = _modulate(_pick(xp_ref, xs_ref, npt), mod_ref, 0).astype(bf16)
    step = 512
    for j in range(n_main // step):
        z_ref[:, j * step:(j + 1) * step] = _dot(h, w_ref[:, j * step:(j + 1) * step]).astype(bf16)
    g_ref[...] = _dot(h, wg_ref[...])


def _inproj(xp, xs, mod, w_main, w_gate, n_prompt_rows, dec_seq, job=None):
    d = xp.shape[1]
    r = xp.shape[0] + xs.shape[0]
    tm = ROW_TILE
    npt = n_prompt_rows // tm
    n_main = w_main.shape[1]
    grp = functools.partial(_group_of_tile, tm=tm, n_prompt_rows=n_prompt_rows, dec_seq=dec_seq)
    return _call_with_job(
        functools.partial(_inproj_kernel, n_main=n_main, npt=npt), job(r // tm) if job else None,
        out_shape=(jax.ShapeDtypeStruct((r, n_main), bf16), jax.ShapeDtypeStruct((r, LANES), f32)),
        grid=(r // tm,),
        in_specs=[
            *_part_specs(tm, d, npt),
            pl.BlockSpec((None, 6, d), lambda i: (grp(i), 0, 0)),
            _const_spec(w_main.shape),
            _const_spec(w_gate.shape),
        ],
        out_specs=(pl.BlockSpec((tm, n_main), lambda i: (i, 0)), pl.BlockSpec((tm, LANES), lambda i: (i, 0))),
        args=(xp, xs, mod, w_main, w_gate),
        compiler_params=_cparams("arbitrary"),
        name="inproj_ab",
    )


def _conv_silu(x, w_ref, b_ref, t):
    row = lax.broadcasted_iota(jnp.int32, (t, 1), 0)
    prev = jnp.where(row == 0, 0.0, pltpu.roll(x, 1, 0))
    nxt = jnp.where(row == t - 1, 0.0, pltpu.roll(x, t - 1, 0))
    y = b_ref[...] + prev * w_ref[0:1, :] + x * w_ref[1:2, :] + nxt * w_ref[2:3, :]
    return _silu(y)


def _mlstm_kernel(q_ref, k_ref, v_ref, o_ref, g_ref, gb_ref, cw_ref, cb_ref, ng_ref, *rest, t, has_state):
    if has_state:
        c0_ref, n0_ref, m0_ref, *rest = rest
    hm_ref, c_out, n_out, m_out, q_s, qt_s, kh_s, kl_s, vt_s, gs_s, hft_s, hbt_s, ct_s, sel_s = rest
    nc = t // M_CHUNK
    lc = M_CHUNK
    nh = M_HEADS
    w = M_W
    q_s[...] = _conv_silu(q_ref[...].astype(f32), cw_ref[:, :w], cb_ref[:, :w], t)
    kf = _conv_silu(k_ref[...].astype(f32), cw_ref[:, w:], cb_ref[:, w:], t) * (M_DIM ** -0.5)
    kh = kf.astype(bf16)
    kh_s[...] = kh
    kl_s[...] = (kf - kh.astype(f32)).astype(bf16)
    sel_row = lax.broadcasted_iota(jnp.int32, (3 * LANES, LANES), 0)
    for d in range(2):
        for h in range(nh):
            ci, ln = d * nh + h, 2 * nh * d + h
            ct_s[ci] = c0_ref[d, h].T if has_state else jnp.zeros((M_DIM, M_DIM), f32)
            sel_s[ci] = jnp.where(sel_row == ln, 1.0, jnp.where(sel_row == LANES + ln, 1.0, jnp.where(
                sel_row == 2 * LANES + ln, 1.0, 0.0))).astype(bf16)
    if has_state:
        n_out[...] = n0_ref[...]
        m_out[...] = m0_ref[...]
    else:
        n_out[...] = jnp.zeros_like(n_out)
        m_out[...] = jnp.zeros_like(m_out)

    lane = lax.broadcasted_iota(jnp.int32, (1, LANES), 1)
    is_lf = ((lane >= nh) & (lane < 2 * nh)) | ((lane >= 3 * nh) & (lane < 4 * nh))
    is_bw_lf = (lane >= 3 * nh) & (lane < 4 * nh)
    g = g_ref[...] + gb_ref[...]
    gs_s[...] = jnp.where(is_lf, _log_sigmoid(g), g)

    rr = lax.broadcasted_iota(jnp.int32, (lc, lc), 0)
    cc = lax.broadcasted_iota(jnp.int32, (lc, lc), 1)
    tri_incl = jnp.where(rr >= cc, 1.0, 0.0).astype(bf16)

    def prep_body(c, _):
        off = pl.multiple_of(c * lc, lc)
        tile = gs_s[pl.ds(off, lc), :]
        hi, mid, lo = _split3(jnp.where(is_lf, tile, 0.0))
        cs = _dot(tri_incl, hi) + _dot(tri_incl, mid) + _dot(tri_incl, lo)
        total = jnp.broadcast_to(cs[lc - 1:lc, :], (lc, LANES))
        b = jnp.where(is_bw_lf, total - cs + tile, cs)
        lmb = tile - pltpu.roll(b, LANES - nh, 1)
        gv = pltpu.roll(total, LANES - nh, 1) + lmb
        low = jnp.where(is_lf, b, lmb)
        high = pltpu.roll(jnp.where(is_lf, total, gv), 4 * nh, 1)
        gs_s[pl.ds(off, lc), :] = jnp.where(lane < 4 * nh, low, jnp.where(lane < 8 * nh, high, 0.0))
        wide = pl.ds(pl.multiple_of(c * w, w), w)
        qt_s[wide, :] = q_s[pl.ds(off, lc), :].T
        vt_s[wide, :] = v_ref[pl.ds(off, lc), :].astype(f32).T.astype(bf16)
        return 0

    lax.fori_loop(0, nc, prep_body, 0)

    sub8 = lax.broadcasted_iota(jnp.int32, (8, LANES), 0)

    def two_rows(x):
        hi = x.astype(bf16).astype(f32)
        return jnp.where(sub8 == 0, hi, jnp.where(sub8 == 1, x - hi, 0.0)).astype(bf16)

    def issue(c, h, d, tile3, tile_t):
        off = pl.multiple_of(c * lc, lc)
        col = slice(h * M_DIM, (h + 1) * M_DIM)
        ci, ln = d * nh + h, 2 * nh * d + h
        head = pl.ds(pl.multiple_of(c * w + h * M_DIM, M_DIM), M_DIM)
        b_row = tile_t[ln + nh:ln + nh + 1, :]
        g_row = tile_t[ln + 4 * nh:ln + 4 * nh + 1, :]
        total = tile_t[ln + 5 * nh:ln + 5 * nh + 1, 0:1]
        ct, nm, mm = ct_s[ci], n_out[d, h], m_out[d, h]
        qt = qt_s[head, :]
        qtb = qt.astype(bf16)
        khb = kh_s[pl.ds(off, lc), col]
        vtb = vt_s[head, :]
        lmb = _dot(tile3, sel_s[ci])
        sraw = _dot(khb, qtb)
        qn = _dot(two_rows(nm), qtb)
        lhs = jnp.concatenate([ct.astype(bf16), vtb], axis=1)
        m_new = jnp.maximum(mm + total, jnp.max(g_row, axis=1, keepdims=True))
        decay = jnp.exp(mm + total - m_new)
        ew = jnp.exp(g_row - m_new)
        ct_s[ci] = decay * ct + _dot(vtb * ew.astype(bf16), khb)
        ew2 = two_rows(ew)
        nk_h = _dot(ew2, khb)
        nk_l = _dot(ew2, kl_s[pl.ds(off, lc), col])
        n_out[d, h] = decay * nm + (nk_h[0:1, :] + nk_h[1:2, :] + nk_l[0:1, :])
        m_out[d, h] = m_new
        return dict(d=d, head=head, b_row=b_row, mm=mm, qt=qt, lmb=lmb, sraw=sraw, qn=qn, lhs=lhs)

    def weigh(st):
        keep = (cc >= rr) if st["d"] == 0 else (rr >= cc)
        dmat = jnp.where(keep, st["lmb"] + st["b_row"], -jnp.inf)
        inter = st["mm"] + st["b_row"]
        mt = jnp.maximum(inter, jnp.max(dmat, axis=0, keepdims=True))
        w_inter = jnp.exp(inter - mt)
        s = st["sraw"] * jnp.exp(dmat - mt)
        qn = st["qn"]
        den = w_inter * (qn[0:1, :] + qn[1:2, :]) + jnp.sum(s, axis=0, keepdims=True)
        inv = 1.0 / jnp.maximum(jnp.abs(den), jnp.exp(-mt))
        return jnp.concatenate([st["qt"] * (w_inter * inv), s * inv], axis=0).astype(bf16)

    def body(i, _):
        states = []
        for d in range(2):
            c = i if d == 0 else nc - 1 - i
            tile = gs_s[pl.ds(pl.multiple_of(c * lc, lc), lc), :]
            tile_t = tile.T
            hi, mid, lo = _split3(tile)
            tile3 = jnp.concatenate([hi, mid, lo], axis=1)
            states += [issue(c, h, d, tile3, tile_t) for h in range(nh)]
        rhss = [weigh(st) for st in states]
        for st, rhs in zip(states, rhss):
            hct = _dot(st["lhs"], rhs)
            if st["d"] == 0:
                hft_s[st["head"], :] = hct
            else:
                hbt_s[st["head"], :] = hct
        return 0

    lax.fori_loop(0, nc, body, 0)

    def out_body(c, _):
        rows = pl.ds(pl.multiple_of(c * lc, lc), lc)
        wide = pl.ds(pl.multiple_of(c * w, w), w)
        hsum = (hft_s[wide, :] + hbt_s[wide, :]).T
        for h in range(nh):
            col = slice(h * M_DIM, (h + 1) * M_DIM)
            mo = _sigmoid(o_ref[rows, col].astype(f32))
            hm_ref[rows, col] = (_rms(hsum[:, col]) * ng_ref[:, col] * mo).astype(hm_ref.dtype)
        return 0

    lax.fori_loop(0, nc, out_body, 0)
    for d in range(2):
        for h in range(nh):
            c_out[d, h] = ct_s[d * nh + h].T


def _mlstm(z, gates, gate_b, conv_w, conv_b, m_norm_g, state0, *, batch, t, row_block0):
    nh = M_HEADS
    w = M_W
    mode = dict(pipeline_mode=pl.Buffered(1)) if t * w * 2 > (1 << 20) else {}
    seq = lambda colblk: pl.BlockSpec((t, w), lambda b: (row_block0 + b, colblk), **mode)
    state = lambda *tail: pl.BlockSpec((None, 2, nh) + tail, lambda b: (b,) + (0,) * (2 + len(tail)))
    state_specs = [state(M_DIM, M_DIM), state(1, M_DIM), state(1, 1)]
    has_state = state0 is not None
    return pl.pallas_call(
        functools.partial(_mlstm_kernel, t=t, has_state=has_state),
        out_shape=(
            jax.ShapeDtypeStruct((batch * t, w), bf16),
            jax.ShapeDtypeStruct((batch, 2, nh, M_DIM, M_DIM), f32),
            jax.ShapeDtypeStruct((batch, 2, nh, 1, M_DIM), f32),
            jax.ShapeDtypeStruct((batch, 2, nh, 1, 1), f32),
        ),
        grid=(batch,),
        in_specs=[
            seq(0), seq(1), seq(2), seq(3),
            pl.BlockSpec((t, LANES), lambda b: (row_block0 + b, 0)),
            _const_spec(gate_b.shape),
            _const_spec(conv_w.shape), _const_spec(conv_b.shape), _const_spec(m_norm_g.shape),
            *(state_specs if has_state else []),
        ],
        out_specs=(pl.BlockSpec((t, w), lambda b: (b, 0)), *state_specs),
        scratch_shapes=[
            pltpu.VMEM((t, w), f32),
            pltpu.VMEM((t * nh, M_DIM), f32),
            pltpu.VMEM((t, w), bf16),
            pltpu.VMEM((t, w), bf16),
            pltpu.VMEM((t * nh, M_DIM), bf16),
            pltpu.VMEM((t, LANES), f32),
            pltpu.VMEM((t * nh, M_DIM), f32),
            pltpu.VMEM((t * nh, M_DIM), f32),
            pltpu.VMEM((2 * nh, M_DIM, M_DIM), f32),
            pltpu.VMEM((2 * nh, 3 * LANES, LANES), bf16),
        ],
        compiler_params=_cparams("arbitrary"),
        name=f"mlstm_t{t}",
    )(z, z, z, z, gates, gate_b, conv_w, conv_b, m_norm_g, *(state0 if has_state else ()))


def _pair_norm(x, gain):
    lane = lax.broadcasted_iota(jnp.int32, (1, LANES), 1)
    first = lane < A_DIM
    sq = x * x
    s_all = jnp.sum(sq, axis=-1, keepdims=True)
    s0 = jnp.sum(jnp.where(first, sq, 0.0), axis=-1, keepdims=True)
    inv0 = lax.rsqrt(s0 * (1.0 / A_DIM) + EPS)
    inv1 = lax.rsqrt((s_all - s0) * (1.0 / A_DIM) + EPS)
    return x * jnp.where(first, inv0, inv1) * gain


def _rope(x, cos, sin_signed):
    lane = lax.broadcasted_iota(jnp.int32, (1, LANES), 1)
    nf = A_DIM // 4
    partner = jnp.where((lane % (2 * nf)) < nf, pltpu.roll(x, LANES - nf, 1), pltpu.roll(x, nf, 1))
    return x * cos + partner * sin_signed


def _attn_kernel(*refs, t, tq, rope, ctx, out_scale):
    it = iter(refs)
    lam_ref = next(it)
    q_ref, k_ref, v_ref = next(it), next(it), next(it)
    qg_ref, kg_ref, ag_ref = next(it), next(it), next(it)
    if rope:
        cosq_ref, sinq_ref, cosk_ref, sink_ref = next(it), next(it), next(it), next(it)
    if ctx:
        kct_ref, vc_ref = next(it), next(it)
    ha_ref = next(it)
    if not ctx:
        newk_ref, newv_ref = next(it), next(it)
    kt_s = next(it)
    nh = A_HEADS

    @pl.when(pl.program_id(1) == 0)
    def _():
        for h in range(nh):
            col = slice(h * LANES, (h + 1) * LANES)
            kn = _pair_norm(k_ref[:, col].astype(f32), kg_ref[...])
            if not ctx:
                newk_ref[h, 0] = kn[:, :A_DIM]
                newk_ref[h, 1] = kn[:, A_DIM:]
                newv_ref[h] = v_ref[:, col].astype(f32)
            if rope:
                kn = _rope(kn, cosk_ref[...], sink_ref[...])
            kt_s[col, :] = kn.T.astype(bf16)

    lane = lax.broadcasted_iota(jnp.int32, (1, LANES), 1)
    lam = lam_ref[0]
    for h in range(nh):
        col = slice(h * LANES, (h + 1) * LANES)
        q = _pair_norm(q_ref[:, col].astype(f32), qg_ref[...])
        if rope:
            q = _rope(q, cosq_ref[...], sinq_ref[...])
        q = q * (A_DIM ** -0.5 * math.log2(math.e))
        qs = [jnp.where(lane < A_DIM, q, 0.0).astype(bf16), jnp.where(lane >= A_DIM, q, 0.0).astype(bf16)]
        kt = kt_s[col, :]
        vb = v_ref[:, col]
        if ctx:
            kctb = kct_ref[h].astype(bf16)
            vcb = vc_ref[h].astype(bf16)
        outs = []
        for i in range(2):
            sn = _dot(qs[i], kt)
            mx = jnp.max(sn, axis=-1, keepdims=True)
            if ctx:
                sc = _dot(qs[i], kctb)
                mx = jnp.maximum(mx, jnp.max(sc, axis=-1, keepdims=True))
            en = jnp.exp2(sn - mx)
            den = jnp.sum(en, axis=-1, keepdims=True)
            o = _dot(en.astype(bf16), vb)
            if ctx:
                ec = jnp.exp2(sc - mx)
                den = den + jnp.sum(ec, axis=-1, keepdims=True)
                o = o + _dot(ec.astype(bf16), vcb)
            outs.append(o * (1.0 / den))
        o = outs[0] - lam * outs[1]
        ha_ref[:, col] = (_rms(o) * ag_ref[...] * out_scale).astype(ha_ref.dtype)


def _attn(z, lam, qg2, kg2, a_norm_g, *, batch, t, row_block0, rope_tabs=None, ctx_kv=None, out_scale):
    nh = A_HEADS
    w = A_W
    tq = min(t, 256)
    nq = t // tq
    rope = rope_tabs is not None
    ctx = ctx_kv is not None
    qblk = 4 * M_W // w
    vec = pl.BlockSpec((1, LANES), lambda b, i: (0, 0))
    in_specs = [
        pl.BlockSpec(memory_space=pltpu.SMEM),
        pl.BlockSpec((tq, w), lambda b, i: ((row_block0 + b) * nq + i, qblk)),
        pl.BlockSpec((t, w), lambda b, i: (row_block0 + b, qblk + 1)),
        pl.BlockSpec((t, w), lambda b, i: (row_block0 + b, qblk + 2)),
        vec, vec, vec,
    ]
    args = [lam, z, z, z, qg2, kg2, a_norm_g]
    if rope:
        cos, sin = rope_tabs
        in_specs += [pl.BlockSpec((tq, LANES), lambda b, i: (i, 0))] * 2
        in_specs += [pl.BlockSpec((t, LANES), lambda b, i: (0, 0))] * 2
        args += [cos, sin, cos, sin]
    if ctx:
        kct, vc = ctx_kv
        in_specs += [pl.BlockSpec((None,) + kct.shape[1:], lambda b, i: (b, 0, 0, 0)),
                     pl.BlockSpec((None,) + vc.shape[1:], lambda b, i: (b, 0, 0, 0))]
        args += [kct, vc]
    out_shape = [jax.ShapeDtypeStruct((batch * t, w), bf16)]
    out_specs = [pl.BlockSpec((tq, w), lambda b, i: (b * nq + i, 0))]
    if not ctx:
        out_shape += [jax.ShapeDtypeStruct((batch, nh, 2, t, A_DIM), f32),
                      jax.ShapeDtypeStruct((batch, nh, t, A_VDIM), f32)]
        out_specs += [pl.BlockSpec((None, nh, 2, t, A_DIM), lambda b, i: (b, 0, 0, 0, 0)),
                      pl.BlockSpec((None, nh, t, A_VDIM), lambda b, i: (b, 0, 0, 0))]
    return pl.pallas_call(
        functools.partial(_attn_kernel, t=t, tq=tq, rope=rope, ctx=ctx, out_scale=out_scale),
        out_shape=tuple(out_shape),
        grid=(batch, nq),
        in_specs=in_specs,
        out_specs=tuple(out_specs),
        scratch_shapes=[pltpu.VMEM((w, t), bf16)],
        compiler_params=_cparams("arbitrary", "arbitrary"),
        name=f"diff_attn_t{t}",
    )(*args)


def _rope_tables(t):
    rows = t // GRID_W
    pos_row = np.repeat(np.arange(rows, dtype=np.float64), GRID_W)
    pos_col = (np.arange(rows * GRID_W) % GRID_W).astype(np.float64)
    nf = A_DIM // 4
    inv = ROPE_THETA ** (-np.arange(nf, dtype=np.float64) / nf)
    lane = np.arange(LANES)
    j = lane % (2 * nf)
    use_col = (lane % A_DIM) >= (A_DIM // 2)
    ang = np.where(use_col[None, :], pos_col[:, None], pos_row[:, None]) * inv[j % nf][None, :]
    sign = np.where(j < nf, -1.0, 1.0)[None, :]
    return jnp.asarray(np.cos(ang), f32), jnp.asarray(np.sin(ang) * sign, f32)


def _outproj_kernel(xp_ref, xs_ref, hmp_ref, hms_ref, hap_ref, has_ref, mod_ref, w_ref, o_ref, *, npt):
    half = hmp_ref.shape[1]
    hm = _pick(hmp_ref, hms_ref, npt)
    ha = _pick(hap_ref, has_ref, npt)
    y = _dot(hm, w_ref[:half, :]) + _dot(ha, w_ref[half:, :])
    o_ref[...] = _pick(xp_ref, xs_ref, npt) + mod_ref[2:3, :] * y


def _outproj(xp, xs, hm_p, hm_s, ha_p, ha_s, mod, w, n_prompt_rows, dec_seq):
    d = xp.shape[1]
    r = xp.shape[0] + xs.shape[0]
    tm = ROW_TILE
    npt = n_prompt_rows // tm
    grp = functools.partial(_group_of_tile, tm=tm, n_prompt_rows=n_prompt_rows, dec_seq=dec_seq)
    return pl.pallas_call(
        functools.partial(_outproj_kernel, npt=npt),
        out_shape=jax.ShapeDtypeStruct((r, d), f32),
        grid=(r // tm,),
        in_specs=[
            *_part_specs(tm, d, npt),
            *_part_specs(tm, hm_p.shape[1], npt),
            *_part_specs(tm, ha_p.shape[1], npt),
            pl.BlockSpec((None, 6, d), lambda i: (grp(i), 0, 0)),
            _const_spec(w.shape),
        ],
        out_specs=pl.BlockSpec((tm, d), lambda i: (i, 0)),
        compiler_params=_cparams("arbitrary"),
        name="outproj_ab",
    )(xp, xs, hm_p, hm_s, ha_p, ha_s, mod, w)


def _ffn_kernel(x_ref, mod_ref, w1_ref, w3_ref, w2_ref, o_ref, *, chunks):
    x = x_ref[...]
    h = _modulate(x, mod_ref, 3).astype(bf16)
    acc = jnp.zeros(x.shape, f32)
    for lo, hi in chunks:
        ab = _dot(h, jnp.concatenate([w1_ref[:, lo:hi], w3_ref[:, lo:hi]], axis=1))
        a = _silu(ab[:, :hi - lo]) * ab[:, hi - lo:]
        acc = acc + _dot(a.astype(bf16), w2_ref[lo:hi, :])
    o_ref[...] = x + mod_ref[5:6, :] * acc


def _ffn(x, mod, w1, w3, w2, n_prompt_rows, dec_seq, job=None):
    r, d = x.shape
    tm = ROW_TILE
    grp = functools.partial(_group_of_tile, tm=tm, n_prompt_rows=n_prompt_rows, dec_seq=dec_seq)
    ff = w1.shape[1]
    cut = MXU_COLS * ((ff // MXU_COLS + 1) // 2) if ff % MXU_COLS == 0 else ff
    chunks = ((0, cut), (cut, ff)) if cut < ff else ((0, ff),)
    return _call_with_job(
        functools.partial(_ffn_kernel, chunks=chunks), job(r // tm) if job else None,
        out_shape=(jax.ShapeDtypeStruct((r, d), f32),),
        grid=(r // tm,),
        in_specs=[
            pl.BlockSpec((tm, d), lambda i: (i, 0)),
            pl.BlockSpec((None, 6, d), lambda i: (grp(i), 0, 0)),
            _const_spec(w1.shape), _const_spec(w3.shape), _const_spec(w2.shape),
        ],
        out_specs=(pl.BlockSpec((tm, d), lambda i: (i, 0)),),
        args=(x, mod, w1, w3, w2),
        compiler_params=_cparams("arbitrary"),
        name="ffn_dense",
    )


def _gelu_tanh(x):
    return 0.5 * x * (1.0 + jnp.tanh(math.sqrt(2.0 / math.pi) * (x + 0.044715 * (x * x * x))))


def _gmlp_kernel(x_ref, mod_ref, win_ref, bin_ref, lng_ref, lnb_ref, ws_ref, bs_ref, wout_ref, o_ref, us_s):
    x = x_ref[...]
    tm = x.shape[0]
    e = wout_ref.shape[0]
    ge = e // C_GROUPS
    h = _modulate(x, mod_ref, 0).astype(bf16)
    u = _gelu_tanh(_dot(h, win_ref[:, :e]) + bin_ref[:, :e])
    v = _gelu_tanh(_dot(h, win_ref[:, e:]) + bin_ref[:, e:])
    mu = jnp.mean(v, axis=-1, keepdims=True)
    vc = v - mu
    var = jnp.mean(vc * vc, axis=-1, keepdims=True)
    vn = (vc * lax.rsqrt(var + EPS) * lng_ref[...] + lnb_ref[...]).astype(bf16)
    for n in range(tm // C_CHUNK):
        rows = slice(n * C_CHUNK, (n + 1) * C_CHUNK)
        for g in range(C_GROUPS):
            cols = slice(g * ge, (g + 1) * ge)
            s = _dot(ws_ref[g], vn[rows, cols]) + bs_ref[:, cols]
            us_s[rows, cols] = (u[rows, cols] * s).astype(bf16)
    o_ref[...] = x + mod_ref[2:3, :] * _dot(us_s[...], wout_ref[...])


def _gmlp(x, mod, w_in, b_in, ln_g, ln_b, ws, bs_full, w_out, n_prompt_rows, dec_seq, job=None):
    r, d = x.shape
    tm = ROW_TILE
    e = w_out.shape[0]
    grp = functools.partial(_group_of_tile, tm=tm, n_prompt_rows=n_prompt_rows, dec_seq=dec_seq)
    return _call_with_job(
        _gmlp_kernel, job(r // tm) if job else None,
        out_shape=(jax.ShapeDtypeStruct((r, d), f32),),
        grid=(r // tm,),
        in_specs=[
            pl.BlockSpec((tm, d), lambda i: (i, 0)),
            pl.BlockSpec((None, 6, d), lambda i: (grp(i), 0, 0)),
            _const_spec(w_in.shape), _const_spec(b_in.shape), _const_spec(ln_g.shape), _const_spec(ln_b.shape),
            _const_spec(ws.shape), _const_spec(bs_full.shape), _const_spec(w_out.shape),
        ],
        out_specs=(pl.BlockSpec((tm, d), lambda i: (i, 0)),),
        args=(x, mod, w_in, b_in, ln_g, ln_b, ws, bs_full, w_out),
        scratch_shapes=[pltpu.VMEM((tm, e), bf16)],
        compiler_params=_cparams("arbitrary"),
        name="gmlp",
    )


def _router_kernel(x_ref, mod_ref, wr_hi_ref, wr_lo_ref, h_ref, meta_ref, counts_ref):
    hf = _modulate(x_ref[...], mod_ref, 3)
    hb = hf.astype(bf16)
    h_ref[...] = hb
    h_lo = (hf - hb.astype(f32)).astype(bf16)
    logits = _dot(hb, wr_hi_ref[...]) + (_dot(hb, wr_lo_ref[...]) + _dot(h_lo, wr_hi_ref[...]))
    lane = lax.broadcasted_iota(jnp.int32, logits.shape, 1).astype(f32)
    logits = jnp.where(lane < N_EXPERTS, logits, -jnp.inf)
    m1 = jnp.max(logits, axis=-1, keepdims=True)
    i1 = jnp.min(jnp.where(logits == m1, lane, float(LANES)), axis=-1, keepdims=True)
    rest = jnp.where(lane == i1, -jnp.inf, logits)
    m2 = jnp.max(rest, axis=-1, keepdims=True)
    i2 = jnp.min(jnp.where(rest == m2, lane, float(LANES)), axis=-1, keepdims=True)
    e2 = jnp.exp(m2 - m1)
    w1 = 1.0 / (1.0 + e2)
    w2 = e2 * w1

    tm = logits.shape[0]
    cnt = jnp.where(lane == i1, 1.0, jnp.where(lane == i2, 1.0, 0.0))
    rr = lax.broadcasted_iota(jnp.int32, (tm, tm), 0)
    cc = lax.broadcasted_iota(jnp.int32, (tm, tm), 1)
    before = jnp.where(rr > cc, 1.0, 0.0).astype(bf16)
    rank = _dot(before, cnt.astype(bf16))
    counts = jnp.sum(cnt, axis=0, keepdims=True)
    padded = jnp.floor((counts + (PIECE_ALIGN - 1)) * (1.0 / PIECE_ALIGN)) * PIECE_ALIGN
    lane1 = lane[0:1, :]
    piece_off = jnp.zeros((1, LANES), f32)
    off = jnp.zeros((1, 1), f32)
    for e in range(N_EXPERTS):
        piece_off = jnp.where(lane1 == e, off, piece_off)
        off = off + padded[:, e:e + 1]
    local = piece_off + rank
    pos1 = jnp.sum(jnp.where(lane == i1, local, 0.0), axis=-1, keepdims=True)
    pos2 = jnp.sum(jnp.where(lane == i2, local, 0.0), axis=-1, keepdims=True)
    meta_ref[...] = jnp.where(lane == 0, pos1, jnp.where(lane == 1, pos2, jnp.where(lane == 2, w1,
                              jnp.where(lane == 3, w2, 0.0))))
    counts_ref[...] = counts


def _router(x, mod, wr_hi, wr_lo, n_prompt_rows, dec_seq):
    r, d = x.shape
    tm = MOE_SORT_BLOCK
    grp = functools.partial(_group_of_tile, tm=tm, n_prompt_rows=n_prompt_rows, dec_seq=dec_seq)
    return pl.pallas_call(
        _router_kernel,
        out_shape=(jax.ShapeDtypeStruct((r, d), bf16), jax.ShapeDtypeStruct((r, LANES), f32),
                   jax.ShapeDtypeStruct((r // tm, 1, LANES), f32)),
        grid=(r // tm,),
        in_specs=[
            pl.BlockSpec((tm, d), lambda i: (i, 0)),
            pl.BlockSpec((None, 6, d), lambda i: (grp(i), 0, 0)),
            _const_spec(wr_hi.shape), _const_spec(wr_lo.shape),
        ],
        out_specs=(pl.BlockSpec((tm, d), lambda i: (i, 0)), pl.BlockSpec((tm, LANES), lambda i: (i, 0)),
                   pl.BlockSpec((None, 1, LANES), lambda i: (i, 0, 0))),
        compiler_params=_cparams("arbitrary"),
        name="router",
    )(x, mod, wr_hi, wr_lo)


def _moe_kernel(n16_ref, loc_ref, dst_ref, seg_off_ref, seg_len_ref, csel_ref, cnum_ref,
                h_ref, meta_ref, x_ref, mod_ref, w1_ref, w3_ref, w2_ref, op_ref, os_ref,
                hs_s, ys_s, loc_s, *, nsub, ne, nf, chunks, nsb_prompt):
    sb = pl.program_id(0)
    p = pl.program_id(1)
    n_exp = ne * nf
    loc_rows = loc_s.shape[0]
    pa = PIECE_ALIGN

    def one_hot_cols(meta, v1, v2):
        lane = lax.broadcasted_iota(jnp.int32, (1, loc_rows), 1).astype(f32)
        return jnp.where(lane == meta[:, 0:1], v1, jnp.where(lane == meta[:, 1:2], v2, 0.0)).astype(bf16)

    def copy_pieces(blk, to_sorted):
        for e in range(ne):
            n = n16_ref[blk * ne + e]
            src = loc_ref[blk * ne + e]
            dst = dst_ref[blk * ne + e]

            def cp(i, _, src=src, dst=dst):
                a = pl.ds(pl.multiple_of(src + pa * i, pa), pa)
                b = pl.ds(pl.multiple_of(dst + pa * i, pa), pa)
                if to_sorted:
                    hs_s[b, :] = loc_s[a, :]
                else:
                    loc_s[a, :] = ys_s[b, :].astype(bf16)
                return 0

            lax.fori_loop(0, n, cp, 0)

    @pl.when(p < nsub)
    def _():
        @pl.when(p == 0)
        def _():
            hs_s[...] = jnp.zeros_like(hs_s)
            ys_s[...] = jnp.zeros_like(ys_s)

        pt = one_hot_cols(meta_ref[...], 1.0, 1.0)
        loc_s[...] = _dot_tn(pt, h_ref[...]).astype(bf16)
        copy_pieces(sb * nsub + p, True)

    @pl.when((p >= nsub) & (p < nsub + n_exp))
    def _():
        e = lax.div(p - nsub, jnp.int32(nf))
        start = seg_off_ref[sb * ne + e]
        ln = seg_len_ref[sb * ne + e]
        which = csel_ref[sb * ne + e]
        count = cnum_ref[sb * ne + e]

        def chunk(lo, size):
            r0 = jnp.minimum(lo, hs_s.shape[0] - size)
            rows = pl.ds(pl.multiple_of(r0, pa), size)
            xc = hs_s[rows, :]
            ab = _dot(xc, jnp.concatenate([w1_ref[...], w3_ref[...]], axis=1))
            tf = w1_ref.shape[1]
            a = _silu(ab[:, :tf]) * ab[:, tf:]
            y = _dot(a.astype(bf16), w2_ref[...])
            ri = r0 + lax.broadcasted_iota(jnp.int32, (size, 1), 0)
            ys_s[rows, :] += jnp.where(ri >= lo, jnp.where(ri < start + ln, y, 0.0), 0.0)

        for k, size in enumerate(chunks):
            @pl.when(which == k)
            def _(size=size):
                def body(j, _):
                    chunk(start + j * size, size)
                    return 0

                lax.fori_loop(0, count, body, 0)

    @pl.when(p >= nsub + n_exp)
    def _():
        copy_pieces(sb * nsub + (p - nsub - n_exp), False)
        meta = meta_ref[...]
        a = one_hot_cols(meta, meta[:, 2:3], meta[:, 3:4])
        y = x_ref[...] + mod_ref[5:6, :] * _dot(a, loc_s[...])

        @pl.when(sb < nsb_prompt)
        def _():
            op_ref[...] = y

        @pl.when(sb >= nsb_prompt)
        def _():
            os_ref[...] = y


def _moe(x, h, meta, counts, mod, w1, w3, w2, n_prompt_rows, dec_seq):
    r, d = x.shape
    ne, ff, _ = w2.shape
    assert w1.shape == (ne, ff // MOE_FF_TILE, d, MOE_FF_TILE)
    t_super, tb, chunks, tf, pa = MOE_SUPER_BLOCK, MOE_SORT_BLOCK, MOE_CHUNKS, MOE_FF_TILE, PIECE_ALIGN
    ch = max(chunks)
    nsub = t_super // tb
    nsb = r // t_super
    npb = n_prompt_rows // tb
    nf = ff // tf
    n_exp = ne * nf
    loc_rows = 2 * tb + LANES
    assert loc_rows >= 2 * tb + ne * (pa - 1)
    max_rows = 2 * t_super + nsub * ne * (pa - 1)
    sort_rows = LANES * (-(-max_rows // LANES))
    assert all(size % pa == 0 and size <= sort_rows for size in chunks)

    cnt = counts[:, 0, :ne].astype(jnp.int32)
    n16 = (cnt + (pa - 1)) // pa
    loc = pa * (jnp.cumsum(n16, axis=1) - n16)
    n16_sb = n16.reshape(nsb, nsub, ne)
    seg_len = pa * jnp.sum(n16_sb, axis=1)
    seg_off = jnp.cumsum(seg_len, axis=1) - seg_len
    dst = seg_off[:, None, :] + pa * (jnp.cumsum(n16_sb, axis=1) - n16_sb)
    cnum = (seg_len + ch - 1) // ch
    need = pa * ((seg_len // pa + jnp.maximum(cnum, 1) - 1) // jnp.maximum(cnum, 1))
    csel = sum((need > size).astype(jnp.int32) for size in chunks[:-1])
    scalars = [a.reshape(-1).astype(jnp.int32) for a in (n16, loc, dst, seg_off, seg_len, csel, cnum)]

    grp = functools.partial(_group_of_tile, tm=t_super, n_prompt_rows=n_prompt_rows, dec_seq=dec_seq)

    def exp_step(p):
        return jnp.clip(p - nsub, 0, n_exp - 1)

    def tok_blk(sb, s):
        return sb * nsub + jnp.clip(s, 0, nsub - 1)

    grid_spec = pltpu.PrefetchScalarGridSpec(
        num_scalar_prefetch=len(scalars),
        grid=(nsb, nsub + n_exp + nsub),
        in_specs=[
            pl.BlockSpec((tb, d), lambda sb, p, *_: (tok_blk(sb, p), 0)),
            pl.BlockSpec((tb, LANES), lambda sb, p, *_: (sb * nsub + jnp.where(p < nsub, p, jnp.clip(p - nsub - n_exp, 0, nsub - 1)), 0)),
            pl.BlockSpec((tb, d), lambda sb, p, *_: (tok_blk(sb, p - nsub - n_exp), 0)),
            pl.BlockSpec((None, 6, d), lambda sb, p, *_: (grp(sb), 0, 0)),
            pl.BlockSpec((None, None, d, tf), lambda sb, p, *_: (exp_step(p) // nf, exp_step(p) % nf, 0, 0)),
            pl.BlockSpec((None, None, d, tf), lambda sb, p, *_: (exp_step(p) // nf, exp_step(p) % nf, 0, 0)),
            pl.BlockSpec((None, tf, d), lambda sb, p, *_: (exp_step(p) // nf, exp_step(p) % nf, 0)),
        ],
        out_specs=(
            pl.BlockSpec((tb, d), lambda sb, p, *_: (jnp.minimum(tok_blk(sb, p - nsub - n_exp), npb - 1), 0)),
            pl.BlockSpec((tb, d), lambda sb, p, *_: (jnp.maximum(tok_blk(sb, p - nsub - n_exp) - npb, 0), 0)),
        ),
        scratch_shapes=[
            pltpu.VMEM((sort_rows, d), bf16),
            pltpu.VMEM((sort_rows, d), f32),
            pltpu.VMEM((loc_rows, d), bf16),
        ],
    )
    return pl.pallas_call(
        functools.partial(_moe_kernel, nsub=nsub, ne=ne, nf=nf, chunks=chunks,
                          nsb_prompt=n_prompt_rows // t_super),
        out_shape=(jax.ShapeDtypeStruct((n_prompt_rows, d), f32), jax.ShapeDtypeStruct((r - n_prompt_rows, d), f32)),
        grid_spec=grid_spec,
        compiler_params=_cparams("arbitrary", "arbitrary"),
        name="moe_sparse",
    )(*scalars, h, meta, x, mod, w1, w3, w2)


def kernel(x_prompt, x_sample, c, cache_dattn_k, cache_dattn_v, state_mlstm_c, state_mlstm_n, state_mlstm_m,
           c_ctx, w_ada, b_ada, w_in_ab, conv_w, conv_b, gate_b, qn_g, kn_g, lam_q1, lam_k1, lam_q2, lam_k2,
           m_norm_g, a_norm_g, w_out_ab, ff_w1, ff_w3, ff_w2, w_in_c, b_in_c, c_ln_g, c_ln_b, c_ws, c_bs,
           w_out_c, w_router, ex_w1, ex_w3, ex_w2):
    bp, seq, d = x_prompt.shape
    bs, dec_seq, _ = x_sample.shape
    depth = w_ada.shape[0]
    n_prompt_rows = bp * seq
    assert n_prompt_rows % dec_seq == 0 and seq % M_CHUNK == 0 and dec_seq % MOE_SUPER_BLOCK == 0
    nh = M_HEADS

    mods = _ada_table(jnp.concatenate([c_ctx[None], c], axis=0), w_ada, b_ada)
    x = (x_prompt.reshape(n_prompt_rows, d), x_sample.reshape(bs * dec_seq, d))
    rows = (n_prompt_rows, dec_seq)

    def joined(v):
        return jnp.concatenate(v, axis=0) if isinstance(v, tuple) else v

    def parts(v):
        return v if isinstance(v, tuple) else (v[:n_prompt_rows], v[n_prompt_rows:])

    new_k, new_v, new_c, new_n, new_m = [], [], [], [], []
    for l in range(depth):
        j = l // 2
        mod = mods[l]
        if l % 2 == 0:
            lam_init = 0.8 - 0.6 * math.exp(-0.3 * l)
            lam = (jnp.exp(jnp.sum((lam_q1[j] * lam_k1[j]).astype(f32)))
                   - jnp.exp(jnp.sum((lam_q2[j] * lam_k2[j]).astype(f32))) + lam_init).reshape(1)
            o3 = 4 * M_W
            o4 = o3 + 4 * nh
            w = w_in_ab[j]
            w_main = jnp.concatenate([w[:, :o3], w[:, o4:]], axis=1).astype(bf16)
            w_gate = jnp.zeros((d, LANES), f32).at[:, :4 * nh].set(w[:, o3:o4]).astype(bf16)
            x = parts(x)
            nxt = j if l + 1 < depth else None
            z, gates, *ex2 = _inproj(*x, mod, w_main, w_gate, *rows,
                                     job=None if nxt is None else functools.partial(_cast_job_rows, ex_w2[nxt]))
            gb = jnp.zeros((1, LANES), f32).at[0, :4 * nh].set(gate_b[j])
            mng = m_norm_g[j].reshape(1, M_W)
            cb = conv_b[j].reshape(1, 2 * M_W)
            hm_p, c_f, n_f, m_f = _mlstm(z, gates, gb, conv_w[j], cb, mng, None, batch=bp, t=seq, row_block0=0)
            state0 = (state_mlstm_c[:, j], state_mlstm_n[:, j].reshape(bs, 2, nh, 1, M_DIM),
                      state_mlstm_m[:, j].reshape(bs, 2, nh, 1, 1))
            hm_s, _, _, _ = _mlstm(z, gates, gb, conv_w[j], cb, mng, state0,
                                   batch=bs, t=dec_seq, row_block0=n_prompt_rows // dec_seq)
            new_c.append(c_f)
            new_n.append(n_f.reshape(bp, 2, nh, M_DIM))
            new_m.append(m_f.reshape(bp, 2, nh))

            qg2 = jnp.tile(qn_g[j], 2).reshape(1, LANES)
            kg2 = jnp.tile(kn_g[j], 2).reshape(1, LANES)
            ag = a_norm_g[j].reshape(1, LANES)
            ha_p, k_ctx, v_ctx = _attn(z, lam, qg2, kg2, ag, batch=bp, t=seq, row_block0=0,
                                       out_scale=1.0 - lam_init)
            kct = cache_dattn_k[:, j].transpose(0, 1, 2, 4, 3).reshape(bs, A_HEADS, LANES, -1)
            (ha_s,) = _attn(z, lam, qg2, kg2, ag, batch=bs, t=dec_seq, row_block0=n_prompt_rows // dec_seq,
                            rope_tabs=_rope_tables(dec_seq), ctx_kv=(kct, cache_dattn_v[:, j]),
                            out_scale=1.0 - lam_init)
            new_k.append(k_ctx)
            new_v.append(v_ctx)
            x = _outproj(*x, hm_p, hm_s, ha_p, ha_s, mod, w_out_ab[j].astype(bf16), *rows)
            x, *ex1 = _ffn(x, mod, ff_w1[j].astype(bf16), ff_w3[j].astype(bf16), ff_w2[j].astype(bf16), *rows,
                           job=None if nxt is None else functools.partial(_cast_job_tiled, ex_w1[nxt], tf=MOE_FF_TILE))
        else:
            e = w_out_c.shape[1]
            bs_full = jnp.repeat(c_bs[j].T, e // C_GROUPS, axis=1)
            x, ex3 = _gmlp(joined(x), mod, w_in_c[j].astype(bf16), b_in_c[j].reshape(1, -1),
                           c_ln_g[j].reshape(1, -1), c_ln_b[j].reshape(1, -1), c_ws[j].astype(bf16), bs_full,
                           w_out_c[j].astype(bf16), *rows,
                           job=functools.partial(_cast_job_tiled, ex_w3[j], tf=MOE_FF_TILE))
            wr = jnp.zeros((d, LANES), f32).at[:, :N_EXPERTS].set(w_router[j])
            wr_hi = wr.astype(bf16)
            wr_lo = (wr - wr_hi.astype(f32)).astype(bf16)
            h, meta, counts = _router(x, mod, wr_hi, wr_lo, *rows)
            x = _moe(x, h, meta, counts, mod, ex1[0], ex3, ex2[0].reshape(ex_w2[j].shape), *rows)

    y_prompt, y_sample = parts(x)
    y_prompt = y_prompt.reshape(bp, seq, d)
    y_sample = y_sample.reshape(bs, dec_seq, d)
    return (y_prompt, y_sample, jnp.stack(new_k, axis=1), jnp.stack(new_v, axis=1),
            jnp.stack(new_c, axis=1), jnp.stack(new_n, axis=1), jnp.stack(new_m, axis=1))
```

```python
import functools
import math
from typing import Callable, NamedTuple

import jax
import jax.numpy as jnp
import numpy as np
from jax import lax
from jax.experimental import pallas as pl
from jax.experimental.pallas import tpu as pltpu

f32 = jnp.float32
bf16 = jnp.bfloat16

D_MODEL = 1024
M_HEADS = 4
M_DIM = 128
M_W = M_HEADS * M_DIM
M_CHUNK = 128
A_HEADS = 4
A_VDIM = 128
A_DIM = 64
A_W = A_HEADS * A_VDIM
GRID_W = 64
ROPE_THETA = 10000.0
C_CHUNK = 128
C_GROUPS = 4
N_EXPERTS = 8
EPS = 1e-6

LANES = 128
MXU_COLS = 256
ROW_TILE = 512
PIECE_ALIGN = 16
MOE_SUPER_BLOCK = 2048
MOE_SORT_BLOCK = 512
MOE_CHUNKS = (128, 192, 256, 320, 384, 448, 480, 512, 544, 576, 608, 640)
MOE_FF_TILE = 896
VMEM_LIMIT = 60 * 1024 * 1024


def _cparams(*sem):
    return pltpu.CompilerParams(dimension_semantics=tuple(sem), vmem_limit_bytes=VMEM_LIMIT)


def _const_spec(shape):
    nd = len(shape)
    return pl.BlockSpec(shape, lambda *_: (0,) * nd, pipeline_mode=pl.Buffered(1))


def _sigmoid(x):
    return 1.0 / (1.0 + jnp.exp(-x))


def _silu(x):
    return x * _sigmoid(x)


def _log_sigmoid(x):
    return jnp.minimum(x, 0.0) - jnp.log(1.0 + jnp.exp(-jnp.abs(x)))


def _rms(x):
    return x * lax.rsqrt(jnp.mean(x * x, axis=-1, keepdims=True) + EPS)


def _modulate(x, mod_ref, first):
    shift = mod_ref[first:first + 1, :]
    scale = mod_ref[first + 1:first + 2, :]
    return _rms(x) * (1.0 + scale) + shift


def _dot(a, b):
    return jnp.dot(a, b, preferred_element_type=f32)


def _dot_nt(a, b):
    return lax.dot_general(a, b, (((1,), (1,)), ((), ())), preferred_element_type=f32)


def _dot_tn(a, b):
    return lax.dot_general(a, b, (((0,), (0,)), ((), ())), preferred_element_type=f32)


def _split3(x):
    hi = x.astype(bf16)
    r1 = x - hi.astype(f32)
    mid = r1.astype(bf16)
    lo = (r1 - mid.astype(f32)).astype(bf16)
    return hi, mid, lo


def _group_of_tile(i, tm, n_prompt_rows, dec_seq):
    pt = n_prompt_rows // tm
    return jnp.where(i < pt, 0, 1 + (i - pt) // (dec_seq // tm))


def _ada_kernel(cv_ref, w_ref, b_ref, o_ref):
    a = _silu(cv_ref[...]).astype(bf16)
    o_ref[...] = _dot(a, w_ref[...].astype(bf16)) + b_ref[...]


def _ada_table(cv, w_ada, b_ada):
    depth, d, n = w_ada.shape
    g = cv.shape[0]
    gp = 8 * ((g + 7) // 8)
    cvp = jnp.zeros((gp, d), f32).at[:g].set(cv)
    tn = 1536
    out = pl.pallas_call(
        _ada_kernel,
        out_shape=jax.ShapeDtypeStruct((depth, gp, n), f32),
        grid=(depth, n // tn),
        in_specs=[
            pl.BlockSpec((gp, d), lambda l, j: (0, 0)),
            pl.BlockSpec((None, d, tn), lambda l, j: (l, 0, j)),
            pl.BlockSpec((None, 1, tn), lambda l, j: (l, 0, j)),
        ],
        out_specs=pl.BlockSpec((None, gp, tn), lambda l, j: (l, 0, j)),
        compiler_params=_cparams("arbitrary", "arbitrary"),
        name="ada_table",
    )(cvp, w_ada, b_ada.reshape(depth, 1, n))
    return out[:, :g].reshape(depth, g, 6, d)


def _part_specs(tm, width, npt):
    first = pl.BlockSpec((tm, width), lambda i: (jnp.minimum(i, npt - 1), 0))
    second = pl.BlockSpec((tm, width), lambda i: (jnp.maximum(i - npt, 0), 0))
    return first, second


def _pick(a_ref, b_ref, npt):
    return jnp.where(pl.program_id(0) < npt, a_ref[...], b_ref[...])


class _CastJob(NamedTuple):
    src: jax.Array
    in_spec: pl.BlockSpec
    out_spec: pl.BlockSpec
    out_shape: jax.ShapeDtypeStruct
    body: Callable


def _cast_job_tiled(src, steps, tf):
    ne, d, ff = src.shape
    bands = steps // ne
    rows = d // bands
    nf = ff // tf

    def body(i_ref, o_ref):
        for f in range(nf):
            o_ref[f] = i_ref[:, f * tf:(f + 1) * tf].astype(bf16)

    return _CastJob(src, pl.BlockSpec((None, rows, ff), lambda i: (i // bands, i % bands, 0)),
                    pl.BlockSpec((None, nf, rows, tf), lambda i: (i // bands, 0, i % bands, 0)),
                    jax.ShapeDtypeStruct((ne, nf, d, tf), bf16), body)


def _cast_job_rows(src, steps):
    ne, ff, d = src.shape
    rows = ne * ff // steps

    def body(i_ref, o_ref):
        o_ref[...] = i_ref[...].astype(bf16)

    return _CastJob(src.reshape(ne * ff, d), pl.BlockSpec((rows, d), lambda i: (i, 0)),
                    pl.BlockSpec((rows, d), lambda i: (i, 0)), jax.ShapeDtypeStruct((ne * ff, d), bf16), body)


def _call_with_job(kernel_fn, job, *, out_shape, in_specs, out_specs, args, **kw):
    if job is None:
        return pl.pallas_call(kernel_fn, out_shape=out_shape, in_specs=in_specs, out_specs=out_specs, **kw)(*args)
    n_in, n_out = len(in_specs), len(out_shape)

    def with_job(*refs):
        kernel_fn(*refs[:n_in], *refs[n_in + 1:n_in + 1 + n_out], *refs[n_in + 2 + n_out:])
        job.body(refs[n_in], refs[n_in + 1 + n_out])

    return pl.pallas_call(with_job, out_shape=(*out_shape, job.out_shape), in_specs=[*in_specs, job.in_spec],
                          out_specs=(*out_specs, job.out_spec), **kw)(*args, job.src)


def _inproj_kernel(xp_ref, xs_ref, mod_ref, w_ref, wg_ref, z_ref, g_ref, *, n_main, npt):
    h = _modulate(_pick(xp_ref, xs_ref, npt), mod_ref, 0).astype(bf16)
    step = 512
    for j in range(n_main // step):
        z_ref[:, j * step:(j + 1) * step] = _dot(h, w_ref[:, j * step:(j + 1) * step]).astype(bf16)
    g_ref[...] = _dot(h, wg_ref[...])


def _inproj(xp, xs, mod, w_main, w_gate, n_prompt_rows, dec_seq, job=None):
    d = xp.shape[1]
    r = xp.shape[0] + xs.shape[0]
    tm = ROW_TILE
    npt = n_prompt_rows // tm
    n_main = w_main.shape[1]
    grp = functools.partial(_group_of_tile, tm=tm, n_prompt_rows=n_prompt_rows, dec_seq=dec_seq)
    return _call_with_job(
        functools.partial(_inproj_kernel, n_main=n_main, npt=npt), job(r // tm) if job else None,
        out_shape=(jax.ShapeDtypeStruct((r, n_main), bf16), jax.ShapeDtypeStruct((r, LANES), f32)),
        grid=(r // tm,),
        in_specs=[
            *_part_specs(tm, d, npt),
            pl.BlockSpec((None, 6, d), lambda i: (grp(i), 0, 0)),
            _const_spec(w_main.shape),
            _const_spec(w_gate.shape),
        ],
        out_specs=(pl.BlockSpec((tm, n_main), lambda i: (i, 0)), pl.BlockSpec((tm, LANES), lambda i: (i, 0))),
        args=(xp, xs, mod, w_main, w_gate),
        compiler_params=_cparams("arbitrary"),
        name="inproj_ab",
    )


def _conv_silu_chunk(x_ref, c, nc, w_ref, b_ref):
    lc, pa = M_CHUNK, PIECE_ALIGN
    t = nc * lc
    x = x_ref[pl.ds(pl.multiple_of(c * lc, lc), lc), :].astype(f32)
    before = x_ref[pl.ds(pl.multiple_of(jnp.maximum(c * lc - pa, 0), pa), pa), :].astype(f32)[pa - 1:pa, :]
    after = x_ref[pl.ds(pl.multiple_of(jnp.minimum((c + 1) * lc, t - pa), pa), pa), :].astype(f32)[0:1, :]
    before = jnp.where(c > 0, before, 0.0)
    after = jnp.where(c < nc - 1, after, 0.0)
    row = lax.broadcasted_iota(jnp.int32, (lc, 1), 0)
    prev = jnp.where(row == 0, before, pltpu.roll(x, 1, 0))
    nxt = jnp.where(row == lc - 1, after, pltpu.roll(x, lc - 1, 0))
    y = b_ref[...] + prev * w_ref[0:1, :] + x * w_ref[1:2, :] + nxt * w_ref[2:3, :]
    return _silu(y)


def _mlstm_kernel(q_ref, k_ref, v_ref, o_ref, g_ref, gb_ref, cw_ref, cb_ref, ng_ref, *rest, t, has_state):
    if has_state:
        c0_ref, n0_ref, m0_ref, *rest = rest
    hm_ref, c_out, n_out, m_out, qt_s, kh_s, kl_s, vt_s, gs_s, hft_s, hbt_s, ct_s, sel_s = rest
    nc = t // M_CHUNK
    lc = M_CHUNK
    nh = M_HEADS
    w = M_W
    sel_row = lax.broadcasted_iota(jnp.int32, (3 * LANES, LANES), 0)
    for d in range(2):
        for h in range(nh):
            ci, ln = d * nh + h, 2 * nh * d + h
            ct_s[ci] = c0_ref[d, h].T if has_state else jnp.zeros((M_DIM, M_DIM), f32)
            sel_s[ci] = jnp.where(sel_row == ln, 1.0, jnp.where(sel_row == LANES + ln, 1.0, jnp.where(
                sel_row == 2 * LANES + ln, 1.0, 0.0))).astype(bf16)
    if has_state:
        n_out[...] = n0_ref[...]
        m_out[...] = m0_ref[...]
    else:
        n_out[...] = jnp.zeros_like(n_out)
        m_out[...] = jnp.zeros_like(m_out)

    lane = lax.broadcasted_iota(jnp.int32, (1, LANES), 1)
    is_lf = ((lane >= nh) & (lane < 2 * nh)) | ((lane >= 3 * nh) & (lane < 4 * nh))
    is_bw_lf = (lane >= 3 * nh) & (lane < 4 * nh)

    rr = lax.broadcasted_iota(jnp.int32, (lc, lc), 0)
    cc = lax.broadcasted_iota(jnp.int32, (lc, lc), 1)
    tri_incl = jnp.where(rr >= cc, 1.0, 0.0).astype(bf16)

    def prep_body(c, _):
        off = pl.multiple_of(c * lc, lc)
        g = g_ref[pl.ds(off, lc), :] + gb_ref[...]
        tile = jnp.where(is_lf, _log_sigmoid(g), g)
        hi, mid, lo = _split3(jnp.where(is_lf, tile, 0.0))
        cs = _dot(tri_incl, hi) + _dot(tri_incl, mid) + _dot(tri_incl, lo)
        total = jnp.broadcast_to(cs[lc - 1:lc, :], (lc, LANES))
        b = jnp.where(is_bw_lf, total - cs + tile, cs)
        lmb = tile - pltpu.roll(b, LANES - nh, 1)
        gv = pltpu.roll(total, LANES - nh, 1) + lmb
        low = jnp.where(is_lf, b, lmb)
        high = pltpu.roll(jnp.where(is_lf, total, gv), 4 * nh, 1)
        gs_s[pl.ds(off, lc), :] = jnp.where(lane < 4 * nh, low, jnp.where(lane < 8 * nh, high, 0.0))
        wide = pl.ds(pl.multiple_of(c * w, w), w)
        qt_s[wide, :] = _conv_silu_chunk(q_ref, c, nc, cw_ref[:, :w], cb_ref[:, :w]).T
        kf = _conv_silu_chunk(k_ref, c, nc, cw_ref[:, w:], cb_ref[:, w:]) * (M_DIM ** -0.5)
        kh = kf.astype(bf16)
        kh_s[pl.ds(off, lc), :] = kh
        kl_s[pl.ds(off, lc), :] = (kf - kh.astype(f32)).astype(bf16)
        vt_s[wide, :] = v_ref[pl.ds(off, lc), :].astype(f32).T.astype(bf16)
        return 0

    lax.fori_loop(0, nc, prep_body, 0)

    sub8 = lax.broadcasted_iota(jnp.int32, (8, LANES), 0)

    def two_rows(x):
        hi = x.astype(bf16).astype(f32)
        return jnp.where(sub8 == 0, hi, jnp.where(sub8 == 1, x - hi, 0.0)).astype(bf16)

    def issue(c, h, d, tile3, tile_t):
        off = pl.multiple_of(c * lc, lc)
        col = slice(h * M_DIM, (h + 1) * M_DIM)
        ci, ln = d * nh + h, 2 * nh * d + h
        head = pl.ds(pl.multiple_of(c * w + h * M_DIM, M_DIM), M_DIM)
        b_row = tile_t[ln + nh:ln + nh + 1, :]
        g_row = tile_t[ln + 4 * nh:ln + 4 * nh + 1, :]
        total = tile_t[ln + 5 * nh:ln + 5 * nh + 1, 0:1]
        ct, nm, mm = ct_s[ci], n_out[d, h], m_out[d, h]
        qt = qt_s[head, :]
        qtb = qt.astype(bf16)
        khb = kh_s[pl.ds(off, lc), col]
        vtb = vt_s[head, :]
        lmb = _dot(tile3, sel_s[ci])
        sraw = _dot(khb, qtb)
        qn = _dot(two_rows(nm), qtb)
        lhs = jnp.concatenate([ct.astype(bf16), vtb], axis=1)
        m_new = jnp.maximum(mm + total, jnp.max(g_row, axis=1, keepdims=True))
        decay = jnp.exp(mm + total - m_new)
        ew = jnp.exp(g_row - m_new)
        ct_s[ci] = decay * ct + _dot(vtb * ew.astype(bf16), khb)
        ew2 = two_rows(ew)
        nk_h = _dot(ew2, khb)
        nk_l = _dot(ew2, kl_s[pl.ds(off, lc), col])
        n_out[d, h] = decay * nm + (nk_h[0:1, :] + nk_h[1:2, :] + nk_l[0:1, :])
        m_out[d, h] = m_new
        return dict(d=d, head=head, b_row=b_row, mm=mm, qt=qt, lmb=lmb, sraw=sraw, qn=qn, lhs=lhs)

    def weigh(st):
        keep = (cc >= rr) if st["d"] == 0 else (rr >= cc)
        dmat = jnp.where(keep, st["lmb"] + st["b_row"], -jnp.inf)
        inter = st["mm"] + st["b_row"]
        mt = jnp.maximum(inter, jnp.max(dmat, axis=0, keepdims=True))
        w_inter = jnp.exp(inter - mt)
        s = st["sraw"] * jnp.exp(dmat - mt)
        qn = st["qn"]
        den = w_inter * (qn[0:1, :] + qn[1:2, :]) + jnp.sum(s, axis=0, keepdims=True)
        inv = 1.0 / jnp.maximum(jnp.abs(den), jnp.exp(-mt))
        return jnp.concatenate([st["qt"] * (w_inter * inv), s * inv], axis=0).astype(bf16)

    def body(i, _):
        states = []
        for d in range(2):
            c = i if d == 0 else nc - 1 - i
            tile = gs_s[pl.ds(pl.multiple_of(c * lc, lc), lc), :]
            tile_t = tile.T
            hi, mid, lo = _split3(tile)
            tile3 = jnp.concatenate([hi, mid, lo], axis=1)
            states += [issue(c, h, d, tile3, tile_t) for h in range(nh)]
        rhss = [weigh(st) for st in states]
        for st, rhs in zip(states, rhss):
            hct = _dot(st["lhs"], rhs)
            if st["d"] == 0:
                hft_s[st["head"], :] = hct
            else:
                hbt_s[st["head"], :] = hct
        return 0

    lax.fori_loop(0, nc, body, 0)

    def out_body(c, _):
        rows = pl.ds(pl.multiple_of(c * lc, lc), lc)
        wide = pl.ds(pl.multiple_of(c * w, w), w)
        hsum = (hft_s[wide, :] + hbt_s[wide, :]).T
        for h in range(nh):
            col = slice(h * M_DIM, (h + 1) * M_DIM)
            mo = _sigmoid(o_ref[rows, col].astype(f32))
            hm_ref[rows, col] = (_rms(hsum[:, col]) * ng_ref[:, col] * mo).astype(hm_ref.dtype)
        return 0

    lax.fori_loop(0, nc, out_body, 0)
    for d in range(2):
        for h in range(nh):
            c_out[d, h] = ct_s[d * nh + h].T


def _mlstm(z, gates, gate_b, conv_w, conv_b, m_norm_g, state0, *, batch, t, row_block0):
    nh = M_HEADS
    w = M_W
    mode = dict(pipeline_mode=pl.Buffered(1)) if t * w * 2 > (1 << 20) else {}
    seq = lambda colblk: pl.BlockSpec((t, w), lambda b: (row_block0 + b, colblk), **mode)
    state = lambda *tail: pl.BlockSpec((None, 2, nh) + tail, lambda b: (b,) + (0,) * (2 + len(tail)))
    state_specs = [state(M_DIM, M_DIM), state(1, M_DIM), state(1, 1)]
    has_state = state0 is not None
    return pl.pallas_call(
        functools.partial(_mlstm_kernel, t=t, has_state=has_state),
        out_shape=(
            jax.ShapeDtypeStruct((batch * t, w), bf16),
            jax.ShapeDtypeStruct((batch, 2, nh, M_DIM, M_DIM), f32),
            jax.ShapeDtypeStruct((batch, 2, nh, 1, M_DIM), f32),
            jax.ShapeDtypeStruct((batch, 2, nh, 1, 1), f32),
        ),
        grid=(batch,),
        in_specs=[
            seq(0), seq(1), seq(2), seq(3),
            pl.BlockSpec((t, LANES), lambda b: (row_block0 + b, 0)),
            _const_spec(gate_b.shape),
            _const_spec(conv_w.shape), _const_spec(conv_b.shape), _const_spec(m_norm_g.shape),
            *(state_specs if has_state else []),
        ],
        out_specs=(pl.BlockSpec((t, w), lambda b: (b, 0)), *state_specs),
        scratch_shapes=[
            pltpu.VMEM((t * nh, M_DIM), f32),
            pltpu.VMEM((t, w), bf16),
            pltpu.VMEM((t, w), bf16),
            pltpu.VMEM((t * nh, M_DIM), bf16),
            pltpu.VMEM((t, LANES), f32),
            pltpu.VMEM((t * nh, M_DIM), f32),
            pltpu.VMEM((t * nh, M_DIM), f32),
            pltpu.VMEM((2 * nh, M_DIM, M_DIM), f32),
            pltpu.VMEM((2 * nh, 3 * LANES, LANES), bf16),
        ],
        compiler_params=_cparams("arbitrary"),
        name=f"mlstm_t{t}",
    )(z, z, z, z, gates, gate_b, conv_w, conv_b, m_norm_g, *(state0 if has_state else ()))


def _pair_norm(x, gain):
    lane = lax.broadcasted_iota(jnp.int32, (1, LANES), 1)
    first = lane < A_DIM
    sq = x * x
    s_all = jnp.sum(sq, axis=-1, keepdims=True)
    s0 = jnp.sum(jnp.where(first, sq, 0.0), axis=-1, keepdims=True)
    inv0 = lax.rsqrt(s0 * (1.0 / A_DIM) + EPS)
    inv1 = lax.rsqrt((s_all - s0) * (1.0 / A_DIM) + EPS)
    return x * jnp.where(first, inv0, inv1) * gain


def _rope(x, cos, sin_signed):
    lane = lax.broadcasted_iota(jnp.int32, (1, LANES), 1)
    nf = A_DIM // 4
    partner = jnp.where((lane % (2 * nf)) < nf, pltpu.roll(x, LANES - nf, 1), pltpu.roll(x, nf, 1))
    return x * cos + partner * sin_signed


def _attn_kernel(*refs, t, tq, rope, ctx, out_scale):
    it = iter(refs)
    lam_ref = next(it)
    q_ref, k_ref, v_ref = next(it), next(it), next(it)
    qg_ref, kg_ref, ag_ref = next(it), next(it), next(it)
    if rope:
        cosq_ref, sinq_ref, cosk_ref, sink_ref = next(it), next(it), next(it), next(it)
    if ctx:
        kct_ref, vc_ref = next(it), next(it)
    ha_ref = next(it)
    if not ctx:
        newk_ref, newv_ref = next(it), next(it)
    kt_s = next(it)
    nh = A_HEADS

    @pl.when(pl.program_id(1) == 0)
    def _():
        for h in range(nh):
            col = slice(h * LANES, (h + 1) * LANES)
            kn = _pair_norm(k_ref[:, col].astype(f32), kg_ref[...])
            if not ctx:
                newk_ref[h, 0] = kn[:, :A_DIM]
                newk_ref[h, 1] = kn[:, A_DIM:]
                newv_ref[h] = v_ref[:, col].astype(f32)
            if rope:
                kn = _rope(kn, cosk_ref[...], sink_ref[...])
            kt_s[col, :] = kn.T.astype(bf16)

    lane = lax.broadcasted_iota(jnp.int32, (1, LANES), 1)
    lam = lam_ref[0]
    for h in range(nh):
        col = slice(h * LANES, (h + 1) * LANES)
        q = _pair_norm(q_ref[:, col].astype(f32), qg_ref[...])
        if rope:
            q = _rope(q, cosq_ref[...], sinq_ref[...])
        q = q * (A_DIM ** -0.5 * math.log2(math.e))
        qs = [jnp.where(lane < A_DIM, q, 0.0).astype(bf16), jnp.where(lane >= A_DIM, q, 0.0).astype(bf16)]
        kt = kt_s[col, :]
        vb = v_ref[:, col]
        if ctx:
            kctb = kct_ref[h].astype(bf16)
            vcb = vc_ref[h].astype(bf16)
        outs = []
        for i in range(2):
            sn = _dot(qs[i], kt)
            mx = jnp.max(sn, axis=-1, keepdims=True)
            if ctx:
                sc = _dot(qs[i], kctb)
                mx = jnp.maximum(mx, jnp.max(sc, axis=-1, keepdims=True))
            en = jnp.exp2(sn - mx)
            den = jnp.sum(en, axis=-1, keepdims=True)
            o = _dot(en.astype(bf16), vb)
            if ctx:
                ec = jnp.exp2(sc - mx)
                den = den + jnp.sum(ec, axis=-1, keepdims=True)
                o = o + _dot(ec.astype(bf16), vcb)
            outs.append(o * (1.0 / den))
        o = outs[0] - lam * outs[1]
        ha_ref[:, col] = (_rms(o) * ag_ref[...] * out_scale).astype(ha_ref.dtype)


def _attn(z, lam, qg2, kg2, a_norm_g, *, batch, t, row_block0, rope_tabs=None, ctx_kv=None, out_scale):
    nh = A_HEADS
    w = A_W
    tq = min(t, 256)
    nq = t // tq
    rope = rope_tabs is not None
    ctx = ctx_kv is not None
    qblk = 4 * M_W // w
    vec = pl.BlockSpec((1, LANES), lambda b, i: (0, 0))
    in_specs = [
        pl.BlockSpec(memory_space=pltpu.SMEM),
        pl.BlockSpec((tq, w), lambda b, i: ((row_block0 + b) * nq + i, qblk)),
        pl.BlockSpec((t, w), lambda b, i: (row_block0 + b, qblk + 1)),
        pl.BlockSpec((t, w), lambda b, i: (row_block0 + b, qblk + 2)),
        vec, vec, vec,
    ]
    args = [lam, z, z, z, qg2, kg2, a_norm_g]
    if rope:
        cos, sin = rope_tabs
        in_specs += [pl.BlockSpec((tq, LANES), lambda b, i: (i, 0))] * 2
        in_specs += [pl.BlockSpec((t, LANES), lambda b, i: (0, 0))] * 2
        args += [cos, sin, cos, sin]
    if ctx:
        kct, vc = ctx_kv
        in_specs += [pl.BlockSpec((None,) + kct.shape[1:], lambda b, i: (b, 0, 0, 0)),
                     pl.BlockSpec((None,) + vc.shape[1:], lambda b, i: (b, 0, 0, 0))]
        args += [kct, vc]
    out_shape = [jax.ShapeDtypeStruct((batch * t, w), bf16)]
    out_specs = [pl.BlockSpec((tq, w), lambda b, i: (b * nq + i, 0))]
    if not ctx:
        out_shape += [jax.ShapeDtypeStruct((batch, nh, 2, t, A_DIM), f32),
                      jax.ShapeDtypeStruct((batch, nh, t, A_VDIM), f32)]
        out_specs += [pl.BlockSpec((None, nh, 2, t, A_DIM), lambda b, i: (b, 0, 0, 0, 0)),
                      pl.BlockSpec((None, nh, t, A_VDIM), lambda b, i: (b, 0, 0, 0))]
    return pl.pallas_call(
        functools.partial(_attn_kernel, t=t, tq=tq, rope=rope, ctx=ctx, out_scale=out_scale),
        out_shape=tuple(out_shape),
        grid=(batch, nq),
        in_specs=in_specs,
        out_specs=tuple(out_specs),
        scratch_shapes=[pltpu.VMEM((w, t), bf16)],
        compiler_params=_cparams("arbitrary", "arbitrary"),
        name=f"diff_attn_t{t}",
    )(*args)


def _rope_tables(t):
    rows = t // GRID_W
    pos_row = np.repeat(np.arange(rows, dtype=np.float64), GRID_W)
    pos_col = (np.arange(rows * GRID_W) % GRID_W).astype(np.float64)
    nf = A_DIM // 4
    inv = ROPE_THETA ** (-np.arange(nf, dtype=np.float64) / nf)
    lane = np.arange(LANES)
    j = lane % (2 * nf)
    use_col = (lane % A_DIM) >= (A_DIM // 2)
    ang = np.where(use_col[None, :], pos_col[:, None], pos_row[:, None]) * inv[j % nf][None, :]
    sign = np.where(j < nf, -1.0, 1.0)[None, :]
    return jnp.asarray(np.cos(ang), f32), jnp.asarray(np.sin(ang) * sign, f32)


def _outproj_kernel(xp_ref, xs_ref, hmp_ref, hms_ref, hap_ref, has_ref, mod_ref, w_ref, o_ref, *, npt):
    half = hmp_ref.shape[1]
    hm = _pick(hmp_ref, hms_ref, npt)
    ha = _pick(hap_ref, has_ref, npt)
    y = _dot(hm, w_ref[:half, :]) + _dot(ha, w_ref[half:, :])
    o_ref[...] = _pick(xp_ref, xs_ref, npt) + mod_ref[2:3, :] * y


def _outproj(xp, xs, hm_p, hm_s, ha_p, ha_s, mod, w, n_prompt_rows, dec_seq):
    d = xp.shape[1]
    r = xp.shape[0] + xs.shape[0]
    tm = ROW_TILE
    npt = n_prompt_rows // tm
    grp = functools.partial(_group_of_tile, tm=tm, n_prompt_rows=n_prompt_rows, dec_seq=dec_seq)
    return pl.pallas_call(
        functools.partial(_outproj_kernel, npt=npt),
        out_shape=jax.ShapeDtypeStruct((r, d), f32),
        grid=(r // tm,),
        in_specs=[
            *_part_specs(tm, d, npt),
            *_part_specs(tm, hm_p.shape[1], npt),
            *_part_specs(tm, ha_p.shape[1], npt),
            pl.BlockSpec((None, 6, d), lambda i: (grp(i), 0, 0)),
            _const_spec(w.shape),
        ],
        out_specs=pl.BlockSpec((tm, d), lambda i: (i, 0)),
        compiler_params=_cparams("arbitrary"),
        name="outproj_ab",
    )(xp, xs, hm_p, hm_s, ha_p, ha_s, mod, w)


def _ffn_kernel(x_ref, mod_ref, w1_ref, w3_ref, w2_ref, o_ref, *, chunks):
    x = x_ref[...]
    h = _modulate(x, mod_ref, 3).astype(bf16)
    acc = jnp.zeros(x.shape, f32)
    for lo, hi in chunks:
        ab = _dot(h, jnp.concatenate([w1_ref[:, lo:hi], w3_ref[:, lo:hi]], axis=1))
        a = _silu(ab[:, :hi - lo]) * ab[:, hi - lo:]
        acc = acc + _dot(a.astype(bf16), w2_ref[lo:hi, :])
    o_ref[...] = x + mod_ref[5:6, :] * acc


def _ffn(x, mod, w1, w3, w2, n_prompt_rows, dec_seq, job=None):
    r, d = x.shape
    tm = ROW_TILE
    grp = functools.partial(_group_of_tile, tm=tm, n_prompt_rows=n_prompt_rows, dec_seq=dec_seq)
    ff = w1.shape[1]
    cut = MXU_COLS * ((ff // MXU_COLS + 1) // 2) if ff % MXU_COLS == 0 else ff
    chunks = ((0, cut), (cut, ff)) if cut < ff else ((0, ff),)
    return _call_with_job(
        functools.partial(_ffn_kernel, chunks=chunks), job(r // tm) if job else None,
        out_shape=(jax.ShapeDtypeStruct((r, d), f32),),
        grid=(r // tm,),
        in_specs=[
            pl.BlockSpec((tm, d), lambda i: (i, 0)),
            pl.BlockSpec((None, 6, d), lambda i: (grp(i), 0, 0)),
            _const_spec(w1.shape), _const_spec(w3.shape), _const_spec(w2.shape),
        ],
        out_specs=(pl.BlockSpec((tm, d), lambda i: (i, 0)),),
        args=(x, mod, w1, w3, w2),
        compiler_params=_cparams("arbitrary"),
        name="ffn_dense",
    )


def _gelu_tanh(x):
    return 0.5 * x * (1.0 + jnp.tanh(math.sqrt(2.0 / math.pi) * (x + 0.044715 * (x * x * x))))


def _gmlp_kernel(x_ref, mod_ref, win_ref, bin_ref, lng_ref, lnb_ref, ws_ref, bs_ref, wout_ref, o_ref, us_s):
    x = x_ref[...]
    tm = x.shape[0]
    e = wout_ref.shape[0]
    ge = e // C_GROUPS
    h = _modulate(x, mod_ref, 0).astype(bf16)
    u = _gelu_tanh(_dot(h, win_ref[:, :e]) + bin_ref[:, :e])
    v = _gelu_tanh(_dot(h, win_ref[:, e:]) + bin_ref[:, e:])
    mu = jnp.mean(v, axis=-1, keepdims=True)
    vc = v - mu
    var = jnp.mean(vc * vc, axis=-1, keepdims=True)
    vn = (vc * lax.rsqrt(var + EPS) * lng_ref[...] + lnb_ref[...]).astype(bf16)
    for n in range(tm // C_CHUNK):
        rows = slice(n * C_CHUNK, (n + 1) * C_CHUNK)
        for g in range(C_GROUPS):
            cols = slice(g * ge, (g + 1) * ge)
            s = _dot(ws_ref[g], vn[rows, cols]) + bs_ref[:, cols]
            us_s[rows, cols] = (u[rows, cols] * s).astype(bf16)
    o_ref[...] = x + mod_ref[2:3, :] * _dot(us_s[...], wout_ref[...])


def _gmlp(x, mod, w_in, b_in, ln_g, ln_b, ws, bs_full, w_out, n_prompt_rows, dec_seq, job=None):
    r, d = x.shape
    tm = ROW_TILE
    e = w_out.shape[0]
    grp = functools.partial(_group_of_tile, tm=tm, n_prompt_rows=n_prompt_rows, dec_seq=dec_seq)
    return _call_with_job(
        _gmlp_kernel, job(r // tm) if job else None,
        out_shape=(jax.ShapeDtypeStruct((r, d), f32),),
        grid=(r // tm,),
        in_specs=[
            pl.BlockSpec((tm, d), lambda i: (i, 0)),
            pl.BlockSpec((None, 6, d), lambda i: (grp(i), 0, 0)),
            _const_spec(w_in.shape), _const_spec(b_in.shape), _const_spec(ln_g.shape), _const_spec(ln_b.shape),
            _const_spec(ws.shape), _const_spec(bs_full.shape), _const_spec(w_out.shape),
        ],
        out_specs=(pl.BlockSpec((tm, d), lambda i: (i, 0)),),
        args=(x, mod, w_in, b_in, ln_g, ln_b, ws, bs_full, w_out),
        scratch_shapes=[pltpu.VMEM((tm, e), bf16)],
        compiler_params=_cparams("arbitrary"),
        name="gmlp",
    )


def _router_kernel(x_ref, mod_ref, wr_hi_ref, wr_lo_ref, h_ref, meta_ref, counts_ref):
    hf = _modulate(x_ref[...], mod_ref, 3)
    hb = hf.astype(bf16)
    h_ref[...] = hb
    h_lo = (hf - hb.astype(f32)).astype(bf16)
    logits = _dot(hb, wr_hi_ref[...]) + (_dot(hb, wr_lo_ref[...]) + _dot(h_lo, wr_hi_ref[...]))
    lane = lax.broadcasted_iota(jnp.int32, logits.shape, 1).astype(f32)
    logits = jnp.where(lane < N_EXPERTS, logits, -jnp.inf)
    m1 = jnp.max(logits, axis=-1, keepdims=True)
    i1 = jnp.min(jnp.where(logits == m1, lane, float(LANES)), axis=-1, keepdims=True)
    rest = jnp.where(lane == i1, -jnp.inf, logits)
    m2 = jnp.max(rest, axis=-1, keepdims=True)
    i2 = jnp.min(jnp.where(rest == m2, lane, float(LANES)), axis=-1, keepdims=True)
    e2 = jnp.exp(m2 - m1)
    w1 = 1.0 / (1.0 + e2)
    w2 = e2 * w1

    tm = logits.shape[0]
    cnt = jnp.where(lane == i1, 1.0, jnp.where(lane == i2, 1.0, 0.0))
    rr = lax.broadcasted_iota(jnp.int32, (tm, tm), 0)
    cc = lax.broadcasted_iota(jnp.int32, (tm, tm), 1)
    before = jnp.where(rr > cc, 1.0, 0.0).astype(bf16)
    rank = _dot(before, cnt.astype(bf16))
    counts = jnp.sum(cnt, axis=0, keepdims=True)
    padded = jnp.floor((counts + (PIECE_ALIGN - 1)) * (1.0 / PIECE_ALIGN)) * PIECE_ALIGN
    lane1 = lane[0:1, :]
    piece_off = jnp.zeros((1, LANES), f32)
    off = jnp.zeros((1, 1), f32)
    for e in range(N_EXPERTS):
        piece_off = jnp.where(lane1 == e, off, piece_off)
        off = off + padded[:, e:e + 1]
    local = piece_off + rank
    pos1 = jnp.sum(jnp.where(lane == i1, local, 0.0), axis=-1, keepdims=True)
    pos2 = jnp.sum(jnp.where(lane == i2, local, 0.0), axis=-1, keepdims=True)
    meta_ref[...] = jnp.where(lane == 0, pos1, jnp.where(lane == 1, pos2, jnp.where(lane == 2, w1,
                              jnp.where(lane == 3, w2, 0.0))))
    counts_ref[...] = counts


def _router(x, mod, wr_hi, wr_lo, n_prompt_rows, dec_seq):
    r, d = x.shape
    tm = MOE_SORT_BLOCK
    grp = functools.partial(_group_of_tile, tm=tm, n_prompt_rows=n_prompt_rows, dec_seq=dec_seq)
    return pl.pallas_call(
        _router_kernel,
        out_shape=(jax.ShapeDtypeStruct((r, d), bf16), jax.ShapeDtypeStruct((r, LANES), f32),
                   jax.ShapeDtypeStruct((r // tm, 1, LANES), f32)),
        grid=(r // tm,),
        in_specs=[
            pl.BlockSpec((tm, d), lambda i: (i, 0)),
            pl.BlockSpec((None, 6, d), lambda i: (grp(i), 0, 0)),
            _const_spec(wr_hi.shape), _const_spec(wr_lo.shape),
        ],
        out_specs=(pl.BlockSpec((tm, d), lambda i: (i, 0)), pl.BlockSpec((tm, LANES), lambda i: (i, 0)),
                   pl.BlockSpec((None, 1, LANES), lambda i: (i, 0, 0))),
        compiler_params=_cparams("arbitrary"),
        name="router",
    )(x, mod, wr_hi, wr_lo)


def _moe_kernel(n16_ref, loc_ref, dst_ref, seg_off_ref, seg_len_ref, csel_ref, cnum_ref,
                h_ref, meta_ref, x_ref, mod_ref, w1_ref, w3_ref, w2_ref, op_ref, os_ref,
                hs_s, ys_s, loc_s, *, nsub, ne, nf, chunks, nsb_prompt):
    sb = pl.program_id(0)
    p = pl.program_id(1)
    n_exp = ne * nf
    loc_rows = loc_s.shape[0]
    pa = PIECE_ALIGN

    def one_hot_cols(meta, v1, v2):
        lane = lax.broadcasted_iota(jnp.int32, (1, loc_rows), 1).astype(f32)
        return jnp.where(lane == meta[:, 0:1], v1, jnp.where(lane == meta[:, 1:2], v2, 0.0)).astype(bf16)

    def copy_pieces(blk, to_sorted):
        for e in range(ne):
            n = n16_ref[blk * ne + e]
            src = loc_ref[blk * ne + e]
            dst = dst_ref[blk * ne + e]

            def cp(i, _, src=src, dst=dst):
                a = pl.ds(pl.multiple_of(src + pa * i, pa), pa)
                b = pl.ds(pl.multiple_of(dst + pa * i, pa), pa)
                if to_sorted:
                    hs_s[b, :] = loc_s[a, :]
                else:
                    loc_s[a, :] = ys_s[b, :].astype(bf16)
                return 0

            lax.fori_loop(0, n, cp, 0)

    @pl.when(p < nsub)
    def _():
        @pl.when(p == 0)
        def _():
            hs_s[...] = jnp.zeros_like(hs_s)
            ys_s[...] = jnp.zeros_like(ys_s)

        pt = one_hot_cols(meta_ref[...], 1.0, 1.0)
        loc_s[...] = _dot_tn(pt, h_ref[...]).astype(bf16)
        copy_pieces(sb * nsub + p, True)

    @pl.when((p >= nsub) & (p < nsub + n_exp))
    def _():
        e = lax.div(p - nsub, jnp.int32(nf))
        start = seg_off_ref[sb * ne + e]
        ln = seg_len_ref[sb * ne + e]
        which = csel_ref[sb * ne + e]
        count = cnum_ref[sb * ne + e]

        def chunk(lo, size):
            r0 = jnp.minimum(lo, hs_s.shape[0] - size)
            rows = pl.ds(pl.multiple_of(r0, pa), size)
            xc = hs_s[rows, :]
            ab = _dot(xc, jnp.concatenate([w1_ref[...], w3_ref[...]], axis=1))
            tf = w1_ref.shape[1]
            a = _silu(ab[:, :tf]) * ab[:, tf:]
            y = _dot(a.astype(bf16), w2_ref[...])
            ri = r0 + lax.broadcasted_iota(jnp.int32, (size, 1), 0)
            ys_s[rows, :] += jnp.where(ri >= lo, jnp.where(ri < start + ln, y, 0.0), 0.0)

        for k, size in enumerate(chunks):
            @pl.when(which == k)
            def _(size=size):
                def body(j, _):
                    chunk(start + j * size, size)
                    return 0

                lax.fori_loop(0, count, body, 0)

    @pl.when(p >= nsub + n_exp)
    def _():
        copy_pieces(sb * nsub + (p - nsub - n_exp), False)
        meta = meta_ref[...]
        a = one_hot_cols(meta, meta[:, 2:3], meta[:, 3:4])
        y = x_ref[...] + mod_ref[5:6, :] * _dot(a, loc_s[...])

        @pl.when(sb < nsb_prompt)
        def _():
            op_ref[...] = y

        @pl.when(sb >= nsb_prompt)
        def _():
            os_ref[...] = y


def _moe(x, h, meta, counts, mod, w1, w3, w2, n_prompt_rows, dec_seq):
    r, d = x.shape
    ne, ff, _ = w2.shape
    assert w1.shape == (ne, ff // MOE_FF_TILE, d, MOE_FF_TILE)
    t_super, tb, chunks, tf, pa = MOE_SUPER_BLOCK, MOE_SORT_BLOCK, MOE_CHUNKS, MOE_FF_TILE, PIECE_ALIGN
    ch = max(chunks)
    nsub = t_super // tb
    nsb = r // t_super
    npb = n_prompt_rows // tb
    nf = ff // tf
    n_exp = ne * nf
    loc_rows = 2 * tb + LANES
    assert loc_rows >= 2 * tb + ne * (pa - 1)
    max_rows = 2 * t_super + nsub * ne * (pa - 1)
    sort_rows = LANES * (-(-max_rows // LANES))
    assert all(size % pa == 0 and size <= sort_rows for size in chunks)

    cnt = counts[:, 0, :ne].astype(jnp.int32)
    n16 = (cnt + (pa - 1)) // pa
    loc = pa * (jnp.cumsum(n16, axis=1) - n16)
    n16_sb = n16.reshape(nsb, nsub, ne)
    seg_len = pa * jnp.sum(n16_sb, axis=1)
    seg_off = jnp.cumsum(seg_len, axis=1) - seg_len
    dst = seg_off[:, None, :] + pa * (jnp.cumsum(n16_sb, axis=1) - n16_sb)
    cnum = (seg_len + ch - 1) // ch
    need = pa * ((seg_len // pa + jnp.maximum(cnum, 1) - 1) // jnp.maximum(cnum, 1))
    csel = sum((need > size).astype(jnp.int32) for size in chunks[:-1])
    scalars = [a.reshape(-1).astype(jnp.int32) for a in (n16, loc, dst, seg_off, seg_len, csel, cnum)]

    grp = functools.partial(_group_of_tile, tm=t_super, n_prompt_rows=n_prompt_rows, dec_seq=dec_seq)

    def exp_step(p):
        return jnp.clip(p - nsub, 0, n_exp - 1)

    def tok_blk(sb, s):
        return sb * nsub + jnp.clip(s, 0, nsub - 1)

    grid_spec = pltpu.PrefetchScalarGridSpec(
        num_scalar_prefetch=len(scalars),
        grid=(nsb, nsub + n_exp + nsub),
        in_specs=[
            pl.BlockSpec((tb, d), lambda sb, p, *_: (tok_blk(sb, p), 0)),
            pl.BlockSpec((tb, LANES), lambda sb, p, *_: (sb * nsub + jnp.where(p < nsub, p, jnp.clip(p - nsub - n_exp, 0, nsub - 1)), 0)),
            pl.BlockSpec((tb, d), lambda sb, p, *_: (tok_blk(sb, p - nsub - n_exp), 0)),
            pl.BlockSpec((None, 6, d), lambda sb, p, *_: (grp(sb), 0, 0)),
            pl.BlockSpec((None, None, d, tf), lambda sb, p, *_: (exp_step(p) // nf, exp_step(p) % nf, 0, 0)),
            pl.BlockSpec((None, None, d, tf), lambda sb, p, *_: (exp_step(p) // nf, exp_step(p) % nf, 0, 0)),
            pl.BlockSpec((None, tf, d), lambda sb, p, *_: (exp_step(p) // nf, exp_step(p) % nf, 0)),
        ],
        out_specs=(
            pl.BlockSpec((tb, d), lambda sb, p, *_: (jnp.minimum(tok_blk(sb, p - nsub - n_exp), npb - 1), 0)),
            pl.BlockSpec((tb, d), lambda sb, p, *_: (jnp.maximum(tok_blk(sb, p - nsub - n_exp) - npb, 0), 0)),
        ),
        scratch_shapes=[
            pltpu.VMEM((sort_rows, d), bf16),
            pltpu.VMEM((sort_rows, d), f32),
            pltpu.VMEM((loc_rows, d), bf16),
        ],
    )
    return pl.pallas_call(
        functools.partial(_moe_kernel, nsub=nsub, ne=ne, nf=nf, chunks=chunks,
                          nsb_prompt=n_prompt_rows // t_super),
        out_shape=(jax.ShapeDtypeStruct((n_prompt_rows, d), f32), jax.ShapeDtypeStruct((r - n_prompt_rows, d), f32)),
        grid_spec=grid_spec,
        compiler_params=_cparams("arbitrary", "arbitrary"),
        name="moe_sparse",
    )(*scalars, h, meta, x, mod, w1, w3, w2)


def kernel(x_prompt, x_sample, c, cache_dattn_k, cache_dattn_v, state_mlstm_c, state_mlstm_n, state_mlstm_m,
           c_ctx, w_ada, b_ada, w_in_ab, conv_w, conv_b, gate_b, qn_g, kn_g, lam_q1, lam_k1, lam_q2, lam_k2,
           m_norm_g, a_norm_g, w_out_ab, ff_w1, ff_w3, ff_w2, w_in_c, b_in_c, c_ln_g, c_ln_b, c_ws, c_bs,
           w_out_c, w_router, ex_w1, ex_w3, ex_w2):
    bp, seq, d = x_prompt.shape
    bs, dec_seq, _ = x_sample.shape
    depth = w_ada.shape[0]
    n_prompt_rows = bp * seq
    assert n_prompt_rows % dec_seq == 0 and seq % M_CHUNK == 0 and dec_seq % MOE_SUPER_BLOCK == 0
    nh = M_HEADS

    mods = _ada_table(jnp.concatenate([c_ctx[None], c], axis=0), w_ada, b_ada)
    x = (x_prompt.reshape(n_prompt_rows, d), x_sample.reshape(bs * dec_seq, d))
    rows = (n_prompt_rows, dec_seq)

    def joined(v):
        return jnp.concatenate(v, axis=0) if isinstance(v, tuple) else v

    def parts(v):
        return v if isinstance(v, tuple) else (v[:n_prompt_rows], v[n_prompt_rows:])

    new_k, new_v, new_c, new_n, new_m = [], [], [], [], []
    for l in range(depth):
        j = l // 2
        mod = mods[l]
        if l % 2 == 0:
            lam_init = 0.8 - 0.6 * math.exp(-0.3 * l)
            lam = (jnp.exp(jnp.sum((lam_q1[j] * lam_k1[j]).astype(f32)))
                   - jnp.exp(jnp.sum((lam_q2[j] * lam_k2[j]).astype(f32))) + lam_init).reshape(1)
            o3 = 4 * M_W
            o4 = o3 + 4 * nh
            w = w_in_ab[j]
            w_main = jnp.concatenate([w[:, :o3], w[:, o4:]], axis=1).astype(bf16)
            w_gate = jnp.zeros((d, LANES), f32).at[:, :4 * nh].set(w[:, o3:o4]).astype(bf16)
            x = parts(x)
            nxt = j if l + 1 < depth else None
            z, gates, *ex2 = _inproj(*x, mod, w_main, w_gate, *rows,
                                     job=None if nxt is None else functools.partial(_cast_job_rows, ex_w2[nxt]))
            gb = jnp.zeros((1, LANES), f32).at[0, :4 * nh].set(gate_b[j])
            mng = m_norm_g[j].reshape(1, M_W)
            cb = conv_b[j].reshape(1, 2 * M_W)
            hm_p, c_f, n_f, m_f = _mlstm(z, gates, gb, conv_w[j], cb, mng, None, batch=bp, t=seq, row_block0=0)
            state0 = (state_mlstm_c[:, j], state_mlstm_n[:, j].reshape(bs, 2, nh, 1, M_DIM),
                      state_mlstm_m[:, j].reshape(bs, 2, nh, 1, 1))
            hm_s, _, _, _ = _mlstm(z, gates, gb, conv_w[j], cb, mng, state0,
                                   batch=bs, t=dec_seq, row_block0=n_prompt_rows // dec_seq)
            new_c.append(c_f)
            new_n.append(n_f.reshape(bp, 2, nh, M_DIM))
            new_m.append(m_f.reshape(bp, 2, nh))

            qg2 = jnp.tile(qn_g[j], 2).reshape(1, LANES)
            kg2 = jnp.tile(kn_g[j], 2).reshape(1, LANES)
            ag = a_norm_g[j].reshape(1, LANES)
            ha_p, k_ctx, v_ctx = _attn(z, lam, qg2, kg2, ag, batch=bp, t=seq, row_block0=0,
                                       out_scale=1.0 - lam_init)
            kct = cache_dattn_k[:, j].transpose(0, 1, 2, 4, 3).reshape(bs, A_HEADS, LANES, -1)
            (ha_s,) = _attn(z, lam, qg2, kg2, ag, batch=bs, t=dec_seq, row_block0=n_prompt_rows // dec_seq,
                            rope_tabs=_rope_tables(dec_seq), ctx_kv=(kct, cache_dattn_v[:, j]),
                            out_scale=1.0 - lam_init)
            new_k.append(k_ctx)
            new_v.append(v_ctx)
            x = _outproj(*x, hm_p, hm_s, ha_p, ha_s, mod, w_out_ab[j].astype(bf16), *rows)
            x, *ex1 = _ffn(x, mod, ff_w1[j].astype(bf16), ff_w3[j].astype(bf16), ff_w2[j].astype(bf16), *rows,
                           job=None if nxt is None else functools.partial(_cast_job_tiled, ex_w1[nxt], tf=MOE_FF_TILE))
        else:
            e = w_out_c.shape[1]
            bs_full = jnp.repeat(c_bs[j].T, e // C_GROUPS, axis=1)
            x, ex3 = _gmlp(joined(x), mod, w_in_c[j].astype(bf16), b_in_c[j].reshape(1, -1),
                           c_ln_g[j].reshape(1, -1), c_ln_b[j].reshape(1, -1), c_ws[j].astype(bf16), bs_full,
                           w_out_c[j].astype(bf16), *rows,
                           job=functools.partial(_cast_job_tiled, ex_w3[j], tf=MOE_FF_TILE))
            wr = jnp.zeros((d, LANES), f32).at[:, :N_EXPERTS].set(w_router[j])
            wr_hi = wr.astype(bf16)
            wr_lo = (wr - wr_hi.astype(f32)).astype(bf16)
            h, meta, counts = _router(x, mod, wr_hi, wr_lo, *rows)
            x = _moe(x, h, meta, counts, mod, ex1[0], ex3, ex2[0].reshape(ex_w2[j].shape), *rows)

    y_prompt, y_sample = parts(x)
    y_prompt = y_prompt.reshape(bp, seq, d)
    y_sample = y_sample.reshape(bs, dec_seq, d)
    return (y_prompt, y_sample, jnp.stack(new_k, axis=1), jnp.stack(new_v, axis=1),
            jnp.stack(new_c, axis=1), jnp.stack(new_n, axis=1), jnp.stack(new_m, axis=1))
```

```python
import functools
import math
from typing import Callable, NamedTuple

import jax
import jax.numpy as jnp
import numpy as np
from jax import lax
from jax.experimental import pallas as pl
from jax.experimental.pallas import tpu as pltpu

f32 = jnp.float32
bf16 = jnp.bfloat16

D_MODEL = 1024
M_HEADS = 4
M_DIM = 128
M_W = M_HEADS * M_DIM
M_CHUNK = 128
A_HEADS = 4
A_VDIM = 128
A_DIM = 64
A_W = A_HEADS * A_VDIM
GRID_W = 64
ROPE_THETA = 10000.0
C_CHUNK = 128
C_GROUPS = 4
N_EXPERTS = 8
EPS = 1e-6

LANES = 128
MXU_COLS = 256
ROW_TILE = 512
PIECE_ALIGN = 16
MOE_SUPER_BLOCK = 2048
MOE_SORT_BLOCK = 512
MOE_CHUNKS = (128, 192, 256, 320, 384, 448, 512, 576, 640)
MOE_FF_TILE = 896
VMEM_LIMIT = 60 * 1024 * 1024


def _cparams(*sem):
    return pltpu.CompilerParams(dimension_semantics=tuple(sem), vmem_limit_bytes=VMEM_LIMIT)


def _const_spec(shape):
    nd = len(shape)
    return pl.BlockSpec(shape, lambda *_: (0,) * nd, pipeline_mode=pl.Buffered(1))


def _sigmoid(x):
    return 1.0 / (1.0 + jnp.exp(-x))


def _silu(x):
    return x * _sigmoid(x)


def _log_sigmoid(x):
    return jnp.minimum(x, 0.0) - jnp.log(1.0 + jnp.exp(-jnp.abs(x)))


def _rms(x):
    return x * lax.rsqrt(jnp.mean(x * x, axis=-1, keepdims=True) + EPS)


def _modulate(x, mod_ref, first):
    shift = mod_ref[first:first + 1, :]
    scale = mod_ref[first + 1:first + 2, :]
    return _rms(x) * (1.0 + scale) + shift


def _dot(a, b):
    return jnp.dot(a, b, preferred_element_type=f32)


def _dot_nt(a, b):
    return lax.dot_general(a, b, (((1,), (1,)), ((), ())), preferred_element_type=f32)


def _dot_tn(a, b):
    return lax.dot_general(a, b, (((0,), (0,)), ((), ())), preferred_element_type=f32)


def _split3(x):
    hi = x.astype(bf16)
    r1 = x - hi.astype(f32)
    mid = r1.astype(bf16)
    lo = (r1 - mid.astype(f32)).astype(bf16)
    return hi, mid, lo


def _group_of_tile(i, tm, n_prompt_rows, dec_seq):
    pt = n_prompt_rows // tm
    return jnp.where(i < pt, 0, 1 + (i - pt) // (dec_seq // tm))


def _ada_kernel(cv_ref, w_ref, b_ref, o_ref):
    a = _silu(cv_ref[...]).astype(bf16)
    o_ref[...] = _dot(a, w_ref[...].astype(bf16)) + b_ref[...]


def _ada_table(cv, w_ada, b_ada):
    depth, d, n = w_ada.shape
    g = cv.shape[0]
    gp = 8 * ((g + 7) // 8)
    cvp = jnp.zeros((gp, d), f32).at[:g].set(cv)
    tn = 1536
    out = pl.pallas_call(
        _ada_kernel,
        out_shape=jax.ShapeDtypeStruct((depth, gp, n), f32),
        grid=(depth, n // tn),
        in_specs=[
            pl.BlockSpec((gp, d), lambda l, j: (0, 0)),
            pl.BlockSpec((None, d, tn), lambda l, j: (l, 0, j)),
            pl.BlockSpec((None, 1, tn), lambda l, j: (l, 0, j)),
        ],
        out_specs=pl.BlockSpec((None, gp, tn), lambda l, j: (l, 0, j)),
        compiler_params=_cparams("arbitrary", "arbitrary"),
        name="ada_table",
    )(cvp, w_ada, b_ada.reshape(depth, 1, n))
    return out[:, :g].reshape(depth, g, 6, d)


def _part_specs(tm, width, npt):
    first = pl.BlockSpec((tm, width), lambda i: (jnp.minimum(i, npt - 1), 0))
    second = pl.BlockSpec((tm, width), lambda i: (jnp.maximum(i - npt, 0), 0))
    return first, second


def _pick(a_ref, b_ref, npt):
    return jnp.where(pl.program_id(0) < npt, a_ref[...], b_ref[...])


class _CastJob(NamedTuple):
    src: jax.Array
    in_spec: pl.BlockSpec
    out_spec: pl.BlockSpec
    out_shape: jax.ShapeDtypeStruct
    body: Callable


def _cast_job_tiled(src, steps, tf):
    ne, d, ff = src.shape
    bands = steps // ne
    rows = d // bands
    nf = ff // tf

    def body(i_ref, o_ref):
        for f in range(nf):
            o_ref[f] = i_ref[:, f * tf:(f + 1) * tf].astype(bf16)

    return _CastJob(src, pl.BlockSpec((None, rows, ff), lambda i: (i // bands, i % bands, 0)),
                    pl.BlockSpec((None, nf, rows, tf), lambda i: (i // bands, 0, i % bands, 0)),
                    jax.ShapeDtypeStruct((ne, nf, d, tf), bf16), body)


def _cast_job_rows(src, steps):
    ne, ff, d = src.shape
    rows = ne * ff // steps

    def body(i_ref, o_ref):
        o_ref[...] = i_ref[...].astype(bf16)

    return _CastJob(src.reshape(ne * ff, d), pl.BlockSpec((rows, d), lambda i: (i, 0)),
                    pl.BlockSpec((rows, d), lambda i: (i, 0)), jax.ShapeDtypeStruct((ne * ff, d), bf16), body)


def _call_with_job(kernel_fn, job, *, out_shape, in_specs, out_specs, args, **kw):
    if job is None:
        return pl.pallas_call(kernel_fn, out_shape=out_shape, in_specs=in_specs, out_specs=out_specs, **kw)(*args)
    n_in, n_out = len(in_specs), len(out_shape)

    def with_job(*refs):
        kernel_fn(*refs[:n_in], *refs[n_in + 1:n_in + 1 + n_out], *refs[n_in + 2 + n_out:])
        job.body(refs[n_in], refs[n_in + 1 + n_out])

    return pl.pallas_call(with_job, out_shape=(*out_shape, job.out_shape), in_specs=[*in_specs, job.in_spec],
                          out_specs=(*out_specs, job.out_spec), **kw)(*args, job.src)


def _inproj_kernel(xp_ref, xs_ref, mod_ref, w_ref, wg_ref, z_ref, g_ref, *, n_main, npt):
    h = _modulate(_pick(xp_ref, xs_ref, npt), mod_ref, 0).astype(bf16)
    step = 512
    for j in range(n_main // step):
        z_ref[:, j * step:(j + 1) * step] = _dot(h, w_ref[:, j * step:(j + 1) * step]).astype(bf16)
    g_ref[...] = _dot(h, wg_ref[...])


def _inproj(xp, xs, mod, w_main, w_gate, n_prompt_rows, dec_seq, job=None):
    d = xp.shape[1]
    r = xp.shape[0] + xs.shape[0]
    tm = ROW_TILE
    npt = n_prompt_rows // tm
    n_main = w_main.shape[1]
    grp = functools.partial(_group_of_tile, tm=tm, n_prompt_rows=n_prompt_rows, dec_seq=dec_seq)
    return _call_with_job(
        functools.partial(_inproj_kernel, n_main=n_main, npt=npt), job(r // tm) if job else None,
        out_shape=(jax.ShapeDtypeStruct((r, n_main), bf16), jax.ShapeDtypeStruct((r, LANES), f32)),
        grid=(r // tm,),
        in_specs=[
            *_part_specs(tm, d, npt),
            pl.BlockSpec((None, 6, d), lambda i: (grp(i), 0, 0)),
            _const_spec(w_main.shape),
            _const_spec(w_gate.shape),
        ],
        out_specs=(pl.BlockSpec((tm, n_main), lambda i: (i, 0)), pl.BlockSpec((tm, LANES), lambda i: (i, 0))),
        args=(xp, xs, mod, w_main, w_gate),
        compiler_params=_cparams("arbitrary"),
        name="inproj_ab",
    )


def _conv_silu_chunk(x_ref, c, nc, w_ref, b_ref):
    lc, pa = M_CHUNK, PIECE_ALIGN
    t = nc * lc
    x = x_ref[pl.ds(pl.multiple_of(c * lc, lc), lc), :].astype(f32)
    before = x_ref[pl.ds(pl.multiple_of(jnp.maximum(c * lc - pa, 0), pa), pa), :].astype(f32)[pa - 1:pa, :]
    after = x_ref[pl.ds(pl.multiple_of(jnp.minimum((c + 1) * lc, t - pa), pa), pa), :].astype(f32)[0:1, :]
    before = jnp.where(c > 0, before, 0.0)
    after = jnp.where(c < nc - 1, after, 0.0)
    row = lax.broadcasted_iota(jnp.int32, (lc, 1), 0)
    prev = jnp.where(row == 0, before, pltpu.roll(x, 1, 0))
    nxt = jnp.where(row == lc - 1, after, pltpu.roll(x, lc - 1, 0))
    y = b_ref[...] + prev * w_ref[0:1, :] + x * w_ref[1:2, :] + nxt * w_ref[2:3, :]
    return _silu(y)


def _mlstm_kernel(q_ref, k_ref, v_ref, o_ref, g_ref, gb_ref, cw_ref, cb_ref, ng_ref, *rest, t, has_state):
    if has_state:
        c0_ref, n0_ref, m0_ref, *rest = rest
    hm_ref, c_out, n_out, m_out, qt_s, kh_s, kl_s, vt_s, gs_s, hft_s, hbt_s, ct_s, sel_s = rest
    nc = t // M_CHUNK
    lc = M_CHUNK
    nh = M_HEADS
    w = M_W
    sel_row = lax.broadcasted_iota(jnp.int32, (3 * LANES, LANES), 0)
    for d in range(2):
        for h in range(nh):
            ci, ln = d * nh + h, 2 * nh * d + h
            ct_s[ci] = c0_ref[d, h].T if has_state else jnp.zeros((M_DIM, M_DIM), f32)
            sel_s[ci] = jnp.where(sel_row == ln, 1.0, jnp.where(sel_row == LANES + ln, 1.0, jnp.where(
                sel_row == 2 * LANES + ln, 1.0, 0.0))).astype(bf16)
    if has_state:
        n_out[...] = n0_ref[...]
        m_out[...] = m0_ref[...]
    else:
        n_out[...] = jnp.zeros_like(n_out)
        m_out[...] = jnp.zeros_like(m_out)

    lane = lax.broadcasted_iota(jnp.int32, (1, LANES), 1)
    is_lf = ((lane >= nh) & (lane < 2 * nh)) | ((lane >= 3 * nh) & (lane < 4 * nh))
    is_bw_lf = (lane >= 3 * nh) & (lane < 4 * nh)

    rr = lax.broadcasted_iota(jnp.int32, (lc, lc), 0)
    cc = lax.broadcasted_iota(jnp.int32, (lc, lc), 1)
    tri_incl = jnp.where(rr >= cc, 1.0, 0.0).astype(bf16)

    def prep_body(c, _):
        off = pl.multiple_of(c * lc, lc)
        g = g_ref[pl.ds(off, lc), :] + gb_ref[...]
        tile = jnp.where(is_lf, _log_sigmoid(g), g)
        hi, mid, lo = _split3(jnp.where(is_lf, tile, 0.0))
        cs = _dot(tri_incl, hi) + _dot(tri_incl, mid) + _dot(tri_incl, lo)
        total = jnp.broadcast_to(cs[lc - 1:lc, :], (lc, LANES))
        b = jnp.where(is_bw_lf, total - cs + tile, cs)
        lmb = tile - pltpu.roll(b, LANES - nh, 1)
        gv = pltpu.roll(total, LANES - nh, 1) + lmb
        low = jnp.where(is_lf, b, lmb)
        high = pltpu.roll(jnp.where(is_lf, total, gv), 4 * nh, 1)
        gs_s[pl.ds(off, lc), :] = jnp.where(lane < 4 * nh, low, jnp.where(lane < 8 * nh, high, 0.0))
        wide = pl.ds(pl.multiple_of(c * w, w), w)
        qt_s[wide, :] = _conv_silu_chunk(q_ref, c, nc, cw_ref[:, :w], cb_ref[:, :w]).T
        kf = _conv_silu_chunk(k_ref, c, nc, cw_ref[:, w:], cb_ref[:, w:]) * (M_DIM ** -0.5)
        kh = kf.astype(bf16)
        kh_s[pl.ds(off, lc), :] = kh
        kl_s[pl.ds(off, lc), :] = (kf - kh.astype(f32)).astype(bf16)
        vt_s[wide, :] = v_ref[pl.ds(off, lc), :].astype(f32).T.astype(bf16)
        return 0

    lax.fori_loop(0, nc, prep_body, 0)

    sub8 = lax.broadcasted_iota(jnp.int32, (8, LANES), 0)

    def two_rows(x):
        hi = x.astype(bf16).astype(f32)
        return jnp.where(sub8 == 0, hi, jnp.where(sub8 == 1, x - hi, 0.0)).astype(bf16)

    def issue(c, h, d, tile3, tile_t):
        off = pl.multiple_of(c * lc, lc)
        col = slice(h * M_DIM, (h + 1) * M_DIM)
        ci, ln = d * nh + h, 2 * nh * d + h
        head = pl.ds(pl.multiple_of(c * w + h * M_DIM, M_DIM), M_DIM)
        b_row = tile_t[ln + nh:ln + nh + 1, :]
        g_row = tile_t[ln + 4 * nh:ln + 4 * nh + 1, :]
        total = tile_t[ln + 5 * nh:ln + 5 * nh + 1, 0:1]
        ct, nm, mm = ct_s[ci], n_out[d, h], m_out[d, h]
        qt = qt_s[head, :]
        qtb = qt.astype(bf16)
        khb = kh_s[pl.ds(off, lc), col]
        vtb = vt_s[head, :]
        lmb = _dot(tile3, sel_s[ci])
        sraw = _dot(khb, qtb)
        qn = _dot(two_rows(nm), qtb)
        lhs = jnp.concatenate([ct.astype(bf16), vtb], axis=1)
        m_new = jnp.maximum(mm + total, jnp.max(g_row, axis=1, keepdims=True))
        decay = jnp.exp(mm + total - m_new)
        ew = jnp.exp(g_row - m_new)
        ct_s[ci] = decay * ct + _dot(vtb * ew.astype(bf16), khb)
        ew2 = two_rows(ew)
        nk_h = _dot(ew2, khb)
        nk_l = _dot(ew2, kl_s[pl.ds(off, lc), col])
        n_out[d, h] = decay * nm + (nk_h[0:1, :] + nk_h[1:2, :] + nk_l[0:1, :])
        m_out[d, h] = m_new
        return dict(d=d, head=head, b_row=b_row, mm=mm, qt=qt, lmb=lmb, sraw=sraw, qn=qn, lhs=lhs)

    def weigh(st):
        keep = (cc >= rr) if st["d"] == 0 else (rr >= cc)
        dmat = jnp.where(keep, st["lmb"] + st["b_row"], -jnp.inf)
        inter = st["mm"] + st["b_row"]
        mt = jnp.maximum(inter, jnp.max(dmat, axis=0, keepdims=True))
        w_inter = jnp.exp(inter - mt)
        s = st["sraw"] * jnp.exp(dmat - mt)
        qn = st["qn"]
        den = w_inter * (qn[0:1, :] + qn[1:2, :]) + jnp.sum(s, axis=0, keepdims=True)
        inv = 1.0 / jnp.maximum(jnp.abs(den), jnp.exp(-mt))
        return jnp.concatenate([st["qt"] * (w_inter * inv), s * inv], axis=0).astype(bf16)

    def body(i, _):
        states = []
        for d in range(2):
            c = i if d == 0 else nc - 1 - i
            tile = gs_s[pl.ds(pl.multiple_of(c * lc, lc), lc), :]
            tile_t = tile.T
            hi, mid, lo = _split3(tile)
            tile3 = jnp.concatenate([hi, mid, lo], axis=1)
            states += [issue(c, h, d, tile3, tile_t) for h in range(nh)]
        rhss = [weigh(st) for st in states]
        for st, rhs in zip(states, rhss):
            hct = _dot(st["lhs"], rhs)
            if st["d"] == 0:
                hft_s[st["head"], :] = hct
            else:
                hbt_s[st["head"], :] = hct
        return 0

    lax.fori_loop(0, nc, body, 0)

    def out_body(c, _):
        rows = pl.ds(pl.multiple_of(c * lc, lc), lc)
        wide = pl.ds(pl.multiple_of(c * w, w), w)
        hsum = (hft_s[wide, :] + hbt_s[wide, :]).T
        for h in range(nh):
            col = slice(h * M_DIM, (h + 1) * M_DIM)
            mo = _sigmoid(o_ref[rows, col].astype(f32))
            hm_ref[rows, col] = (_rms(hsum[:, col]) * ng_ref[:, col] * mo).astype(hm_ref.dtype)
        return 0

    lax.fori_loop(0, nc, out_body, 0)
    for d in range(2):
        for h in range(nh):
            c_out[d, h] = ct_s[d * nh + h].T


def _mlstm(z, gates, gate_b, conv_w, conv_b, m_norm_g, state0, *, batch, t, row_block0):
    nh = M_HEADS
    w = M_W
    mode = dict(pipeline_mode=pl.Buffered(1)) if t * w * 2 > (1 << 20) else {}
    seq = lambda colblk: pl.BlockSpec((t, w), lambda b: (row_block0 + b, colblk), **mode)
    state = lambda *tail: pl.BlockSpec((None, 2, nh) + tail, lambda b: (b,) + (0,) * (2 + len(tail)))
    state_specs = [state(M_DIM, M_DIM), state(1, M_DIM), state(1, 1)]
    has_state = state0 is not None
    return pl.pallas_call(
        functools.partial(_mlstm_kernel, t=t, has_state=has_state),
        out_shape=(
            jax.ShapeDtypeStruct((batch * t, w), bf16),
            jax.ShapeDtypeStruct((batch, 2, nh, M_DIM, M_DIM), f32),
            jax.ShapeDtypeStruct((batch, 2, nh, 1, M_DIM), f32),
            jax.ShapeDtypeStruct((batch, 2, nh, 1, 1), f32),
        ),
        grid=(batch,),
        in_specs=[
            seq(0), seq(1), seq(2), seq(3),
            pl.BlockSpec((t, LANES), lambda b: (row_block0 + b, 0)),
            _const_spec(gate_b.shape),
            _const_spec(conv_w.shape), _const_spec(conv_b.shape), _const_spec(m_norm_g.shape),
            *(state_specs if has_state else []),
        ],
        out_specs=(pl.BlockSpec((t, w), lambda b: (b, 0)), *state_specs),
        scratch_shapes=[
            pltpu.VMEM((t * nh, M_DIM), f32),
            pltpu.VMEM((t, w), bf16),
            pltpu.VMEM((t, w), bf16),
            pltpu.VMEM((t * nh, M_DIM), bf16),
            pltpu.VMEM((t, LANES), f32),
            pltpu.VMEM((t * nh, M_DIM), f32),
            pltpu.VMEM((t * nh, M_DIM), f32),
            pltpu.VMEM((2 * nh, M_DIM, M_DIM), f32),
            pltpu.VMEM((2 * nh, 3 * LANES, LANES), bf16),
        ],
        compiler_params=_cparams("arbitrary"),
        name=f"mlstm_t{t}",
    )(z, z, z, z, gates, gate_b, conv_w, conv_b, m_norm_g, *(state0 if has_state else ()))


def _pair_norm(x, gain):
    lane = lax.broadcasted_iota(jnp.int32, (1, LANES), 1)
    first = lane < A_DIM
    sq = x * x
    s_all = jnp.sum(sq, axis=-1, keepdims=True)
    s0 = jnp.sum(jnp.where(first, sq, 0.0), axis=-1, keepdims=True)
    inv0 = lax.rsqrt(s0 * (1.0 / A_DIM) + EPS)
    inv1 = lax.rsqrt((s_all - s0) * (1.0 / A_DIM) + EPS)
    return x * jnp.where(first, inv0, inv1) * gain


def _rope(x, cos, sin_signed):
    lane = lax.broadcasted_iota(jnp.int32, (1, LANES), 1)
    nf = A_DIM // 4
    partner = jnp.where((lane % (2 * nf)) < nf, pltpu.roll(x, LANES - nf, 1), pltpu.roll(x, nf, 1))
    return x * cos + partner * sin_signed


def _attn_kernel(*refs, t, tq, rope, ctx, out_scale):
    it = iter(refs)
    lam_ref = next(it)
    q_ref, k_ref, v_ref = next(it), next(it), next(it)
    qg_ref, kg_ref, ag_ref = next(it), next(it), next(it)
    if rope:
        cosq_ref, sinq_ref, cosk_ref, sink_ref = next(it), next(it), next(it), next(it)
    if ctx:
        kct_ref, vc_ref = next(it), next(it)
    ha_ref = next(it)
    if not ctx:
        newk_ref, newv_ref = next(it), next(it)
    kt_s = next(it)
    nh = A_HEADS

    @pl.when(pl.program_id(1) == 0)
    def _():
        for h in range(nh):
            col = slice(h * LANES, (h + 1) * LANES)
            kn = _pair_norm(k_ref[:, col].astype(f32), kg_ref[...])
            if not ctx:
                newk_ref[h, 0] = kn[:, :A_DIM]
                newk_ref[h, 1] = kn[:, A_DIM:]
                newv_ref[h] = v_ref[:, col].astype(f32)
            if rope:
                kn = _rope(kn, cosk_ref[...], sink_ref[...])
            kt_s[col, :] = kn.T.astype(bf16)

    lane = lax.broadcasted_iota(jnp.int32, (1, LANES), 1)
    lam = lam_ref[0]
    for h in range(nh):
        col = slice(h * LANES, (h + 1) * LANES)
        q = _pair_norm(q_ref[:, col].astype(f32), qg_ref[...])
        if rope:
            q = _rope(q, cosq_ref[...], sinq_ref[...])
        q = q * (A_DIM ** -0.5 * math.log2(math.e))
        qs = [jnp.where(lane < A_DIM, q, 0.0).astype(bf16), jnp.where(lane >= A_DIM, q, 0.0).astype(bf16)]
        kt = kt_s[col, :]
        vb = v_ref[:, col]
        if ctx:
            kctb = kct_ref[h].astype(bf16)
            vcb = vc_ref[h].astype(bf16)
        outs = []
        for i in range(2):
            sn = _dot(qs[i], kt)
            mx = jnp.max(sn, axis=-1, keepdims=True)
            if ctx:
                sc = _dot(qs[i], kctb)
                mx = jnp.maximum(mx, jnp.max(sc, axis=-1, keepdims=True))
            en = jnp.exp2(sn - mx)
            den = jnp.sum(en, axis=-1, keepdims=True)
            o = _dot(en.astype(bf16), vb)
            if ctx:
                ec = jnp.exp2(sc - mx)
                den = den + jnp.sum(ec, axis=-1, keepdims=True)
                o = o + _dot(ec.astype(bf16), vcb)
            outs.append(o * (1.0 / den))
        o = outs[0] - lam * outs[1]
        ha_ref[:, col] = (_rms(o) * ag_ref[...] * out_scale).astype(ha_ref.dtype)


def _attn(z, lam, qg2, kg2, a_norm_g, *, batch, t, row_block0, rope_tabs=None, ctx_kv=None, out_scale):
    nh = A_HEADS
    w = A_W
    tq = min(t, 256)
    nq = t // tq
    rope = rope_tabs is not None
    ctx = ctx_kv is not None
    qblk = 4 * M_W // w
    vec = pl.BlockSpec((1, LANES), lambda b, i: (0, 0))
    in_specs = [
        pl.BlockSpec(memory_space=pltpu.SMEM),
        pl.BlockSpec((tq, w), lambda b, i: ((row_block0 + b) * nq + i, qblk)),
        pl.BlockSpec((t, w), lambda b, i: (row_block0 + b, qblk + 1)),
        pl.BlockSpec((t, w), lambda b, i: (row_block0 + b, qblk + 2)),
        vec, vec, vec,
    ]
    args = [lam, z, z, z, qg2, kg2, a_norm_g]
    if rope:
        cos, sin = rope_tabs
        in_specs += [pl.BlockSpec((tq, LANES), lambda b, i: (i, 0))] * 2
        in_specs += [pl.BlockSpec((t, LANES), lambda b, i: (0, 0))] * 2
        args += [cos, sin, cos, sin]
    if ctx:
        kct, vc = ctx_kv
        in_specs += [pl.BlockSpec((None,) + kct.shape[1:], lambda b, i: (b, 0, 0, 0)),
                     pl.BlockSpec((None,) + vc.shape[1:], lambda b, i: (b, 0, 0, 0))]
        args += [kct, vc]
    out_shape = [jax.ShapeDtypeStruct((batch * t, w), bf16)]
    out_specs = [pl.BlockSpec((tq, w), lambda b, i: (b * nq + i, 0))]
    if not ctx:
        out_shape += [jax.ShapeDtypeStruct((batch, nh, 2, t, A_DIM), f32),
                      jax.ShapeDtypeStruct((batch, nh, t, A_VDIM), f32)]
        out_specs += [pl.BlockSpec((None, nh, 2, t, A_DIM), lambda b, i: (b, 0, 0, 0, 0)),
                      pl.BlockSpec((None, nh, t, A_VDIM), lambda b, i: (b, 0, 0, 0))]
    return pl.pallas_call(
        functools.partial(_attn_kernel, t=t, tq=tq, rope=rope, ctx=ctx, out_scale=out_scale),
        out_shape=tuple(out_shape),
        grid=(batch, nq),
        in_specs=in_specs,
        out_specs=tuple(out_specs),
        scratch_shapes=[pltpu.VMEM((w, t), bf16)],
        compiler_params=_cparams("arbitrary", "arbitrary"),
        name=f"diff_attn_t{t}",
    )(*args)


def _rope_tables(t):
    rows = t // GRID_W
    pos_row = np.repeat(np.arange(rows, dtype=np.float64), GRID_W)
    pos_col = (np.arange(rows * GRID_W) % GRID_W).astype(np.float64)
    nf = A_DIM // 4
    inv = ROPE_THETA ** (-np.arange(nf, dtype=np.float64) / nf)
    lane = np.arange(LANES)
    j = lane % (2 * nf)
    use_col = (lane % A_DIM) >= (A_DIM // 2)
    ang = np.where(use_col[None, :], pos_col[:, None], pos_row[:, None]) * inv[j % nf][None, :]
    sign = np.where(j < nf, -1.0, 1.0)[None, :]
    return jnp.asarray(np.cos(ang), f32), jnp.asarray(np.sin(ang) * sign, f32)


def _ffn_kernel(xp_ref, xs_ref, hmp_ref, hms_ref, hap_ref, has_ref, mod_ref, wo_ref, w1_ref, w3_ref, w2_ref, o_ref,
                *, chunks, npt):
    half = hmp_ref.shape[1]
    hm = _pick(hmp_ref, hms_ref, npt)
    ha = _pick(hap_ref, has_ref, npt)
    x = _pick(xp_ref, xs_ref, npt) + mod_ref[2:3, :] * (_dot(hm, wo_ref[:half, :]) + _dot(ha, wo_ref[half:, :]))
    h = _modulate(x, mod_ref, 3).astype(bf16)
    acc = jnp.zeros(x.shape, f32)
    for lo, hi in chunks:
        ab = _dot(h, jnp.concatenate([w1_ref[:, lo:hi], w3_ref[:, lo:hi]], axis=1))
        a = _silu(ab[:, :hi - lo]) * ab[:, hi - lo:]
        acc = acc + _dot(a.astype(bf16), w2_ref[lo:hi, :])
    o_ref[...] = x + mod_ref[5:6, :] * acc


def _ffn(xp, xs, hm_p, hm_s, ha_p, ha_s, mod, w_out, w1, w3, w2, n_prompt_rows, dec_seq, job=None):
    d = xp.shape[1]
    r = xp.shape[0] + xs.shape[0]
    tm = ROW_TILE
    npt = n_prompt_rows // tm
    grp = functools.partial(_group_of_tile, tm=tm, n_prompt_rows=n_prompt_rows, dec_seq=dec_seq)
    ff = w1.shape[1]
    cut = MXU_COLS * ((ff // MXU_COLS + 1) // 2) if ff % MXU_COLS == 0 else ff
    chunks = ((0, cut), (cut, ff)) if cut < ff else ((0, ff),)
    return _call_with_job(
        functools.partial(_ffn_kernel, chunks=chunks, npt=npt), job(r // tm) if job else None,
        out_shape=(jax.ShapeDtypeStruct((r, d), f32),),
        grid=(r // tm,),
        in_specs=[
            *_part_specs(tm, d, npt),
            *_part_specs(tm, hm_p.shape[1], npt),
            *_part_specs(tm, ha_p.shape[1], npt),
            pl.BlockSpec((None, 6, d), lambda i: (grp(i), 0, 0)),
            _const_spec(w_out.shape), _const_spec(w1.shape), _const_spec(w3.shape), _const_spec(w2.shape),
        ],
        out_specs=(pl.BlockSpec((tm, d), lambda i: (i, 0)),),
        args=(xp, xs, hm_p, hm_s, ha_p, ha_s, mod, w_out, w1, w3, w2),
        compiler_params=_cparams("arbitrary"),
        name="outproj_ffn",
    )


def _gelu_tanh(x):
    return 0.5 * x * (1.0 + jnp.tanh(math.sqrt(2.0 / math.pi) * (x + 0.044715 * (x * x * x))))


def _gmlp_kernel(x_ref, mod_ref, win_ref, bin_ref, lng_ref, lnb_ref, ws_ref, bs_ref, wout_ref, o_ref, us_s):
    x = x_ref[...]
    tm = x.shape[0]
    e = wout_ref.shape[0]
    ge = e // C_GROUPS
    h = _modulate(x, mod_ref, 0).astype(bf16)
    u = _gelu_tanh(_dot(h, win_ref[:, :e]) + bin_ref[:, :e])
    v = _gelu_tanh(_dot(h, win_ref[:, e:]) + bin_ref[:, e:])
    mu = jnp.mean(v, axis=-1, keepdims=True)
    vc = v - mu
    var = jnp.mean(vc * vc, axis=-1, keepdims=True)
    vn = (vc * lax.rsqrt(var + EPS) * lng_ref[...] + lnb_ref[...]).astype(bf16)
    for n in range(tm // C_CHUNK):
        rows = slice(n * C_CHUNK, (n + 1) * C_CHUNK)
        for g in range(C_GROUPS):
            cols = slice(g * ge, (g + 1) * ge)
            s = _dot(ws_ref[g], vn[rows, cols]) + bs_ref[:, cols]
            us_s[rows, cols] = (u[rows, cols] * s).astype(bf16)
    o_ref[...] = x + mod_ref[2:3, :] * _dot(us_s[...], wout_ref[...])


def _gmlp(x, mod, w_in, b_in, ln_g, ln_b, ws, bs_full, w_out, n_prompt_rows, dec_seq, job=None):
    r, d = x.shape
    tm = ROW_TILE
    e = w_out.shape[0]
    grp = functools.partial(_group_of_tile, tm=tm, n_prompt_rows=n_prompt_rows, dec_seq=dec_seq)
    return _call_with_job(
        _gmlp_kernel, job(r // tm) if job else None,
        out_shape=(jax.ShapeDtypeStruct((r, d), f32),),
        grid=(r // tm,),
        in_specs=[
            pl.BlockSpec((tm, d), lambda i: (i, 0)),
            pl.BlockSpec((None, 6, d), lambda i: (grp(i), 0, 0)),
            _const_spec(w_in.shape), _const_spec(b_in.shape), _const_spec(ln_g.shape), _const_spec(ln_b.shape),
            _const_spec(ws.shape), _const_spec(bs_full.shape), _const_spec(w_out.shape),
        ],
        out_specs=(pl.BlockSpec((tm, d), lambda i: (i, 0)),),
        args=(x, mod, w_in, b_in, ln_g, ln_b, ws, bs_full, w_out),
        scratch_shapes=[pltpu.VMEM((tm, e), bf16)],
        compiler_params=_cparams("arbitrary"),
        name="gmlp",
    )


def _router_kernel(x_ref, mod_ref, wr_hi_ref, wr_lo_ref, h_ref, meta_ref, counts_ref):
    hf = _modulate(x_ref[...], mod_ref, 3)
    hb = hf.astype(bf16)
    h_ref[...] = hb
    h_lo = (hf - hb.astype(f32)).astype(bf16)
    logits = _dot(hb, wr_hi_ref[...]) + (_dot(hb, wr_lo_ref[...]) + _dot(h_lo, wr_hi_ref[...]))
    lane = lax.broadcasted_iota(jnp.int32, logits.shape, 1).astype(f32)
    logits = jnp.where(lane < N_EXPERTS, logits, -jnp.inf)
    m1 = jnp.max(logits, axis=-1, keepdims=True)
    i1 = jnp.min(jnp.where(logits == m1, lane, float(LANES)), axis=-1, keepdims=True)
    rest = jnp.where(lane == i1, -jnp.inf, logits)
    m2 = jnp.max(rest, axis=-1, keepdims=True)
    i2 = jnp.min(jnp.where(rest == m2, lane, float(LANES)), axis=-1, keepdims=True)
    e2 = jnp.exp(m2 - m1)
    w1 = 1.0 / (1.0 + e2)
    w2 = e2 * w1

    tm = logits.shape[0]
    cnt = jnp.where(lane == i1, 1.0, jnp.where(lane == i2, 1.0, 0.0))
    rr = lax.broadcasted_iota(jnp.int32, (tm, tm), 0)
    cc = lax.broadcasted_iota(jnp.int32, (tm, tm), 1)
    before = jnp.where(rr > cc, 1.0, 0.0).astype(bf16)
    rank = _dot(before, cnt.astype(bf16))
    counts = jnp.sum(cnt, axis=0, keepdims=True)
    padded = jnp.floor((counts + (PIECE_ALIGN - 1)) * (1.0 / PIECE_ALIGN)) * PIECE_ALIGN
    lane1 = lane[0:1, :]
    piece_off = jnp.zeros((1, LANES), f32)
    off = jnp.zeros((1, 1), f32)
    for e in range(N_EXPERTS):
        piece_off = jnp.where(lane1 == e, off, piece_off)
        off = off + padded[:, e:e + 1]
    local = piece_off + rank
    pos1 = jnp.sum(jnp.where(lane == i1, local, 0.0), axis=-1, keepdims=True)
    pos2 = jnp.sum(jnp.where(lane == i2, local, 0.0), axis=-1, keepdims=True)
    meta_ref[...] = jnp.where(lane == 0, pos1, jnp.where(lane == 1, pos2, jnp.where(lane == 2, w1,
                              jnp.where(lane == 3, w2, 0.0))))
    counts_ref[...] = counts


def _router(x, mod, wr_hi, wr_lo, n_prompt_rows, dec_seq):
    r, d = x.shape
    tm = MOE_SORT_BLOCK
    grp = functools.partial(_group_of_tile, tm=tm, n_prompt_rows=n_prompt_rows, dec_seq=dec_seq)
    return pl.pallas_call(
        _router_kernel,
        out_shape=(jax.ShapeDtypeStruct((r, d), bf16), jax.ShapeDtypeStruct((r, LANES), f32),
                   jax.ShapeDtypeStruct((r // tm, 1, LANES), f32)),
        grid=(r // tm,),
        in_specs=[
            pl.BlockSpec((tm, d), lambda i: (i, 0)),
            pl.BlockSpec((None, 6, d), lambda i: (grp(i), 0, 0)),
            _const_spec(wr_hi.shape), _const_spec(wr_lo.shape),
        ],
        out_specs=(pl.BlockSpec((tm, d), lambda i: (i, 0)), pl.BlockSpec((tm, LANES), lambda i: (i, 0)),
                   pl.BlockSpec((None, 1, LANES), lambda i: (i, 0, 0))),
        compiler_params=_cparams("arbitrary"),
        name="router",
    )(x, mod, wr_hi, wr_lo)


def _moe_kernel(n16_ref, loc_ref, dst_ref, seg_off_ref, seg_len_ref, csel_ref, cnum_ref,
                h_ref, meta_ref, x_ref, mod_ref, w1_ref, w3_ref, w2_ref, op_ref, os_ref,
                hs_s, ys_s, loc_s, *, nsub, ne, nf, chunks, nsb_prompt):
    sb = pl.program_id(0)
    p = pl.program_id(1)
    n_exp = ne * nf
    loc_rows = loc_s.shape[0]
    pa = PIECE_ALIGN

    def one_hot_cols(meta, v1, v2):
        lane = lax.broadcasted_iota(jnp.int32, (1, loc_rows), 1).astype(f32)
        return jnp.where(lane == meta[:, 0:1], v1, jnp.where(lane == meta[:, 1:2], v2, 0.0)).astype(bf16)

    def copy_pieces(blk, to_sorted):
        for e in range(ne):
            n = n16_ref[blk * ne + e]
            src = loc_ref[blk * ne + e]
            dst = dst_ref[blk * ne + e]

            def cp(i, _, src=src, dst=dst):
                a = pl.ds(pl.multiple_of(src + pa * i, pa), pa)
                b = pl.ds(pl.multiple_of(dst + pa * i, pa), pa)
                if to_sorted:
                    hs_s[b, :] = loc_s[a, :]
                else:
                    loc_s[a, :] = ys_s[b, :].astype(bf16)
                return 0

            lax.fori_loop(0, n, cp, 0)

    @pl.when(p < nsub)
    def _():
        @pl.when(p == 0)
        def _():
            hs_s[...] = jnp.zeros_like(hs_s)
            ys_s[...] = jnp.zeros_like(ys_s)

        pt = one_hot_cols(meta_ref[...], 1.0, 1.0)
        loc_s[...] = _dot_tn(pt, h_ref[...]).astype(bf16)
        copy_pieces(sb * nsub + p, True)

    @pl.when((p >= nsub) & (p < nsub + n_exp))
    def _():
        e = lax.div(p - nsub, jnp.int32(nf))
        start = seg_off_ref[sb * ne + e]
        ln = seg_len_ref[sb * ne + e]
        which = csel_ref[sb * ne + e]
        count = cnum_ref[sb * ne + e]

        def chunk(lo, size):
            r0 = jnp.minimum(lo, hs_s.shape[0] - size)
            rows = pl.ds(pl.multiple_of(r0, pa), size)
            xc = hs_s[rows, :]
            ab = _dot(xc, jnp.concatenate([w1_ref[...], w3_ref[...]], axis=1))
            tf = w1_ref.shape[1]
            a = _silu(ab[:, :tf]) * ab[:, tf:]
            y = _dot(a.astype(bf16), w2_ref[...])
            ri = r0 + lax.broadcasted_iota(jnp.int32, (size, 1), 0)
            ys_s[rows, :] += jnp.where(ri >= lo, jnp.where(ri < start + ln, y, 0.0), 0.0)

        for k, size in enumerate(chunks):
            @pl.when(which == k)
            def _(size=size):
                def body(j, _):
                    chunk(start + j * size, size)
                    return 0

                lax.fori_loop(0, count, body, 0)

    @pl.when(p >= nsub + n_exp)
    def _():
        copy_pieces(sb * nsub + (p - nsub - n_exp), False)
        meta = meta_ref[...]
        a = one_hot_cols(meta, meta[:, 2:3], meta[:, 3:4])
        y = x_ref[...] + mod_ref[5:6, :] * _dot(a, loc_s[...])

        @pl.when(sb < nsb_prompt)
        def _():
            op_ref[...] = y

        @pl.when(sb >= nsb_prompt)
        def _():
            os_ref[...] = y


def _moe(x, h, meta, counts, mod, w1, w3, w2, n_prompt_rows, dec_seq):
    r, d = x.shape
    ne, ff, _ = w2.shape
    assert w1.shape == (ne, ff // MOE_FF_TILE, d, MOE_FF_TILE)
    t_super, tb, chunks, tf, pa = MOE_SUPER_BLOCK, MOE_SORT_BLOCK, MOE_CHUNKS, MOE_FF_TILE, PIECE_ALIGN
    ch = max(chunks)
    nsub = t_super // tb
    nsb = r // t_super
    npb = n_prompt_rows // tb
    nf = ff // tf
    n_exp = ne * nf
    loc_rows = 2 * tb + LANES
    assert loc_rows >= 2 * tb + ne * (pa - 1)
    max_rows = 2 * t_super + nsub * ne * (pa - 1)
    sort_rows = LANES * (-(-max_rows // LANES))
    assert all(size % pa == 0 and size <= sort_rows for size in chunks)

    cnt = counts[:, 0, :ne].astype(jnp.int32)
    n16 = (cnt + (pa - 1)) // pa
    loc = pa * (jnp.cumsum(n16, axis=1) - n16)
    n16_sb = n16.reshape(nsb, nsub, ne)
    seg_len = pa * jnp.sum(n16_sb, axis=1)
    seg_off = jnp.cumsum(seg_len, axis=1) - seg_len
    dst = seg_off[:, None, :] + pa * (jnp.cumsum(n16_sb, axis=1) - n16_sb)
    cnum = (seg_len + ch - 1) // ch
    need = pa * ((seg_len // pa + jnp.maximum(cnum, 1) - 1) // jnp.maximum(cnum, 1))
    csel = sum((need > size).astype(jnp.int32) for size in chunks[:-1])
    scalars = [a.reshape(-1).astype(jnp.int32) for a in (n16, loc, dst, seg_off, seg_len, csel, cnum)]

    grp = functools.partial(_group_of_tile, tm=t_super, n_prompt_rows=n_prompt_rows, dec_seq=dec_seq)

    def exp_step(p):
        return jnp.clip(p - nsub, 0, n_exp - 1)

    def tok_blk(sb, s):
        return sb * nsub + jnp.clip(s, 0, nsub - 1)

    grid_spec = pltpu.PrefetchScalarGridSpec(
        num_scalar_prefetch=len(scalars),
        grid=(nsb, nsub + n_exp + nsub),
        in_specs=[
            pl.BlockSpec((tb, d), lambda sb, p, *_: (tok_blk(sb, p), 0)),
            pl.BlockSpec((tb, LANES), lambda sb, p, *_: (sb * nsub + jnp.where(p < nsub, p, jnp.clip(p - nsub - n_exp, 0, nsub - 1)), 0)),
            pl.BlockSpec((tb, d), lambda sb, p, *_: (tok_blk(sb, p - nsub - n_exp), 0)),
            pl.BlockSpec((None, 6, d), lambda sb, p, *_: (grp(sb), 0, 0)),
            pl.BlockSpec((None, None, d, tf), lambda sb, p, *_: (exp_step(p) // nf, exp_step(p) % nf, 0, 0)),
            pl.BlockSpec((None, None, d, tf), lambda sb, p, *_: (exp_step(p) // nf, exp_step(p) % nf, 0, 0)),
            pl.BlockSpec((None, tf, d), lambda sb, p, *_: (exp_step(p) // nf, exp_step(p) % nf, 0)),
        ],
        out_specs=(
            pl.BlockSpec((tb, d), lambda sb, p, *_: (jnp.minimum(tok_blk(sb, p - nsub - n_exp), npb - 1), 0)),
            pl.BlockSpec((tb, d), lambda sb, p, *_: (jnp.maximum(tok_blk(sb, p - nsub - n_exp) - npb, 0), 0)),
        ),
        scratch_shapes=[
            pltpu.VMEM((sort_rows, d), bf16),
            pltpu.VMEM((sort_rows, d), f32),
            pltpu.VMEM((loc_rows, d), bf16),
        ],
    )
    return pl.pallas_call(
        functools.partial(_moe_kernel, nsub=nsub, ne=ne, nf=nf, chunks=chunks,
                          nsb_prompt=n_prompt_rows // t_super),
        out_shape=(jax.ShapeDtypeStruct((n_prompt_rows, d), f32), jax.ShapeDtypeStruct((r - n_prompt_rows, d), f32)),
        grid_spec=grid_spec,
        compiler_params=_cparams("arbitrary", "arbitrary"),
        name="moe_sparse",
    )(*scalars, h, meta, x, mod, w1, w3, w2)


def kernel(x_prompt, x_sample, c, cache_dattn_k, cache_dattn_v, state_mlstm_c, state_mlstm_n, state_mlstm_m,
           c_ctx, w_ada, b_ada, w_in_ab, conv_w, conv_b, gate_b, qn_g, kn_g, lam_q1, lam_k1, lam_q2, lam_k2,
           m_norm_g, a_norm_g, w_out_ab, ff_w1, ff_w3, ff_w2, w_in_c, b_in_c, c_ln_g, c_ln_b, c_ws, c_bs,
           w_out_c, w_router, ex_w1, ex_w3, ex_w2):
    bp, seq, d = x_prompt.shape
    bs, dec_seq, _ = x_sample.shape
    depth = w_ada.shape[0]
    n_prompt_rows = bp * seq
    assert n_prompt_rows % dec_seq == 0 and seq % M_CHUNK == 0 and dec_seq % MOE_SUPER_BLOCK == 0
    nh = M_HEADS

    mods = _ada_table(jnp.concatenate([c_ctx[None], c], axis=0), w_ada, b_ada)
    x = (x_prompt.reshape(n_prompt_rows, d), x_sample.reshape(bs * dec_seq, d))
    rows = (n_prompt_rows, dec_seq)

    def joined(v):
        return jnp.concatenate(v, axis=0) if isinstance(v, tuple) else v

    def parts(v):
        return v if isinstance(v, tuple) else (v[:n_prompt_rows], v[n_prompt_rows:])

    new_k, new_v, new_c, new_n, new_m = [], [], [], [], []
    for l in range(depth):
        j = l // 2
        mod = mods[l]
        if l % 2 == 0:
            lam_init = 0.8 - 0.6 * math.exp(-0.3 * l)
            lam = (jnp.exp(jnp.sum((lam_q1[j] * lam_k1[j]).astype(f32)))
                   - jnp.exp(jnp.sum((lam_q2[j] * lam_k2[j]).astype(f32))) + lam_init).reshape(1)
            o3 = 4 * M_W
            o4 = o3 + 4 * nh
            w = w_in_ab[j]
            w_main = jnp.concatenate([w[:, :o3], w[:, o4:]], axis=1).astype(bf16)
            w_gate = jnp.zeros((d, LANES), f32).at[:, :4 * nh].set(w[:, o3:o4]).astype(bf16)
            x = parts(x)
            nxt = j if l + 1 < depth else None
            z, gates, *ex2 = _inproj(*x, mod, w_main, w_gate, *rows,
                                     job=None if nxt is None else functools.partial(_cast_job_rows, ex_w2[nxt]))
            gb = jnp.zeros((1, LANES), f32).at[0, :4 * nh].set(gate_b[j])
            mng = m_norm_g[j].reshape(1, M_W)
            cb = conv_b[j].reshape(1, 2 * M_W)
            hm_p, c_f, n_f, m_f = _mlstm(z, gates, gb, conv_w[j], cb, mng, None, batch=bp, t=seq, row_block0=0)
            state0 = (state_mlstm_c[:, j], state_mlstm_n[:, j].reshape(bs, 2, nh, 1, M_DIM),
                      state_mlstm_m[:, j].reshape(bs, 2, nh, 1, 1))
            hm_s, _, _, _ = _mlstm(z, gates, gb, conv_w[j], cb, mng, state0,
                                   batch=bs, t=dec_seq, row_block0=n_prompt_rows // dec_seq)
            new_c.append(c_f)
            new_n.append(n_f.reshape(bp, 2, nh, M_DIM))
            new_m.append(m_f.reshape(bp, 2, nh))

            qg2 = jnp.tile(qn_g[j], 2).reshape(1, LANES)
            kg2 = jnp.tile(kn_g[j], 2).reshape(1, LANES)
            ag = a_norm_g[j].reshape(1, LANES)
            ha_p, k_ctx, v_ctx = _attn(z, lam, qg2, kg2, ag, batch=bp, t=seq, row_block0=0,
                                       out_scale=1.0 - lam_init)
            kct = cache_dattn_k[:, j].transpose(0, 1, 2, 4, 3).reshape(bs, A_HEADS, LANES, -1)
            (ha_s,) = _attn(z, lam, qg2, kg2, ag, batch=bs, t=dec_seq, row_block0=n_prompt_rows // dec_seq,
                            rope_tabs=_rope_tables(dec_seq), ctx_kv=(kct, cache_dattn_v[:, j]),
                            out_scale=1.0 - lam_init)
            new_k.append(k_ctx)
            new_v.append(v_ctx)
            x, *ex1 = _ffn(*x, hm_p, hm_s, ha_p, ha_s, mod, w_out_ab[j].astype(bf16), ff_w1[j].astype(bf16),
                           ff_w3[j].astype(bf16), ff_w2[j].astype(bf16), *rows,
                           job=None if nxt is None else functools.partial(_cast_job_tiled, ex_w1[nxt], tf=MOE_FF_TILE))
        else:
            e = w_out_c.shape[1]
            bs_full = jnp.repeat(c_bs[j].T, e // C_GROUPS, axis=1)
            x, ex3 = _gmlp(joined(x), mod, w_in_c[j].astype(bf16), b_in_c[j].reshape(1, -1),
                           c_ln_g[j].reshape(1, -1), c_ln_b[j].reshape(1, -1), c_ws[j].astype(bf16), bs_full,
                           w_out_c[j].astype(bf16), *rows,
                           job=functools.partial(_cast_job_tiled, ex_w3[j], tf=MOE_FF_TILE))
            wr = jnp.zeros((d, LANES), f32).at[:, :N_EXPERTS].set(w_router[j])
            wr_hi = wr.astype(bf16)
            wr_lo = (wr - wr_hi.astype(f32)).astype(bf16)
            h, meta, counts = _router(x, mod, wr_hi, wr_lo, *rows)
            x = _moe(x, h, meta, counts, mod, ex1[0], ex3, ex2[0].reshape(ex_w2[j].shape), *rows)

    y_prompt, y_sample = parts(x)
    y_prompt = y_prompt.reshape(bp, seq, d)
    y_sample = y_sample.reshape(bs, dec_seq, d)
    return (y_prompt, y_sample, jnp.stack(new_k, axis=1), jnp.stack(new_v, axis=1),
            jnp.stack(new_c, axis=1), jnp.stack(new_n, axis=1), jnp.stack(new_m, axis=1))
```

```python
import functools
import math
from typing import Callable, NamedTuple

import jax
import jax.numpy as jnp
import numpy as np
from jax import lax
from jax.experimental import pallas as pl
from jax.experimental.pallas import tpu as pltpu

f32 = jnp.float32
bf16 = jnp.bfloat16

D_MODEL = 1024
M_HEADS = 4
M_DIM = 128
M_W = M_HEADS * M_DIM
M_CHUNK = 128
A_HEADS = 4
A_VDIM = 128
A_DIM = 64
A_W = A_HEADS * A_VDIM
GRID_W = 64
ROPE_THETA = 10000.0
C_CHUNK = 128
C_GROUPS = 4
N_EXPERTS = 8
EPS = 1e-6

LANES = 128
MXU_COLS = 256
ROW_TILE = 512
PIECE_ALIGN = 16
MOE_SUPER_BLOCK = 2048
MOE_SORT_BLOCK = 512
MOE_CHUNKS = (256, 320, 384, 448, 512, 544, 576, 608, 640)
MOE_FF_TILE = 896
VMEM_LIMIT = 60 * 1024 * 1024


def _cparams(*sem):
    return pltpu.CompilerParams(dimension_semantics=tuple(sem), vmem_limit_bytes=VMEM_LIMIT)


def _const_spec(shape):
    nd = len(shape)
    return pl.BlockSpec(shape, lambda *_: (0,) * nd, pipeline_mode=pl.Buffered(1))


def _sigmoid(x):
    return 1.0 / (1.0 + jnp.exp(-x))


def _silu(x):
    return x * _sigmoid(x)


def _log_sigmoid(x):
    return jnp.minimum(x, 0.0) - jnp.log(1.0 + jnp.exp(-jnp.abs(x)))


def _rms(x):
    return x * lax.rsqrt(jnp.mean(x * x, axis=-1, keepdims=True) + EPS)


def _modulate(x, mod_ref, first):
    shift = mod_ref[first:first + 1, :]
    scale = mod_ref[first + 1:first + 2, :]
    return _rms(x) * (1.0 + scale) + shift


def _dot(a, b):
    return jnp.dot(a, b, preferred_element_type=f32)


def _dot_nt(a, b):
    return lax.dot_general(a, b, (((1,), (1,)), ((), ())), preferred_element_type=f32)


def _dot_tn(a, b):
    return lax.dot_general(a, b, (((0,), (0,)), ((), ())), preferred_element_type=f32)


def _split3(x):
    hi = x.astype(bf16)
    r1 = x - hi.astype(f32)
    mid = r1.astype(bf16)
    lo = (r1 - mid.astype(f32)).astype(bf16)
    return hi, mid, lo


def _group_of_tile(i, tm, n_prompt_rows, dec_seq):
    pt = n_prompt_rows // tm
    return jnp.where(i < pt, 0, 1 + (i - pt) // (dec_seq // tm))


def _ada_kernel(cv_ref, w_ref, b_ref, o_ref):
    a = _silu(cv_ref[...]).astype(bf16)
    o_ref[...] = _dot(a, w_ref[...].astype(bf16)) + b_ref[...]


def _ada_table(cv, w_ada, b_ada):
    depth, d, n = w_ada.shape
    g = cv.shape[0]
    gp = 8 * ((g + 7) // 8)
    cvp = jnp.zeros((gp, d), f32).at[:g].set(cv)
    tn = 1536
    out = pl.pallas_call(
        _ada_kernel,
        out_shape=jax.ShapeDtypeStruct((depth, gp, n), f32),
        grid=(depth, n // tn),
        in_specs=[
            pl.BlockSpec((gp, d), lambda l, j: (0, 0)),
            pl.BlockSpec((None, d, tn), lambda l, j: (l, 0, j)),
            pl.BlockSpec((None, 1, tn), lambda l, j: (l, 0, j)),
        ],
        out_specs=pl.BlockSpec((None, gp, tn), lambda l, j: (l, 0, j)),
        compiler_params=_cparams("arbitrary", "arbitrary"),
        name="ada_table",
    )(cvp, w_ada, b_ada.reshape(depth, 1, n))
    return out[:, :g].reshape(depth, g, 6, d)


def _part_specs(tm, width, npt):
    first = pl.BlockSpec((tm, width), lambda i: (jnp.minimum(i, npt - 1), 0))
    second = pl.BlockSpec((tm, width), lambda i: (jnp.maximum(i - npt, 0), 0))
    return first, second


def _pick(a_ref, b_ref, npt):
    return jnp.where(pl.program_id(0) < npt, a_ref[...], b_ref[...])


class _CastJob(NamedTuple):
    src: jax.Array
    in_spec: pl.BlockSpec
    out_spec: pl.BlockSpec
    out_shape: jax.ShapeDtypeStruct
    body: Callable


def _cast_job_tiled(src, steps, tf):
    ne, d, ff = src.shape
    bands = steps // ne
    rows = d // bands
    nf = ff // tf

    def body(i_ref, o_ref):
        for f in range(nf):
            o_ref[f] = i_ref[:, f * tf:(f + 1) * tf].astype(bf16)

    return _CastJob(src, pl.BlockSpec((None, rows, ff), lambda i: (i // bands, i % bands, 0)),
                    pl.BlockSpec((None, nf, rows, tf), lambda i: (i // bands, 0, i % bands, 0)),
                    jax.ShapeDtypeStruct((ne, nf, d, tf), bf16), body)


def _cast_job_rows(src, steps):
    ne, ff, d = src.shape
    rows = ne * ff // steps

    def body(i_ref, o_ref):
        o_ref[...] = i_ref[...].astype(bf16)

    return _CastJob(src.reshape(ne * ff, d), pl.BlockSpec((rows, d), lambda i: (i, 0)),
                    pl.BlockSpec((rows, d), lambda i: (i, 0)), jax.ShapeDtypeStruct((ne * ff, d), bf16), body)


def _call_with_job(kernel_fn, job, *, out_shape, in_specs, out_specs, args, **kw):
    if job is None:
        return pl.pallas_call(kernel_fn, out_shape=out_shape, in_specs=in_specs, out_specs=out_specs, **kw)(*args)
    n_in, n_out = len(in_specs), len(out_shape)

    def with_job(*refs):
        kernel_fn(*refs[:n_in], *refs[n_in + 1:n_in + 1 + n_out], *refs[n_in + 2 + n_out:])
        job.body(refs[n_in], refs[n_in + 1 + n_out])

    return pl.pallas_call(with_job, out_shape=(*out_shape, job.out_shape), in_specs=[*in_specs, job.in_spec],
                          out_specs=(*out_specs, job.out_spec), **kw)(*args, job.src)


def _inproj_kernel(xp_ref, xs_ref, mod_ref, w_ref, wg_ref, z_ref, g_ref, *, n_main, npt):
    h = _modulate(_pick(xp_ref, xs_ref, npt), mod_ref, 0).astype(bf16)
    step = 512
    for j in range(n_main // step):
        z_ref[:, j * step:(j + 1) * step] = _dot(h, w_ref[:, j * step:(j + 1) * step]).astype(bf16)
    g_ref[...] = _dot(h, wg_ref[...])


def _inproj(xp, xs, mod, w_main, w_gate, n_prompt_rows, dec_seq, job=None):
    d = xp.shape[1]
    r = xp.shape[0] + xs.shape[0]
    tm = ROW_TILE
    npt = n_prompt_rows // tm
    n_main = w_main.shape[1]
    grp = functools.partial(_group_of_tile, tm=tm, n_prompt_rows=n_prompt_rows, dec_seq=dec_seq)
    return _call_with_job(
        functools.partial(_inproj_kernel, n_main=n_main, npt=npt), job(r // tm) if job else None,
        out_shape=(jax.ShapeDtypeStruct((r, n_main), bf16), jax.ShapeDtypeStruct((r, LANES), f32)),
        grid=(r // tm,),
        in_specs=[
            *_part_specs(tm, d, npt),
            pl.BlockSpec((None, 6, d), lambda i: (grp(i), 0, 0)),
            _const_spec(w_main.shape),
            _const_spec(w_gate.shape),
        ],
        out_specs=(pl.BlockSpec((tm, n_main), lambda i: (i, 0)), pl.BlockSpec((tm, LANES), lambda i: (i, 0))),
        args=(xp, xs, mod, w_main, w_gate),
        compiler_params=_cparams("arbitrary"),
        name="inproj_ab",
    )


def _conv_silu_chunk(x_ref, c, nc, w_ref, b_ref):
    lc, pa = M_CHUNK, PIECE_ALIGN
    t = nc * lc
    x = x_ref[pl.ds(pl.multiple_of(c * lc, lc), lc), :].astype(f32)
    before = x_ref[pl.ds(pl.multiple_of(jnp.maximum(c * lc - pa, 0), pa), pa), :].astype(f32)[pa - 1:pa, :]
    after = x_ref[pl.ds(pl.multiple_of(jnp.minimum((c + 1) * lc, t - pa), pa), pa), :].astype(f32)[0:1, :]
    before = jnp.where(c > 0, before, 0.0)
    after = jnp.where(c < nc - 1, after, 0.0)
    row = lax.broadcasted_iota(jnp.int32, (lc, 1), 0)
    prev = jnp.where(row == 0, before, pltpu.roll(x, 1, 0))
    nxt = jnp.where(row == lc - 1, after, pltpu.roll(x, lc - 1, 0))
    y = b_ref[...] + prev * w_ref[0:1, :] + x * w_ref[1:2, :] + nxt * w_ref[2:3, :]
    return _silu(y)


def _mlstm_kernel(q_ref, k_ref, v_ref, o_ref, g_ref, gb_ref, cw_ref, cb_ref, ng_ref, *rest, t, has_state):
    if has_state:
        c0_ref, n0_ref, m0_ref, *rest = rest
    hm_ref, c_out, n_out, m_out, qt_s, kh_s, kl_s, vt_s, gs_s, hft_s, hbt_s, ct_s, sel_s = rest
    nc = t // M_CHUNK
    lc = M_CHUNK
    nh = M_HEADS
    w = M_W
    sel_row = lax.broadcasted_iota(jnp.int32, (3 * LANES, LANES), 0)
    for d in range(2):
        for h in range(nh):
            ci, ln = d * nh + h, 2 * nh * d + h
            ct_s[ci] = c0_ref[d, h].T if has_state else jnp.zeros((M_DIM, M_DIM), f32)
            sel_s[ci] = jnp.where(sel_row == ln, 1.0, jnp.where(sel_row == LANES + ln, 1.0, jnp.where(
                sel_row == 2 * LANES + ln, 1.0, 0.0))).astype(bf16)
    if has_state:
        n_out[...] = n0_ref[...]
        m_out[...] = m0_ref[...]
    else:
        n_out[...] = jnp.zeros_like(n_out)
        m_out[...] = jnp.zeros_like(m_out)

    lane = lax.broadcasted_iota(jnp.int32, (1, LANES), 1)
    is_lf = ((lane >= nh) & (lane < 2 * nh)) | ((lane >= 3 * nh) & (lane < 4 * nh))
    is_bw_lf = (lane >= 3 * nh) & (lane < 4 * nh)

    rr = lax.broadcasted_iota(jnp.int32, (lc, lc), 0)
    cc = lax.broadcasted_iota(jnp.int32, (lc, lc), 1)
    tri_incl = jnp.where(rr >= cc, 1.0, 0.0).astype(bf16)

    def prep_body(c, _):
        off = pl.multiple_of(c * lc, lc)
        g = g_ref[pl.ds(off, lc), :] + gb_ref[...]
        tile = jnp.where(is_lf, _log_sigmoid(g), g)
        hi, mid, lo = _split3(jnp.where(is_lf, tile, 0.0))
        cs = _dot(tri_incl, hi) + _dot(tri_incl, mid) + _dot(tri_incl, lo)
        total = jnp.broadcast_to(cs[lc - 1:lc, :], (lc, LANES))
        b = jnp.where(is_bw_lf, total - cs + tile, cs)
        lmb = tile - pltpu.roll(b, LANES - nh, 1)
        gv = pltpu.roll(total, LANES - nh, 1) + lmb
        low = jnp.where(is_lf, b, lmb)
        high = pltpu.roll(jnp.where(is_lf, total, gv), 4 * nh, 1)
        gs_s[pl.ds(off, lc), :] = jnp.where(lane < 4 * nh, low, jnp.where(lane < 8 * nh, high, 0.0))
        wide = pl.ds(pl.multiple_of(c * w, w), w)
        qt_s[wide, :] = _conv_silu_chunk(q_ref, c, nc, cw_ref[:, :w], cb_ref[:, :w]).T
        kf = _conv_silu_chunk(k_ref, c, nc, cw_ref[:, w:], cb_ref[:, w:]) * (M_DIM ** -0.5)
        kh = kf.astype(bf16)
        kh_s[pl.ds(off, lc), :] = kh
        kl_s[pl.ds(off, lc), :] = (kf - kh.astype(f32)).astype(bf16)
        vt_s[wide, :] = v_ref[pl.ds(off, lc), :].astype(f32).T.astype(bf16)
        return 0

    lax.fori_loop(0, nc, prep_body, 0)

    sub8 = lax.broadcasted_iota(jnp.int32, (8, LANES), 0)

    def two_rows(x):
        hi = x.astype(bf16).astype(f32)
        return jnp.where(sub8 == 0, hi, jnp.where(sub8 == 1, x - hi, 0.0)).astype(bf16)

    def issue(c, h, d, tile3, tile_t):
        off = pl.multiple_of(c * lc, lc)
        col = slice(h * M_DIM, (h + 1) * M_DIM)
        ci, ln = d * nh + h, 2 * nh * d + h
        head = pl.ds(pl.multiple_of(c * w + h * M_DIM, M_DIM), M_DIM)
        b_row = tile_t[ln + nh:ln + nh + 1, :]
        g_row = tile_t[ln + 4 * nh:ln + 4 * nh + 1, :]
        total = tile_t[ln + 5 * nh:ln + 5 * nh + 1, 0:1]
        ct, nm, mm = ct_s[ci], n_out[d, h], m_out[d, h]
        qt = qt_s[head, :]
        qtb = qt.astype(bf16)
        khb = kh_s[pl.ds(off, lc), col]
        vtb = vt_s[head, :]
        lmb = _dot(tile3, sel_s[ci])
        sraw = _dot(khb, qtb)
        qn = _dot(two_rows(nm), qtb)
        lhs = jnp.concatenate([ct.astype(bf16), vtb], axis=1)
        m_new = jnp.maximum(mm + total, jnp.max(g_row, axis=1, keepdims=True))
        decay = jnp.exp(mm + total - m_new)
        ew = jnp.exp(g_row - m_new)
        ct_s[ci] = decay * ct + _dot(vtb * ew.astype(bf16), khb)
        ew2 = two_rows(ew)
        nk_h = _dot(ew2, khb)
        nk_l = _dot(ew2, kl_s[pl.ds(off, lc), col])
        n_out[d, h] = decay * nm + (nk_h[0:1, :] + nk_h[1:2, :] + nk_l[0:1, :])
        m_out[d, h] = m_new
        return dict(d=d, head=head, b_row=b_row, mm=mm, qt=qt, lmb=lmb, sraw=sraw, qn=qn, lhs=lhs)

    def weigh(st):
        keep = (cc >= rr) if st["d"] == 0 else (rr >= cc)
        dmat = jnp.where(keep, st["lmb"] + st["b_row"], -jnp.inf)
        inter = st["mm"] + st["b_row"]
        mt = jnp.maximum(inter, jnp.max(dmat, axis=0, keepdims=True))
        w_inter = jnp.exp(inter - mt)
        s = st["sraw"] * jnp.exp(dmat - mt)
        qn = st["qn"]
        den = w_inter * (qn[0:1, :] + qn[1:2, :]) + jnp.sum(s, axis=0, keepdims=True)
        inv = 1.0 / jnp.maximum(jnp.abs(den), jnp.exp(-mt))
        return jnp.concatenate([st["qt"] * (w_inter * inv), s * inv], axis=0).astype(bf16)

    def body(i, _):
        states = []
        for d in range(2):
            c = i if d == 0 else nc - 1 - i
            tile = gs_s[pl.ds(pl.multiple_of(c * lc, lc), lc), :]
            tile_t = tile.T
            hi, mid, lo = _split3(tile)
            tile3 = jnp.concatenate([hi, mid, lo], axis=1)
            states += [issue(c, h, d, tile3, tile_t) for h in range(nh)]
        rhss = [weigh(st) for st in states]
        for st, rhs in zip(states, rhss):
            hct = _dot(st["lhs"], rhs)
            if st["d"] == 0:
                hft_s[st["head"], :] = hct
            else:
                hbt_s[st["head"], :] = hct
        return 0

    lax.fori_loop(0, nc, body, 0)

    def out_body(c, _):
        rows = pl.ds(pl.multiple_of(c * lc, lc), lc)
        wide = pl.ds(pl.multiple_of(c * w, w), w)
        hsum = (hft_s[wide, :] + hbt_s[wide, :]).T
        for h in range(nh):
            col = slice(h * M_DIM, (h + 1) * M_DIM)
            mo = _sigmoid(o_ref[rows, col].astype(f32))
            hm_ref[rows, col] = (_rms(hsum[:, col]) * ng_ref[:, col] * mo).astype(hm_ref.dtype)
        return 0

    lax.fori_loop(0, nc, out_body, 0)
    for d in range(2):
        for h in range(nh):
            c_out[d, h] = ct_s[d * nh + h].T


def _mlstm(z, gates, gate_b, conv_w, conv_b, m_norm_g, state0, *, batch, t, row_block0):
    nh = M_HEADS
    w = M_W
    mode = dict(pipeline_mode=pl.Buffered(1)) if t * w * 2 > (1 << 20) else {}
    seq = lambda colblk: pl.BlockSpec((t, w), lambda b: (row_block0 + b, colblk), **mode)
    state = lambda *tail: pl.BlockSpec((None, 2, nh) + tail, lambda b: (b,) + (0,) * (2 + len(tail)))
    state_specs = [state(M_DIM, M_DIM), state(1, M_DIM), state(1, 1)]
    has_state = state0 is not None
    return pl.pallas_call(
        functools.partial(_mlstm_kernel, t=t, has_state=has_state),
        out_shape=(
            jax.ShapeDtypeStruct((batch * t, w), bf16),
            jax.ShapeDtypeStruct((batch, 2, nh, M_DIM, M_DIM), f32),
            jax.ShapeDtypeStruct((batch, 2, nh, 1, M_DIM), f32),
            jax.ShapeDtypeStruct((batch, 2, nh, 1, 1), f32),
        ),
        grid=(batch,),
        in_specs=[
            seq(0), seq(1), seq(2), seq(3),
            pl.BlockSpec((t, LANES), lambda b: (row_block0 + b, 0)),
            _const_spec(gate_b.shape),
            _const_spec(conv_w.shape), _const_spec(conv_b.shape), _const_spec(m_norm_g.shape),
            *(state_specs if has_state else []),
        ],
        out_specs=(pl.BlockSpec((t, w), lambda b: (b, 0)), *state_specs),
        scratch_shapes=[
            pltpu.VMEM((t * nh, M_DIM), f32),
            pltpu.VMEM((t, w), bf16),
            pltpu.VMEM((t, w), bf16),
            pltpu.VMEM((t * nh, M_DIM), bf16),
            pltpu.VMEM((t, LANES), f32),
            pltpu.VMEM((t * nh, M_DIM), f32),
            pltpu.VMEM((t * nh, M_DIM), f32),
            pltpu.VMEM((2 * nh, M_DIM, M_DIM), f32),
            pltpu.VMEM((2 * nh, 3 * LANES, LANES), bf16),
        ],
        compiler_params=_cparams("arbitrary"),
        name=f"mlstm_t{t}",
    )(z, z, z, z, gates, gate_b, conv_w, conv_b, m_norm_g, *(state0 if has_state else ()))


def _pair_norm(x, gain):
    lane = lax.broadcasted_iota(jnp.int32, (1, LANES), 1)
    first = lane < A_DIM
    sq = x * x
    s_all = jnp.sum(sq, axis=-1, keepdims=True)
    s0 = jnp.sum(jnp.where(first, sq, 0.0), axis=-1, keepdims=True)
    inv0 = lax.rsqrt(s0 * (1.0 / A_DIM) + EPS)
    inv1 = lax.rsqrt((s_all - s0) * (1.0 / A_DIM) + EPS)
    return x * jnp.where(first, inv0, inv1) * gain


def _rope(x, cos, sin_signed):
    lane = lax.broadcasted_iota(jnp.int32, (1, LANES), 1)
    nf = A_DIM // 4
    partner = jnp.where((lane % (2 * nf)) < nf, pltpu.roll(x, LANES - nf, 1), pltpu.roll(x, nf, 1))
    return x * cos + partner * sin_signed


def _attn_kernel(*refs, t, tq, rope, ctx, out_scale):
    it = iter(refs)
    lam_ref = next(it)
    q_ref, k_ref, v_ref = next(it), next(it), next(it)
    qg_ref, kg_ref, ag_ref = next(it), next(it), next(it)
    if rope:
        cosq_ref, sinq_ref, cosk_ref, sink_ref = next(it), next(it), next(it), next(it)
    if ctx:
        kct_ref, vc_ref = next(it), next(it)
    ha_ref = next(it)
    if not ctx:
        newk_ref, newv_ref = next(it), next(it)
    kt_s = next(it)
    nh = A_HEADS

    @pl.when(pl.program_id(1) == 0)
    def _():
        for h in range(nh):
            col = slice(h * LANES, (h + 1) * LANES)
            kn = _pair_norm(k_ref[:, col].astype(f32), kg_ref[...])
            if not ctx:
                newk_ref[h, 0] = kn[:, :A_DIM]
                newk_ref[h, 1] = kn[:, A_DIM:]
                newv_ref[h] = v_ref[:, col].astype(f32)
            if rope:
                kn = _rope(kn, cosk_ref[...], sink_ref[...])
            kt_s[col, :] = kn.T.astype(bf16)

    lane = lax.broadcasted_iota(jnp.int32, (1, LANES), 1)
    lam = lam_ref[0]
    for h in range(nh):
        col = slice(h * LANES, (h + 1) * LANES)
        q = _pair_norm(q_ref[:, col].astype(f32), qg_ref[...])
        if rope:
            q = _rope(q, cosq_ref[...], sinq_ref[...])
        q = q * (A_DIM ** -0.5 * math.log2(math.e))
        qs = [jnp.where(lane < A_DIM, q, 0.0).astype(bf16), jnp.where(lane >= A_DIM, q, 0.0).astype(bf16)]
        kt = kt_s[col, :]
        vb = v_ref[:, col]
        if ctx:
            kctb = kct_ref[h].astype(bf16)
            vcb = vc_ref[h].astype(bf16)
        outs = []
        for i in range(2):
            sn = _dot(qs[i], kt)
            mx = jnp.max(sn, axis=-1, keepdims=True)
            if ctx:
                sc = _dot(qs[i], kctb)
                mx = jnp.maximum(mx, jnp.max(sc, axis=-1, keepdims=True))
            en = jnp.exp2(sn - mx)
            den = jnp.sum(en, axis=-1, keepdims=True)
            o = _dot(en.astype(bf16), vb)
            if ctx:
                ec = jnp.exp2(sc - mx)
                den = den + jnp.sum(ec, axis=-1, keepdims=True)
                o = o + _dot(ec.astype(bf16), vcb)
            outs.append(o * (1.0 / den))
        o = outs[0] - lam * outs[1]
        ha_ref[:, col] = (_rms(o) * ag_ref[...] * out_scale).astype(ha_ref.dtype)


def _attn(z, lam, qg2, kg2, a_norm_g, *, batch, t, row_block0, rope_tabs=None, ctx_kv=None, out_scale):
    nh = A_HEADS
    w = A_W
    tq = min(t, 256)
    nq = t // tq
    rope = rope_tabs is not None
    ctx = ctx_kv is not None
    qblk = 4 * M_W // w
    vec = pl.BlockSpec((1, LANES), lambda b, i: (0, 0))
    in_specs = [
        pl.BlockSpec(memory_space=pltpu.SMEM),
        pl.BlockSpec((tq, w), lambda b, i: ((row_block0 + b) * nq + i, qblk)),
        pl.BlockSpec((t, w), lambda b, i: (row_block0 + b, qblk + 1)),
        pl.BlockSpec((t, w), lambda b, i: (row_block0 + b, qblk + 2)),
        vec, vec, vec,
    ]
    args = [lam, z, z, z, qg2, kg2, a_norm_g]
    if rope:
        cos, sin = rope_tabs
        in_specs += [pl.BlockSpec((tq, LANES), lambda b, i: (i, 0))] * 2
        in_specs += [pl.BlockSpec((t, LANES), lambda b, i: (0, 0))] * 2
        args += [cos, sin, cos, sin]
    if ctx:
        kct, vc = ctx_kv
        in_specs += [pl.BlockSpec((None,) + kct.shape[1:], lambda b, i: (b, 0, 0, 0)),
                     pl.BlockSpec((None,) + vc.shape[1:], lambda b, i: (b, 0, 0, 0))]
        args += [kct, vc]
    out_shape = [jax.ShapeDtypeStruct((batch * t, w), bf16)]
    out_specs = [pl.BlockSpec((tq, w), lambda b, i: (b * nq + i, 0))]
    if not ctx:
        out_shape += [jax.ShapeDtypeStruct((batch, nh, 2, t, A_DIM), f32),
                      jax.ShapeDtypeStruct((batch, nh, t, A_VDIM), f32)]
        out_specs += [pl.BlockSpec((None, nh, 2, t, A_DIM), lambda b, i: (b, 0, 0, 0, 0)),
                      pl.BlockSpec((None, nh, t, A_VDIM), lambda b, i: (b, 0, 0, 0))]
    return pl.pallas_call(
        functools.partial(_attn_kernel, t=t, tq=tq, rope=rope, ctx=ctx, out_scale=out_scale),
        out_shape=tuple(out_shape),
        grid=(batch, nq),
        in_specs=in_specs,
        out_specs=tuple(out_specs),
        scratch_shapes=[pltpu.VMEM((w, t), bf16)],
        compiler_params=_cparams("arbitrary", "arbitrary"),
        name=f"diff_attn_t{t}",
    )(*args)


def _rope_tables(t):
    rows = t // GRID_W
    pos_row = np.repeat(np.arange(rows, dtype=np.float64), GRID_W)
    pos_col = (np.arange(rows * GRID_W) % GRID_W).astype(np.float64)
    nf = A_DIM // 4
    inv = ROPE_THETA ** (-np.arange(nf, dtype=np.float64) / nf)
    lane = np.arange(LANES)
    j = lane % (2 * nf)
    use_col = (lane % A_DIM) >= (A_DIM // 2)
    ang = np.where(use_col[None, :], pos_col[:, None], pos_row[:, None]) * inv[j % nf][None, :]
    sign = np.where(j < nf, -1.0, 1.0)[None, :]
    return jnp.asarray(np.cos(ang), f32), jnp.asarray(np.sin(ang) * sign, f32)


def _ffn_kernel(xp_ref, xs_ref, hmp_ref, hms_ref, hap_ref, has_ref, mod_ref, wo_ref, w1_ref, w3_ref, w2_ref, o_ref,
                *, chunks, npt):
    half = hmp_ref.shape[1]
    hm = _pick(hmp_ref, hms_ref, npt)
    ha = _pick(hap_ref, has_ref, npt)
    x = _pick(xp_ref, xs_ref, npt) + mod_ref[2:3, :] * (_dot(hm, wo_ref[:half, :]) + _dot(ha, wo_ref[half:, :]))
    h = _modulate(x, mod_ref, 3).astype(bf16)
    acc = jnp.zeros(x.shape, f32)
    for lo, hi in chunks:
        ab = _dot(h, jnp.concatenate([w1_ref[:, lo:hi], w3_ref[:, lo:hi]], axis=1))
        a = _silu(ab[:, :hi - lo]) * ab[:, hi - lo:]
        acc = acc + _dot(a.astype(bf16), w2_ref[lo:hi, :])
    o_ref[...] = x + mod_ref[5:6, :] * acc


def _ffn(xp, xs, hm_p, hm_s, ha_p, ha_s, mod, w_out, w1, w3, w2, n_prompt_rows, dec_seq, job=None):
    d = xp.shape[1]
    r = xp.shape[0] + xs.shape[0]
    tm = ROW_TILE
    npt = n_prompt_rows // tm
    grp = functools.partial(_group_of_tile, tm=tm, n_prompt_rows=n_prompt_rows, dec_seq=dec_seq)
    ff = w1.shape[1]
    cut = MXU_COLS * ((ff // MXU_COLS + 1) // 2) if ff % MXU_COLS == 0 else ff
    chunks = ((0, cut), (cut, ff)) if cut < ff else ((0, ff),)
    return _call_with_job(
        functools.partial(_ffn_kernel, chunks=chunks, npt=npt), job(r // tm) if job else None,
        out_shape=(jax.ShapeDtypeStruct((r, d), f32),),
        grid=(r // tm,),
        in_specs=[
            *_part_specs(tm, d, npt),
            *_part_specs(tm, hm_p.shape[1], npt),
            *_part_specs(tm, ha_p.shape[1], npt),
            pl.BlockSpec((None, 6, d), lambda i: (grp(i), 0, 0)),
            _const_spec(w_out.shape), _const_spec(w1.shape), _const_spec(w3.shape), _const_spec(w2.shape),
        ],
        out_specs=(pl.BlockSpec((tm, d), lambda i: (i, 0)),),
        args=(xp, xs, hm_p, hm_s, ha_p, ha_s, mod, w_out, w1, w3, w2),
        compiler_params=_cparams("arbitrary"),
        name="outproj_ffn",
    )


def _gelu_tanh(x):
    return 0.5 * x * (1.0 + jnp.tanh(math.sqrt(2.0 / math.pi) * (x + 0.044715 * (x * x * x))))


def _gmlp_kernel(x_ref, mod_ref, win_ref, bin_ref, lng_ref, lnb_ref, ws_ref, bs_ref, wout_ref, o_ref, us_s):
    x = x_ref[...]
    tm = x.shape[0]
    e = wout_ref.shape[0]
    ge = e // C_GROUPS
    h = _modulate(x, mod_ref, 0).astype(bf16)
    u = _gelu_tanh(_dot(h, win_ref[:, :e]) + bin_ref[:, :e])
    v = _gelu_tanh(_dot(h, win_ref[:, e:]) + bin_ref[:, e:])
    mu = jnp.mean(v, axis=-1, keepdims=True)
    vc = v - mu
    var = jnp.mean(vc * vc, axis=-1, keepdims=True)
    vn = (vc * lax.rsqrt(var + EPS) * lng_ref[...] + lnb_ref[...]).astype(bf16)
    for n in range(tm // C_CHUNK):
        rows = slice(n * C_CHUNK, (n + 1) * C_CHUNK)
        for g in range(C_GROUPS):
            cols = slice(g * ge, (g + 1) * ge)
            s = _dot(ws_ref[g], vn[rows, cols]) + bs_ref[:, cols]
            us_s[rows, cols] = (u[rows, cols] * s).astype(bf16)
    o_ref[...] = x + mod_ref[2:3, :] * _dot(us_s[...], wout_ref[...])


def _gmlp(x, mod, w_in, b_in, ln_g, ln_b, ws, bs_full, w_out, n_prompt_rows, dec_seq, job=None):
    r, d = x.shape
    tm = ROW_TILE
    e = w_out.shape[0]
    grp = functools.partial(_group_of_tile, tm=tm, n_prompt_rows=n_prompt_rows, dec_seq=dec_seq)
    return _call_with_job(
        _gmlp_kernel, job(r // tm) if job else None,
        out_shape=(jax.ShapeDtypeStruct((r, d), f32),),
        grid=(r // tm,),
        in_specs=[
            pl.BlockSpec((tm, d), lambda i: (i, 0)),
            pl.BlockSpec((None, 6, d), lambda i: (grp(i), 0, 0)),
            _const_spec(w_in.shape), _const_spec(b_in.shape), _const_spec(ln_g.shape), _const_spec(ln_b.shape),
            _const_spec(ws.shape), _const_spec(bs_full.shape), _const_spec(w_out.shape),
        ],
        out_specs=(pl.BlockSpec((tm, d), lambda i: (i, 0)),),
        args=(x, mod, w_in, b_in, ln_g, ln_b, ws, bs_full, w_out),
        scratch_shapes=[pltpu.VMEM((tm, e), bf16)],
        compiler_params=_cparams("arbitrary"),
        name="gmlp",
    )


def _router_kernel(x_ref, mod_ref, wr_hi_ref, wr_lo_ref, h_ref, meta_ref, counts_ref):
    hf = _modulate(x_ref[...], mod_ref, 3)
    hb = hf.astype(bf16)
    h_ref[...] = hb
    h_lo = (hf - hb.astype(f32)).astype(bf16)
    logits = _dot(hb, wr_hi_ref[...]) + (_dot(hb, wr_lo_ref[...]) + _dot(h_lo, wr_hi_ref[...]))
    lane = lax.broadcasted_iota(jnp.int32, logits.shape, 1).astype(f32)
    logits = jnp.where(lane < N_EXPERTS, logits, -jnp.inf)
    m1 = jnp.max(logits, axis=-1, keepdims=True)
    i1 = jnp.min(jnp.where(logits == m1, lane, float(LANES)), axis=-1, keepdims=True)
    rest = jnp.where(lane == i1, -jnp.inf, logits)
    m2 = jnp.max(rest, axis=-1, keepdims=True)
    i2 = jnp.min(jnp.where(rest == m2, lane, float(LANES)), axis=-1, keepdims=True)
    e2 = jnp.exp(m2 - m1)
    w1 = 1.0 / (1.0 + e2)
    w2 = e2 * w1

    tm = logits.shape[0]
    cnt = jnp.where(lane == i1, 1.0, jnp.where(lane == i2, 1.0, 0.0))
    rr = lax.broadcasted_iota(jnp.int32, (tm, tm), 0)
    cc = lax.broadcasted_iota(jnp.int32, (tm, tm), 1)
    before = jnp.where(rr > cc, 1.0, 0.0).astype(bf16)
    rank = _dot(before, cnt.astype(bf16))
    counts = jnp.sum(cnt, axis=0, keepdims=True)
    padded = jnp.floor((counts + (PIECE_ALIGN - 1)) * (1.0 / PIECE_ALIGN)) * PIECE_ALIGN
    lane1 = lane[0:1, :]
    piece_off = jnp.zeros((1, LANES), f32)
    off = jnp.zeros((1, 1), f32)
    for e in range(N_EXPERTS):
        piece_off = jnp.where(lane1 == e, off, piece_off)
        off = off + padded[:, e:e + 1]
    local = piece_off + rank
    pos1 = jnp.sum(jnp.where(lane == i1, local, 0.0), axis=-1, keepdims=True)
    pos2 = jnp.sum(jnp.where(lane == i2, local, 0.0), axis=-1, keepdims=True)
    meta_ref[...] = jnp.where(lane == 0, pos1, jnp.where(lane == 1, pos2, jnp.where(lane == 2, w1,
                              jnp.where(lane == 3, w2, 0.0))))
    counts_ref[...] = counts


def _router(x, mod, wr_hi, wr_lo, n_prompt_rows, dec_seq):
    r, d = x.shape
    tm = MOE_SORT_BLOCK
    grp = functools.partial(_group_of_tile, tm=tm, n_prompt_rows=n_prompt_rows, dec_seq=dec_seq)
    return pl.pallas_call(
        _router_kernel,
        out_shape=(jax.ShapeDtypeStruct((r, d), bf16), jax.ShapeDtypeStruct((r, LANES), f32),
                   jax.ShapeDtypeStruct((r // tm, 1, LANES), f32)),
        grid=(r // tm,),
        in_specs=[
            pl.BlockSpec((tm, d), lambda i: (i, 0)),
            pl.BlockSpec((None, 6, d), lambda i: (grp(i), 0, 0)),
            _const_spec(wr_hi.shape), _const_spec(wr_lo.shape),
        ],
        out_specs=(pl.BlockSpec((tm, d), lambda i: (i, 0)), pl.BlockSpec((tm, LANES), lambda i: (i, 0)),
                   pl.BlockSpec((None, 1, LANES), lambda i: (i, 0, 0))),
        compiler_params=_cparams("arbitrary"),
        name="router",
    )(x, mod, wr_hi, wr_lo)


def _moe_kernel(n16_ref, loc_ref, dst_ref, seg_off_ref, seg_len_ref, csel_ref, cnum_ref,
                h_ref, meta_ref, x_ref, mod_ref, w1_ref, w3_ref, w2_ref, op_ref, os_ref,
                hs_s, ys_s, loc_s, *, nsub, ne, nf, chunks, nsb_prompt):
    sb = pl.program_id(0)
    p = pl.program_id(1)
    n_exp = ne * nf
    loc_rows = loc_s.shape[0]
    pa = PIECE_ALIGN

    def one_hot_cols(meta, v1, v2):
        lane = lax.broadcasted_iota(jnp.int32, (1, loc_rows), 1).astype(f32)
        return jnp.where(lane == meta[:, 0:1], v1, jnp.where(lane == meta[:, 1:2], v2, 0.0)).astype(bf16)

    def copy_pieces(blk, to_sorted):
        for e in range(ne):
            n = n16_ref[blk * ne + e]
            src = loc_ref[blk * ne + e]
            dst = dst_ref[blk * ne + e]

            def cp(i, _, src=src, dst=dst):
                a = pl.ds(pl.multiple_of(src + pa * i, pa), pa)
                b = pl.ds(pl.multiple_of(dst + pa * i, pa), pa)
                if to_sorted:
                    hs_s[b, :] = loc_s[a, :]
                else:
                    loc_s[a, :] = ys_s[b, :].astype(bf16)
                return 0

            lax.fori_loop(0, n, cp, 0)

    @pl.when(p < nsub)
    def _():
        @pl.when(p == 0)
        def _():
            hs_s[...] = jnp.zeros_like(hs_s)
            ys_s[...] = jnp.zeros_like(ys_s)

        pt = one_hot_cols(meta_ref[...], 1.0, 1.0)
        loc_s[...] = _dot_tn(pt, h_ref[...]).astype(bf16)
        copy_pieces(sb * nsub + p, True)

    @pl.when((p >= nsub) & (p < nsub + n_exp))
    def _():
        e = lax.div(p - nsub, jnp.int32(nf))
        start = seg_off_ref[sb * ne + e]
        ln = seg_len_ref[sb * ne + e]
        which = csel_ref[sb * ne + e]
        count = cnum_ref[sb * ne + e]

        def chunk(lo, size):
            r0 = jnp.minimum(lo, hs_s.shape[0] - size)
            rows = pl.ds(pl.multiple_of(r0, pa), size)
            xc = hs_s[rows, :]
            ab = _dot(xc, jnp.concatenate([w1_ref[...], w3_ref[...]], axis=1))
            tf = w1_ref.shape[1]
            a = _silu(ab[:, :tf]) * ab[:, tf:]
            y = _dot(a.astype(bf16), w2_ref[...])
            ri = r0 + lax.broadcasted_iota(jnp.int32, (size, 1), 0)
            ys_s[rows, :] += jnp.where(ri >= lo, jnp.where(ri < start + ln, y, 0.0), 0.0)

        for k, size in enumerate(chunks):
            @pl.when(which == k)
            def _(size=size):
                def body(j, _):
                    chunk(start + j * size, size)
                    return 0

                lax.fori_loop(0, count, body, 0)

    @pl.when(p >= nsub + n_exp)
    def _():
        copy_pieces(sb * nsub + (p - nsub - n_exp), False)
        meta = meta_ref[...]
        a = one_hot_cols(meta, meta[:, 2:3], meta[:, 3:4])
        y = x_ref[...] + mod_ref[5:6, :] * _dot(a, loc_s[...])

        @pl.when(sb < nsb_prompt)
        def _():
            op_ref[...] = y

        @pl.when(sb >= nsb_prompt)
        def _():
            os_ref[...] = y


def _moe(x, h, meta, counts, mod, w1, w3, w2, n_prompt_rows, dec_seq):
    r, d = x.shape
    ne, ff, _ = w2.shape
    assert w1.shape == (ne, ff // MOE_FF_TILE, d, MOE_FF_TILE)
    t_super, tb, chunks, tf, pa = MOE_SUPER_BLOCK, MOE_SORT_BLOCK, MOE_CHUNKS, MOE_FF_TILE, PIECE_ALIGN
    ch = max(chunks)
    nsub = t_super // tb
    nsb = r // t_super
    npb = n_prompt_rows // tb
    nf = ff // tf
    n_exp = ne * nf
    loc_rows = 2 * tb + LANES
    assert loc_rows >= 2 * tb + ne * (pa - 1)
    max_rows = 2 * t_super + nsub * ne * (pa - 1)
    sort_rows = LANES * (-(-max_rows // LANES))
    assert all(size % pa == 0 and size <= sort_rows for size in chunks)

    cnt = counts[:, 0, :ne].astype(jnp.int32)
    n16 = (cnt + (pa - 1)) // pa
    loc = pa * (jnp.cumsum(n16, axis=1) - n16)
    n16_sb = n16.reshape(nsb, nsub, ne)
    seg_len = pa * jnp.sum(n16_sb, axis=1)
    seg_off = jnp.cumsum(seg_len, axis=1) - seg_len
    dst = seg_off[:, None, :] + pa * (jnp.cumsum(n16_sb, axis=1) - n16_sb)
    cnum = (seg_len + ch - 1) // ch
    need = pa * ((seg_len // pa + jnp.maximum(cnum, 1) - 1) // jnp.maximum(cnum, 1))
    csel = sum((need > size).astype(jnp.int32) for size in chunks[:-1])
    scalars = [a.reshape(-1).astype(jnp.int32) for a in (n16, loc, dst, seg_off, seg_len, csel, cnum)]

    grp = functools.partial(_group_of_tile, tm=t_super, n_prompt_rows=n_prompt_rows, dec_seq=dec_seq)

    def exp_step(p):
        return jnp.clip(p - nsub, 0, n_exp - 1)

    def tok_blk(sb, s):
        return sb * nsub + jnp.clip(s, 0, nsub - 1)

    grid_spec = pltpu.PrefetchScalarGridSpec(
        num_scalar_prefetch=len(scalars),
        grid=(nsb, nsub + n_exp + nsub),
        in_specs=[
            pl.BlockSpec((tb, d), lambda sb, p, *_: (tok_blk(sb, p), 0)),
            pl.BlockSpec((tb, LANES), lambda sb, p, *_: (sb * nsub + jnp.where(p < nsub, p, jnp.clip(p - nsub - n_exp, 0, nsub - 1)), 0)),
            pl.BlockSpec((tb, d), lambda sb, p, *_: (tok_blk(sb, p - nsub - n_exp), 0)),
            pl.BlockSpec((None, 6, d), lambda sb, p, *_: (grp(sb), 0, 0)),
            pl.BlockSpec((None, None, d, tf), lambda sb, p, *_: (exp_step(p) // nf, exp_step(p) % nf, 0, 0)),
            pl.BlockSpec((None, None, d, tf), lambda sb, p, *_: (exp_step(p) // nf, exp_step(p) % nf, 0, 0)),
            pl.BlockSpec((None, tf, d), lambda sb, p, *_: (exp_step(p) // nf, exp_step(p) % nf, 0)),
        ],
        out_specs=(
            pl.BlockSpec((tb, d), lambda sb, p, *_: (jnp.minimum(tok_blk(sb, p - nsub - n_exp), npb - 1), 0)),
            pl.BlockSpec((tb, d), lambda sb, p, *_: (jnp.maximum(tok_blk(sb, p - nsub - n_exp) - npb, 0), 0)),
        ),
        scratch_shapes=[
            pltpu.VMEM((sort_rows, d), bf16),
            pltpu.VMEM((sort_rows, d), f32),
            pltpu.VMEM((loc_rows, d), bf16),
        ],
    )
    return pl.pallas_call(
        functools.partial(_moe_kernel, nsub=nsub, ne=ne, nf=nf, chunks=chunks,
                          nsb_prompt=n_prompt_rows // t_super),
        out_shape=(jax.ShapeDtypeStruct((n_prompt_rows, d), f32), jax.ShapeDtypeStruct((r - n_prompt_rows, d), f32)),
        grid_spec=grid_spec,
        compiler_params=_cparams("arbitrary", "arbitrary"),
        name="moe_sparse",
    )(*scalars, h, meta, x, mod, w1, w3, w2)


def kernel(x_prompt, x_sample, c, cache_dattn_k, cache_dattn_v, state_mlstm_c, state_mlstm_n, state_mlstm_m,
           c_ctx, w_ada, b_ada, w_in_ab, conv_w, conv_b, gate_b, qn_g, kn_g, lam_q1, lam_k1, lam_q2, lam_k2,
           m_norm_g, a_norm_g, w_out_ab, ff_w1, ff_w3, ff_w2, w_in_c, b_in_c, c_ln_g, c_ln_b, c_ws, c_bs,
           w_out_c, w_router, ex_w1, ex_w3, ex_w2):
    bp, seq, d = x_prompt.shape
    bs, dec_seq, _ = x_sample.shape
    depth = w_ada.shape[0]
    n_prompt_rows = bp * seq
    assert n_prompt_rows % dec_seq == 0 and seq % M_CHUNK == 0 and dec_seq % MOE_SUPER_BLOCK == 0
    nh = M_HEADS

    mods = _ada_table(jnp.concatenate([c_ctx[None], c], axis=0), w_ada, b_ada)
    x = (x_prompt.reshape(n_prompt_rows, d), x_sample.reshape(bs * dec_seq, d))
    rows = (n_prompt_rows, dec_seq)

    def joined(v):
        return jnp.concatenate(v, axis=0) if isinstance(v, tuple) else v

    def parts(v):
        return v if isinstance(v, tuple) else (v[:n_prompt_rows], v[n_prompt_rows:])

    new_k, new_v, new_c, new_n, new_m = [], [], [], [], []
    for l in range(depth):
        j = l // 2
        mod = mods[l]
        if l % 2 == 0:
            lam_init = 0.8 - 0.6 * math.exp(-0.3 * l)
            lam = (jnp.exp(jnp.sum((lam_q1[j] * lam_k1[j]).astype(f32)))
                   - jnp.exp(jnp.sum((lam_q2[j] * lam_k2[j]).astype(f32))) + lam_init).reshape(1)
            o3 = 4 * M_W
            o4 = o3 + 4 * nh
            w = w_in_ab[j]
            w_main = jnp.concatenate([w[:, :o3], w[:, o4:]], axis=1).astype(bf16)
            w_gate = jnp.zeros((d, LANES), f32).at[:, :4 * nh].set(w[:, o3:o4]).astype(bf16)
            x = parts(x)
            nxt = j if l + 1 < depth else None
            z, gates, *ex2 = _inproj(*x, mod, w_main, w_gate, *rows,
                                     job=None if nxt is None else functools.partial(_cast_job_rows, ex_w2[nxt]))
            gb = jnp.zeros((1, LANES), f32).at[0, :4 * nh].set(gate_b[j])
            mng = m_norm_g[j].reshape(1, M_W)
            cb = conv_b[j].reshape(1, 2 * M_W)
            hm_p, c_f, n_f, m_f = _mlstm(z, gates, gb, conv_w[j], cb, mng, None, batch=bp, t=seq, row_block0=0)
            state0 = (state_mlstm_c[:, j], state_mlstm_n[:, j].reshape(bs, 2, nh, 1, M_DIM),
                      state_mlstm_m[:, j].reshape(bs, 2, nh, 1, 1))
            hm_s, _, _, _ = _mlstm(z, gates, gb, conv_w[j], cb, mng, state0,
                                   batch=bs, t=dec_seq, row_block0=n_prompt_rows // dec_seq)
            new_c.append(c_f)
            new_n.append(n_f.reshape(bp, 2, nh, M_DIM))
            new_m.append(m_f.reshape(bp, 2, nh))

            qg2 = jnp.tile(qn_g[j], 2).reshape(1, LANES)
            kg2 = jnp.tile(kn_g[j], 2).reshape(1, LANES)
            ag = a_norm_g[j].reshape(1, LANES)
            ha_p, k_ctx, v_ctx = _attn(z, lam, qg2, kg2, ag, batch=bp, t=seq, row_block0=0,
                                       out_scale=1.0 - lam_init)
            kct = cache_dattn_k[:, j].transpose(0, 1, 2, 4, 3).reshape(bs, A_HEADS, LANES, -1)
            (ha_s,) = _attn(z, lam, qg2, kg2, ag, batch=bs, t=dec_seq, row_block0=n_prompt_rows // dec_seq,
                            rope_tabs=_rope_tables(dec_seq), ctx_kv=(kct, cache_dattn_v[:, j]),
                            out_scale=1.0 - lam_init)
            new_k.append(k_ctx)
            new_v.append(v_ctx)
            x, *ex1 = _ffn(*x, hm_p, hm_s, ha_p, ha_s, mod, w_out_ab[j].astype(bf16), ff_w1[j].astype(bf16),
                           ff_w3[j].astype(bf16), ff_w2[j].astype(bf16), *rows,
                           job=None if nxt is None else functools.partial(_cast_job_tiled, ex_w1[nxt], tf=MOE_FF_TILE))
        else:
            e = w_out_c.shape[1]
            bs_full = jnp.repeat(c_bs[j].T, e // C_GROUPS, axis=1)
            x, ex3 = _gmlp(joined(x), mod, w_in_c[j].astype(bf16), b_in_c[j].reshape(1, -1),
                           c_ln_g[j].reshape(1, -1), c_ln_b[j].reshape(1, -1), c_ws[j].astype(bf16), bs_full,
                           w_out_c[j].astype(bf16), *rows,
                           job=functools.partial(_cast_job_tiled, ex_w3[j], tf=MOE_FF_TILE))
            wr = jnp.zeros((d, LANES), f32).at[:, :N_EXPERTS].set(w_router[j])
            wr_hi = wr.astype(bf16)
            wr_lo = (wr - wr_hi.astype(f32)).astype(bf16)
            h, meta, counts = _router(x, mod, wr_hi, wr_lo, *rows)
            x = _moe(x, h, meta, counts, mod, ex1[0], ex3, ex2[0].reshape(ex_w2[j].shape), *rows)

    y_prompt, y_sample = parts(x)
    y_prompt = y_prompt.reshape(bp, seq, d)
    y_sample = y_sample.reshape(bs, dec_seq, d)
    return (y_prompt, y_sample, jnp.stack(new_k, axis=1), jnp.stack(new_v, axis=1),
            jnp.stack(new_c, axis=1), jnp.stack(new_n, axis=1), jnp.stack(new_m, axis=1))
```

```python
import functools
import math
from typing import Callable, NamedTuple

import jax
import jax.numpy as jnp
import numpy as np
from jax import lax
from jax.experimental import pallas as pl
from jax.experimental.pallas import tpu as pltpu

f32 = jnp.float32
bf16 = jnp.bfloat16

D_MODEL = 1024
M_HEADS = 4
M_DIM = 128
M_W = M_HEADS * M_DIM
M_CHUNK = 128
A_HEADS = 4
A_VDIM = 128
A_DIM = 64
A_W = A_HEADS * A_VDIM
GRID_W = 64
ROPE_THETA = 10000.0
C_CHUNK = 128
C_GROUPS = 4
N_EXPERTS = 8
EPS = 1e-6

LANES = 128
MXU_COLS = 256
ROW_TILE = 512
PIECE_ALIGN = 16
MOE_SUPER_BLOCK = 2048
MOE_SORT_BLOCK = 512
MOE_CHUNKS = (128, 192, 256, 320, 384, 448, 512, 576, 640)
MOE_FF_TILE = 896
VMEM_LIMIT = 60 * 1024 * 1024


def _cparams(*sem):
    return pltpu.CompilerParams(dimension_semantics=tuple(sem), vmem_limit_bytes=VMEM_LIMIT)


def _const_spec(shape):
    nd = len(shape)
    return pl.BlockSpec(shape, lambda *_: (0,) * nd, pipeline_mode=pl.Buffered(1))


def _sigmoid(x):
    return 1.0 / (1.0 + jnp.exp(-x))


def _silu(x):
    return x * _sigmoid(x)


def _log_sigmoid(x):
    return jnp.minimum(x, 0.0) - jnp.log(1.0 + jnp.exp(-jnp.abs(x)))


def _rms(x):
    return x * lax.rsqrt(jnp.mean(x * x, axis=-1, keepdims=True) + EPS)


def _modulate(x, mod_ref, first):
    shift = mod_ref[first:first + 1, :]
    scale = mod_ref[first + 1:first + 2, :]
    return _rms(x) * (1.0 + scale) + shift


def _dot(a, b):
    return jnp.dot(a, b, preferred_element_type=f32)


def _dot_tn(a, b):
    return lax.dot_general(a, b, (((0,), (0,)), ((), ())), preferred_element_type=f32)


def _split3(x):
    hi = x.astype(bf16)
    r1 = x - hi.astype(f32)
    mid = r1.astype(bf16)
    lo = (r1 - mid.astype(f32)).astype(bf16)
    return hi, mid, lo


def _group_of_tile(i, tm, n_prompt_rows, dec_seq):
    pt = n_prompt_rows // tm
    return jnp.where(i < pt, 0, 1 + (i - pt) // (dec_seq // tm))


def _ada_kernel(cv_ref, w_ref, b_ref, o_ref):
    a = _silu(cv_ref[...]).astype(bf16)
    o_ref[...] = _dot(a, w_ref[...].astype(bf16)) + b_ref[...]


def _ada_table(cv, w_ada, b_ada):
    depth, d, n = w_ada.shape
    g = cv.shape[0]
    gp = 8 * ((g + 7) // 8)
    cvp = jnp.zeros((gp, d), f32).at[:g].set(cv)
    tn = 1536
    out = pl.pallas_call(
        _ada_kernel,
        out_shape=jax.ShapeDtypeStruct((depth, gp, n), f32),
        grid=(depth, n // tn),
        in_specs=[
            pl.BlockSpec((gp, d), lambda l, j: (0, 0)),
            pl.BlockSpec((None, d, tn), lambda l, j: (l, 0, j)),
            pl.BlockSpec((None, 1, tn), lambda l, j: (l, 0, j)),
        ],
        out_specs=pl.BlockSpec((None, gp, tn), lambda l, j: (l, 0, j)),
        compiler_params=_cparams("arbitrary", "arbitrary"),
        name="ada_table",
    )(cvp, w_ada, b_ada.reshape(depth, 1, n))
    return out[:, :g].reshape(depth, g, 6, d)


def _part_specs(tm, width, npt):
    first = pl.BlockSpec((tm, width), lambda i: (jnp.minimum(i, npt - 1), 0))
    second = pl.BlockSpec((tm, width), lambda i: (jnp.maximum(i - npt, 0), 0))
    return first, second


def _pick(a_ref, b_ref, npt):
    return jnp.where(pl.program_id(0) < npt, a_ref[...], b_ref[...])


class _CastJob(NamedTuple):
    src: jax.Array
    in_spec: pl.BlockSpec
    out_spec: pl.BlockSpec
    out_shape: jax.ShapeDtypeStruct
    body: Callable


def _cast_job_tiled(src, steps, tf):
    ne, d, ff = src.shape
    bands = steps // ne
    rows = d // bands
    nf = ff // tf

    def body(i_ref, o_ref):
        for f in range(nf):
            o_ref[f] = i_ref[:, f * tf:(f + 1) * tf].astype(bf16)

    return _CastJob(src, pl.BlockSpec((None, rows, ff), lambda i: (i // bands, i % bands, 0)),
                    pl.BlockSpec((None, nf, rows, tf), lambda i: (i // bands, 0, i % bands, 0)),
                    jax.ShapeDtypeStruct((ne, nf, d, tf), bf16), body)


def _cast_job_rows(src, steps):
    ne, ff, d = src.shape
    rows = ne * ff // steps

    def body(i_ref, o_ref):
        o_ref[...] = i_ref[...].astype(bf16)

    return _CastJob(src.reshape(ne * ff, d), pl.BlockSpec((rows, d), lambda i: (i, 0)),
                    pl.BlockSpec((rows, d), lambda i: (i, 0)), jax.ShapeDtypeStruct((ne * ff, d), bf16), body)


def _call_with_job(kernel_fn, job, *, out_shape, in_specs, out_specs, args, **kw):
    if job is None:
        return pl.pallas_call(kernel_fn, out_shape=out_shape, in_specs=in_specs, out_specs=out_specs, **kw)(*args)
    n_in, n_out = len(in_specs), len(out_shape)

    def with_job(*refs):
        kernel_fn(*refs[:n_in], *refs[n_in + 1:n_in + 1 + n_out], *refs[n_in + 2 + n_out:])
        job.body(refs[n_in], refs[n_in + 1 + n_out])

    return pl.pallas_call(with_job, out_shape=(*out_shape, job.out_shape), in_specs=[*in_specs, job.in_spec],
                          out_specs=(*out_specs, job.out_spec), **kw)(*args, job.src)


def _inproj_kernel(xp_ref, xs_ref, mod_ref, w_ref, wg_ref, z_ref, g_ref, *, n_main, npt):
    h = _modulate(_pick(xp_ref, xs_ref, npt), mod_ref, 0).astype(bf16)
    step = 512
    for j in range(n_main // step):
        z_ref[:, j * step:(j + 1) * step] = _dot(h, w_ref[:, j * step:(j + 1) * step]).astype(bf16)
    g_ref[...] = _dot(h, wg_ref[...])


def _inproj(xp, xs, mod, w_main, w_gate, n_prompt_rows, dec_seq, job=None):
    d = xp.shape[1]
    r = xp.shape[0] + xs.shape[0]
    tm = ROW_TILE
    npt = n_prompt_rows // tm
    n_main = w_main.shape[1]
    grp = functools.partial(_group_of_tile, tm=tm, n_prompt_rows=n_prompt_rows, dec_seq=dec_seq)
    return _call_with_job(
        functools.partial(_inproj_kernel, n_main=n_main, npt=npt), job(r // tm) if job else None,
        out_shape=(jax.ShapeDtypeStruct((r, n_main), bf16), jax.ShapeDtypeStruct((r, LANES), f32)),
        grid=(r // tm,),
        in_specs=[
            *_part_specs(tm, d, npt),
            pl.BlockSpec((None, 6, d), lambda i: (grp(i), 0, 0)),
            _const_spec(w_main.shape),
            _const_spec(w_gate.shape),
        ],
        out_specs=(pl.BlockSpec((tm, n_main), lambda i: (i, 0)), pl.BlockSpec((tm, LANES), lambda i: (i, 0))),
        args=(xp, xs, mod, w_main, w_gate),
        compiler_params=_cparams("arbitrary"),
        name="inproj_ab",
    )


def _conv_silu_chunk(x_ref, c, nc, w_ref, b_ref):
    lc, pa = M_CHUNK, PIECE_ALIGN
    t = nc * lc
    x = x_ref[pl.ds(pl.multiple_of(c * lc, lc), lc), :].astype(f32)
    before = x_ref[pl.ds(pl.multiple_of(jnp.maximum(c * lc - pa, 0), pa), pa), :].astype(f32)[pa - 1:pa, :]
    after = x_ref[pl.ds(pl.multiple_of(jnp.minimum((c + 1) * lc, t - pa), pa), pa), :].astype(f32)[0:1, :]
    before = jnp.where(c > 0, before, 0.0)
    after = jnp.where(c < nc - 1, after, 0.0)
    row = lax.broadcasted_iota(jnp.int32, (lc, 1), 0)
    prev = jnp.where(row == 0, before, pltpu.roll(x, 1, 0))
    nxt = jnp.where(row == lc - 1, after, pltpu.roll(x, lc - 1, 0))
    y = b_ref[...] + prev * w_ref[0:1, :] + x * w_ref[1:2, :] + nxt * w_ref[2:3, :]
    return _silu(y)


def _mlstm_kernel(q_ref, k_ref, v_ref, o_ref, g_ref, gb_ref, cw_ref, cb_ref, ng_ref, *rest, t, has_state):
    if has_state:
        c0_ref, n0_ref, m0_ref, *rest = rest
    hm_ref, c_out, n_out, m_out, qt_s, kh_s, kl_s, vt_s, gs_s, hft_s, hbt_s, ct_s, sel_s = rest
    nc = t // M_CHUNK
    lc = M_CHUNK
    nh = M_HEADS
    w = M_W
    sel_row = lax.broadcasted_iota(jnp.int32, (3 * LANES, LANES), 0)
    for d in range(2):
        for h in range(nh):
            ci, ln = d * nh + h, 2 * nh * d + h
            ct_s[ci] = c0_ref[d, h].T if has_state else jnp.zeros((M_DIM, M_DIM), f32)
            sel_s[ci] = jnp.where(sel_row == ln, 1.0, jnp.where(sel_row == LANES + ln, 1.0, jnp.where(
                sel_row == 2 * LANES + ln, 1.0, 0.0))).astype(bf16)
    if has_state:
        n_out[...] = n0_ref[...]
        m_out[...] = m0_ref[...]
    else:
        n_out[...] = jnp.zeros_like(n_out)
        m_out[...] = jnp.zeros_like(m_out)

    lane = lax.broadcasted_iota(jnp.int32, (1, LANES), 1)
    is_lf = ((lane >= nh) & (lane < 2 * nh)) | ((lane >= 3 * nh) & (lane < 4 * nh))
    is_bw_lf = (lane >= 3 * nh) & (lane < 4 * nh)

    rr = lax.broadcasted_iota(jnp.int32, (lc, lc), 0)
    cc = lax.broadcasted_iota(jnp.int32, (lc, lc), 1)
    tri_incl = jnp.where(rr >= cc, 1.0, 0.0).astype(bf16)

    def prep_body(c, _):
        off = pl.multiple_of(c * lc, lc)
        g = g_ref[pl.ds(off, lc), :] + gb_ref[...]
        tile = jnp.where(is_lf, _log_sigmoid(g), g)
        hi, mid, lo = _split3(jnp.where(is_lf, tile, 0.0))
        cs = _dot(tri_incl, hi) + _dot(tri_incl, mid) + _dot(tri_incl, lo)
        total = jnp.broadcast_to(cs[lc - 1:lc, :], (lc, LANES))
        b = jnp.where(is_bw_lf, total - cs + tile, cs)
        lmb = tile - pltpu.roll(b, LANES - nh, 1)
        gv = pltpu.roll(total, LANES - nh, 1) + lmb
        low = jnp.where(is_lf, b, lmb)
        high = pltpu.roll(jnp.where(is_lf, total, gv), 4 * nh, 1)
        gs_s[pl.ds(off, lc), :] = jnp.where(lane < 4 * nh, low, jnp.where(lane < 8 * nh, high, 0.0))
        wide = pl.ds(pl.multiple_of(c * w, w), w)
        qt_s[wide, :] = _conv_silu_chunk(q_ref, c, nc, cw_ref[:, :w], cb_ref[:, :w]).T
        kf = _conv_silu_chunk(k_ref, c, nc, cw_ref[:, w:], cb_ref[:, w:]) * (M_DIM ** -0.5)
        kh = kf.astype(bf16)
        kh_s[pl.ds(off, lc), :] = kh
        kl_s[pl.ds(off, lc), :] = (kf - kh.astype(f32)).astype(bf16)
        vt_s[wide, :] = v_ref[pl.ds(off, lc), :].astype(f32).T.astype(bf16)
        return 0

    lax.fori_loop(0, nc, prep_body, 0)

    sub8 = lax.broadcasted_iota(jnp.int32, (8, LANES), 0)

    def two_rows(x):
        hi = x.astype(bf16).astype(f32)
        return jnp.where(sub8 == 0, hi, jnp.where(sub8 == 1, x - hi, 0.0)).astype(bf16)

    def issue(c, h, d, tile3, tile_t):
        off = pl.multiple_of(c * lc, lc)
        col = slice(h * M_DIM, (h + 1) * M_DIM)
        ci, ln = d * nh + h, 2 * nh * d + h
        head = pl.ds(pl.multiple_of(c * w + h * M_DIM, M_DIM), M_DIM)
        b_row = tile_t[ln + nh:ln + nh + 1, :]
        g_row = tile_t[ln + 4 * nh:ln + 4 * nh + 1, :]
        total = tile_t[ln + 5 * nh:ln + 5 * nh + 1, 0:1]
        ct, nm, mm = ct_s[ci], n_out[d, h], m_out[d, h]
        qt = qt_s[head, :]
        qtb = qt.astype(bf16)
        khb = kh_s[pl.ds(off, lc), col]
        vtb = vt_s[head, :]
        lmb = _dot(tile3, sel_s[ci])
        sraw = _dot(khb, qtb)
        qn = _dot(two_rows(nm), qtb)
        lhs = jnp.concatenate([ct.astype(bf16), vtb], axis=1)
        m_new = jnp.maximum(mm + total, jnp.max(g_row, axis=1, keepdims=True))
        decay = jnp.exp(mm + total - m_new)
        ew = jnp.exp(g_row - m_new)
        ct_s[ci] = decay * ct + _dot(vtb * ew.astype(bf16), khb)
        ew2 = two_rows(ew)
        nk_h = _dot(ew2, khb)
        nk_l = _dot(ew2, kl_s[pl.ds(off, lc), col])
        n_out[d, h] = decay * nm + (nk_h[0:1, :] + nk_h[1:2, :] + nk_l[0:1, :])
        m_out[d, h] = m_new
        return dict(d=d, head=head, b_row=b_row, mm=mm, qt=qt, lmb=lmb, sraw=sraw, qn=qn, lhs=lhs)

    def weigh(st):
        keep = (cc >= rr) if st["d"] == 0 else (rr >= cc)
        dmat = jnp.where(keep, st["lmb"] + st["b_row"], -jnp.inf)
        inter = st["mm"] + st["b_row"]
        mt = jnp.maximum(inter, jnp.max(dmat, axis=0, keepdims=True))
        w_inter = jnp.exp(inter - mt)
        s = st["sraw"] * jnp.exp(dmat - mt)
        qn = st["qn"]
        den = w_inter * (qn[0:1, :] + qn[1:2, :]) + jnp.sum(s, axis=0, keepdims=True)
        inv = 1.0 / jnp.maximum(jnp.abs(den), jnp.exp(-mt))
        return jnp.concatenate([st["qt"] * (w_inter * inv), s * inv], axis=0).astype(bf16)

    def body(i, _):
        states = []
        for d in range(2):
            c = i if d == 0 else nc - 1 - i
            tile = gs_s[pl.ds(pl.multiple_of(c * lc, lc), lc), :]
            tile_t = tile.T
            hi, mid, lo = _split3(tile)
            tile3 = jnp.concatenate([hi, mid, lo], axis=1)
            states += [issue(c, h, d, tile3, tile_t) for h in range(nh)]
        rhss = [weigh(st) for st in states]
        for st, rhs in zip(states, rhss):
            hct = _dot(st["lhs"], rhs)
            if st["d"] == 0:
                hft_s[st["head"], :] = hct
            else:
                hbt_s[st["head"], :] = hct
        return 0

    lax.fori_loop(0, nc, body, 0)

    def out_body(c, _):
        rows = pl.ds(pl.multiple_of(c * lc, lc), lc)
        wide = pl.ds(pl.multiple_of(c * w, w), w)
        hsum = (hft_s[wide, :] + hbt_s[wide, :]).T
        for h in range(nh):
            col = slice(h * M_DIM, (h + 1) * M_DIM)
            mo = _sigmoid(o_ref[rows, col].astype(f32))
            hm_ref[rows, col] = (_rms(hsum[:, col]) * ng_ref[:, col] * mo).astype(hm_ref.dtype)
        return 0

    lax.fori_loop(0, nc, out_body, 0)
    for d in range(2):
        for h in range(nh):
            c_out[d, h] = ct_s[d * nh + h].T


def _mlstm(z, gates, gate_b, conv_w, conv_b, m_norm_g, state0, *, batch, t, row_block0):
    nh = M_HEADS
    w = M_W
    mode = dict(pipeline_mode=pl.Buffered(1)) if t * w * 2 > (1 << 20) else {}
    seq = lambda colblk: pl.BlockSpec((t, w), lambda b: (row_block0 + b, colblk), **mode)
    state = lambda *tail: pl.BlockSpec((None, 2, nh) + tail, lambda b: (b,) + (0,) * (2 + len(tail)))
    state_specs = [state(M_DIM, M_DIM), state(1, M_DIM), state(1, 1)]
    has_state = state0 is not None
    return pl.pallas_call(
        functools.partial(_mlstm_kernel, t=t, has_state=has_state),
        out_shape=(
            jax.ShapeDtypeStruct((batch * t, w), bf16),
            jax.ShapeDtypeStruct((batch, 2, nh, M_DIM, M_DIM), f32),
            jax.ShapeDtypeStruct((batch, 2, nh, 1, M_DIM), f32),
            jax.ShapeDtypeStruct((batch, 2, nh, 1, 1), f32),
        ),
        grid=(batch,),
        in_specs=[
            seq(0), seq(1), seq(2), seq(3),
            pl.BlockSpec((t, LANES), lambda b: (row_block0 + b, 0)),
            _const_spec(gate_b.shape),
            _const_spec(conv_w.shape), _const_spec(conv_b.shape), _const_spec(m_norm_g.shape),
            *(state_specs if has_state else []),
        ],
        out_specs=(pl.BlockSpec((t, w), lambda b: (b, 0)), *state_specs),
        scratch_shapes=[
            pltpu.VMEM((t * nh, M_DIM), f32),
            pltpu.VMEM((t, w), bf16),
            pltpu.VMEM((t, w), bf16),
            pltpu.VMEM((t * nh, M_DIM), bf16),
            pltpu.VMEM((t, LANES), f32),
            pltpu.VMEM((t * nh, M_DIM), f32),
            pltpu.VMEM((t * nh, M_DIM), f32),
            pltpu.VMEM((2 * nh, M_DIM, M_DIM), f32),
            pltpu.VMEM((2 * nh, 3 * LANES, LANES), bf16),
        ],
        compiler_params=_cparams("arbitrary"),
        name=f"mlstm_t{t}",
    )(z, z, z, z, gates, gate_b, conv_w, conv_b, m_norm_g, *(state0 if has_state else ()))


def _pair_norm(x, gain, on_mxu):
    sq = x * x
    if on_mxu:
        same_half = jnp.where((lax.broadcasted_iota(jnp.int32, (LANES, LANES), 0) < A_DIM)
                              == (lax.broadcasted_iota(jnp.int32, (LANES, LANES), 1) < A_DIM), 1.0, 0.0).astype(bf16)
        hi = sq.astype(bf16)
        lo = (sq - hi.astype(f32)).astype(bf16)
        s = _dot(hi, same_half) + _dot(lo, same_half)
        return x * lax.rsqrt(s * (1.0 / A_DIM) + EPS) * gain
    lane = lax.broadcasted_iota(jnp.int32, (1, LANES), 1)
    first = lane < A_DIM
    s_all = jnp.sum(sq, axis=-1, keepdims=True)
    s0 = jnp.sum(jnp.where(first, sq, 0.0), axis=-1, keepdims=True)
    inv0 = lax.rsqrt(s0 * (1.0 / A_DIM) + EPS)
    inv1 = lax.rsqrt((s_all - s0) * (1.0 / A_DIM) + EPS)
    return x * jnp.where(first, inv0, inv1) * gain


def _rope(x, cos, sin_signed):
    lane = lax.broadcasted_iota(jnp.int32, (1, LANES), 1)
    nf = A_DIM // 4
    partner = jnp.where((lane % (2 * nf)) < nf, pltpu.roll(x, LANES - nf, 1), pltpu.roll(x, nf, 1))
    return x * cos + partner * sin_signed


def _attn_kernel(*refs, t, tq, rope, ctx, out_scale):
    it = iter(refs)
    lam_ref = next(it)
    q_ref, k_ref, v_ref = next(it), next(it), next(it)
    qg_ref, kg_ref, ag_ref = next(it), next(it), next(it)
    if rope:
        cosq_ref, sinq_ref, cosk_ref, sink_ref = next(it), next(it), next(it), next(it)
    if ctx:
        kct_ref, vc_ref = next(it), next(it)
    ha_ref = next(it)
    if not ctx:
        newk_ref, newv_ref = next(it), next(it)
    kt_s = next(it)
    nh = A_HEADS

    @pl.when(pl.program_id(1) == 0)
    def _():
        for h in range(nh):
            col = slice(h * LANES, (h + 1) * LANES)
            kn = _pair_norm(k_ref[:, col].astype(f32), kg_ref[...], True)
            if not ctx:
                newk_ref[h, 0] = kn[:, :A_DIM]
                newk_ref[h, 1] = kn[:, A_DIM:]
                newv_ref[h] = v_ref[:, col].astype(f32)
            if rope:
                kn = _rope(kn, cosk_ref[...], sink_ref[...])
            kt_s[col, :] = kn.T.astype(bf16)

    lane = lax.broadcasted_iota(jnp.int32, (1, LANES), 1)
    lam = lam_ref[0]
    for h in range(nh):
        col = slice(h * LANES, (h + 1) * LANES)
        q = _pair_norm(q_ref[:, col].astype(f32), qg_ref[...], not ctx)
        if rope:
            q = _rope(q, cosq_ref[...], sinq_ref[...])
        q = q * (A_DIM ** -0.5 * math.log2(math.e))
        qs = [jnp.where(lane < A_DIM, q, 0.0).astype(bf16), jnp.where(lane >= A_DIM, q, 0.0).astype(bf16)]
        kt = kt_s[col, :]
        vb = v_ref[:, col]
        if ctx:
            kctb = kct_ref[h].astype(bf16)
            vcb = vc_ref[h].astype(bf16)
        outs = []
        for i in range(2):
            sn = _dot(qs[i], kt)
            mx = jnp.max(sn, axis=-1, keepdims=True)
            if ctx:
                sc = _dot(qs[i], kctb)
                mx = jnp.maximum(mx, jnp.max(sc, axis=-1, keepdims=True))
            en = jnp.exp2(sn - mx)
            den = jnp.sum(en, axis=-1, keepdims=True)
            o = _dot(en.astype(bf16), vb)
            if ctx:
                ec = jnp.exp2(sc - mx)
                den = den + jnp.sum(ec, axis=-1, keepdims=True)
                o = o + _dot(ec.astype(bf16), vcb)
            outs.append(o * (1.0 / den))
        o = outs[0] - lam * outs[1]
        ha_ref[:, col] = (_rms(o) * ag_ref[...] * out_scale).astype(ha_ref.dtype)


def _attn(z, lam, qg2, kg2, a_norm_g, *, batch, t, row_block0, rope_tabs=None, ctx_kv=None, out_scale):
    nh = A_HEADS
    w = A_W
    tq = min(t, 256)
    nq = t // tq
    rope = rope_tabs is not None
    ctx = ctx_kv is not None
    qblk = 4 * M_W // w
    vec = pl.BlockSpec((1, LANES), lambda b, i: (0, 0))
    in_specs = [
        pl.BlockSpec(memory_space=pltpu.SMEM),
        pl.BlockSpec((tq, w), lambda b, i: ((row_block0 + b) * nq + i, qblk)),
        pl.BlockSpec((t, w), lambda b, i: (row_block0 + b, qblk + 1)),
        pl.BlockSpec((t, w), lambda b, i: (row_block0 + b, qblk + 2)),
        vec, vec, vec,
    ]
    args = [lam, z, z, z, qg2, kg2, a_norm_g]
    if rope:
        cos, sin = rope_tabs
        in_specs += [pl.BlockSpec((tq, LANES), lambda b, i: (i, 0))] * 2
        in_specs += [pl.BlockSpec((t, LANES), lambda b, i: (0, 0))] * 2
        args += [cos, sin, cos, sin]
    if ctx:
        kct, vc = ctx_kv
        in_specs += [pl.BlockSpec((None,) + kct.shape[1:], lambda b, i: (b, 0, 0, 0)),
                     pl.BlockSpec((None,) + vc.shape[1:], lambda b, i: (b, 0, 0, 0))]
        args += [kct, vc]
    out_shape = [jax.ShapeDtypeStruct((batch * t, w), bf16)]
    out_specs = [pl.BlockSpec((tq, w), lambda b, i: (b * nq + i, 0))]
    if not ctx:
        out_shape += [jax.ShapeDtypeStruct((batch, nh, 2, t, A_DIM), f32),
                      jax.ShapeDtypeStruct((batch, nh, t, A_VDIM), f32)]
        out_specs += [pl.BlockSpec((None, nh, 2, t, A_DIM), lambda b, i: (b, 0, 0, 0, 0)),
                      pl.BlockSpec((None, nh, t, A_VDIM), lambda b, i: (b, 0, 0, 0))]
    return pl.pallas_call(
        functools.partial(_attn_kernel, t=t, tq=tq, rope=rope, ctx=ctx, out_scale=out_scale),
        out_shape=tuple(out_shape),
        grid=(batch, nq),
        in_specs=in_specs,
        out_specs=tuple(out_specs),
        scratch_shapes=[pltpu.VMEM((w, t), bf16)],
        compiler_params=_cparams("arbitrary", "arbitrary"),
        name=f"diff_attn_t{t}",
    )(*args)


def _rope_tables(t):
    rows = t // GRID_W
    pos_row = np.repeat(np.arange(rows, dtype=np.float64), GRID_W)
    pos_col = (np.arange(rows * GRID_W) % GRID_W).astype(np.float64)
    nf = A_DIM // 4
    inv = ROPE_THETA ** (-np.arange(nf, dtype=np.float64) / nf)
    lane = np.arange(LANES)
    j = lane % (2 * nf)
    use_col = (lane % A_DIM) >= (A_DIM // 2)
    ang = np.where(use_col[None, :], pos_col[:, None], pos_row[:, None]) * inv[j % nf][None, :]
    sign = np.where(j < nf, -1.0, 1.0)[None, :]
    return jnp.asarray(np.cos(ang), f32), jnp.asarray(np.sin(ang) * sign, f32)


def _ffn_kernel(xp_ref, xs_ref, hmp_ref, hms_ref, hap_ref, has_ref, mod_ref, wo_ref, w1_ref, w3_ref, w2_ref, o_ref,
                *, chunks, npt):
    half = hmp_ref.shape[1]
    hm = _pick(hmp_ref, hms_ref, npt)
    ha = _pick(hap_ref, has_ref, npt)
    x = _pick(xp_ref, xs_ref, npt) + mod_ref[2:3, :] * (_dot(hm, wo_ref[:half, :]) + _dot(ha, wo_ref[half:, :]))
    h = _modulate(x, mod_ref, 3).astype(bf16)
    acc = jnp.zeros(x.shape, f32)
    for lo, hi in chunks:
        ab = _dot(h, jnp.concatenate([w1_ref[:, lo:hi], w3_ref[:, lo:hi]], axis=1))
        a = _silu(ab[:, :hi - lo]) * ab[:, hi - lo:]
        acc = acc + _dot(a.astype(bf16), w2_ref[lo:hi, :])
    o_ref[...] = x + mod_ref[5:6, :] * acc


def _ffn(xp, xs, hm_p, hm_s, ha_p, ha_s, mod, w_out, w1, w3, w2, n_prompt_rows, dec_seq, job=None):
    d = xp.shape[1]
    r = xp.shape[0] + xs.shape[0]
    tm = ROW_TILE
    npt = n_prompt_rows // tm
    grp = functools.partial(_group_of_tile, tm=tm, n_prompt_rows=n_prompt_rows, dec_seq=dec_seq)
    ff = w1.shape[1]
    cut = MXU_COLS * ((ff // MXU_COLS + 1) // 2) if ff % MXU_COLS == 0 else ff
    chunks = ((0, cut), (cut, ff)) if cut < ff else ((0, ff),)
    return _call_with_job(
        functools.partial(_ffn_kernel, chunks=chunks, npt=npt), job(r // tm) if job else None,
        out_shape=(jax.ShapeDtypeStruct((r, d), f32),),
        grid=(r // tm,),
        in_specs=[
            *_part_specs(tm, d, npt),
            *_part_specs(tm, hm_p.shape[1], npt),
            *_part_specs(tm, ha_p.shape[1], npt),
            pl.BlockSpec((None, 6, d), lambda i: (grp(i), 0, 0)),
            _const_spec(w_out.shape), _const_spec(w1.shape), _const_spec(w3.shape), _const_spec(w2.shape),
        ],
        out_specs=(pl.BlockSpec((tm, d), lambda i: (i, 0)),),
        args=(xp, xs, hm_p, hm_s, ha_p, ha_s, mod, w_out, w1, w3, w2),
        compiler_params=_cparams("arbitrary"),
        name="outproj_ffn",
    )


def _gelu_tanh(x):
    k1 = -2.0 * math.sqrt(2.0 / math.pi) * math.log2(math.e)
    k3 = k1 * 0.044715
    return x / (1.0 + jnp.exp2(x * (k1 + k3 * (x * x))))


def _gmlp_kernel(x_ref, mod_ref, win_ref, bin_ref, lng_ref, lnb_ref, ws_ref, bs_ref, wout_ref, o_ref, us_s):
    x = x_ref[...]
    tm = x.shape[0]
    e = wout_ref.shape[0]
    ge = e // C_GROUPS
    h = _modulate(x, mod_ref, 0).astype(bf16)
    u = _gelu_tanh(_dot(h, win_ref[:, :e]) + bin_ref[:, :e])
    v = _gelu_tanh(_dot(h, win_ref[:, e:]) + bin_ref[:, e:])
    mu = jnp.mean(v, axis=-1, keepdims=True)
    vc = v - mu
    var = jnp.mean(vc * vc, axis=-1, keepdims=True)
    vn = (vc * lax.rsqrt(var + EPS) * lng_ref[...] + lnb_ref[...]).astype(bf16)
    for n in range(tm // C_CHUNK):
        rows = slice(n * C_CHUNK, (n + 1) * C_CHUNK)
        for g in range(C_GROUPS):
            cols = slice(g * ge, (g + 1) * ge)
            s = _dot(ws_ref[g], vn[rows, cols]) + bs_ref[:, cols]
            us_s[rows, cols] = (u[rows, cols] * s).astype(bf16)
    o_ref[...] = x + mod_ref[2:3, :] * _dot(us_s[...], wout_ref[...])


def _gmlp(x, mod, w_in, b_in, ln_g, ln_b, ws, bs_full, w_out, n_prompt_rows, dec_seq, job=None):
    r, d = x.shape
    tm = ROW_TILE
    e = w_out.shape[0]
    grp = functools.partial(_group_of_tile, tm=tm, n_prompt_rows=n_prompt_rows, dec_seq=dec_seq)
    return _call_with_job(
        _gmlp_kernel, job(r // tm) if job else None,
        out_shape=(jax.ShapeDtypeStruct((r, d), f32),),
        grid=(r // tm,),
        in_specs=[
            pl.BlockSpec((tm, d), lambda i: (i, 0)),
            pl.BlockSpec((None, 6, d), lambda i: (grp(i), 0, 0)),
            _const_spec(w_in.shape), _const_spec(b_in.shape), _const_spec(ln_g.shape), _const_spec(ln_b.shape),
            _const_spec(ws.shape), _const_spec(bs_full.shape), _const_spec(w_out.shape),
        ],
        out_specs=(pl.BlockSpec((tm, d), lambda i: (i, 0)),),
        args=(x, mod, w_in, b_in, ln_g, ln_b, ws, bs_full, w_out),
        scratch_shapes=[pltpu.VMEM((tm, e), bf16)],
        compiler_params=_cparams("arbitrary"),
        name="gmlp",
    )


def _router_kernel(x_ref, mod_ref, wr_hi_ref, wr_lo_ref, h_ref, meta_ref, counts_ref):
    hf = _modulate(x_ref[...], mod_ref, 3)
    hb = hf.astype(bf16)
    h_ref[...] = hb
    h_lo = (hf - hb.astype(f32)).astype(bf16)
    logits = _dot(hb, wr_hi_ref[...]) + (_dot(hb, wr_lo_ref[...]) + _dot(h_lo, wr_hi_ref[...]))
    lane = lax.broadcasted_iota(jnp.int32, logits.shape, 1).astype(f32)
    logits = jnp.where(lane < N_EXPERTS, logits, -jnp.inf)
    m1 = jnp.max(logits, axis=-1, keepdims=True)
    i1 = jnp.min(jnp.where(logits == m1, lane, float(LANES)), axis=-1, keepdims=True)
    rest = jnp.where(lane == i1, -jnp.inf, logits)
    m2 = jnp.max(rest, axis=-1, keepdims=True)
    i2 = jnp.min(jnp.where(rest == m2, lane, float(LANES)), axis=-1, keepdims=True)
    e2 = jnp.exp(m2 - m1)
    w1 = 1.0 / (1.0 + e2)
    w2 = e2 * w1

    tm = logits.shape[0]
    cnt = jnp.where(lane == i1, 1.0, jnp.where(lane == i2, 1.0, 0.0))
    rr = lax.broadcasted_iota(jnp.int32, (tm, tm), 0)
    cc = lax.broadcasted_iota(jnp.int32, (tm, tm), 1)
    before = jnp.where(rr > cc, 1.0, 0.0).astype(bf16)
    rank = _dot(before, cnt.astype(bf16))
    counts = jnp.sum(cnt, axis=0, keepdims=True)
    padded = jnp.floor((counts + (PIECE_ALIGN - 1)) * (1.0 / PIECE_ALIGN)) * PIECE_ALIGN
    lane1 = lane[0:1, :]
    piece_off = jnp.zeros((1, LANES), f32)
    off = jnp.zeros((1, 1), f32)
    for e in range(N_EXPERTS):
        piece_off = jnp.where(lane1 == e, off, piece_off)
        off = off + padded[:, e:e + 1]
    local = piece_off + rank
    pos1 = jnp.sum(jnp.where(lane == i1, local, 0.0), axis=-1, keepdims=True)
    pos2 = jnp.sum(jnp.where(lane == i2, local, 0.0), axis=-1, keepdims=True)
    meta_ref[...] = jnp.where(lane == 0, pos1, jnp.where(lane == 1, pos2, jnp.where(lane == 2, w1,
                              jnp.where(lane == 3, w2, 0.0))))
    counts_ref[...] = counts


def _router(x, mod, wr_hi, wr_lo, n_prompt_rows, dec_seq):
    r, d = x.shape
    tm = MOE_SORT_BLOCK
    grp = functools.partial(_group_of_tile, tm=tm, n_prompt_rows=n_prompt_rows, dec_seq=dec_seq)
    return pl.pallas_call(
        _router_kernel,
        out_shape=(jax.ShapeDtypeStruct((r, d), bf16), jax.ShapeDtypeStruct((r, LANES), f32),
                   jax.ShapeDtypeStruct((r // tm, 1, LANES), f32)),
        grid=(r // tm,),
        in_specs=[
            pl.BlockSpec((tm, d), lambda i: (i, 0)),
            pl.BlockSpec((None, 6, d), lambda i: (grp(i), 0, 0)),
            _const_spec(wr_hi.shape), _const_spec(wr_lo.shape),
        ],
        out_specs=(pl.BlockSpec((tm, d), lambda i: (i, 0)), pl.BlockSpec((tm, LANES), lambda i: (i, 0)),
                   pl.BlockSpec((None, 1, LANES), lambda i: (i, 0, 0))),
        compiler_params=_cparams("arbitrary"),
        name="router",
    )(x, mod, wr_hi, wr_lo)


def _moe_kernel(n16_ref, loc_ref, dst_ref, seg_off_ref, seg_len_ref, csel_ref, cnum_ref,
                h_ref, meta_ref, x_ref, mod_ref, w1_ref, w3_ref, w2_ref, op_ref, os_ref,
                hs_s, ys_s, loc_s, *, nsub, ne, nf, chunks, nsb_prompt):
    sb = pl.program_id(0)
    p = pl.program_id(1)
    n_exp = ne * nf
    loc_rows = loc_s.shape[0]
    pa = PIECE_ALIGN

    def one_hot_cols(meta, v1, v2):
        lane = lax.broadcasted_iota(jnp.int32, (1, loc_rows), 1).astype(f32)
        return jnp.where(lane == meta[:, 0:1], v1, jnp.where(lane == meta[:, 1:2], v2, 0.0)).astype(bf16)

    def copy_pieces(blk, to_sorted):
        for e in range(ne):
            n = n16_ref[blk * ne + e]
            src = loc_ref[blk * ne + e]
            dst = dst_ref[blk * ne + e]

            def cp(i, _, src=src, dst=dst):
                a = pl.ds(pl.multiple_of(src + pa * i, pa), pa)
                b = pl.ds(pl.multiple_of(dst + pa * i, pa), pa)
                if to_sorted:
                    hs_s[b, :] = loc_s[a, :]
                else:
                    loc_s[a, :] = ys_s[b, :].astype(bf16)
                return 0

            lax.fori_loop(0, n, cp, 0)

    @pl.when(p < nsub)
    def _():
        @pl.when(p == 0)
        def _():
            hs_s[...] = jnp.zeros_like(hs_s)
            ys_s[...] = jnp.zeros_like(ys_s)

        pt = one_hot_cols(meta_ref[...], 1.0, 1.0)
        loc_s[...] = _dot_tn(pt, h_ref[...]).astype(bf16)
        copy_pieces(sb * nsub + p, True)

    @pl.when((p >= nsub) & (p < nsub + n_exp))
    def _():
        e = lax.div(p - nsub, jnp.int32(nf))
        start = seg_off_ref[sb * ne + e]
        ln = seg_len_ref[sb * ne + e]
        which = csel_ref[sb * ne + e]
        count = cnum_ref[sb * ne + e]

        def chunk(lo, size):
            r0 = jnp.minimum(lo, hs_s.shape[0] - size)
            rows = pl.ds(pl.multiple_of(r0, pa), size)
            xc = hs_s[rows, :]
            ab = _dot(xc, jnp.concatenate([w1_ref[...], w3_ref[...]], axis=1))
            tf = w1_ref.shape[1]
            a = _silu(ab[:, :tf]) * ab[:, tf:]
            y = _dot(a.astype(bf16), w2_ref[...])
            ri = r0 + lax.broadcasted_iota(jnp.int32, (size, 1), 0)
            ys_s[rows, :] += jnp.where(ri >= lo, jnp.where(ri < start + ln, y, 0.0), 0.0)

        for k, size in enumerate(chunks):
            @pl.when(which == k)
            def _(size=size):
                def body(j, _):
                    chunk(start + j * size, size)
                    return 0

                lax.fori_loop(0, count, body, 0)

    @pl.when(p >= nsub + n_exp)
    def _():
        copy_pieces(sb * nsub + (p - nsub - n_exp), False)
        meta = meta_ref[...]
        a = one_hot_cols(meta, meta[:, 2:3], meta[:, 3:4])
        y = x_ref[...] + mod_ref[5:6, :] * _dot(a, loc_s[...])

        @pl.when(sb < nsb_prompt)
        def _():
            op_ref[...] = y

        @pl.when(sb >= nsb_prompt)
        def _():
            os_ref[...] = y


def _moe(x, h, meta, counts, mod, w1, w3, w2, n_prompt_rows, dec_seq):
    r, d = x.shape
    ne, ff, _ = w2.shape
    assert w1.shape == (ne, ff // MOE_FF_TILE, d, MOE_FF_TILE)
    t_super, tb, chunks, tf, pa = MOE_SUPER_BLOCK, MOE_SORT_BLOCK, MOE_CHUNKS, MOE_FF_TILE, PIECE_ALIGN
    ch = max(chunks)
    nsub = t_super // tb
    nsb = r // t_super
    npb = n_prompt_rows // tb
    nf = ff // tf
    n_exp = ne * nf
    loc_rows = 2 * tb + LANES
    assert loc_rows >= 2 * tb + ne * (pa - 1)
    max_rows = 2 * t_super + nsub * ne * (pa - 1)
    sort_rows = LANES * (-(-max_rows // LANES))
    assert all(size % pa == 0 and size <= sort_rows for size in chunks)

    cnt = counts[:, 0, :ne].astype(jnp.int32)
    n16 = (cnt + (pa - 1)) // pa
    loc = pa * (jnp.cumsum(n16, axis=1) - n16)
    n16_sb = n16.reshape(nsb, nsub, ne)
    seg_len = pa * jnp.sum(n16_sb, axis=1)
    seg_off = jnp.cumsum(seg_len, axis=1) - seg_len
    dst = seg_off[:, None, :] + pa * (jnp.cumsum(n16_sb, axis=1) - n16_sb)
    cnum = (seg_len + ch - 1) // ch
    need = pa * ((seg_len // pa + jnp.maximum(cnum, 1) - 1) // jnp.maximum(cnum, 1))
    csel = sum((need > size).astype(jnp.int32) for size in chunks[:-1])
    scalars = [a.reshape(-1).astype(jnp.int32) for a in (n16, loc, dst, seg_off, seg_len, csel, cnum)]

    grp = functools.partial(_group_of_tile, tm=t_super, n_prompt_rows=n_prompt_rows, dec_seq=dec_seq)

    def exp_step(p):
        return jnp.clip(p - nsub, 0, n_exp - 1)

    def tok_blk(sb, s):
        return sb * nsub + jnp.clip(s, 0, nsub - 1)

    grid_spec = pltpu.PrefetchScalarGridSpec(
        num_scalar_prefetch=len(scalars),
        grid=(nsb, nsub + n_exp + nsub),
        in_specs=[
            pl.BlockSpec((tb, d), lambda sb, p, *_: (tok_blk(sb, p), 0)),
            pl.BlockSpec((tb, LANES), lambda sb, p, *_: (sb * nsub + jnp.where(p < nsub, p, jnp.clip(p - nsub - n_exp, 0, nsub - 1)), 0)),
            pl.BlockSpec((tb, d), lambda sb, p, *_: (tok_blk(sb, p - nsub - n_exp), 0)),
            pl.BlockSpec((None, 6, d), lambda sb, p, *_: (grp(sb), 0, 0)),
            pl.BlockSpec((None, None, d, tf), lambda sb, p, *_: (exp_step(p) // nf, exp_step(p) % nf, 0, 0)),
            pl.BlockSpec((None, None, d, tf), lambda sb, p, *_: (exp_step(p) // nf, exp_step(p) % nf, 0, 0)),
            pl.BlockSpec((None, tf, d), lambda sb, p, *_: (exp_step(p) // nf, exp_step(p) % nf, 0)),
        ],
        out_specs=(
            pl.BlockSpec((tb, d), lambda sb, p, *_: (jnp.minimum(tok_blk(sb, p - nsub - n_exp), npb - 1), 0)),
            pl.BlockSpec((tb, d), lambda sb, p, *_: (jnp.maximum(tok_blk(sb, p - nsub - n_exp) - npb, 0), 0)),
        ),
        scratch_shapes=[
            pltpu.VMEM((sort_rows, d), bf16),
            pltpu.VMEM((sort_rows, d), f32),
            pltpu.VMEM((loc_rows, d), bf16),
        ],
    )
    return pl.pallas_call(
        functools.partial(_moe_kernel, nsub=nsub, ne=ne, nf=nf, chunks=chunks,
                          nsb_prompt=n_prompt_rows // t_super),
        out_shape=(jax.ShapeDtypeStruct((n_prompt_rows, d), f32), jax.ShapeDtypeStruct((r - n_prompt_rows, d), f32)),
        grid_spec=grid_spec,
        compiler_params=_cparams("arbitrary", "arbitrary"),
        name="moe_sparse",
    )(*scalars, h, meta, x, mod, w1, w3, w2)


def kernel(x_prompt, x_sample, c, cache_dattn_k, cache_dattn_v, state_mlstm_c, state_mlstm_n, state_mlstm_m,
           c_ctx, w_ada, b_ada, w_in_ab, conv_w, conv_b, gate_b, qn_g, kn_g, lam_q1, lam_k1, lam_q2, lam_k2,
           m_norm_g, a_norm_g, w_out_ab, ff_w1, ff_w3, ff_w2, w_in_c, b_in_c, c_ln_g, c_ln_b, c_ws, c_bs,
           w_out_c, w_router, ex_w1, ex_w3, ex_w2):
    bp, seq, d = x_prompt.shape
    bs, dec_seq, _ = x_sample.shape
    depth = w_ada.shape[0]
    n_prompt_rows = bp * seq
    assert n_prompt_rows % dec_seq == 0 and seq % M_CHUNK == 0 and dec_seq % MOE_SUPER_BLOCK == 0
    nh = M_HEADS

    mods = _ada_table(jnp.concatenate([c_ctx[None], c], axis=0), w_ada, b_ada)
    x = (x_prompt.reshape(n_prompt_rows, d), x_sample.reshape(bs * dec_seq, d))
    rows = (n_prompt_rows, dec_seq)

    def joined(v):
        return jnp.concatenate(v, axis=0) if isinstance(v, tuple) else v

    def parts(v):
        return v if isinstance(v, tuple) else (v[:n_prompt_rows], v[n_prompt_rows:])

    new_k, new_v, new_c, new_n, new_m = [], [], [], [], []
    for l in range(depth):
        j = l // 2
        mod = mods[l]
        if l % 2 == 0:
            lam_init = 0.8 - 0.6 * math.exp(-0.3 * l)
            lam = (jnp.exp(jnp.sum((lam_q1[j] * lam_k1[j]).astype(f32)))
                   - jnp.exp(jnp.sum((lam_q2[j] * lam_k2[j]).astype(f32))) + lam_init).reshape(1)
            o3 = 4 * M_W
            o4 = o3 + 4 * nh
            w = w_in_ab[j]
            w_main = jnp.concatenate([w[:, :o3], w[:, o4:]], axis=1).astype(bf16)
            w_gate = jnp.zeros((d, LANES), f32).at[:, :4 * nh].set(w[:, o3:o4]).astype(bf16)
            x = parts(x)
            nxt = j if l + 1 < depth else None
            z, gates, *ex2 = _inproj(*x, mod, w_main, w_gate, *rows,
                                     job=None if nxt is None else functools.partial(_cast_job_rows, ex_w2[nxt]))
            gb = jnp.zeros((1, LANES), f32).at[0, :4 * nh].set(gate_b[j])
            mng = m_norm_g[j].reshape(1, M_W)
            cb = conv_b[j].reshape(1, 2 * M_W)
            hm_p, c_f, n_f, m_f = _mlstm(z, gates, gb, conv_w[j], cb, mng, None, batch=bp, t=seq, row_block0=0)
            state0 = (state_mlstm_c[:, j], state_mlstm_n[:, j].reshape(bs, 2, nh, 1, M_DIM),
                      state_mlstm_m[:, j].reshape(bs, 2, nh, 1, 1))
            hm_s, _, _, _ = _mlstm(z, gates, gb, conv_w[j], cb, mng, state0,
                                   batch=bs, t=dec_seq, row_block0=n_prompt_rows // dec_seq)
            new_c.append(c_f)
            new_n.append(n_f.reshape(bp, 2, nh, M_DIM))
            new_m.append(m_f.reshape(bp, 2, nh))

            qg2 = jnp.tile(qn_g[j], 2).reshape(1, LANES)
            kg2 = jnp.tile(kn_g[j], 2).reshape(1, LANES)
            ag = a_norm_g[j].reshape(1, LANES)
            ha_p, k_ctx, v_ctx = _attn(z, lam, qg2, kg2, ag, batch=bp, t=seq, row_block0=0,
                                       out_scale=1.0 - lam_init)
            kct = cache_dattn_k[:, j].transpose(0, 1, 2, 4, 3).reshape(bs, A_HEADS, LANES, -1)
            (ha_s,) = _attn(z, lam, qg2, kg2, ag, batch=bs, t=dec_seq, row_block0=n_prompt_rows // dec_seq,
                            rope_tabs=_rope_tables(dec_seq), ctx_kv=(kct, cache_dattn_v[:, j]),
                            out_scale=1.0 - lam_init)
            new_k.append(k_ctx)
            new_v.append(v_ctx)
            x, *ex1 = _ffn(*x, hm_p, hm_s, ha_p, ha_s, mod, w_out_ab[j].astype(bf16), ff_w1[j].astype(bf16),
                           ff_w3[j].astype(bf16), ff_w2[j].astype(bf16), *rows,
                           job=None if nxt is None else functools.partial(_cast_job_tiled, ex_w1[nxt], tf=MOE_FF_TILE))
        else:
            e = w_out_c.shape[1]
            bs_full = jnp.repeat(c_bs[j].T, e // C_GROUPS, axis=1)
            x, ex3 = _gmlp(joined(x), mod, w_in_c[j].astype(bf16), b_in_c[j].reshape(1, -1),
                           c_ln_g[j].reshape(1, -1), c_ln_b[j].reshape(1, -1), c_ws[j].astype(bf16), bs_full,
                           w_out_c[j].astype(bf16), *rows,
                           job=functools.partial(_cast_job_tiled, ex_w3[j], tf=MOE_FF_TILE))
            wr = jnp.zeros((d, LANES), f32).at[:, :N_EXPERTS].set(w_router[j])
            wr_hi = wr.astype(bf16)
            wr_lo = (wr - wr_hi.astype(f32)).astype(bf16)
            h, meta, counts = _router(x, mod, wr_hi, wr_lo, *rows)
            x = _moe(x, h, meta, counts, mod, ex1[0], ex3, ex2[0].reshape(ex_w2[j].shape), *rows)

    y_prompt, y_sample = parts(x)
    y_prompt = y_prompt.reshape(bp, seq, d)
    y_sample = y_sample.reshape(bs, dec_seq, d)
    return (y_prompt, y_sample, jnp.stack(new_k, axis=1), jnp.stack(new_v, axis=1),
            jnp.stack(new_c, axis=1), jnp.stack(new_n, axis=1), jnp.stack(new_m, axis=1))
```

```python
import functools
import math
from typing import Callable, NamedTuple

import jax
import jax.numpy as jnp
import numpy as np
from jax import lax
from jax.experimental import pallas as pl
from jax.experimental.pallas import tpu as pltpu

f32 = jnp.float32
bf16 = jnp.bfloat16

D_MODEL = 1024
M_HEADS = 4
M_DIM = 128
M_W = M_HEADS * M_DIM
M_CHUNK = 128
A_HEADS = 4
A_VDIM = 128
A_DIM = 64
A_W = A_HEADS * A_VDIM
GRID_W = 64
ROPE_THETA = 10000.0
C_CHUNK = 128
C_GROUPS = 4
N_EXPERTS = 8
EPS = 1e-6

LANES = 128
MXU_COLS = 256
ROW_TILE = 512
PIECE_ALIGN = 16
MOE_SUPER_BLOCK = 2048
MOE_SORT_BLOCK = 512
MOE_CHUNKS = (128, 192, 256, 320, 384, 448, 512, 576, 640)
MOE_FF_TILE = 896
VMEM_LIMIT = 60 * 1024 * 1024


def _cparams(*sem):
    return pltpu.CompilerParams(dimension_semantics=tuple(sem), vmem_limit_bytes=VMEM_LIMIT)


def _const_spec(shape):
    nd = len(shape)
    return pl.BlockSpec(shape, lambda *_: (0,) * nd, pipeline_mode=pl.Buffered(1))


def _sigmoid(x):
    return 1.0 / (1.0 + jnp.exp(-x))


def _silu(x):
    return x * _sigmoid(x)


def _log_sigmoid(x):
    return jnp.minimum(x, 0.0) - jnp.log(1.0 + jnp.exp(-jnp.abs(x)))


def _rms(x):
    return x * lax.rsqrt(jnp.mean(x * x, axis=-1, keepdims=True) + EPS)


def _modulate(x, mod_ref, first):
    shift = mod_ref[first:first + 1, :]
    scale = mod_ref[first + 1:first + 2, :]
    return _rms(x) * (1.0 + scale) + shift


def _dot(a, b):
    return jnp.dot(a, b, preferred_element_type=f32)


def _dot_tn(a, b):
    return lax.dot_general(a, b, (((0,), (0,)), ((), ())), preferred_element_type=f32)


def _split3(x):
    hi = x.astype(bf16)
    r1 = x - hi.astype(f32)
    mid = r1.astype(bf16)
    lo = (r1 - mid.astype(f32)).astype(bf16)
    return hi, mid, lo


def _group_of_tile(i, tm, n_prompt_rows, dec_seq):
    pt = n_prompt_rows // tm
    return jnp.where(i < pt, 0, 1 + (i - pt) // (dec_seq // tm))


def _ada_kernel(cv_ref, w_ref, b_ref, o_ref):
    a = _silu(cv_ref[...]).astype(bf16)
    o_ref[...] = _dot(a, w_ref[...].astype(bf16)) + b_ref[...]


def _ada_table(cv, w_ada, b_ada):
    depth, d, n = w_ada.shape
    g = cv.shape[0]
    gp = 8 * ((g + 7) // 8)
    cvp = jnp.zeros((gp, d), f32).at[:g].set(cv)
    tn = 1536
    out = pl.pallas_call(
        _ada_kernel,
        out_shape=jax.ShapeDtypeStruct((depth, gp, n), f32),
        grid=(depth, n // tn),
        in_specs=[
            pl.BlockSpec((gp, d), lambda l, j: (0, 0)),
            pl.BlockSpec((None, d, tn), lambda l, j: (l, 0, j)),
            pl.BlockSpec((None, 1, tn), lambda l, j: (l, 0, j)),
        ],
        out_specs=pl.BlockSpec((None, gp, tn), lambda l, j: (l, 0, j)),
        compiler_params=_cparams("arbitrary", "arbitrary"),
        name="ada_table",
    )(cvp, w_ada, b_ada.reshape(depth, 1, n))
    return out[:, :g].reshape(depth, g, 6, d)


def _part_specs(tm, width, npt):
    first = pl.BlockSpec((tm, width), lambda i: (jnp.minimum(i, npt - 1), 0))
    second = pl.BlockSpec((tm, width), lambda i: (jnp.maximum(i - npt, 0), 0))
    return first, second


def _pick(a_ref, b_ref, npt):
    return jnp.where(pl.program_id(0) < npt, a_ref[...], b_ref[...])


class _CastJob(NamedTuple):
    src: jax.Array
    in_spec: pl.BlockSpec
    out_spec: pl.BlockSpec
    out_shape: jax.ShapeDtypeStruct
    body: Callable


def _cast_job_tiled(src, steps, tf):
    ne, d, ff = src.shape
    bands = steps // ne
    rows = d // bands
    nf = ff // tf

    def body(i_ref, o_ref):
        for f in range(nf):
            o_ref[f] = i_ref[:, f * tf:(f + 1) * tf].astype(bf16)

    return _CastJob(src, pl.BlockSpec((None, rows, ff), lambda i: (i // bands, i % bands, 0)),
                    pl.BlockSpec((None, nf, rows, tf), lambda i: (i // bands, 0, i % bands, 0)),
                    jax.ShapeDtypeStruct((ne, nf, d, tf), bf16), body)


def _cast_job_rows(src, steps):
    ne, ff, d = src.shape
    rows = ne * ff // steps

    def body(i_ref, o_ref):
        o_ref[...] = i_ref[...].astype(bf16)

    return _CastJob(src.reshape(ne * ff, d), pl.BlockSpec((rows, d), lambda i: (i, 0)),
                    pl.BlockSpec((rows, d), lambda i: (i, 0)), jax.ShapeDtypeStruct((ne * ff, d), bf16), body)


def _call_with_job(kernel_fn, job, *, out_shape, in_specs, out_specs, args, **kw):
    if job is None:
        return pl.pallas_call(kernel_fn, out_shape=out_shape, in_specs=in_specs, out_specs=out_specs, **kw)(*args)
    n_in, n_out = len(in_specs), len(out_shape)

    def with_job(*refs):
        kernel_fn(*refs[:n_in], *refs[n_in + 1:n_in + 1 + n_out], *refs[n_in + 2 + n_out:])
        job.body(refs[n_in], refs[n_in + 1 + n_out])

    return pl.pallas_call(with_job, out_shape=(*out_shape, job.out_shape), in_specs=[*in_specs, job.in_spec],
                          out_specs=(*out_specs, job.out_spec), **kw)(*args, job.src)


def _inproj_kernel(xp_ref, xs_ref, mod_ref, w_ref, z_ref, g_ref, wm_s, wg_s, *, npt, gate_cols):
    o3, o4 = gate_cols
    n_main = wm_s.shape[1]

    @pl.when(pl.program_id(0) == 0)
    def _():
        wm_s[:, :o3] = w_ref[:, :o3].astype(bf16)
        wm_s[:, o3:] = w_ref[:, o4:].astype(bf16)
        wg_s[...] = jnp.zeros_like(wg_s)
        wg_s[:, :o4 - o3] = w_ref[:, o3:o4].astype(bf16)

    h = _modulate(_pick(xp_ref, xs_ref, npt), mod_ref, 0).astype(bf16)
    step = 512
    for j in range(n_main // step):
        z_ref[:, j * step:(j + 1) * step] = _dot(h, wm_s[:, j * step:(j + 1) * step]).astype(bf16)
    g_ref[...] = _dot(h, wg_s[...])


def _inproj(xp, xs, mod, w, gate_cols, n_prompt_rows, dec_seq, job=None):
    d = xp.shape[1]
    r = xp.shape[0] + xs.shape[0]
    tm = ROW_TILE
    npt = n_prompt_rows // tm
    n_main = w.shape[1] - (gate_cols[1] - gate_cols[0])
    grp = functools.partial(_group_of_tile, tm=tm, n_prompt_rows=n_prompt_rows, dec_seq=dec_seq)
    return _call_with_job(
        functools.partial(_inproj_kernel, npt=npt, gate_cols=gate_cols), job(r // tm) if job else None,
        out_shape=(jax.ShapeDtypeStruct((r, n_main), bf16), jax.ShapeDtypeStruct((r, LANES), f32)),
        grid=(r // tm,),
        in_specs=[
            *_part_specs(tm, d, npt),
            pl.BlockSpec((None, 6, d), lambda i: (grp(i), 0, 0)),
            _const_spec(w.shape),
        ],
        out_specs=(pl.BlockSpec((tm, n_main), lambda i: (i, 0)), pl.BlockSpec((tm, LANES), lambda i: (i, 0))),
        args=(xp, xs, mod, w),
        scratch_shapes=[pltpu.VMEM((d, n_main), bf16), pltpu.VMEM((d, LANES), bf16)],
        compiler_params=_cparams("arbitrary"),
        name="inproj_ab",
    )


def _conv_silu_chunk(x_ref, c, nc, w_ref, b_ref):
    lc, pa = M_CHUNK, PIECE_ALIGN
    t = nc * lc
    x = x_ref[pl.ds(pl.multiple_of(c * lc, lc), lc), :].astype(f32)
    before = x_ref[pl.ds(pl.multiple_of(jnp.maximum(c * lc - pa, 0), pa), pa), :].astype(f32)[pa - 1:pa, :]
    after = x_ref[pl.ds(pl.multiple_of(jnp.minimum((c + 1) * lc, t - pa), pa), pa), :].astype(f32)[0:1, :]
    before = jnp.where(c > 0, before, 0.0)
    after = jnp.where(c < nc - 1, after, 0.0)
    row = lax.broadcasted_iota(jnp.int32, (lc, 1), 0)
    prev = jnp.where(row == 0, before, pltpu.roll(x, 1, 0))
    nxt = jnp.where(row == lc - 1, after, pltpu.roll(x, lc - 1, 0))
    y = b_ref[...] + prev * w_ref[0:1, :] + x * w_ref[1:2, :] + nxt * w_ref[2:3, :]
    return _silu(y)


def _mlstm_kernel(q_ref, k_ref, v_ref, o_ref, g_ref, gb_ref, cw_ref, cb_ref, ng_ref, *rest, t, has_state):
    if has_state:
        c0_ref, n0_ref, m0_ref, *rest = rest
    hm_ref, c_out, n_out, m_out, qt_s, kh_s, kl_s, vt_s, gs_s, hft_s, hbt_s, ct_s, sel_s = rest
    nc = t // M_CHUNK
    lc = M_CHUNK
    nh = M_HEADS
    w = M_W
    sel_row = lax.broadcasted_iota(jnp.int32, (3 * LANES, LANES), 0)
    for d in range(2):
        for h in range(nh):
            ci, ln = d * nh + h, 2 * nh * d + h
            ct_s[ci] = c0_ref[d, h].T if has_state else jnp.zeros((M_DIM, M_DIM), f32)
            sel_s[ci] = jnp.where(sel_row == ln, 1.0, jnp.where(sel_row == LANES + ln, 1.0, jnp.where(
                sel_row == 2 * LANES + ln, 1.0, 0.0))).astype(bf16)
    if has_state:
        n_out[...] = n0_ref[...]
        m_out[...] = m0_ref[...]
    else:
        n_out[...] = jnp.zeros_like(n_out)
        m_out[...] = jnp.zeros_like(m_out)

    lane = lax.broadcasted_iota(jnp.int32, (1, LANES), 1)
    is_lf = ((lane >= nh) & (lane < 2 * nh)) | ((lane >= 3 * nh) & (lane < 4 * nh))
    is_bw_lf = (lane >= 3 * nh) & (lane < 4 * nh)

    rr = lax.broadcasted_iota(jnp.int32, (lc, lc), 0)
    cc = lax.broadcasted_iota(jnp.int32, (lc, lc), 1)
    tri_incl = jnp.where(rr >= cc, 1.0, 0.0).astype(bf16)

    def prep_body(c, _):
        off = pl.multiple_of(c * lc, lc)
        g = g_ref[pl.ds(off, lc), :] + gb_ref[...]
        tile = jnp.where(is_lf, _log_sigmoid(g), g)
        hi, mid, lo = _split3(jnp.where(is_lf, tile, 0.0))
        cs = _dot(tri_incl, hi) + _dot(tri_incl, mid) + _dot(tri_incl, lo)
        total = jnp.broadcast_to(cs[lc - 1:lc, :], (lc, LANES))
        b = jnp.where(is_bw_lf, total - cs + tile, cs)
        lmb = tile - pltpu.roll(b, LANES - nh, 1)
        gv = pltpu.roll(total, LANES - nh, 1) + lmb
        low = jnp.where(is_lf, b, lmb)
        high = pltpu.roll(jnp.where(is_lf, total, gv), 4 * nh, 1)
        gs_s[pl.ds(off, lc), :] = jnp.where(lane < 4 * nh, low, jnp.where(lane < 8 * nh, high, 0.0))
        wide = pl.ds(pl.multiple_of(c * w, w), w)
        qt_s[wide, :] = _conv_silu_chunk(q_ref, c, nc, cw_ref[:, :w], cb_ref[:, :w]).T
        kf = _conv_silu_chunk(k_ref, c, nc, cw_ref[:, w:], cb_ref[:, w:]) * (M_DIM ** -0.5)
        kh = kf.astype(bf16)
        kh_s[pl.ds(off, lc), :] = kh
        kl_s[pl.ds(off, lc), :] = (kf - kh.astype(f32)).astype(bf16)
        vt_s[wide, :] = v_ref[pl.ds(off, lc), :].astype(f32).T.astype(bf16)
        return 0

    lax.fori_loop(0, nc, prep_body, 0)

    sub8 = lax.broadcasted_iota(jnp.int32, (8, LANES), 0)

    def two_rows(x):
        hi = x.astype(bf16).astype(f32)
        return jnp.where(sub8 == 0, hi, jnp.where(sub8 == 1, x - hi, 0.0)).astype(bf16)

    def issue(c, h, d, tile3, tile_t):
        off = pl.multiple_of(c * lc, lc)
        col = slice(h * M_DIM, (h + 1) * M_DIM)
        ci, ln = d * nh + h, 2 * nh * d + h
        head = pl.ds(pl.multiple_of(c * w + h * M_DIM, M_DIM), M_DIM)
        b_row = tile_t[ln + nh:ln + nh + 1, :]
        g_row = tile_t[ln + 4 * nh:ln + 4 * nh + 1, :]
        total = tile_t[ln + 5 * nh:ln + 5 * nh + 1, 0:1]
        ct, nm, mm = ct_s[ci], n_out[d, h], m_out[d, h]
        qt = qt_s[head, :]
        qtb = qt.astype(bf16)
        khb = kh_s[pl.ds(off, lc), col]
        vtb = vt_s[head, :]
        lmb = _dot(tile3, sel_s[ci])
        sraw = _dot(khb, qtb)
        qn = _dot(two_rows(nm), qtb)
        lhs = jnp.concatenate([ct.astype(bf16), vtb], axis=1)
        m_new = jnp.maximum(mm + total, jnp.max(g_row, axis=1, keepdims=True))
        decay = jnp.exp(mm + total - m_new)
        ew = jnp.exp(g_row - m_new)
        ct_s[ci] = decay * ct + _dot(vtb * ew.astype(bf16), khb)
        ew2 = two_rows(ew)
        nk_h = _dot(ew2, khb)
        nk_l = _dot(ew2, kl_s[pl.ds(off, lc), col])
        n_out[d, h] = decay * nm + (nk_h[0:1, :] + nk_h[1:2, :] + nk_l[0:1, :])
        m_out[d, h] = m_new
        return dict(d=d, head=head, b_row=b_row, mm=mm, qt=qt, lmb=lmb, sraw=sraw, qn=qn, lhs=lhs)

    def weigh(st):
        keep = (cc >= rr) if st["d"] == 0 else (rr >= cc)
        dmat = jnp.where(keep, st["lmb"] + st["b_row"], -jnp.inf)
        inter = st["mm"] + st["b_row"]
        mt = jnp.maximum(inter, jnp.max(dmat, axis=0, keepdims=True))
        w_inter = jnp.exp(inter - mt)
        s = st["sraw"] * jnp.exp(dmat - mt)
        qn = st["qn"]
        den = w_inter * (qn[0:1, :] + qn[1:2, :]) + jnp.sum(s, axis=0, keepdims=True)
        inv = 1.0 / jnp.maximum(jnp.abs(den), jnp.exp(-mt))
        return jnp.concatenate([st["qt"] * (w_inter * inv), s * inv], axis=0).astype(bf16)

    def body(i, _):
        states = []
        for d in range(2):
            c = i if d == 0 else nc - 1 - i
            tile = gs_s[pl.ds(pl.multiple_of(c * lc, lc), lc), :]
            tile_t = tile.T
            hi, mid, lo = _split3(tile)
            tile3 = jnp.concatenate([hi, mid, lo], axis=1)
            states += [issue(c, h, d, tile3, tile_t) for h in range(nh)]
        rhss = [weigh(st) for st in states]
        for st, rhs in zip(states, rhss):
            hct = _dot(st["lhs"], rhs)
            if st["d"] == 0:
                hft_s[st["head"], :] = hct
            else:
                hbt_s[st["head"], :] = hct
        return 0

    lax.fori_loop(0, nc, body, 0)

    def out_body(c, _):
        rows = pl.ds(pl.multiple_of(c * lc, lc), lc)
        wide = pl.ds(pl.multiple_of(c * w, w), w)
        hsum = (hft_s[wide, :] + hbt_s[wide, :]).T
        for h in range(nh):
            col = slice(h * M_DIM, (h + 1) * M_DIM)
            mo = _sigmoid(o_ref[rows, col].astype(f32))
            hm_ref[rows, col] = (_rms(hsum[:, col]) * ng_ref[:, col] * mo).astype(hm_ref.dtype)
        return 0

    lax.fori_loop(0, nc, out_body, 0)
    for d in range(2):
        for h in range(nh):
            c_out[d, h] = ct_s[d * nh + h].T


def _mlstm(z, gates, gate_b, conv_w, conv_b, m_norm_g, state0, *, batch, t, row_block0):
    nh = M_HEADS
    w = M_W
    mode = dict(pipeline_mode=pl.Buffered(1)) if t * w * 2 > (1 << 20) else {}
    seq = lambda colblk: pl.BlockSpec((t, w), lambda b: (row_block0 + b, colblk), **mode)
    state = lambda *tail: pl.BlockSpec((None, 2, nh) + tail, lambda b: (b,) + (0,) * (2 + len(tail)))
    state_specs = [state(M_DIM, M_DIM), state(1, M_DIM), state(1, 1)]
    has_state = state0 is not None
    return pl.pallas_call(
        functools.partial(_mlstm_kernel, t=t, has_state=has_state),
        out_shape=(
            jax.ShapeDtypeStruct((batch * t, w), bf16),
            jax.ShapeDtypeStruct((batch, 2, nh, M_DIM, M_DIM), f32),
            jax.ShapeDtypeStruct((batch, 2, nh, 1, M_DIM), f32),
            jax.ShapeDtypeStruct((batch, 2, nh, 1, 1), f32),
        ),
        grid=(batch,),
        in_specs=[
            seq(0), seq(1), seq(2), seq(3),
            pl.BlockSpec((t, LANES), lambda b: (row_block0 + b, 0)),
            _const_spec(gate_b.shape),
            _const_spec(conv_w.shape), _const_spec(conv_b.shape), _const_spec(m_norm_g.shape),
            *(state_specs if has_state else []),
        ],
        out_specs=(pl.BlockSpec((t, w), lambda b: (b, 0)), *state_specs),
        scratch_shapes=[
            pltpu.VMEM((t * nh, M_DIM), f32),
            pltpu.VMEM((t, w), bf16),
            pltpu.VMEM((t, w), bf16),
            pltpu.VMEM((t * nh, M_DIM), bf16),
            pltpu.VMEM((t, LANES), f32),
            pltpu.VMEM((t * nh, M_DIM), f32),
            pltpu.VMEM((t * nh, M_DIM), f32),
            pltpu.VMEM((2 * nh, M_DIM, M_DIM), f32),
            pltpu.VMEM((2 * nh, 3 * LANES, LANES), bf16),
        ],
        compiler_params=_cparams("arbitrary"),
        name=f"mlstm_t{t}",
    )(z, z, z, z, gates, gate_b, conv_w, conv_b, m_norm_g, *(state0 if has_state else ()))


def _pair_norm(x, gain, on_mxu):
    sq = x * x
    if on_mxu:
        same_half = jnp.where((lax.broadcasted_iota(jnp.int32, (LANES, LANES), 0) < A_DIM)
                              == (lax.broadcasted_iota(jnp.int32, (LANES, LANES), 1) < A_DIM), 1.0, 0.0).astype(bf16)
        hi = sq.astype(bf16)
        lo = (sq - hi.astype(f32)).astype(bf16)
        s = _dot(hi, same_half) + _dot(lo, same_half)
        return x * lax.rsqrt(s * (1.0 / A_DIM) + EPS) * gain
    lane = lax.broadcasted_iota(jnp.int32, (1, LANES), 1)
    first = lane < A_DIM
    s_all = jnp.sum(sq, axis=-1, keepdims=True)
    s0 = jnp.sum(jnp.where(first, sq, 0.0), axis=-1, keepdims=True)
    inv0 = lax.rsqrt(s0 * (1.0 / A_DIM) + EPS)
    inv1 = lax.rsqrt((s_all - s0) * (1.0 / A_DIM) + EPS)
    return x * jnp.where(first, inv0, inv1) * gain


def _rope(x, cos, sin_signed):
    lane = lax.broadcasted_iota(jnp.int32, (1, LANES), 1)
    nf = A_DIM // 4
    partner = jnp.where((lane % (2 * nf)) < nf, pltpu.roll(x, LANES - nf, 1), pltpu.roll(x, nf, 1))
    return x * cos + partner * sin_signed


def _attn_kernel(*refs, t, tq, rope, ctx, out_scale):
    it = iter(refs)
    lam_ref = next(it)
    q_ref, k_ref, v_ref = next(it), next(it), next(it)
    qg_ref, kg_ref, ag_ref = next(it), next(it), next(it)
    if rope:
        cosq_ref, sinq_ref, cosk_ref, sink_ref = next(it), next(it), next(it), next(it)
    if ctx:
        kct_ref, vc_ref = next(it), next(it)
    ha_ref = next(it)
    if not ctx:
        newk_ref, newv_ref = next(it), next(it)
    kt_s = next(it)
    nh = A_HEADS

    @pl.when(pl.program_id(1) == 0)
    def _():
        for h in range(nh):
            col = slice(h * LANES, (h + 1) * LANES)
            kn = _pair_norm(k_ref[:, col].astype(f32), kg_ref[...], True)
            if not ctx:
                newk_ref[h, 0] = kn[:, :A_DIM]
                newk_ref[h, 1] = kn[:, A_DIM:]
                newv_ref[h] = v_ref[:, col].astype(f32)
            if rope:
                kn = _rope(kn, cosk_ref[...], sink_ref[...])
            kt_s[col, :] = kn.T.astype(bf16)

    lane = lax.broadcasted_iota(jnp.int32, (1, LANES), 1)
    lam = lam_ref[0]
    for h in range(nh):
        col = slice(h * LANES, (h + 1) * LANES)
        q = _pair_norm(q_ref[:, col].astype(f32), qg_ref[...], not ctx)
        if rope:
            q = _rope(q, cosq_ref[...], sinq_ref[...])
        q = q * (A_DIM ** -0.5 * math.log2(math.e))
        qs = [jnp.where(lane < A_DIM, q, 0.0).astype(bf16), jnp.where(lane >= A_DIM, q, 0.0).astype(bf16)]
        kt = kt_s[col, :]
        vb = v_ref[:, col]
        if ctx:
            kctb = kct_ref[h].astype(bf16)
            vcb = vc_ref[h].astype(bf16)
        outs = []
        for i in range(2):
            sn = _dot(qs[i], kt)
            mx = jnp.max(sn, axis=-1, keepdims=True)
            if ctx:
                sc = _dot(qs[i], kctb)
                mx = jnp.maximum(mx, jnp.max(sc, axis=-1, keepdims=True))
            en = jnp.exp2(sn - mx)
            den = jnp.sum(en, axis=-1, keepdims=True)
            o = _dot(en.astype(bf16), vb)
            if ctx:
                ec = jnp.exp2(sc - mx)
                den = den + jnp.sum(ec, axis=-1, keepdims=True)
                o = o + _dot(ec.astype(bf16), vcb)
            outs.append(o * (1.0 / den))
        o = outs[0] - lam * outs[1]
        ha_ref[:, col] = (_rms(o) * ag_ref[...] * out_scale).astype(ha_ref.dtype)


def _attn(z, lam, qg2, kg2, a_norm_g, *, batch, t, row_block0, rope_tabs=None, ctx_kv=None, out_scale):
    nh = A_HEADS
    w = A_W
    tq = min(t, 256)
    nq = t // tq
    rope = rope_tabs is not None
    ctx = ctx_kv is not None
    qblk = 4 * M_W // w
    vec = pl.BlockSpec((1, LANES), lambda b, i: (0, 0))
    in_specs = [
        pl.BlockSpec(memory_space=pltpu.SMEM),
        pl.BlockSpec((tq, w), lambda b, i: ((row_block0 + b) * nq + i, qblk)),
        pl.BlockSpec((t, w), lambda b, i: (row_block0 + b, qblk + 1)),
        pl.BlockSpec((t, w), lambda b, i: (row_block0 + b, qblk + 2)),
        vec, vec, vec,
    ]
    args = [lam, z, z, z, qg2, kg2, a_norm_g]
    if rope:
        cos, sin = rope_tabs
        in_specs += [pl.BlockSpec((tq, LANES), lambda b, i: (i, 0))] * 2
        in_specs += [pl.BlockSpec((t, LANES), lambda b, i: (0, 0))] * 2
        args += [cos, sin, cos, sin]
    if ctx:
        kct, vc = ctx_kv
        in_specs += [pl.BlockSpec((None,) + kct.shape[1:], lambda b, i: (b, 0, 0, 0)),
                     pl.BlockSpec((None,) + vc.shape[1:], lambda b, i: (b, 0, 0, 0))]
        args += [kct, vc]
    out_shape = [jax.ShapeDtypeStruct((batch * t, w), bf16)]
    out_specs = [pl.BlockSpec((tq, w), lambda b, i: (b * nq + i, 0))]
    if not ctx:
        out_shape += [jax.ShapeDtypeStruct((batch, nh, 2, t, A_DIM), f32),
                      jax.ShapeDtypeStruct((batch, nh, t, A_VDIM), f32)]
        out_specs += [pl.BlockSpec((None, nh, 2, t, A_DIM), lambda b, i: (b, 0, 0, 0, 0)),
                      pl.BlockSpec((None, nh, t, A_VDIM), lambda b, i: (b, 0, 0, 0))]
    return pl.pallas_call(
        functools.partial(_attn_kernel, t=t, tq=tq, rope=rope, ctx=ctx, out_scale=out_scale),
        out_shape=tuple(out_shape),
        grid=(batch, nq),
        in_specs=in_specs,
        out_specs=tuple(out_specs),
        scratch_shapes=[pltpu.VMEM((w, t), bf16)],
        compiler_params=_cparams("arbitrary", "arbitrary"),
        name=f"diff_attn_t{t}",
    )(*args)


def _rope_tables(t):
    rows = t // GRID_W
    pos_row = np.repeat(np.arange(rows, dtype=np.float64), GRID_W)
    pos_col = (np.arange(rows * GRID_W) % GRID_W).astype(np.float64)
    nf = A_DIM // 4
    inv = ROPE_THETA ** (-np.arange(nf, dtype=np.float64) / nf)
    lane = np.arange(LANES)
    j = lane % (2 * nf)
    use_col = (lane % A_DIM) >= (A_DIM // 2)
    ang = np.where(use_col[None, :], pos_col[:, None], pos_row[:, None]) * inv[j % nf][None, :]
    sign = np.where(j < nf, -1.0, 1.0)[None, :]
    return jnp.asarray(np.cos(ang), f32), jnp.asarray(np.sin(ang) * sign, f32)


def _ffn_kernel(xp_ref, xs_ref, hmp_ref, hms_ref, hap_ref, has_ref, mod_ref, wo_ref, w1_ref, w3_ref, w2_ref, o_ref,
                *, chunks, npt):
    half = hmp_ref.shape[1]
    hm = _pick(hmp_ref, hms_ref, npt)
    ha = _pick(hap_ref, has_ref, npt)
    x = _pick(xp_ref, xs_ref, npt) + mod_ref[2:3, :] * (_dot(hm, wo_ref[:half, :]) + _dot(ha, wo_ref[half:, :]))
    h = _modulate(x, mod_ref, 3).astype(bf16)
    acc = jnp.zeros(x.shape, f32)
    for lo, hi in chunks:
        ab = _dot(h, jnp.concatenate([w1_ref[:, lo:hi], w3_ref[:, lo:hi]], axis=1))
        a = _silu(ab[:, :hi - lo]) * ab[:, hi - lo:]
        acc = acc + _dot(a.astype(bf16), w2_ref[lo:hi, :])
    o_ref[...] = x + mod_ref[5:6, :] * acc


def _ffn(xp, xs, hm_p, hm_s, ha_p, ha_s, mod, w_out, w1, w3, w2, n_prompt_rows, dec_seq, job=None):
    d = xp.shape[1]
    r = xp.shape[0] + xs.shape[0]
    tm = ROW_TILE
    npt = n_prompt_rows // tm
    grp = functools.partial(_group_of_tile, tm=tm, n_prompt_rows=n_prompt_rows, dec_seq=dec_seq)
    ff = w1.shape[1]
    cut = MXU_COLS * ((ff // MXU_COLS + 1) // 2) if ff % MXU_COLS == 0 else ff
    chunks = ((0, cut), (cut, ff)) if cut < ff else ((0, ff),)
    return _call_with_job(
        functools.partial(_ffn_kernel, chunks=chunks, npt=npt), job(r // tm) if job else None,
        out_shape=(jax.ShapeDtypeStruct((r, d), f32),),
        grid=(r // tm,),
        in_specs=[
            *_part_specs(tm, d, npt),
            *_part_specs(tm, hm_p.shape[1], npt),
            *_part_specs(tm, ha_p.shape[1], npt),
            pl.BlockSpec((None, 6, d), lambda i: (grp(i), 0, 0)),
            _const_spec(w_out.shape), _const_spec(w1.shape), _const_spec(w3.shape), _const_spec(w2.shape),
        ],
        out_specs=(pl.BlockSpec((tm, d), lambda i: (i, 0)),),
        args=(xp, xs, hm_p, hm_s, ha_p, ha_s, mod, w_out, w1, w3, w2),
        compiler_params=_cparams("arbitrary"),
        name="outproj_ffn",
    )


def _gelu_tanh(x):
    k1 = -2.0 * math.sqrt(2.0 / math.pi) * math.log2(math.e)
    k3 = k1 * 0.044715
    return x / (1.0 + jnp.exp2(x * (k1 + k3 * (x * x))))


def _gmlp_kernel(x_ref, mod_ref, win_ref, bin_ref, lng_ref, lnb_ref, ws_ref, bs_ref, wout_ref, o_ref, us_s):
    x = x_ref[...]
    tm = x.shape[0]
    e = wout_ref.shape[0]
    ge = e // C_GROUPS
    h = _modulate(x, mod_ref, 0).astype(bf16)
    u = _gelu_tanh(_dot(h, win_ref[:, :e]) + bin_ref[:, :e])
    v = _gelu_tanh(_dot(h, win_ref[:, e:]) + bin_ref[:, e:])
    mu = jnp.mean(v, axis=-1, keepdims=True)
    vc = v - mu
    var = jnp.mean(vc * vc, axis=-1, keepdims=True)
    vn = (vc * lax.rsqrt(var + EPS) * lng_ref[...] + lnb_ref[...]).astype(bf16)
    for n in range(tm // C_CHUNK):
        rows = slice(n * C_CHUNK, (n + 1) * C_CHUNK)
        for g in range(C_GROUPS):
            cols = slice(g * ge, (g + 1) * ge)
            s = _dot(ws_ref[g], vn[rows, cols]) + bs_ref[:, cols]
            us_s[rows, cols] = (u[rows, cols] * s).astype(bf16)
    o_ref[...] = x + mod_ref[2:3, :] * _dot(us_s[...], wout_ref[...])


def _gmlp(x, mod, w_in, b_in, ln_g, ln_b, ws, bs_full, w_out, n_prompt_rows, dec_seq, job=None):
    r, d = x.shape
    tm = ROW_TILE
    e = w_out.shape[0]
    grp = functools.partial(_group_of_tile, tm=tm, n_prompt_rows=n_prompt_rows, dec_seq=dec_seq)
    return _call_with_job(
        _gmlp_kernel, job(r // tm) if job else None,
        out_shape=(jax.ShapeDtypeStruct((r, d), f32),),
        grid=(r // tm,),
        in_specs=[
            pl.BlockSpec((tm, d), lambda i: (i, 0)),
            pl.BlockSpec((None, 6, d), lambda i: (grp(i), 0, 0)),
            _const_spec(w_in.shape), _const_spec(b_in.shape), _const_spec(ln_g.shape), _const_spec(ln_b.shape),
            _const_spec(ws.shape), _const_spec(bs_full.shape), _const_spec(w_out.shape),
        ],
        out_specs=(pl.BlockSpec((tm, d), lambda i: (i, 0)),),
        args=(x, mod, w_in, b_in, ln_g, ln_b, ws, bs_full, w_out),
        scratch_shapes=[pltpu.VMEM((tm, e), bf16)],
        compiler_params=_cparams("arbitrary"),
        name="gmlp",
    )


def _router_kernel(x_ref, mod_ref, wr_hi_ref, wr_lo_ref, h_ref, meta_ref, counts_ref):
    hf = _modulate(x_ref[...], mod_ref, 3)
    hb = hf.astype(bf16)
    h_ref[...] = hb
    h_lo = (hf - hb.astype(f32)).astype(bf16)
    logits = _dot(hb, wr_hi_ref[...]) + (_dot(hb, wr_lo_ref[...]) + _dot(h_lo, wr_hi_ref[...]))
    lane = lax.broadcasted_iota(jnp.int32, logits.shape, 1).astype(f32)
    logits = jnp.where(lane < N_EXPERTS, logits, -jnp.inf)
    m1 = jnp.max(logits, axis=-1, keepdims=True)
    i1 = jnp.min(jnp.where(logits == m1, lane, float(LANES)), axis=-1, keepdims=True)
    rest = jnp.where(lane == i1, -jnp.inf, logits)
    m2 = jnp.max(rest, axis=-1, keepdims=True)
    i2 = jnp.min(jnp.where(rest == m2, lane, float(LANES)), axis=-1, keepdims=True)
    e2 = jnp.exp(m2 - m1)
    w1 = 1.0 / (1.0 + e2)
    w2 = e2 * w1

    tm = logits.shape[0]
    cnt = jnp.where(lane == i1, 1.0, jnp.where(lane == i2, 1.0, 0.0))
    rr = lax.broadcasted_iota(jnp.int32, (tm, tm), 0)
    cc = lax.broadcasted_iota(jnp.int32, (tm, tm), 1)
    before = jnp.where(rr > cc, 1.0, 0.0).astype(bf16)
    rank = _dot(before, cnt.astype(bf16))
    counts = jnp.sum(cnt, axis=0, keepdims=True)
    padded = jnp.floor((counts + (PIECE_ALIGN - 1)) * (1.0 / PIECE_ALIGN)) * PIECE_ALIGN
    lane1 = lane[0:1, :]
    piece_off = jnp.zeros((1, LANES), f32)
    off = jnp.zeros((1, 1), f32)
    for e in range(N_EXPERTS):
        piece_off = jnp.where(lane1 == e, off, piece_off)
        off = off + padded[:, e:e + 1]
    local = piece_off + rank
    pos1 = jnp.sum(jnp.where(lane == i1, local, 0.0), axis=-1, keepdims=True)
    pos2 = jnp.sum(jnp.where(lane == i2, local, 0.0), axis=-1, keepdims=True)
    meta_ref[...] = jnp.where(lane == 0, pos1, jnp.where(lane == 1, pos2, jnp.where(lane == 2, w1,
                              jnp.where(lane == 3, w2, 0.0))))
    counts_ref[...] = counts


def _router(x, mod, wr_hi, wr_lo, n_prompt_rows, dec_seq):
    r, d = x.shape
    tm = MOE_SORT_BLOCK
    grp = functools.partial(_group_of_tile, tm=tm, n_prompt_rows=n_prompt_rows, dec_seq=dec_seq)
    return pl.pallas_call(
        _router_kernel,
        out_shape=(jax.ShapeDtypeStruct((r, d), bf16), jax.ShapeDtypeStruct((r, LANES), f32),
                   jax.ShapeDtypeStruct((r // tm, 1, LANES), f32)),
        grid=(r // tm,),
        in_specs=[
            pl.BlockSpec((tm, d), lambda i: (i, 0)),
            pl.BlockSpec((None, 6, d), lambda i: (grp(i), 0, 0)),
            _const_spec(wr_hi.shape), _const_spec(wr_lo.shape),
        ],
        out_specs=(pl.BlockSpec((tm, d), lambda i: (i, 0)), pl.BlockSpec((tm, LANES), lambda i: (i, 0)),
                   pl.BlockSpec((None, 1, LANES), lambda i: (i, 0, 0))),
        compiler_params=_cparams("arbitrary"),
        name="router",
    )(x, mod, wr_hi, wr_lo)


def _moe_kernel(n16_ref, loc_ref, dst_ref, seg_off_ref, seg_len_ref, csel_ref, cnum_ref,
                h_ref, meta_ref, x_ref, mod_ref, w1_ref, w3_ref, w2_ref, op_ref, os_ref,
                hs_s, ys_s, loc_s, *, nsub, ne, nf, chunks, nsb_prompt):
    sb = pl.program_id(0)
    p = pl.program_id(1)
    n_exp = ne * nf
    loc_rows = loc_s.shape[0]
    pa = PIECE_ALIGN

    def one_hot_cols(meta, v1, v2):
        lane = lax.broadcasted_iota(jnp.int32, (1, loc_rows), 1).astype(f32)
        return jnp.where(lane == meta[:, 0:1], v1, jnp.where(lane == meta[:, 1:2], v2, 0.0)).astype(bf16)

    def copy_pieces(blk, to_sorted):
        for e in range(ne):
            n = n16_ref[blk * ne + e]
            src = loc_ref[blk * ne + e]
            dst = dst_ref[blk * ne + e]

            def cp(i, _, src=src, dst=dst):
                a = pl.ds(pl.multiple_of(src + pa * i, pa), pa)
                b = pl.ds(pl.multiple_of(dst + pa * i, pa), pa)
                if to_sorted:
                    hs_s[b, :] = loc_s[a, :]
                else:
                    loc_s[a, :] = ys_s[b, :].astype(bf16)
                return 0

            lax.fori_loop(0, n, cp, 0)

    @pl.when(p < nsub)
    def _():
        @pl.when(p == 0)
        def _():
            hs_s[...] = jnp.zeros_like(hs_s)
            ys_s[...] = jnp.zeros_like(ys_s)

        pt = one_hot_cols(meta_ref[...], 1.0, 1.0)
        loc_s[...] = _dot_tn(pt, h_ref[...]).astype(bf16)
        copy_pieces(sb * nsub + p, True)

    @pl.when((p >= nsub) & (p < nsub + n_exp))
    def _():
        e = lax.div(p - nsub, jnp.int32(nf))
        start = seg_off_ref[sb * ne + e]
        ln = seg_len_ref[sb * ne + e]
        which = csel_ref[sb * ne + e]
        count = cnum_ref[sb * ne + e]

        def chunk(lo, size):
            r0 = jnp.minimum(lo, hs_s.shape[0] - size)
            rows = pl.ds(pl.multiple_of(r0, pa), size)
            xc = hs_s[rows, :]
            ab = _dot(xc, jnp.concatenate([w1_ref[...], w3_ref[...]], axis=1))
            tf = w1_ref.shape[1]
            a = _silu(ab[:, :tf]) * ab[:, tf:]
            y = _dot(a.astype(bf16), w2_ref[...])
            ri = r0 + lax.broadcasted_iota(jnp.int32, (size, 1), 0)
            ys_s[rows, :] += jnp.where(ri >= lo, jnp.where(ri < start + ln, y, 0.0), 0.0)

        for k, size in enumerate(chunks):
            @pl.when(which == k)
            def _(size=size):
                def body(j, _):
                    chunk(start + j * size, size)
                    return 0

                lax.fori_loop(0, count, body, 0)

    @pl.when(p >= nsub + n_exp)
    def _():
        copy_pieces(sb * nsub + (p - nsub - n_exp), False)
        meta = meta_ref[...]
        a = one_hot_cols(meta, meta[:, 2:3], meta[:, 3:4])
        y = x_ref[...] + mod_ref[5:6, :] * _dot(a, loc_s[...])

        @pl.when(sb < nsb_prompt)
        def _():
            op_ref[...] = y

        @pl.when(sb >= nsb_prompt)
        def _():
            os_ref[...] = y


def _moe(x, h, meta, counts, mod, w1, w3, w2, n_prompt_rows, dec_seq):
    r, d = x.shape
    ne, ff, _ = w2.shape
    assert w1.shape == (ne, ff // MOE_FF_TILE, d, MOE_FF_TILE)
    t_super, tb, chunks, tf, pa = MOE_SUPER_BLOCK, MOE_SORT_BLOCK, MOE_CHUNKS, MOE_FF_TILE, PIECE_ALIGN
    ch = max(chunks)
    nsub = t_super // tb
    nsb = r // t_super
    npb = n_prompt_rows // tb
    nf = ff // tf
    n_exp = ne * nf
    loc_rows = 2 * tb + LANES
    assert loc_rows >= 2 * tb + ne * (pa - 1)
    max_rows = 2 * t_super + nsub * ne * (pa - 1)
    sort_rows = LANES * (-(-max_rows // LANES))
    assert all(size % pa == 0 and size <= sort_rows for size in chunks)

    cnt = counts[:, 0, :ne].astype(jnp.int32)
    n16 = (cnt + (pa - 1)) // pa
    loc = pa * (jnp.cumsum(n16, axis=1) - n16)
    n16_sb = n16.reshape(nsb, nsub, ne)
    seg_len = pa * jnp.sum(n16_sb, axis=1)
    seg_off = jnp.cumsum(seg_len, axis=1) - seg_len
    dst = seg_off[:, None, :] + pa * (jnp.cumsum(n16_sb, axis=1) - n16_sb)
    cnum = (seg_len + ch - 1) // ch
    need = pa * ((seg_len // pa + jnp.maximum(cnum, 1) - 1) // jnp.maximum(cnum, 1))
    csel = sum((need > size).astype(jnp.int32) for size in chunks[:-1])
    scalars = [a.reshape(-1).astype(jnp.int32) for a in (n16, loc, dst, seg_off, seg_len, csel, cnum)]

    grp = functools.partial(_group_of_tile, tm=t_super, n_prompt_rows=n_prompt_rows, dec_seq=dec_seq)

    def exp_step(p):
        return jnp.clip(p - nsub, 0, n_exp - 1)

    def tok_blk(sb, s):
        return sb * nsub + jnp.clip(s, 0, nsub - 1)

    grid_spec = pltpu.PrefetchScalarGridSpec(
        num_scalar_prefetch=len(scalars),
        grid=(nsb, nsub + n_exp + nsub),
        in_specs=[
            pl.BlockSpec((tb, d), lambda sb, p, *_: (tok_blk(sb, p), 0)),
            pl.BlockSpec((tb, LANES), lambda sb, p, *_: (sb * nsub + jnp.where(p < nsub, p, jnp.clip(p - nsub - n_exp, 0, nsub - 1)), 0)),
            pl.BlockSpec((tb, d), lambda sb, p, *_: (tok_blk(sb, p - nsub - n_exp), 0)),
            pl.BlockSpec((None, 6, d), lambda sb, p, *_: (grp(sb), 0, 0)),
            pl.BlockSpec((None, None, d, tf), lambda sb, p, *_: (exp_step(p) // nf, exp_step(p) % nf, 0, 0)),
            pl.BlockSpec((None, None, d, tf), lambda sb, p, *_: (exp_step(p) // nf, exp_step(p) % nf, 0, 0)),
            pl.BlockSpec((None, tf, d), lambda sb, p, *_: (exp_step(p) // nf, exp_step(p) % nf, 0)),
        ],
        out_specs=(
            pl.BlockSpec((tb, d), lambda sb, p, *_: (jnp.minimum(tok_blk(sb, p - nsub - n_exp), npb - 1), 0)),
            pl.BlockSpec((tb, d), lambda sb, p, *_: (jnp.maximum(tok_blk(sb, p - nsub - n_exp) - npb, 0), 0)),
        ),
        scratch_shapes=[
            pltpu.VMEM((sort_rows, d), bf16),
            pltpu.VMEM((sort_rows, d), f32),
            pltpu.VMEM((loc_rows, d), bf16),
        ],
    )
    return pl.pallas_call(
        functools.partial(_moe_kernel, nsub=nsub, ne=ne, nf=nf, chunks=chunks,
                          nsb_prompt=n_prompt_rows // t_super),
        out_shape=(jax.ShapeDtypeStruct((n_prompt_rows, d), f32), jax.ShapeDtypeStruct((r - n_prompt_rows, d), f32)),
        grid_spec=grid_spec,
        compiler_params=_cparams("arbitrary", "arbitrary"),
        name="moe_sparse",
    )(*scalars, h, meta, x, mod, w1, w3, w2)


def kernel(x_prompt, x_sample, c, cache_dattn_k, cache_dattn_v, state_mlstm_c, state_mlstm_n, state_mlstm_m,
           c_ctx, w_ada, b_ada, w_in_ab, conv_w, conv_b, gate_b, qn_g, kn_g, lam_q1, lam_k1, lam_q2, lam_k2,
           m_norm_g, a_norm_g, w_out_ab, ff_w1, ff_w3, ff_w2, w_in_c, b_in_c, c_ln_g, c_ln_b, c_ws, c_bs,
           w_out_c, w_router, ex_w1, ex_w3, ex_w2):
    bp, seq, d = x_prompt.shape
    bs, dec_seq, _ = x_sample.shape
    depth = w_ada.shape[0]
    n_prompt_rows = bp * seq
    assert n_prompt_rows % dec_seq == 0 and seq % M_CHUNK == 0 and dec_seq % MOE_SUPER_BLOCK == 0
    nh = M_HEADS

    mods = _ada_table(jnp.concatenate([c_ctx[None], c], axis=0), w_ada, b_ada)
    x = (x_prompt.reshape(n_prompt_rows, d), x_sample.reshape(bs * dec_seq, d))
    rows = (n_prompt_rows, dec_seq)

    def joined(v):
        return jnp.concatenate(v, axis=0) if isinstance(v, tuple) else v

    def parts(v):
        return v if isinstance(v, tuple) else (v[:n_prompt_rows], v[n_prompt_rows:])

    new_k, new_v, new_c, new_n, new_m = [], [], [], [], []
    for l in range(depth):
        j = l // 2
        mod = mods[l]
        if l % 2 == 0:
            lam_init = 0.8 - 0.6 * math.exp(-0.3 * l)
            lam = (jnp.exp(jnp.sum((lam_q1[j] * lam_k1[j]).astype(f32)))
                   - jnp.exp(jnp.sum((lam_q2[j] * lam_k2[j]).astype(f32))) + lam_init).reshape(1)
            x = parts(x)
            nxt = j if l + 1 < depth else None
            z, gates, *ex2 = _inproj(*x, mod, w_in_ab[j], (4 * M_W, 4 * M_W + 4 * nh), *rows,
                                     job=None if nxt is None else functools.partial(_cast_job_rows, ex_w2[nxt]))
            gb = jnp.zeros((1, LANES), f32).at[0, :4 * nh].set(gate_b[j])
            mng = m_norm_g[j].reshape(1, M_W)
            cb = conv_b[j].reshape(1, 2 * M_W)
            hm_p, c_f, n_f, m_f = _mlstm(z, gates, gb, conv_w[j], cb, mng, None, batch=bp, t=seq, row_block0=0)
            state0 = (state_mlstm_c[:, j], state_mlstm_n[:, j].reshape(bs, 2, nh, 1, M_DIM),
                      state_mlstm_m[:, j].reshape(bs, 2, nh, 1, 1))
            hm_s, _, _, _ = _mlstm(z, gates, gb, conv_w[j], cb, mng, state0,
                                   batch=bs, t=dec_seq, row_block0=n_prompt_rows // dec_seq)
            new_c.append(c_f)
            new_n.append(n_f.reshape(bp, 2, nh, M_DIM))
            new_m.append(m_f.reshape(bp, 2, nh))

            qg2 = jnp.tile(qn_g[j], 2).reshape(1, LANES)
            kg2 = jnp.tile(kn_g[j], 2).reshape(1, LANES)
            ag = a_norm_g[j].reshape(1, LANES)
            ha_p, k_ctx, v_ctx = _attn(z, lam, qg2, kg2, ag, batch=bp, t=seq, row_block0=0,
                                       out_scale=1.0 - lam_init)
            kct = cache_dattn_k[:, j].transpose(0, 1, 2, 4, 3).reshape(bs, A_HEADS, LANES, -1)
            (ha_s,) = _attn(z, lam, qg2, kg2, ag, batch=bs, t=dec_seq, row_block0=n_prompt_rows // dec_seq,
                            rope_tabs=_rope_tables(dec_seq), ctx_kv=(kct, cache_dattn_v[:, j]),
                            out_scale=1.0 - lam_init)
            new_k.append(k_ctx)
            new_v.append(v_ctx)
            x, *ex1 = _ffn(*x, hm_p, hm_s, ha_p, ha_s, mod, w_out_ab[j].astype(bf16), ff_w1[j].astype(bf16),
                           ff_w3[j].astype(bf16), ff_w2[j].astype(bf16), *rows,
                           job=None if nxt is None else functools.partial(_cast_job_tiled, ex_w1[nxt], tf=MOE_FF_TILE))
        else:
            e = w_out_c.shape[1]
            bs_full = jnp.repeat(c_bs[j].T, e // C_GROUPS, axis=1)
            x, ex3 = _gmlp(joined(x), mod, w_in_c[j].astype(bf16), b_in_c[j].reshape(1, -1),
                           c_ln_g[j].reshape(1, -1), c_ln_b[j].reshape(1, -1), c_ws[j].astype(bf16), bs_full,
                           w_out_c[j].astype(bf16), *rows,
                           job=functools.partial(_cast_job_tiled, ex_w3[j], tf=MOE_FF_TILE))
            wr = jnp.zeros((d, LANES), f32).at[:, :N_EXPERTS].set(w_router[j])
            wr_hi = wr.astype(bf16)
            wr_lo = (wr - wr_hi.astype(f32)).astype(bf16)
            h, meta, counts = _router(x, mod, wr_hi, wr_lo, *rows)
            x = _moe(x, h, meta, counts, mod, ex1[0], ex3, ex2[0].reshape(ex_w2[j].shape), *rows)

    y_prompt, y_sample = parts(x)
    y_prompt = y_prompt.reshape(bp, seq, d)
    y_sample = y_sample.reshape(bs, dec_seq, d)
    return (y_prompt, y_sample, jnp.stack(new_k, axis=1), jnp.stack(new_v, axis=1),
            jnp.stack(new_c, axis=1), jnp.stack(new_n, axis=1), jnp.stack(new_m, axis=1))
```

```python
import functools
import math
from typing import Callable, NamedTuple

import jax
import jax.numpy as jnp
import numpy as np
from jax import lax
from jax.experimental import pallas as pl
from jax.experimental.pallas import tpu as pltpu

f32 = jnp.float32
bf16 = jnp.bfloat16

D_MODEL = 1024
M_HEADS = 4
M_DIM = 128
M_W = M_HEADS * M_DIM
M_CHUNK = 128
A_HEADS = 4
A_VDIM = 128
A_DIM = 64
A_W = A_HEADS * A_VDIM
GRID_W = 64
ROPE_THETA = 10000.0
C_CHUNK = 128
C_GROUPS = 4
N_EXPERTS = 8
EPS = 1e-6

LANES = 128
MXU_COLS = 256
ROW_TILE = 512
ADA_COL_TILE = 1536
INPROJ_COL_CHUNK = 512
ATTN_Q_BLOCK = 256
PIECE_ALIGN = 16
MOE_SUPER_BLOCK = 2048
MOE_SORT_BLOCK = 512
MOE_CHUNKS = (128, 192, 256, 320, 384, 448, 512, 576, 640)
MOE_FF_TILE = 896
VMEM_LIMIT = 60 * 1024 * 1024


def _cparams(*sem):
    return pltpu.CompilerParams(dimension_semantics=tuple(sem), vmem_limit_bytes=VMEM_LIMIT)


def _const_spec(shape):
    nd = len(shape)
    return pl.BlockSpec(shape, lambda *_: (0,) * nd, pipeline_mode=pl.Buffered(1))


def _sigmoid(x):
    return 1.0 / (1.0 + jnp.exp(-x))


def _silu(x):
    return x * _sigmoid(x)


def _log_sigmoid(x):
    return jnp.minimum(x, 0.0) - jnp.log(1.0 + jnp.exp(-jnp.abs(x)))


def _rms(x):
    return x * lax.rsqrt(jnp.mean(x * x, axis=-1, keepdims=True) + EPS)


def _modulate(x, mod_ref, first):
    shift = mod_ref[first:first + 1, :]
    scale = mod_ref[first + 1:first + 2, :]
    return _rms(x) * (1.0 + scale) + shift


def _dot(a, b):
    return jnp.dot(a, b, preferred_element_type=f32)


def _dot_tn(a, b):
    return lax.dot_general(a, b, (((0,), (0,)), ((), ())), preferred_element_type=f32)


def _split3(x):
    hi = x.astype(bf16)
    r1 = x - hi.astype(f32)
    mid = r1.astype(bf16)
    lo = (r1 - mid.astype(f32)).astype(bf16)
    return hi, mid, lo


def _group_of_tile(i, tm, n_prompt_rows, dec_seq):
    pt = n_prompt_rows // tm
    return jnp.where(i < pt, 0, 1 + (i - pt) // (dec_seq // tm))


def _ada_kernel(cv_ref, w_ref, b_ref, o_ref):
    a = _silu(cv_ref[...]).astype(bf16)
    o_ref[...] = _dot(a, w_ref[...].astype(bf16)) + b_ref[...]


def _ada_table(cv, w_ada, b_ada):
    depth, d, n = w_ada.shape
    g = cv.shape[0]
    gp = 8 * ((g + 7) // 8)
    cvp = jnp.zeros((gp, d), f32).at[:g].set(cv)
    tn = ADA_COL_TILE
    out = pl.pallas_call(
        _ada_kernel,
        out_shape=jax.ShapeDtypeStruct((depth, gp, n), f32),
        grid=(depth, n // tn),
        in_specs=[
            pl.BlockSpec((gp, d), lambda l, j: (0, 0)),
            pl.BlockSpec((None, d, tn), lambda l, j: (l, 0, j)),
            pl.BlockSpec((None, 1, tn), lambda l, j: (l, 0, j)),
        ],
        out_specs=pl.BlockSpec((None, gp, tn), lambda l, j: (l, 0, j)),
        compiler_params=_cparams("arbitrary", "arbitrary"),
        name="ada_table",
    )(cvp, w_ada, b_ada.reshape(depth, 1, n))
    return out[:, :g].reshape(depth, g, 6, d)


def _part_specs(tm, width, npt):
    first = pl.BlockSpec((tm, width), lambda i: (jnp.minimum(i, npt - 1), 0))
    second = pl.BlockSpec((tm, width), lambda i: (jnp.maximum(i - npt, 0), 0))
    return first, second


def _pick(a_ref, b_ref, npt):
    return jnp.where(pl.program_id(0) < npt, a_ref[...], b_ref[...])


class _CastJob(NamedTuple):
    src: jax.Array
    in_spec: pl.BlockSpec
    out_spec: pl.BlockSpec
    out_shape: jax.ShapeDtypeStruct
    body: Callable


def _cast_job_tiled(src, steps, tf):
    ne, d, ff = src.shape
    bands = steps // ne
    rows = d // bands
    nf = ff // tf

    def body(i_ref, o_ref):
        for f in range(nf):
            o_ref[f] = i_ref[:, f * tf:(f + 1) * tf].astype(bf16)

    return _CastJob(src, pl.BlockSpec((None, rows, ff), lambda i: (i // bands, i % bands, 0)),
                    pl.BlockSpec((None, nf, rows, tf), lambda i: (i // bands, 0, i % bands, 0)),
                    jax.ShapeDtypeStruct((ne, nf, d, tf), bf16), body)


def _cast_job_rows(src, steps):
    ne, ff, d = src.shape
    rows = ne * ff // steps

    def body(i_ref, o_ref):
        o_ref[...] = i_ref[...].astype(bf16)

    return _CastJob(src.reshape(ne * ff, d), pl.BlockSpec((rows, d), lambda i: (i, 0)),
                    pl.BlockSpec((rows, d), lambda i: (i, 0)), jax.ShapeDtypeStruct((ne * ff, d), bf16), body)


def _call_with_job(kernel_fn, job, *, out_shape, in_specs, out_specs, args, **kw):
    if job is None:
        return pl.pallas_call(kernel_fn, out_shape=out_shape, in_specs=in_specs, out_specs=out_specs, **kw)(*args)
    n_in, n_out = len(in_specs), len(out_shape)

    def with_job(*refs):
        kernel_fn(*refs[:n_in], *refs[n_in + 1:n_in + 1 + n_out], *refs[n_in + 2 + n_out:])
        job.body(refs[n_in], refs[n_in + 1 + n_out])

    return pl.pallas_call(with_job, out_shape=(*out_shape, job.out_shape), in_specs=[*in_specs, job.in_spec],
                          out_specs=(*out_specs, job.out_spec), **kw)(*args, job.src)


def _inproj_kernel(xp_ref, xs_ref, mod_ref, w_ref, z_ref, g_ref, wm_s, wg_s, *, npt, gate_cols):
    o3, o4 = gate_cols
    n_main = wm_s.shape[1]

    @pl.when(pl.program_id(0) == 0)
    def _():
        wm_s[:, :o3] = w_ref[:, :o3].astype(bf16)
        wm_s[:, o3:] = w_ref[:, o4:].astype(bf16)
        wg_s[...] = jnp.zeros_like(wg_s)
        wg_s[:, :o4 - o3] = w_ref[:, o3:o4].astype(bf16)

    h = _modulate(_pick(xp_ref, xs_ref, npt), mod_ref, 0).astype(bf16)
    step = INPROJ_COL_CHUNK
    for j in range(n_main // step):
        z_ref[:, j * step:(j + 1) * step] = _dot(h, wm_s[:, j * step:(j + 1) * step]).astype(bf16)
    g_ref[...] = _dot(h, wg_s[...])


def _inproj(xp, xs, mod, w, gate_cols, n_prompt_rows, dec_seq, job=None):
    d = xp.shape[1]
    r = xp.shape[0] + xs.shape[0]
    tm = ROW_TILE
    npt = n_prompt_rows // tm
    n_main = w.shape[1] - (gate_cols[1] - gate_cols[0])
    grp = functools.partial(_group_of_tile, tm=tm, n_prompt_rows=n_prompt_rows, dec_seq=dec_seq)
    return _call_with_job(
        functools.partial(_inproj_kernel, npt=npt, gate_cols=gate_cols), job(r // tm) if job else None,
        out_shape=(jax.ShapeDtypeStruct((r, n_main), bf16), jax.ShapeDtypeStruct((r, LANES), f32)),
        grid=(r // tm,),
        in_specs=[
            *_part_specs(tm, d, npt),
            pl.BlockSpec((None, 6, d), lambda i: (grp(i), 0, 0)),
            _const_spec(w.shape),
        ],
        out_specs=(pl.BlockSpec((tm, n_main), lambda i: (i, 0)), pl.BlockSpec((tm, LANES), lambda i: (i, 0))),
        args=(xp, xs, mod, w),
        scratch_shapes=[pltpu.VMEM((d, n_main), bf16), pltpu.VMEM((d, LANES), bf16)],
        compiler_params=_cparams("arbitrary"),
        name="inproj_ab",
    )


def _conv_silu_chunk(x_ref, c, nc, w_ref, b_ref):
    lc, pa = M_CHUNK, PIECE_ALIGN
    t = nc * lc
    x = x_ref[pl.ds(pl.multiple_of(c * lc, lc), lc), :].astype(f32)
    before = x_ref[pl.ds(pl.multiple_of(jnp.maximum(c * lc - pa, 0), pa), pa), :].astype(f32)[pa - 1:pa, :]
    after = x_ref[pl.ds(pl.multiple_of(jnp.minimum((c + 1) * lc, t - pa), pa), pa), :].astype(f32)[0:1, :]
    before = jnp.where(c > 0, before, 0.0)
    after = jnp.where(c < nc - 1, after, 0.0)
    row = lax.broadcasted_iota(jnp.int32, (lc, 1), 0)
    prev = jnp.where(row == 0, before, pltpu.roll(x, 1, 0))
    nxt = jnp.where(row == lc - 1, after, pltpu.roll(x, lc - 1, 0))
    y = b_ref[...] + prev * w_ref[0:1, :] + x * w_ref[1:2, :] + nxt * w_ref[2:3, :]
    return _silu(y)


def _mlstm_kernel(q_ref, k_ref, v_ref, o_ref, g_ref, gb_ref, cw_ref, cb_ref, ng_ref, *rest, t, has_state):
    if has_state:
        c0_ref, n0_ref, m0_ref, *rest = rest
    hm_ref, c_out, n_out, m_out, qt_s, kh_s, kl_s, vt_s, gs_s, hft_s, hbt_s, ct_s, sel_s = rest
    nc = t // M_CHUNK
    lc = M_CHUNK
    nh = M_HEADS
    w = M_W
    sel_row = lax.broadcasted_iota(jnp.int32, (3 * LANES, LANES), 0)
    for d in range(2):
        for h in range(nh):
            ci, ln = d * nh + h, 2 * nh * d + h
            ct_s[ci] = c0_ref[d, h].T if has_state else jnp.zeros((M_DIM, M_DIM), f32)
            sel_s[ci] = jnp.where(sel_row == ln, 1.0, jnp.where(sel_row == LANES + ln, 1.0, jnp.where(
                sel_row == 2 * LANES + ln, 1.0, 0.0))).astype(bf16)
    if has_state:
        n_out[...] = n0_ref[...]
        m_out[...] = m0_ref[...]
    else:
        n_out[...] = jnp.zeros_like(n_out)
        m_out[...] = jnp.zeros_like(m_out)

    lane = lax.broadcasted_iota(jnp.int32, (1, LANES), 1)
    is_lf = ((lane >= nh) & (lane < 2 * nh)) | ((lane >= 3 * nh) & (lane < 4 * nh))
    is_bw_lf = (lane >= 3 * nh) & (lane < 4 * nh)

    rr = lax.broadcasted_iota(jnp.int32, (lc, lc), 0)
    cc = lax.broadcasted_iota(jnp.int32, (lc, lc), 1)
    tri_incl = jnp.where(rr >= cc, 1.0, 0.0).astype(bf16)

    def prep_body(c, _):
        off = pl.multiple_of(c * lc, lc)
        g = g_ref[pl.ds(off, lc), :] + gb_ref[...]
        tile = jnp.where(is_lf, _log_sigmoid(g), g)
        hi, mid, lo = _split3(jnp.where(is_lf, tile, 0.0))
        cs = _dot(tri_incl, hi) + _dot(tri_incl, mid) + _dot(tri_incl, lo)
        total = jnp.broadcast_to(cs[lc - 1:lc, :], (lc, LANES))
        b = jnp.where(is_bw_lf, total - cs + tile, cs)
        lmb = tile - pltpu.roll(b, LANES - nh, 1)
        gv = pltpu.roll(total, LANES - nh, 1) + lmb
        low = jnp.where(is_lf, b, lmb)
        high = pltpu.roll(jnp.where(is_lf, total, gv), 4 * nh, 1)
        gs_s[pl.ds(off, lc), :] = jnp.where(lane < 4 * nh, low, jnp.where(lane < 8 * nh, high, 0.0))
        wide = pl.ds(pl.multiple_of(c * w, w), w)
        qt_s[wide, :] = _conv_silu_chunk(q_ref, c, nc, cw_ref[:, :w], cb_ref[:, :w]).T
        kf = _conv_silu_chunk(k_ref, c, nc, cw_ref[:, w:], cb_ref[:, w:]) * (M_DIM ** -0.5)
        kh = kf.astype(bf16)
        kh_s[pl.ds(off, lc), :] = kh
        kl_s[pl.ds(off, lc), :] = (kf - kh.astype(f32)).astype(bf16)
        vt_s[wide, :] = v_ref[pl.ds(off, lc), :].astype(f32).T.astype(bf16)
        return 0

    lax.fori_loop(0, nc, prep_body, 0)

    sub8 = lax.broadcasted_iota(jnp.int32, (8, LANES), 0)

    def two_rows(x):
        hi = x.astype(bf16).astype(f32)
        return jnp.where(sub8 == 0, hi, jnp.where(sub8 == 1, x - hi, 0.0)).astype(bf16)

    def issue(c, h, d, tile3, tile_t):
        off = pl.multiple_of(c * lc, lc)
        col = slice(h * M_DIM, (h + 1) * M_DIM)
        ci, ln = d * nh + h, 2 * nh * d + h
        head = pl.ds(pl.multiple_of(c * w + h * M_DIM, M_DIM), M_DIM)
        b_row = tile_t[ln + nh:ln + nh + 1, :]
        g_row = tile_t[ln + 4 * nh:ln + 4 * nh + 1, :]
        total = tile_t[ln + 5 * nh:ln + 5 * nh + 1, 0:1]
        ct, nm, mm = ct_s[ci], n_out[d, h], m_out[d, h]
        qt = qt_s[head, :]
        qtb = qt.astype(bf16)
        khb = kh_s[pl.ds(off, lc), col]
        vtb = vt_s[head, :]
        lmb = _dot(tile3, sel_s[ci])
        sraw = _dot(khb, qtb)
        qn = _dot(two_rows(nm), qtb)
        lhs = jnp.concatenate([ct.astype(bf16), vtb], axis=1)
        m_new = jnp.maximum(mm + total, jnp.max(g_row, axis=1, keepdims=True))
        decay = jnp.exp(mm + total - m_new)
        ew = jnp.exp(g_row - m_new)
        ct_s[ci] = decay * ct + _dot(vtb * ew.astype(bf16), khb)
        ew2 = two_rows(ew)
        nk_h = _dot(ew2, khb)
        nk_l = _dot(ew2, kl_s[pl.ds(off, lc), col])
        n_out[d, h] = decay * nm + (nk_h[0:1, :] + nk_h[1:2, :] + nk_l[0:1, :])
        m_out[d, h] = m_new
        return dict(d=d, head=head, b_row=b_row, mm=mm, qt=qt, lmb=lmb, sraw=sraw, qn=qn, lhs=lhs)

    def weigh(st):
        keep = (cc >= rr) if st["d"] == 0 else (rr >= cc)
        dmat = jnp.where(keep, st["lmb"] + st["b_row"], -jnp.inf)
        inter = st["mm"] + st["b_row"]
        mt = jnp.maximum(inter, jnp.max(dmat, axis=0, keepdims=True))
        w_inter = jnp.exp(inter - mt)
        s = st["sraw"] * jnp.exp(dmat - mt)
        qn = st["qn"]
        den = w_inter * (qn[0:1, :] + qn[1:2, :]) + jnp.sum(s, axis=0, keepdims=True)
        inv = 1.0 / jnp.maximum(jnp.abs(den), jnp.exp(-mt))
        return jnp.concatenate([st["qt"] * (w_inter * inv), s * inv], axis=0).astype(bf16)

    def body(i, _):
        states = []
        for d in range(2):
            c = i if d == 0 else nc - 1 - i
            tile = gs_s[pl.ds(pl.multiple_of(c * lc, lc), lc), :]
            tile_t = tile.T
            hi, mid, lo = _split3(tile)
            tile3 = jnp.concatenate([hi, mid, lo], axis=1)
            states += [issue(c, h, d, tile3, tile_t) for h in range(nh)]
        rhss = [weigh(st) for st in states]
        for st, rhs in zip(states, rhss):
            hct = _dot(st["lhs"], rhs)
            if st["d"] == 0:
                hft_s[st["head"], :] = hct
            else:
                hbt_s[st["head"], :] = hct
        return 0

    lax.fori_loop(0, nc, body, 0)

    def out_body(c, _):
        rows = pl.ds(pl.multiple_of(c * lc, lc), lc)
        wide = pl.ds(pl.multiple_of(c * w, w), w)
        hsum = (hft_s[wide, :] + hbt_s[wide, :]).T
        for h in range(nh):
            col = slice(h * M_DIM, (h + 1) * M_DIM)
            mo = _sigmoid(o_ref[rows, col].astype(f32))
            hm_ref[rows, col] = (_rms(hsum[:, col]) * ng_ref[:, col] * mo).astype(hm_ref.dtype)
        return 0

    lax.fori_loop(0, nc, out_body, 0)
    for d in range(2):
        for h in range(nh):
            c_out[d, h] = ct_s[d * nh + h].T


def _mlstm(z, gates, gate_b, conv_w, conv_b, m_norm_g, state0, *, batch, t, row_block0):
    nh = M_HEADS
    w = M_W
    mode = dict(pipeline_mode=pl.Buffered(1)) if t * w * 2 > (1 << 20) else {}
    seq = lambda colblk: pl.BlockSpec((t, w), lambda b: (row_block0 + b, colblk), **mode)
    state = lambda *tail: pl.BlockSpec((None, 2, nh) + tail, lambda b: (b,) + (0,) * (2 + len(tail)))
    state_specs = [state(M_DIM, M_DIM), state(1, M_DIM), state(1, 1)]
    has_state = state0 is not None
    return pl.pallas_call(
        functools.partial(_mlstm_kernel, t=t, has_state=has_state),
        out_shape=(
            jax.ShapeDtypeStruct((batch * t, w), bf16),
            jax.ShapeDtypeStruct((batch, 2, nh, M_DIM, M_DIM), f32),
            jax.ShapeDtypeStruct((batch, 2, nh, 1, M_DIM), f32),
            jax.ShapeDtypeStruct((batch, 2, nh, 1, 1), f32),
        ),
        grid=(batch,),
        in_specs=[
            seq(0), seq(1), seq(2), seq(3),
            pl.BlockSpec((t, LANES), lambda b: (row_block0 + b, 0)),
            _const_spec(gate_b.shape),
            _const_spec(conv_w.shape), _const_spec(conv_b.shape), _const_spec(m_norm_g.shape),
            *(state_specs if has_state else []),
        ],
        out_specs=(pl.BlockSpec((t, w), lambda b: (b, 0)), *state_specs),
        scratch_shapes=[
            pltpu.VMEM((t * nh, M_DIM), f32),
            pltpu.VMEM((t, w), bf16),
            pltpu.VMEM((t, w), bf16),
            pltpu.VMEM((t * nh, M_DIM), bf16),
            pltpu.VMEM((t, LANES), f32),
            pltpu.VMEM((t * nh, M_DIM), f32),
            pltpu.VMEM((t * nh, M_DIM), f32),
            pltpu.VMEM((2 * nh, M_DIM, M_DIM), f32),
            pltpu.VMEM((2 * nh, 3 * LANES, LANES), bf16),
        ],
        compiler_params=_cparams("arbitrary"),
        name=f"mlstm_t{t}",
    )(z, z, z, z, gates, gate_b, conv_w, conv_b, m_norm_g, *(state0 if has_state else ()))


def _pair_norm(x, gain, on_mxu):
    sq = x * x
    if on_mxu:
        same_half = jnp.where((lax.broadcasted_iota(jnp.int32, (LANES, LANES), 0) < A_DIM)
                              == (lax.broadcasted_iota(jnp.int32, (LANES, LANES), 1) < A_DIM), 1.0, 0.0).astype(bf16)
        hi = sq.astype(bf16)
        lo = (sq - hi.astype(f32)).astype(bf16)
        s = _dot(hi, same_half) + _dot(lo, same_half)
        return x * lax.rsqrt(s * (1.0 / A_DIM) + EPS) * gain
    lane = lax.broadcasted_iota(jnp.int32, (1, LANES), 1)
    first = lane < A_DIM
    s_all = jnp.sum(sq, axis=-1, keepdims=True)
    s0 = jnp.sum(jnp.where(first, sq, 0.0), axis=-1, keepdims=True)
    inv0 = lax.rsqrt(s0 * (1.0 / A_DIM) + EPS)
    inv1 = lax.rsqrt((s_all - s0) * (1.0 / A_DIM) + EPS)
    return x * jnp.where(first, inv0, inv1) * gain


def _rope(x, cos, sin_signed):
    lane = lax.broadcasted_iota(jnp.int32, (1, LANES), 1)
    nf = A_DIM // 4
    partner = jnp.where((lane % (2 * nf)) < nf, pltpu.roll(x, LANES - nf, 1), pltpu.roll(x, nf, 1))
    return x * cos + partner * sin_signed


def _attn_kernel(*refs, t, tq, rope, ctx, out_scale):
    it = iter(refs)
    lam_ref = next(it)
    q_ref, k_ref, v_ref = next(it), next(it), next(it)
    qg_ref, kg_ref, ag_ref = next(it), next(it), next(it)
    if rope:
        cosq_ref, sinq_ref, cosk_ref, sink_ref = next(it), next(it), next(it), next(it)
    if ctx:
        kct_ref, vc_ref = next(it), next(it)
    ha_ref = next(it)
    if not ctx:
        newk_ref, newv_ref = next(it), next(it)
    kt_s = next(it)
    nh = A_HEADS

    @pl.when(pl.program_id(1) == 0)
    def _():
        for h in range(nh):
            col = slice(h * LANES, (h + 1) * LANES)
            kn = _pair_norm(k_ref[:, col].astype(f32), kg_ref[...], True)
            if not ctx:
                newk_ref[h, 0] = kn[:, :A_DIM]
                newk_ref[h, 1] = kn[:, A_DIM:]
                newv_ref[h] = v_ref[:, col].astype(f32)
            if rope:
                kn = _rope(kn, cosk_ref[...], sink_ref[...])
            kt_s[col, :] = kn.T.astype(bf16)

    lane = lax.broadcasted_iota(jnp.int32, (1, LANES), 1)
    lam = lam_ref[0]
    for h in range(nh):
        col = slice(h * LANES, (h + 1) * LANES)
        q = _pair_norm(q_ref[:, col].astype(f32), qg_ref[...], not ctx)
        if rope:
            q = _rope(q, cosq_ref[...], sinq_ref[...])
        q = q * (A_DIM ** -0.5 * math.log2(math.e))
        qs = [jnp.where(lane < A_DIM, q, 0.0).astype(bf16), jnp.where(lane >= A_DIM, q, 0.0).astype(bf16)]
        kt = kt_s[col, :]
        vb = v_ref[:, col]
        if ctx:
            kctb = kct_ref[h].astype(bf16)
            vcb = vc_ref[h].astype(bf16)
        outs = []
        for i in range(2):
            sn = _dot(qs[i], kt)
            mx = jnp.max(sn, axis=-1, keepdims=True)
            if ctx:
                sc = _dot(qs[i], kctb)
                mx = jnp.maximum(mx, jnp.max(sc, axis=-1, keepdims=True))
            en = jnp.exp2(sn - mx)
            den = jnp.sum(en, axis=-1, keepdims=True)
            o = _dot(en.astype(bf16), vb)
            if ctx:
                ec = jnp.exp2(sc - mx)
                den = den + jnp.sum(ec, axis=-1, keepdims=True)
                o = o + _dot(ec.astype(bf16), vcb)
            outs.append(o * (1.0 / den))
        o = outs[0] - lam * outs[1]
        ha_ref[:, col] = (_rms(o) * ag_ref[...] * out_scale).astype(ha_ref.dtype)


def _attn(z, lam, qg2, kg2, a_norm_g, *, batch, t, row_block0, rope_tabs=None, ctx_kv=None, out_scale):
    nh = A_HEADS
    w = A_W
    tq = min(t, ATTN_Q_BLOCK)
    nq = t // tq
    rope = rope_tabs is not None
    ctx = ctx_kv is not None
    qblk = 4 * M_W // w
    vec = pl.BlockSpec((1, LANES), lambda b, i: (0, 0))
    in_specs = [
        pl.BlockSpec(memory_space=pltpu.SMEM),
        pl.BlockSpec((tq, w), lambda b, i: ((row_block0 + b) * nq + i, qblk)),
        pl.BlockSpec((t, w), lambda b, i: (row_block0 + b, qblk + 1)),
        pl.BlockSpec((t, w), lambda b, i: (row_block0 + b, qblk + 2)),
        vec, vec, vec,
    ]
    args = [lam, z, z, z, qg2, kg2, a_norm_g]
    if rope:
        cos, sin = rope_tabs
        in_specs += [pl.BlockSpec((tq, LANES), lambda b, i: (i, 0))] * 2
        in_specs += [pl.BlockSpec((t, LANES), lambda b, i: (0, 0))] * 2
        args += [cos, sin, cos, sin]
    if ctx:
        kct, vc = ctx_kv
        in_specs += [pl.BlockSpec((None,) + kct.shape[1:], lambda b, i: (b, 0, 0, 0)),
                     pl.BlockSpec((None,) + vc.shape[1:], lambda b, i: (b, 0, 0, 0))]
        args += [kct, vc]
    out_shape = [jax.ShapeDtypeStruct((batch * t, w), bf16)]
    out_specs = [pl.BlockSpec((tq, w), lambda b, i: (b * nq + i, 0))]
    if not ctx:
        out_shape += [jax.ShapeDtypeStruct((batch, nh, 2, t, A_DIM), f32),
                      jax.ShapeDtypeStruct((batch, nh, t, A_VDIM), f32)]
        out_specs += [pl.BlockSpec((None, nh, 2, t, A_DIM), lambda b, i: (b, 0, 0, 0, 0)),
                      pl.BlockSpec((None, nh, t, A_VDIM), lambda b, i: (b, 0, 0, 0))]
    return pl.pallas_call(
        functools.partial(_attn_kernel, t=t, tq=tq, rope=rope, ctx=ctx, out_scale=out_scale),
        out_shape=tuple(out_shape),
        grid=(batch, nq),
        in_specs=in_specs,
        out_specs=tuple(out_specs),
        scratch_shapes=[pltpu.VMEM((w, t), bf16)],
        compiler_params=_cparams("arbitrary", "arbitrary"),
        name=f"diff_attn_t{t}",
    )(*args)


def _rope_tables(t):
    rows = t // GRID_W
    pos_row = np.repeat(np.arange(rows, dtype=np.float64), GRID_W)
    pos_col = (np.arange(rows * GRID_W) % GRID_W).astype(np.float64)
    nf = A_DIM // 4
    inv = ROPE_THETA ** (-np.arange(nf, dtype=np.float64) / nf)
    lane = np.arange(LANES)
    j = lane % (2 * nf)
    use_col = (lane % A_DIM) >= (A_DIM // 2)
    ang = np.where(use_col[None, :], pos_col[:, None], pos_row[:, None]) * inv[j % nf][None, :]
    sign = np.where(j < nf, -1.0, 1.0)[None, :]
    return jnp.asarray(np.cos(ang), f32), jnp.asarray(np.sin(ang) * sign, f32)


def _ffn_kernel(xp_ref, xs_ref, hmp_ref, hms_ref, hap_ref, has_ref, mod_ref, wo_ref, w1_ref, w3_ref, w2_ref, o_ref,
                *, chunks, npt):
    half = hmp_ref.shape[1]
    hm = _pick(hmp_ref, hms_ref, npt)
    ha = _pick(hap_ref, has_ref, npt)
    x = _pick(xp_ref, xs_ref, npt) + mod_ref[2:3, :] * (_dot(hm, wo_ref[:half, :]) + _dot(ha, wo_ref[half:, :]))
    h = _modulate(x, mod_ref, 3).astype(bf16)
    acc = jnp.zeros(x.shape, f32)
    for lo, hi in chunks:
        ab = _dot(h, jnp.concatenate([w1_ref[:, lo:hi], w3_ref[:, lo:hi]], axis=1))
        a = _silu(ab[:, :hi - lo]) * ab[:, hi - lo:]
        acc = acc + _dot(a.astype(bf16), w2_ref[lo:hi, :])
    o_ref[...] = x + mod_ref[5:6, :] * acc


def _ffn(xp, xs, hm_p, hm_s, ha_p, ha_s, mod, w_out, w1, w3, w2, n_prompt_rows, dec_seq, job=None):
    d = xp.shape[1]
    r = xp.shape[0] + xs.shape[0]
    tm = ROW_TILE
    npt = n_prompt_rows // tm
    grp = functools.partial(_group_of_tile, tm=tm, n_prompt_rows=n_prompt_rows, dec_seq=dec_seq)
    ff = w1.shape[1]
    cut = MXU_COLS * ((ff // MXU_COLS + 1) // 2) if ff % MXU_COLS == 0 else ff
    chunks = ((0, cut), (cut, ff)) if cut < ff else ((0, ff),)
    return _call_with_job(
        functools.partial(_ffn_kernel, chunks=chunks, npt=npt), job(r // tm) if job else None,
        out_shape=(jax.ShapeDtypeStruct((r, d), f32),),
        grid=(r // tm,),
        in_specs=[
            *_part_specs(tm, d, npt),
            *_part_specs(tm, hm_p.shape[1], npt),
            *_part_specs(tm, ha_p.shape[1], npt),
            pl.BlockSpec((None, 6, d), lambda i: (grp(i), 0, 0)),
            _const_spec(w_out.shape), _const_spec(w1.shape), _const_spec(w3.shape), _const_spec(w2.shape),
        ],
        out_specs=(pl.BlockSpec((tm, d), lambda i: (i, 0)),),
        args=(xp, xs, hm_p, hm_s, ha_p, ha_s, mod, w_out, w1, w3, w2),
        compiler_params=_cparams("arbitrary"),
        name="outproj_ffn",
    )


def _gelu_tanh(x):
    k1 = -2.0 * math.sqrt(2.0 / math.pi) * math.log2(math.e)
    k3 = k1 * 0.044715
    return x / (1.0 + jnp.exp2(x * (k1 + k3 * (x * x))))


def _gmlp_kernel(x_ref, mod_ref, win_ref, bin_ref, lng_ref, lnb_ref, ws_ref, bs_ref, wout_ref, o_ref, us_s):
    x = x_ref[...]
    tm = x.shape[0]
    e = wout_ref.shape[0]
    ge = e // C_GROUPS
    h = _modulate(x, mod_ref, 0).astype(bf16)
    u = _gelu_tanh(_dot(h, win_ref[:, :e]) + bin_ref[:, :e])
    v = _gelu_tanh(_dot(h, win_ref[:, e:]) + bin_ref[:, e:])
    mu = jnp.mean(v, axis=-1, keepdims=True)
    vc = v - mu
    var = jnp.mean(vc * vc, axis=-1, keepdims=True)
    vn = (vc * lax.rsqrt(var + EPS) * lng_ref[...] + lnb_ref[...]).astype(bf16)
    for n in range(tm // C_CHUNK):
        rows = slice(n * C_CHUNK, (n + 1) * C_CHUNK)
        for g in range(C_GROUPS):
            cols = slice(g * ge, (g + 1) * ge)
            s = _dot(ws_ref[g], vn[rows, cols]) + bs_ref[:, cols]
            us_s[rows, cols] = (u[rows, cols] * s).astype(bf16)
    o_ref[...] = x + mod_ref[2:3, :] * _dot(us_s[...], wout_ref[...])


def _gmlp(x, mod, w_in, b_in, ln_g, ln_b, ws, bs_full, w_out, n_prompt_rows, dec_seq, job=None):
    r, d = x.shape
    tm = ROW_TILE
    e = w_out.shape[0]
    grp = functools.partial(_group_of_tile, tm=tm, n_prompt_rows=n_prompt_rows, dec_seq=dec_seq)
    return _call_with_job(
        _gmlp_kernel, job(r // tm) if job else None,
        out_shape=(jax.ShapeDtypeStruct((r, d), f32),),
        grid=(r // tm,),
        in_specs=[
            pl.BlockSpec((tm, d), lambda i: (i, 0)),
            pl.BlockSpec((None, 6, d), lambda i: (grp(i), 0, 0)),
            _const_spec(w_in.shape), _const_spec(b_in.shape), _const_spec(ln_g.shape), _const_spec(ln_b.shape),
            _const_spec(ws.shape), _const_spec(bs_full.shape), _const_spec(w_out.shape),
        ],
        out_specs=(pl.BlockSpec((tm, d), lambda i: (i, 0)),),
        args=(x, mod, w_in, b_in, ln_g, ln_b, ws, bs_full, w_out),
        scratch_shapes=[pltpu.VMEM((tm, e), bf16)],
        compiler_params=_cparams("arbitrary"),
        name="gmlp",
    )


def _router_kernel(x_ref, mod_ref, wr_hi_ref, wr_lo_ref, h_ref, meta_ref, counts_ref):
    hf = _modulate(x_ref[...], mod_ref, 3)
    hb = hf.astype(bf16)
    h_ref[...] = hb
    h_lo = (hf - hb.astype(f32)).astype(bf16)
    logits = _dot(hb, wr_hi_ref[...]) + (_dot(hb, wr_lo_ref[...]) + _dot(h_lo, wr_hi_ref[...]))
    lane = lax.broadcasted_iota(jnp.int32, logits.shape, 1).astype(f32)
    logits = jnp.where(lane < N_EXPERTS, logits, -jnp.inf)
    m1 = jnp.max(logits, axis=-1, keepdims=True)
    i1 = jnp.min(jnp.where(logits == m1, lane, float(LANES)), axis=-1, keepdims=True)
    rest = jnp.where(lane == i1, -jnp.inf, logits)
    m2 = jnp.max(rest, axis=-1, keepdims=True)
    i2 = jnp.min(jnp.where(rest == m2, lane, float(LANES)), axis=-1, keepdims=True)
    e2 = jnp.exp(m2 - m1)
    w1 = 1.0 / (1.0 + e2)
    w2 = e2 * w1

    tm = logits.shape[0]
    cnt = jnp.where(lane == i1, 1.0, jnp.where(lane == i2, 1.0, 0.0))
    rr = lax.broadcasted_iota(jnp.int32, (tm, tm), 0)
    cc = lax.broadcasted_iota(jnp.int32, (tm, tm), 1)
    before = jnp.where(rr > cc, 1.0, 0.0).astype(bf16)
    rank = _dot(before, cnt.astype(bf16))
    counts = jnp.sum(cnt, axis=0, keepdims=True)
    padded = jnp.floor((counts + (PIECE_ALIGN - 1)) * (1.0 / PIECE_ALIGN)) * PIECE_ALIGN
    lane1 = lane[0:1, :]
    piece_off = jnp.zeros((1, LANES), f32)
    off = jnp.zeros((1, 1), f32)
    for e in range(N_EXPERTS):
        piece_off = jnp.where(lane1 == e, off, piece_off)
        off = off + padded[:, e:e + 1]
    local = piece_off + rank
    pos1 = jnp.sum(jnp.where(lane == i1, local, 0.0), axis=-1, keepdims=True)
    pos2 = jnp.sum(jnp.where(lane == i2, local, 0.0), axis=-1, keepdims=True)
    meta_ref[...] = jnp.where(lane == 0, pos1, jnp.where(lane == 1, pos2, jnp.where(lane == 2, w1,
                              jnp.where(lane == 3, w2, 0.0))))
    counts_ref[...] = counts


def _router(x, mod, wr_hi, wr_lo, n_prompt_rows, dec_seq):
    r, d = x.shape
    tm = MOE_SORT_BLOCK
    grp = functools.partial(_group_of_tile, tm=tm, n_prompt_rows=n_prompt_rows, dec_seq=dec_seq)
    return pl.pallas_call(
        _router_kernel,
        out_shape=(jax.ShapeDtypeStruct((r, d), bf16), jax.ShapeDtypeStruct((r, LANES), f32),
                   jax.ShapeDtypeStruct((r // tm, 1, LANES), f32)),
        grid=(r // tm,),
        in_specs=[
            pl.BlockSpec((tm, d), lambda i: (i, 0)),
            pl.BlockSpec((None, 6, d), lambda i: (grp(i), 0, 0)),
            _const_spec(wr_hi.shape), _const_spec(wr_lo.shape),
        ],
        out_specs=(pl.BlockSpec((tm, d), lambda i: (i, 0)), pl.BlockSpec((tm, LANES), lambda i: (i, 0)),
                   pl.BlockSpec((None, 1, LANES), lambda i: (i, 0, 0))),
        compiler_params=_cparams("arbitrary"),
        name="router",
    )(x, mod, wr_hi, wr_lo)


def _moe_kernel(n16_ref, loc_ref, dst_ref, seg_off_ref, seg_len_ref, csel_ref, cnum_ref,
                h_ref, meta_ref, x_ref, mod_ref, w1_ref, w3_ref, w2_ref, op_ref, os_ref,
                hs_s, ys_s, loc_s, *, nsub, ne, nf, chunks, nsb_prompt):
    sb = pl.program_id(0)
    p = pl.program_id(1)
    n_exp = ne * nf
    loc_rows = loc_s.shape[0]
    pa = PIECE_ALIGN

    def one_hot_cols(meta, v1, v2):
        lane = lax.broadcasted_iota(jnp.int32, (1, loc_rows), 1).astype(f32)
        return jnp.where(lane == meta[:, 0:1], v1, jnp.where(lane == meta[:, 1:2], v2, 0.0)).astype(bf16)

    def copy_pieces(blk, to_sorted):
        for e in range(ne):
            n = n16_ref[blk * ne + e]
            src = loc_ref[blk * ne + e]
            dst = dst_ref[blk * ne + e]

            def cp(i, _, src=src, dst=dst):
                a = pl.ds(pl.multiple_of(src + pa * i, pa), pa)
                b = pl.ds(pl.multiple_of(dst + pa * i, pa), pa)
                if to_sorted:
                    hs_s[b, :] = loc_s[a, :]
                else:
                    loc_s[a, :] = ys_s[b, :].astype(bf16)
                return 0

            lax.fori_loop(0, n, cp, 0)

    @pl.when(p < nsub)
    def _():
        @pl.when(p == 0)
        def _():
            hs_s[...] = jnp.zeros_like(hs_s)
            ys_s[...] = jnp.zeros_like(ys_s)

        pt = one_hot_cols(meta_ref[...], 1.0, 1.0)
        loc_s[...] = _dot_tn(pt, h_ref[...]).astype(bf16)
        copy_pieces(sb * nsub + p, True)

    @pl.when((p >= nsub) & (p < nsub + n_exp))
    def _():
        e = lax.div(p - nsub, jnp.int32(nf))
        start = seg_off_ref[sb * ne + e]
        ln = seg_len_ref[sb * ne + e]
        which = csel_ref[sb * ne + e]
        count = cnum_ref[sb * ne + e]

        def chunk(lo, size):
            r0 = jnp.minimum(lo, hs_s.shape[0] - size)
            rows = pl.ds(pl.multiple_of(r0, pa), size)
            xc = hs_s[rows, :]
            ab = _dot(xc, jnp.concatenate([w1_ref[...], w3_ref[...]], axis=1))
            tf = w1_ref.shape[1]
            a = _silu(ab[:, :tf]) * ab[:, tf:]
            y = _dot(a.astype(bf16), w2_ref[...])
            ri = r0 + lax.broadcasted_iota(jnp.int32, (size, 1), 0)
            ys_s[rows, :] += jnp.where(ri >= lo, jnp.where(ri < start + ln, y, 0.0), 0.0)

        for k, size in enumerate(chunks):
            @pl.when(which == k)
            def _(size=size):
                def body(j, _):
                    chunk(start + j * size, size)
                    return 0

                lax.fori_loop(0, count, body, 0)

    @pl.when(p >= nsub + n_exp)
    def _():
        copy_pieces(sb * nsub + (p - nsub - n_exp), False)
        meta = meta_ref[...]
        a = one_hot_cols(meta, meta[:, 2:3], meta[:, 3:4])
        y = x_ref[...] + mod_ref[5:6, :] * _dot(a, loc_s[...])

        @pl.when(sb < nsb_prompt)
        def _():
            op_ref[...] = y

        @pl.when(sb >= nsb_prompt)
        def _():
            os_ref[...] = y


def _moe(x, h, meta, counts, mod, w1, w3, w2, n_prompt_rows, dec_seq):
    r, d = x.shape
    ne, ff, _ = w2.shape
    assert w1.shape == (ne, ff // MOE_FF_TILE, d, MOE_FF_TILE)
    t_super, tb, chunks, tf, pa = MOE_SUPER_BLOCK, MOE_SORT_BLOCK, MOE_CHUNKS, MOE_FF_TILE, PIECE_ALIGN
    ch = max(chunks)
    nsub = t_super // tb
    nsb = r // t_super
    npb = n_prompt_rows // tb
    nf = ff // tf
    n_exp = ne * nf
    loc_rows = 2 * tb + LANES
    assert loc_rows >= 2 * tb + ne * (pa - 1)
    max_rows = 2 * t_super + nsub * ne * (pa - 1)
    sort_rows = LANES * (-(-max_rows // LANES))
    assert all(size % pa == 0 and size <= sort_rows for size in chunks)

    cnt = counts[:, 0, :ne].astype(jnp.int32)
    n16 = (cnt + (pa - 1)) // pa
    loc = pa * (jnp.cumsum(n16, axis=1) - n16)
    n16_sb = n16.reshape(nsb, nsub, ne)
    seg_len = pa * jnp.sum(n16_sb, axis=1)
    seg_off = jnp.cumsum(seg_len, axis=1) - seg_len
    dst = seg_off[:, None, :] + pa * (jnp.cumsum(n16_sb, axis=1) - n16_sb)
    cnum = (seg_len + ch - 1) // ch
    need = pa * ((seg_len // pa + jnp.maximum(cnum, 1) - 1) // jnp.maximum(cnum, 1))
    csel = sum((need > size).astype(jnp.int32) for size in chunks[:-1])
    scalars = [a.reshape(-1).astype(jnp.int32) for a in (n16, loc, dst, seg_off, seg_len, csel, cnum)]

    grp = functools.partial(_group_of_tile, tm=t_super, n_prompt_rows=n_prompt_rows, dec_seq=dec_seq)

    def exp_step(p):
        return jnp.clip(p - nsub, 0, n_exp - 1)

    def tok_blk(sb, s):
        return sb * nsub + jnp.clip(s, 0, nsub - 1)

    grid_spec = pltpu.PrefetchScalarGridSpec(
        num_scalar_prefetch=len(scalars),
        grid=(nsb, nsub + n_exp + nsub),
        in_specs=[
            pl.BlockSpec((tb, d), lambda sb, p, *_: (tok_blk(sb, p), 0)),
            pl.BlockSpec((tb, LANES), lambda sb, p, *_: (sb * nsub + jnp.where(p < nsub, p, jnp.clip(p - nsub - n_exp, 0, nsub - 1)), 0)),
            pl.BlockSpec((tb, d), lambda sb, p, *_: (tok_blk(sb, p - nsub - n_exp), 0)),
            pl.BlockSpec((None, 6, d), lambda sb, p, *_: (grp(sb), 0, 0)),
            pl.BlockSpec((None, None, d, tf), lambda sb, p, *_: (exp_step(p) // nf, exp_step(p) % nf, 0, 0)),
            pl.BlockSpec((None, None, d, tf), lambda sb, p, *_: (exp_step(p) // nf, exp_step(p) % nf, 0, 0)),
            pl.BlockSpec((None, tf, d), lambda sb, p, *_: (exp_step(p) // nf, exp_step(p) % nf, 0)),
        ],
        out_specs=(
            pl.BlockSpec((tb, d), lambda sb, p, *_: (jnp.minimum(tok_blk(sb, p - nsub - n_exp), npb - 1), 0)),
            pl.BlockSpec((tb, d), lambda sb, p, *_: (jnp.maximum(tok_blk(sb, p - nsub - n_exp) - npb, 0), 0)),
        ),
        scratch_shapes=[
            pltpu.VMEM((sort_rows, d), bf16),
            pltpu.VMEM((sort_rows, d), f32),
            pltpu.VMEM((loc_rows, d), bf16),
        ],
    )
    return pl.pallas_call(
        functools.partial(_moe_kernel, nsub=nsub, ne=ne, nf=nf, chunks=chunks,
                          nsb_prompt=n_prompt_rows // t_super),
        out_shape=(jax.ShapeDtypeStruct((n_prompt_rows, d), f32), jax.ShapeDtypeStruct((r - n_prompt_rows, d), f32)),
        grid_spec=grid_spec,
        compiler_params=_cparams("arbitrary", "arbitrary"),
        name="moe_sparse",
    )(*scalars, h, meta, x, mod, w1, w3, w2)


def kernel(x_prompt, x_sample, c, cache_dattn_k, cache_dattn_v, state_mlstm_c, state_mlstm_n, state_mlstm_m,
           c_ctx, w_ada, b_ada, w_in_ab, conv_w, conv_b, gate_b, qn_g, kn_g, lam_q1, lam_k1, lam_q2, lam_k2,
           m_norm_g, a_norm_g, w_out_ab, ff_w1, ff_w3, ff_w2, w_in_c, b_in_c, c_ln_g, c_ln_b, c_ws, c_bs,
           w_out_c, w_router, ex_w1, ex_w3, ex_w2):
    bp, seq, d = x_prompt.shape
    bs, dec_seq, _ = x_sample.shape
    depth = w_ada.shape[0]
    n_prompt_rows = bp * seq
    assert n_prompt_rows % dec_seq == 0 and seq % M_CHUNK == 0 and dec_seq % MOE_SUPER_BLOCK == 0
    nh = M_HEADS

    mods = _ada_table(jnp.concatenate([c_ctx[None], c], axis=0), w_ada, b_ada)
    x = (x_prompt.reshape(n_prompt_rows, d), x_sample.reshape(bs * dec_seq, d))
    rows = (n_prompt_rows, dec_seq)

    def joined(v):
        return jnp.concatenate(v, axis=0) if isinstance(v, tuple) else v

    def parts(v):
        return v if isinstance(v, tuple) else (v[:n_prompt_rows], v[n_prompt_rows:])

    new_k, new_v, new_c, new_n, new_m = [], [], [], [], []
    for l in range(depth):
        j = l // 2
        mod = mods[l]
        if l % 2 == 0:
            lam_init = 0.8 - 0.6 * math.exp(-0.3 * l)
            lam = (jnp.exp(jnp.sum((lam_q1[j] * lam_k1[j]).astype(f32)))
                   - jnp.exp(jnp.sum((lam_q2[j] * lam_k2[j]).astype(f32))) + lam_init).reshape(1)
            x = parts(x)
            nxt = j if l + 1 < depth else None
            z, gates, *ex2 = _inproj(*x, mod, w_in_ab[j], (4 * M_W, 4 * M_W + 4 * nh), *rows,
                                     job=None if nxt is None else functools.partial(_cast_job_rows, ex_w2[nxt]))
            gb = jnp.zeros((1, LANES), f32).at[0, :4 * nh].set(gate_b[j])
            mng = m_norm_g[j].reshape(1, M_W)
            cb = conv_b[j].reshape(1, 2 * M_W)
            hm_p, c_f, n_f, m_f = _mlstm(z, gates, gb, conv_w[j], cb, mng, None, batch=bp, t=seq, row_block0=0)
            state0 = (state_mlstm_c[:, j], state_mlstm_n[:, j].reshape(bs, 2, nh, 1, M_DIM),
                      state_mlstm_m[:, j].reshape(bs, 2, nh, 1, 1))
            hm_s, _, _, _ = _mlstm(z, gates, gb, conv_w[j], cb, mng, state0,
                                   batch=bs, t=dec_seq, row_block0=n_prompt_rows // dec_seq)
            new_c.append(c_f)
            new_n.append(n_f.reshape(bp, 2, nh, M_DIM))
            new_m.append(m_f.reshape(bp, 2, nh))

            qg2 = jnp.tile(qn_g[j], 2).reshape(1, LANES)
            kg2 = jnp.tile(kn_g[j], 2).reshape(1, LANES)
            ag = a_norm_g[j].reshape(1, LANES)
            ha_p, k_ctx, v_ctx = _attn(z, lam, qg2, kg2, ag, batch=bp, t=seq, row_block0=0,
                                       out_scale=1.0 - lam_init)
            kct = cache_dattn_k[:, j].transpose(0, 1, 2, 4, 3).reshape(bs, A_HEADS, LANES, -1)
            (ha_s,) = _attn(z, lam, qg2, kg2, ag, batch=bs, t=dec_seq, row_block0=n_prompt_rows // dec_seq,
                            rope_tabs=_rope_tables(dec_seq), ctx_kv=(kct, cache_dattn_v[:, j]),
                            out_scale=1.0 - lam_init)
            new_k.append(k_ctx)
            new_v.append(v_ctx)
            x, *ex1 = _ffn(*x, hm_p, hm_s, ha_p, ha_s, mod, w_out_ab[j].astype(bf16), ff_w1[j].astype(bf16),
                           ff_w3[j].astype(bf16), ff_w2[j].astype(bf16), *rows,
                           job=None if nxt is None else functools.partial(_cast_job_tiled, ex_w1[nxt], tf=MOE_FF_TILE))
        else:
            e = w_out_c.shape[1]
            bs_full = jnp.repeat(c_bs[j].T, e // C_GROUPS, axis=1)
            x, ex3 = _gmlp(joined(x), mod, w_in_c[j].astype(bf16), b_in_c[j].reshape(1, -1),
                           c_ln_g[j].reshape(1, -1), c_ln_b[j].reshape(1, -1), c_ws[j].astype(bf16), bs_full,
                           w_out_c[j].astype(bf16), *rows,
                           job=functools.partial(_cast_job_tiled, ex_w3[j], tf=MOE_FF_TILE))
            wr = jnp.zeros((d, LANES), f32).at[:, :N_EXPERTS].set(w_router[j])
            wr_hi = wr.astype(bf16)
            wr_lo = (wr - wr_hi.astype(f32)).astype(bf16)
            h, meta, counts = _router(x, mod, wr_hi, wr_lo, *rows)
            x = _moe(x, h, meta, counts, mod, ex1[0], ex3, ex2[0].reshape(ex_w2[j].shape), *rows)

    y_prompt, y_sample = parts(x)
    y_prompt = y_prompt.reshape(bp, seq, d)
    y_sample = y_sample.reshape(bs, dec_seq, d)
    return (y_prompt, y_sample, jnp.stack(new_k, axis=1), jnp.stack(new_v, axis=1),
            jnp.stack(new_c, axis=1), jnp.stack(new_n, axis=1), jnp.stack(new_m, axis=1))
```

```python
import functools
import math
from typing import Callable, NamedTuple

import jax
import jax.numpy as jnp
import numpy as np
from jax import lax
from jax.experimental import pallas as pl
from jax.experimental.pallas import tpu as pltpu

f32 = jnp.float32
bf16 = jnp.bfloat16

D_MODEL = 1024
M_HEADS = 4
M_DIM = 128
M_W = M_HEADS * M_DIM
M_CHUNK = 128
A_HEADS = 4
A_VDIM = 128
A_DIM = 64
A_W = A_HEADS * A_VDIM
GRID_W = 64
ROPE_THETA = 10000.0
C_CHUNK = 128
C_GROUPS = 4
N_EXPERTS = 8
EPS = 1e-6

LANES = 128
MXU_COLS = 256
ROW_TILE = 512
ADA_COL_TILE = 1536
INPROJ_COL_CHUNK = 512
ATTN_Q_BLOCK = 256
PIECE_ALIGN = 16
MOE_SUPER_BLOCK = 2048
MOE_SORT_BLOCK = 512
MOE_CHUNKS = (128, 192, 256, 320, 384, 448, 512, 576, 640)
MOE_FF_TILE = 896
VMEM_LIMIT = 60 * 1024 * 1024


def _cparams(*sem):
    return pltpu.CompilerParams(dimension_semantics=tuple(sem), vmem_limit_bytes=VMEM_LIMIT)


def _const_spec(shape):
    nd = len(shape)
    return pl.BlockSpec(shape, lambda *_: (0,) * nd, pipeline_mode=pl.Buffered(1))


def _sigmoid(x):
    return 1.0 / (1.0 + jnp.exp(-x))


def _silu(x):
    return x * _sigmoid(x)


def _log_sigmoid(x):
    return jnp.minimum(x, 0.0) - jnp.log(1.0 + jnp.exp(-jnp.abs(x)))


def _rms(x):
    return x * lax.rsqrt(jnp.mean(x * x, axis=-1, keepdims=True) + EPS)


def _modulate(x, mod_ref, first):
    shift = mod_ref[first:first + 1, :]
    scale = mod_ref[first + 1:first + 2, :]
    return _rms(x) * (1.0 + scale) + shift


def _dot(a, b):
    return jnp.dot(a, b, preferred_element_type=f32)


def _dot_tn(a, b):
    return lax.dot_general(a, b, (((0,), (0,)), ((), ())), preferred_element_type=f32)


def _split3(x):
    hi = x.astype(bf16)
    r1 = x - hi.astype(f32)
    mid = r1.astype(bf16)
    lo = (r1 - mid.astype(f32)).astype(bf16)
    return hi, mid, lo


def _group_of_tile(i, tm, n_prompt_rows, dec_seq):
    pt = n_prompt_rows // tm
    return jnp.where(i < pt, 0, 1 + (i - pt) // (dec_seq // tm))


def _ada_kernel(cv_ref, w_ref, b_ref, o_ref):
    a = _silu(cv_ref[...]).astype(bf16)
    o_ref[...] = _dot(a, w_ref[...].astype(bf16)) + b_ref[...]


def _ada_table(cv, w_ada, b_ada):
    depth, d, n = w_ada.shape
    g = cv.shape[0]
    gp = 8 * ((g + 7) // 8)
    cvp = jnp.zeros((gp, d), f32).at[:g].set(cv)
    tn = ADA_COL_TILE
    out = pl.pallas_call(
        _ada_kernel,
        out_shape=jax.ShapeDtypeStruct((depth, gp, n), f32),
        grid=(depth, n // tn),
        in_specs=[
            pl.BlockSpec((gp, d), lambda l, j: (0, 0)),
            pl.BlockSpec((None, d, tn), lambda l, j: (l, 0, j)),
            pl.BlockSpec((None, 1, tn), lambda l, j: (l, 0, j)),
        ],
        out_specs=pl.BlockSpec((None, gp, tn), lambda l, j: (l, 0, j)),
        compiler_params=_cparams("arbitrary", "arbitrary"),
        name="ada_table",
    )(cvp, w_ada, b_ada.reshape(depth, 1, n))
    return out[:, :g].reshape(depth, g, 6, d)


def _part_specs(tm, width, npt):
    first = pl.BlockSpec((tm, width), lambda i: (jnp.minimum(i, npt - 1), 0))
    second = pl.BlockSpec((tm, width), lambda i: (jnp.maximum(i - npt, 0), 0))
    return first, second


def _pick(a_ref, b_ref, npt):
    return jnp.where(pl.program_id(0) < npt, a_ref[...], b_ref[...])


class _CastJob(NamedTuple):
    src: jax.Array
    in_spec: pl.BlockSpec
    out_spec: pl.BlockSpec
    out_shape: jax.ShapeDtypeStruct
    body: Callable


def _cast_job_tiled(src, steps, tf):
    ne, d, ff = src.shape
    bands = steps // ne
    rows = d // bands
    nf = ff // tf

    def body(i_ref, o_ref):
        for f in range(nf):
            o_ref[f] = i_ref[:, f * tf:(f + 1) * tf].astype(bf16)

    return _CastJob(src, pl.BlockSpec((None, rows, ff), lambda i: (i // bands, i % bands, 0)),
                    pl.BlockSpec((None, nf, rows, tf), lambda i: (i // bands, 0, i % bands, 0)),
                    jax.ShapeDtypeStruct((ne, nf, d, tf), bf16), body)


def _cast_job_rows(src, steps):
    ne, ff, d = src.shape
    rows = ne * ff // steps

    def body(i_ref, o_ref):
        o_ref[...] = i_ref[...].astype(bf16)

    return _CastJob(src.reshape(ne * ff, d), pl.BlockSpec((rows, d), lambda i: (i, 0)),
                    pl.BlockSpec((rows, d), lambda i: (i, 0)), jax.ShapeDtypeStruct((ne * ff, d), bf16), body)


def _call_with_job(kernel_fn, job, *, out_shape, in_specs, out_specs, args, **kw):
    if job is None:
        return pl.pallas_call(kernel_fn, out_shape=out_shape, in_specs=in_specs, out_specs=out_specs, **kw)(*args)
    n_in, n_out = len(in_specs), len(out_shape)

    def with_job(*refs):
        kernel_fn(*refs[:n_in], *refs[n_in + 1:n_in + 1 + n_out], *refs[n_in + 2 + n_out:])
        job.body(refs[n_in], refs[n_in + 1 + n_out])

    return pl.pallas_call(with_job, out_shape=(*out_shape, job.out_shape), in_specs=[*in_specs, job.in_spec],
                          out_specs=(*out_specs, job.out_spec), **kw)(*args, job.src)


def _inproj_kernel(xp_ref, xs_ref, mod_ref, w_ref, z_ref, g_ref, wm_s, wg_s, *, npt, gate_cols):
    o3, o4 = gate_cols
    n_main = wm_s.shape[1]

    @pl.when(pl.program_id(0) == 0)
    def _():
        wm_s[:, :o3] = w_ref[:, :o3].astype(bf16)
        wm_s[:, o3:] = w_ref[:, o4:].astype(bf16)
        wg_s[...] = jnp.zeros_like(wg_s)
        wg_s[:, :o4 - o3] = w_ref[:, o3:o4].astype(bf16)

    h = _modulate(_pick(xp_ref, xs_ref, npt), mod_ref, 0).astype(bf16)
    step = INPROJ_COL_CHUNK
    for j in range(n_main // step):
        z_ref[:, j * step:(j + 1) * step] = _dot(h, wm_s[:, j * step:(j + 1) * step]).astype(bf16)
    g_ref[...] = _dot(h, wg_s[...])


def _inproj(xp, xs, mod, w, gate_cols, n_prompt_rows, dec_seq, job=None):
    d = xp.shape[1]
    r = xp.shape[0] + xs.shape[0]
    tm = ROW_TILE
    npt = n_prompt_rows // tm
    n_main = w.shape[1] - (gate_cols[1] - gate_cols[0])
    grp = functools.partial(_group_of_tile, tm=tm, n_prompt_rows=n_prompt_rows, dec_seq=dec_seq)
    return _call_with_job(
        functools.partial(_inproj_kernel, npt=npt, gate_cols=gate_cols), job(r // tm) if job else None,
        out_shape=(jax.ShapeDtypeStruct((r, n_main), bf16), jax.ShapeDtypeStruct((r, LANES), f32)),
        grid=(r // tm,),
        in_specs=[
            *_part_specs(tm, d, npt),
            pl.BlockSpec((None, 6, d), lambda i: (grp(i), 0, 0)),
            _const_spec(w.shape),
        ],
        out_specs=(pl.BlockSpec((tm, n_main), lambda i: (i, 0)), pl.BlockSpec((tm, LANES), lambda i: (i, 0))),
        args=(xp, xs, mod, w),
        scratch_shapes=[pltpu.VMEM((d, n_main), bf16), pltpu.VMEM((d, LANES), bf16)],
        compiler_params=_cparams("arbitrary"),
        name="inproj_ab",
    )


def _conv_silu_chunk(x_ref, c, nc, w_ref, b_ref):
    lc, pa = M_CHUNK, PIECE_ALIGN
    t = nc * lc
    x = x_ref[pl.ds(pl.multiple_of(c * lc, lc), lc), :].astype(f32)
    before = x_ref[pl.ds(pl.multiple_of(jnp.maximum(c * lc - pa, 0), pa), pa), :].astype(f32)[pa - 1:pa, :]
    after = x_ref[pl.ds(pl.multiple_of(jnp.minimum((c + 1) * lc, t - pa), pa), pa), :].astype(f32)[0:1, :]
    before = jnp.where(c > 0, before, 0.0)
    after = jnp.where(c < nc - 1, after, 0.0)
    row = lax.broadcasted_iota(jnp.int32, (lc, 1), 0)
    prev = jnp.where(row == 0, before, pltpu.roll(x, 1, 0))
    nxt = jnp.where(row == lc - 1, after, pltpu.roll(x, lc - 1, 0))
    y = b_ref[...] + prev * w_ref[0:1, :] + x * w_ref[1:2, :] + nxt * w_ref[2:3, :]
    return _silu(y)


def _mlstm_kernel(q_ref, k_ref, v_ref, o_ref, g_ref, gb_ref, cw_ref, cb_ref, ng_ref, *rest, t, has_state):
    if has_state:
        c0_ref, n0_ref, m0_ref, *rest = rest
    hm_ref, c_out, n_out, m_out, qt_s, kh_s, kl_s, vt_s, gs_s, hft_s, hbt_s, ct_s, sel_s = rest
    nc = t // M_CHUNK
    lc = M_CHUNK
    nh = M_HEADS
    w = M_W
    sel_row = lax.broadcasted_iota(jnp.int32, (3 * LANES, LANES), 0)
    for d in range(2):
        for h in range(nh):
            ci, ln = d * nh + h, 2 * nh * d + h
            ct_s[ci] = c0_ref[d, h].T if has_state else jnp.zeros((M_DIM, M_DIM), f32)
            sel_s[ci] = jnp.where(sel_row == ln, 1.0, jnp.where(sel_row == LANES + ln, 1.0, jnp.where(
                sel_row == 2 * LANES + ln, 1.0, 0.0))).astype(bf16)
    if has_state:
        n_out[...] = n0_ref[...]
        m_out[...] = m0_ref[...]
    else:
        n_out[...] = jnp.zeros_like(n_out)
        m_out[...] = jnp.zeros_like(m_out)

    lane = lax.broadcasted_iota(jnp.int32, (1, LANES), 1)
    is_lf = ((lane >= nh) & (lane < 2 * nh)) | ((lane >= 3 * nh) & (lane < 4 * nh))
    is_bw_lf = (lane >= 3 * nh) & (lane < 4 * nh)

    rr = lax.broadcasted_iota(jnp.int32, (lc, lc), 0)
    cc = lax.broadcasted_iota(jnp.int32, (lc, lc), 1)
    tri_incl = jnp.where(rr >= cc, 1.0, 0.0).astype(bf16)

    def prep_body(c, _):
        off = pl.multiple_of(c * lc, lc)
        g = g_ref[pl.ds(off, lc), :] + gb_ref[...]
        tile = jnp.where(is_lf, _log_sigmoid(g), g)
        hi, mid, lo = _split3(jnp.where(is_lf, tile, 0.0))
        cs = _dot(tri_incl, hi) + _dot(tri_incl, mid) + _dot(tri_incl, lo)
        total = jnp.broadcast_to(cs[lc - 1:lc, :], (lc, LANES))
        b = jnp.where(is_bw_lf, total - cs + tile, cs)
        lmb = tile - pltpu.roll(b, LANES - nh, 1)
        gv = pltpu.roll(total, LANES - nh, 1) + lmb
        low = jnp.where(is_lf, b, lmb)
        high = pltpu.roll(jnp.where(is_lf, total, gv), 4 * nh, 1)
        gs_s[pl.ds(off, lc), :] = jnp.where(lane < 4 * nh, low, jnp.where(lane < 8 * nh, high, 0.0))
        wide = pl.ds(pl.multiple_of(c * w, w), w)
        qt_s[wide, :] = _conv_silu_chunk(q_ref, c, nc, cw_ref[:, :w], cb_ref[:, :w]).T
        kf = _conv_silu_chunk(k_ref, c, nc, cw_ref[:, w:], cb_ref[:, w:]) * (M_DIM ** -0.5)
        kh = kf.astype(bf16)
        kh_s[pl.ds(off, lc), :] = kh
        kl_s[pl.ds(off, lc), :] = (kf - kh.astype(f32)).astype(bf16)
        vt_s[wide, :] = v_ref[pl.ds(off, lc), :].astype(f32).T.astype(bf16)
        return 0

    lax.fori_loop(0, nc, prep_body, 0)

    sub8 = lax.broadcasted_iota(jnp.int32, (8, LANES), 0)

    def two_rows(x):
        hi = x.astype(bf16).astype(f32)
        return jnp.where(sub8 == 0, hi, jnp.where(sub8 == 1, x - hi, 0.0)).astype(bf16)

    def issue(c, h, d, tile3, tile_t):
        off = pl.multiple_of(c * lc, lc)
        col = slice(h * M_DIM, (h + 1) * M_DIM)
        ci, ln = d * nh + h, 2 * nh * d + h
        head = pl.ds(pl.multiple_of(c * w + h * M_DIM, M_DIM), M_DIM)
        b_row = tile_t[ln + nh:ln + nh + 1, :]
        g_row = tile_t[ln + 4 * nh:ln + 4 * nh + 1, :]
        total = tile_t[ln + 5 * nh:ln + 5 * nh + 1, 0:1]
        ct, nm, mm = ct_s[ci], n_out[d, h], m_out[d, h]
        qt = qt_s[head, :]
        qtb = qt.astype(bf16)
        khb = kh_s[pl.ds(off, lc), col]
        vtb = vt_s[head, :]
        lmb = _dot(tile3, sel_s[ci])
        sraw = _dot(khb, qtb)
        qn = _dot(two_rows(nm), qtb)
        lhs = jnp.concatenate([ct.astype(bf16), vtb], axis=1)
        m_new = jnp.maximum(mm + total, jnp.max(g_row, axis=1, keepdims=True))
        decay = jnp.exp(mm + total - m_new)
        ew = jnp.exp(g_row - m_new)
        ct_s[ci] = decay * ct + _dot(vtb * ew.astype(bf16), khb)
        ew2 = two_rows(ew)
        nk_h = _dot(ew2, khb)
        nk_l = _dot(ew2, kl_s[pl.ds(off, lc), col])
        n_out[d, h] = decay * nm + (nk_h[0:1, :] + nk_h[1:2, :] + nk_l[0:1, :])
        m_out[d, h] = m_new
        return dict(d=d, head=head, b_row=b_row, mm=mm, qt=qt, lmb=lmb, sraw=sraw, qn=qn, lhs=lhs)

    def weigh(st):
        keep = (cc >= rr) if st["d"] == 0 else (rr >= cc)
        dmat = jnp.where(keep, st["lmb"] + st["b_row"], -jnp.inf)
        inter = st["mm"] + st["b_row"]
        mt = jnp.maximum(inter, jnp.max(dmat, axis=0, keepdims=True))
        w_inter = jnp.exp(inter - mt)
        s = st["sraw"] * jnp.exp(dmat - mt)
        qn = st["qn"]
        den = w_inter * (qn[0:1, :] + qn[1:2, :]) + jnp.sum(s, axis=0, keepdims=True)
        inv = 1.0 / jnp.maximum(jnp.abs(den), jnp.exp(-mt))
        return jnp.concatenate([st["qt"] * (w_inter * inv), s * inv], axis=0).astype(bf16)

    def body(i, _):
        states = []
        for d in range(2):
            c = i if d == 0 else nc - 1 - i
            tile = gs_s[pl.ds(pl.multiple_of(c * lc, lc), lc), :]
            tile_t = tile.T
            hi, mid, lo = _split3(tile)
            tile3 = jnp.concatenate([hi, mid, lo], axis=1)
            states += [issue(c, h, d, tile3, tile_t) for h in range(nh)]
        rhss = [weigh(st) for st in states]
        for st, rhs in zip(states, rhss):
            hct = _dot(st["lhs"], rhs)
            if st["d"] == 0:
                hft_s[st["head"], :] = hct
            else:
                hbt_s[st["head"], :] = hct
        return 0

    lax.fori_loop(0, nc, body, 0)

    def out_body(c, _):
        rows = pl.ds(pl.multiple_of(c * lc, lc), lc)
        wide = pl.ds(pl.multiple_of(c * w, w), w)
        hsum = (hft_s[wide, :] + hbt_s[wide, :]).T
        for h in range(nh):
            col = slice(h * M_DIM, (h + 1) * M_DIM)
            mo = _sigmoid(o_ref[rows, col].astype(f32))
            hm_ref[rows, col] = (_rms(hsum[:, col]) * ng_ref[:, col] * mo).astype(hm_ref.dtype)
        return 0

    lax.fori_loop(0, nc, out_body, 0)
    for d in range(2):
        for h in range(nh):
            c_out[d, h] = ct_s[d * nh + h].T


def _mlstm(z, gates, gate_b, conv_w, conv_b, m_norm_g, state0, *, batch, t, row_block0):
    nh = M_HEADS
    w = M_W
    mode = dict(pipeline_mode=pl.Buffered(1)) if t * w * 2 > (1 << 20) else {}
    seq = lambda colblk: pl.BlockSpec((t, w), lambda b: (row_block0 + b, colblk), **mode)
    state = lambda *tail: pl.BlockSpec((None, 2, nh) + tail, lambda b: (b,) + (0,) * (2 + len(tail)))
    state_specs = [state(M_DIM, M_DIM), state(1, M_DIM), state(1, 1)]
    has_state = state0 is not None
    return pl.pallas_call(
        functools.partial(_mlstm_kernel, t=t, has_state=has_state),
        out_shape=(
            jax.ShapeDtypeStruct((batch * t, w), bf16),
            jax.ShapeDtypeStruct((batch, 2, nh, M_DIM, M_DIM), f32),
            jax.ShapeDtypeStruct((batch, 2, nh, 1, M_DIM), f32),
            jax.ShapeDtypeStruct((batch, 2, nh, 1, 1), f32),
        ),
        grid=(batch,),
        in_specs=[
            seq(0), seq(1), seq(2), seq(3),
            pl.BlockSpec((t, LANES), lambda b: (row_block0 + b, 0)),
            _const_spec(gate_b.shape),
            _const_spec(conv_w.shape), _const_spec(conv_b.shape), _const_spec(m_norm_g.shape),
            *(state_specs if has_state else []),
        ],
        out_specs=(pl.BlockSpec((t, w), lambda b: (b, 0)), *state_specs),
        scratch_shapes=[
            pltpu.VMEM((t * nh, M_DIM), f32),
            pltpu.VMEM((t, w), bf16),
            pltpu.VMEM((t, w), bf16),
            pltpu.VMEM((t * nh, M_DIM), bf16),
            pltpu.VMEM((t, LANES), f32),
            pltpu.VMEM((t * nh, M_DIM), f32),
            pltpu.VMEM((t * nh, M_DIM), f32),
            pltpu.VMEM((2 * nh, M_DIM, M_DIM), f32),
            pltpu.VMEM((2 * nh, 3 * LANES, LANES), bf16),
        ],
        compiler_params=_cparams("arbitrary"),
        name=f"mlstm_t{t}",
    )(z, z, z, z, gates, gate_b, conv_w, conv_b, m_norm_g, *(state0 if has_state else ()))


def _pair_norm(x, gain, on_mxu):
    sq = x * x
    if on_mxu:
        same_half = jnp.where((lax.broadcasted_iota(jnp.int32, (LANES, LANES), 0) < A_DIM)
                              == (lax.broadcasted_iota(jnp.int32, (LANES, LANES), 1) < A_DIM), 1.0, 0.0).astype(bf16)
        hi = sq.astype(bf16)
        lo = (sq - hi.astype(f32)).astype(bf16)
        s = _dot(hi, same_half) + _dot(lo, same_half)
        return x * lax.rsqrt(s * (1.0 / A_DIM) + EPS) * gain
    lane = lax.broadcasted_iota(jnp.int32, (1, LANES), 1)
    first = lane < A_DIM
    s_all = jnp.sum(sq, axis=-1, keepdims=True)
    s0 = jnp.sum(jnp.where(first, sq, 0.0), axis=-1, keepdims=True)
    inv0 = lax.rsqrt(s0 * (1.0 / A_DIM) + EPS)
    inv1 = lax.rsqrt((s_all - s0) * (1.0 / A_DIM) + EPS)
    return x * jnp.where(first, inv0, inv1) * gain


def _rope(x, cos, sin_signed):
    lane = lax.broadcasted_iota(jnp.int32, (1, LANES), 1)
    nf = A_DIM // 4
    partner = jnp.where((lane % (2 * nf)) < nf, pltpu.roll(x, LANES - nf, 1), pltpu.roll(x, nf, 1))
    return x * cos + partner * sin_signed


def _attn_kernel(*refs, t, tq, rope, ctx, out_scale):
    it = iter(refs)
    lam_ref = next(it)
    q_ref, k_ref, v_ref = next(it), next(it), next(it)
    qg_ref, kg_ref, ag_ref = next(it), next(it), next(it)
    if rope:
        cosq_ref, sinq_ref, cosk_ref, sink_ref = next(it), next(it), next(it), next(it)
    if ctx:
        kct_ref, vc_ref = next(it), next(it)
    ha_ref = next(it)
    if not ctx:
        newk_ref, newv_ref = next(it), next(it)
    kt_s = next(it)
    nh = A_HEADS

    @pl.when(pl.program_id(1) == 0)
    def _():
        for h in range(nh):
            col = slice(h * LANES, (h + 1) * LANES)
            kn = _pair_norm(k_ref[:, col].astype(f32), kg_ref[...], True)
            if not ctx:
                newk_ref[h, 0] = kn[:, :A_DIM]
                newk_ref[h, 1] = kn[:, A_DIM:]
                newv_ref[h] = v_ref[:, col].astype(f32)
            if rope:
                kn = _rope(kn, cosk_ref[...], sink_ref[...])
            kt_s[col, :] = kn.T.astype(bf16)

    lane = lax.broadcasted_iota(jnp.int32, (1, LANES), 1)
    lam = lam_ref[0]
    for h in range(nh):
        col = slice(h * LANES, (h + 1) * LANES)
        q = _pair_norm(q_ref[:, col].astype(f32), qg_ref[...], not ctx)
        if rope:
            q = _rope(q, cosq_ref[...], sinq_ref[...])
        q = q * (A_DIM ** -0.5 * math.log2(math.e))
        qs = [jnp.where(lane < A_DIM, q, 0.0).astype(bf16), jnp.where(lane >= A_DIM, q, 0.0).astype(bf16)]
        kt = kt_s[col, :]
        vb = v_ref[:, col]
        if ctx:
            kctb = kct_ref[h].astype(bf16)
            vcb = vc_ref[h].astype(bf16)
        outs = []
        for i in range(2):
            sn = _dot(qs[i], kt)
            mx = jnp.max(sn, axis=-1, keepdims=True)
            if ctx:
                sc = _dot(qs[i], kctb)
                mx = jnp.maximum(mx, jnp.max(sc, axis=-1, keepdims=True))
            en = jnp.exp2(sn - mx)
            den = jnp.sum(en, axis=-1, keepdims=True)
            o = _dot(en.astype(bf16), vb)
            if ctx:
                ec = jnp.exp2(sc - mx)
                den = den + jnp.sum(ec, axis=-1, keepdims=True)
                o = o + _dot(ec.astype(bf16), vcb)
            outs.append(o * (1.0 / den))
        o = outs[0] - lam * outs[1]
        ha_ref[:, col] = (_rms(o) * ag_ref[...] * out_scale).astype(ha_ref.dtype)


def _attn(z, lam, qg2, kg2, a_norm_g, *, batch, t, row_block0, rope_tabs=None, ctx_kv=None, out_scale):
    nh = A_HEADS
    w = A_W
    tq = min(t, ATTN_Q_BLOCK)
    nq = t // tq
    rope = rope_tabs is not None
    ctx = ctx_kv is not None
    qblk = 4 * M_W // w
    vec = pl.BlockSpec((1, LANES), lambda b, i: (0, 0))
    in_specs = [
        pl.BlockSpec(memory_space=pltpu.SMEM),
        pl.BlockSpec((tq, w), lambda b, i: ((row_block0 + b) * nq + i, qblk)),
        pl.BlockSpec((t, w), lambda b, i: (row_block0 + b, qblk + 1)),
        pl.BlockSpec((t, w), lambda b, i: (row_block0 + b, qblk + 2)),
        vec, vec, vec,
    ]
    args = [lam, z, z, z, qg2, kg2, a_norm_g]
    if rope:
        cos, sin = rope_tabs
        in_specs += [pl.BlockSpec((tq, LANES), lambda b, i: (i, 0))] * 2
        in_specs += [pl.BlockSpec((t, LANES), lambda b, i: (0, 0))] * 2
        args += [cos, sin, cos, sin]
    if ctx:
        kct, vc = ctx_kv
        in_specs += [pl.BlockSpec((None,) + kct.shape[1:], lambda b, i: (b, 0, 0, 0)),
                     pl.BlockSpec((None,) + vc.shape[1:], lambda b, i: (b, 0, 0, 0))]
        args += [kct, vc]
    out_shape = [jax.ShapeDtypeStruct((batch * t, w), bf16)]
    out_specs = [pl.BlockSpec((tq, w), lambda b, i: (b * nq + i, 0))]
    if not ctx:
        out_shape += [jax.ShapeDtypeStruct((batch, nh, 2, t, A_DIM), f32),
                      jax.ShapeDtypeStruct((batch, nh, t, A_VDIM), f32)]
        out_specs += [pl.BlockSpec((None, nh, 2, t, A_DIM), lambda b, i: (b, 0, 0, 0, 0)),
                      pl.BlockSpec((None, nh, t, A_VDIM), lambda b, i: (b, 0, 0, 0))]
    return pl.pallas_call(
        functools.partial(_attn_kernel, t=t, tq=tq, rope=rope, ctx=ctx, out_scale=out_scale),
        out_shape=tuple(out_shape),
        grid=(batch, nq),
        in_specs=in_specs,
        out_specs=tuple(out_specs),
        scratch_shapes=[pltpu.VMEM((w, t), bf16)],
        compiler_params=_cparams("arbitrary", "arbitrary"),
        name=f"diff_attn_t{t}",
    )(*args)


def _rope_tables(t):
    rows = t // GRID_W
    pos_row = np.repeat(np.arange(rows, dtype=np.float64), GRID_W)
    pos_col = (np.arange(rows * GRID_W) % GRID_W).astype(np.float64)
    nf = A_DIM // 4
    inv = ROPE_THETA ** (-np.arange(nf, dtype=np.float64) / nf)
    lane = np.arange(LANES)
    j = lane % (2 * nf)
    use_col = (lane % A_DIM) >= (A_DIM // 2)
    ang = np.where(use_col[None, :], pos_col[:, None], pos_row[:, None]) * inv[j % nf][None, :]
    sign = np.where(j < nf, -1.0, 1.0)[None, :]
    return jnp.asarray(np.cos(ang), f32), jnp.asarray(np.sin(ang) * sign, f32)


def _ffn_kernel(xp_ref, xs_ref, hmp_ref, hms_ref, hap_ref, has_ref, mod_ref, wo_ref, w1_ref, w3_ref, w2_ref, o_ref,
                *, chunks, npt):
    half = hmp_ref.shape[1]
    hm = _pick(hmp_ref, hms_ref, npt)
    ha = _pick(hap_ref, has_ref, npt)
    x = _pick(xp_ref, xs_ref, npt) + mod_ref[2:3, :] * (_dot(hm, wo_ref[:half, :]) + _dot(ha, wo_ref[half:, :]))
    h = _modulate(x, mod_ref, 3).astype(bf16)
    acc = jnp.zeros(x.shape, f32)
    for lo, hi in chunks:
        ab = _dot(h, jnp.concatenate([w1_ref[:, lo:hi], w3_ref[:, lo:hi]], axis=1))
        a = _silu(ab[:, :hi - lo]) * ab[:, hi - lo:]
        acc = acc + _dot(a.astype(bf16), w2_ref[lo:hi, :])
    o_ref[...] = x + mod_ref[5:6, :] * acc


def _ffn(xp, xs, hm_p, hm_s, ha_p, ha_s, mod, w_out, w1, w3, w2, n_prompt_rows, dec_seq, job=None):
    d = xp.shape[1]
    r = xp.shape[0] + xs.shape[0]
    tm = ROW_TILE
    npt = n_prompt_rows // tm
    grp = functools.partial(_group_of_tile, tm=tm, n_prompt_rows=n_prompt_rows, dec_seq=dec_seq)
    ff = w1.shape[1]
    cut = MXU_COLS * ((ff // MXU_COLS + 1) // 2) if ff % MXU_COLS == 0 else ff
    chunks = ((0, cut), (cut, ff)) if cut < ff else ((0, ff),)
    return _call_with_job(
        functools.partial(_ffn_kernel, chunks=chunks, npt=npt), job(r // tm) if job else None,
        out_shape=(jax.ShapeDtypeStruct((r, d), f32),),
        grid=(r // tm,),
        in_specs=[
            *_part_specs(tm, d, npt),
            *_part_specs(tm, hm_p.shape[1], npt),
            *_part_specs(tm, ha_p.shape[1], npt),
            pl.BlockSpec((None, 6, d), lambda i: (grp(i), 0, 0)),
            _const_spec(w_out.shape), _const_spec(w1.shape), _const_spec(w3.shape), _const_spec(w2.shape),
        ],
        out_specs=(pl.BlockSpec((tm, d), lambda i: (i, 0)),),
        args=(xp, xs, hm_p, hm_s, ha_p, ha_s, mod, w_out, w1, w3, w2),
        compiler_params=_cparams("arbitrary"),
        name="outproj_ffn",
    )


def _gelu_tanh(x):
    k1 = -2.0 * math.sqrt(2.0 / math.pi) * math.log2(math.e)
    k3 = k1 * 0.044715
    return x / (1.0 + jnp.exp2(x * (k1 + k3 * (x * x))))


def _gmlp_kernel(x_ref, mod_ref, win_ref, bin_ref, lng_ref, lnb_ref, ws_ref, bs_ref, wout_ref, o_ref, us_s):
    x = x_ref[...]
    tm = x.shape[0]
    e = wout_ref.shape[0]
    ge = e // C_GROUPS
    h = _modulate(x, mod_ref, 0).astype(bf16)
    u = _gelu_tanh(_dot(h, win_ref[:, :e]) + bin_ref[:, :e])
    v = _gelu_tanh(_dot(h, win_ref[:, e:]) + bin_ref[:, e:])
    mu = jnp.mean(v, axis=-1, keepdims=True)
    vc = v - mu
    var = jnp.mean(vc * vc, axis=-1, keepdims=True)
    vn = (vc * lax.rsqrt(var + EPS) * lng_ref[...] + lnb_ref[...]).astype(bf16)
    for n in range(tm // C_CHUNK):
        rows = slice(n * C_CHUNK, (n + 1) * C_CHUNK)
        for g in range(C_GROUPS):
            cols = slice(g * ge, (g + 1) * ge)
            s = _dot(ws_ref[g], vn[rows, cols]) + bs_ref[:, cols]
            us_s[rows, cols] = (u[rows, cols] * s).astype(bf16)
    o_ref[...] = x + mod_ref[2:3, :] * _dot(us_s[...], wout_ref[...])


def _gmlp(x, mod, w_in, b_in, ln_g, ln_b, ws, bs_full, w_out, n_prompt_rows, dec_seq, job=None):
    r, d = x.shape
    tm = ROW_TILE
    e = w_out.shape[0]
    grp = functools.partial(_group_of_tile, tm=tm, n_prompt_rows=n_prompt_rows, dec_seq=dec_seq)
    return _call_with_job(
        _gmlp_kernel, job(r // tm) if job else None,
        out_shape=(jax.ShapeDtypeStruct((r, d), f32),),
        grid=(r // tm,),
        in_specs=[
            pl.BlockSpec((tm, d), lambda i: (i, 0)),
            pl.BlockSpec((None, 6, d), lambda i: (grp(i), 0, 0)),
            _const_spec(w_in.shape), _const_spec(b_in.shape), _const_spec(ln_g.shape), _const_spec(ln_b.shape),
            _const_spec(ws.shape), _const_spec(bs_full.shape), _const_spec(w_out.shape),
        ],
        out_specs=(pl.BlockSpec((tm, d), lambda i: (i, 0)),),
        args=(x, mod, w_in, b_in, ln_g, ln_b, ws, bs_full, w_out),
        scratch_shapes=[pltpu.VMEM((tm, e), bf16)],
        compiler_params=_cparams("arbitrary"),
        name="gmlp",
    )


def _router_kernel(x_ref, mod_ref, wr_ref, h_ref, meta_ref, counts_ref):
    hf = _modulate(x_ref[...], mod_ref, 3)
    hb = hf.astype(bf16)
    h_ref[...] = hb
    h_lo = (hf - hb.astype(f32)).astype(bf16)
    both = _dot(jnp.concatenate([hb, h_lo], axis=1), wr_ref[...])
    logits = both[:, :LANES] + both[:, LANES:]
    lane = lax.broadcasted_iota(jnp.int32, logits.shape, 1).astype(f32)
    logits = jnp.where(lane < N_EXPERTS, logits, -jnp.inf)
    m1 = jnp.max(logits, axis=-1, keepdims=True)
    i1 = jnp.min(jnp.where(logits == m1, lane, float(LANES)), axis=-1, keepdims=True)
    rest = jnp.where(lane == i1, -jnp.inf, logits)
    m2 = jnp.max(rest, axis=-1, keepdims=True)
    i2 = jnp.min(jnp.where(rest == m2, lane, float(LANES)), axis=-1, keepdims=True)
    e2 = jnp.exp(m2 - m1)
    w1 = 1.0 / (1.0 + e2)
    w2 = e2 * w1

    tm = logits.shape[0]
    cnt = jnp.where(lane == i1, 1.0, jnp.where(lane == i2, 1.0, 0.0))
    rr = lax.broadcasted_iota(jnp.int32, (tm, tm), 0)
    cc = lax.broadcasted_iota(jnp.int32, (tm, tm), 1)
    before = jnp.where(rr > cc, 1.0, 0.0).astype(bf16)
    rank = _dot(before, cnt.astype(bf16))
    counts = jnp.sum(cnt, axis=0, keepdims=True)
    padded = jnp.floor((counts + (PIECE_ALIGN - 1)) * (1.0 / PIECE_ALIGN)) * PIECE_ALIGN
    lane1 = lane[0:1, :]
    piece_off = jnp.zeros((1, LANES), f32)
    off = jnp.zeros((1, 1), f32)
    for e in range(N_EXPERTS):
        piece_off = jnp.where(lane1 == e, off, piece_off)
        off = off + padded[:, e:e + 1]
    local = piece_off + rank
    pos1 = jnp.sum(jnp.where(lane == i1, local, 0.0), axis=-1, keepdims=True)
    pos2 = jnp.sum(jnp.where(lane == i2, local, 0.0), axis=-1, keepdims=True)
    meta_ref[...] = jnp.where(lane == 0, pos1, jnp.where(lane == 1, pos2, jnp.where(lane == 2, w1,
                              jnp.where(lane == 3, w2, 0.0))))
    counts_ref[...] = counts


def _router(x, mod, wr, n_prompt_rows, dec_seq):
    r, d = x.shape
    tm = MOE_SORT_BLOCK
    grp = functools.partial(_group_of_tile, tm=tm, n_prompt_rows=n_prompt_rows, dec_seq=dec_seq)
    return pl.pallas_call(
        _router_kernel,
        out_shape=(jax.ShapeDtypeStruct((r, d), bf16), jax.ShapeDtypeStruct((r, LANES), f32),
                   jax.ShapeDtypeStruct((r // tm, 1, LANES), f32)),
        grid=(r // tm,),
        in_specs=[
            pl.BlockSpec((tm, d), lambda i: (i, 0)),
            pl.BlockSpec((None, 6, d), lambda i: (grp(i), 0, 0)),
            _const_spec(wr.shape),
        ],
        out_specs=(pl.BlockSpec((tm, d), lambda i: (i, 0)), pl.BlockSpec((tm, LANES), lambda i: (i, 0)),
                   pl.BlockSpec((None, 1, LANES), lambda i: (i, 0, 0))),
        compiler_params=_cparams("arbitrary"),
        name="router",
    )(x, mod, wr)


def _moe_kernel(n16_ref, loc_ref, dst_ref, seg_off_ref, seg_len_ref, csel_ref, cnum_ref,
                h_ref, meta_ref, x_ref, mod_ref, w1_ref, w3_ref, w2_ref, op_ref, os_ref,
                hs_s, ys_s, loc_s, *, nsub, ne, nf, chunks, nsb_prompt):
    sb = pl.program_id(0)
    p = pl.program_id(1)
    n_exp = ne * nf
    loc_rows = loc_s.shape[0]
    pa = PIECE_ALIGN

    def one_hot_cols(meta, v1, v2):
        lane = lax.broadcasted_iota(jnp.int32, (1, loc_rows), 1).astype(f32)
        return jnp.where(lane == meta[:, 0:1], v1, jnp.where(lane == meta[:, 1:2], v2, 0.0)).astype(bf16)

    def copy_pieces(blk, to_sorted):
        for e in range(ne):
            n = n16_ref[blk * ne + e]
            src = loc_ref[blk * ne + e]
            dst = dst_ref[blk * ne + e]

            def cp(i, _, src=src, dst=dst):
                a = pl.ds(pl.multiple_of(src + pa * i, pa), pa)
                b = pl.ds(pl.multiple_of(dst + pa * i, pa), pa)
                if to_sorted:
                    hs_s[b, :] = loc_s[a, :]
                else:
                    loc_s[a, :] = ys_s[b, :].astype(bf16)
                return 0

            lax.fori_loop(0, n, cp, 0)

    @pl.when(p < nsub)
    def _():
        @pl.when(p == 0)
        def _():
            hs_s[...] = jnp.zeros_like(hs_s)
            ys_s[...] = jnp.zeros_like(ys_s)

        pt = one_hot_cols(meta_ref[...], 1.0, 1.0)
        loc_s[...] = _dot_tn(pt, h_ref[...]).astype(bf16)
        copy_pieces(sb * nsub + p, True)

    @pl.when((p >= nsub) & (p < nsub + n_exp))
    def _():
        e = lax.div(p - nsub, jnp.int32(nf))
        start = seg_off_ref[sb * ne + e]
        ln = seg_len_ref[sb * ne + e]
        which = csel_ref[sb * ne + e]
        count = cnum_ref[sb * ne + e]

        def chunk(lo, size):
            r0 = jnp.minimum(lo, hs_s.shape[0] - size)
            rows = pl.ds(pl.multiple_of(r0, pa), size)
            xc = hs_s[rows, :]
            ab = _dot(xc, jnp.concatenate([w1_ref[...], w3_ref[...]], axis=1))
            tf = w1_ref.shape[1]
            a = _silu(ab[:, :tf]) * ab[:, tf:]
            y = _dot(a.astype(bf16), w2_ref[...])
            ri = r0 + lax.broadcasted_iota(jnp.int32, (size, 1), 0)
            ys_s[rows, :] += jnp.where(ri >= lo, jnp.where(ri < start + ln, y, 0.0), 0.0)

        for k, size in enumerate(chunks):
            @pl.when(which == k)
            def _(size=size):
                def body(j, _):
                    chunk(start + j * size, size)
                    return 0

                lax.fori_loop(0, count, body, 0)

    @pl.when(p >= nsub + n_exp)
    def _():
        copy_pieces(sb * nsub + (p - nsub - n_exp), False)
        meta = meta_ref[...]
        a = one_hot_cols(meta, meta[:, 2:3], meta[:, 3:4])
        y = x_ref[...] + mod_ref[5:6, :] * _dot(a, loc_s[...])

        @pl.when(sb < nsb_prompt)
        def _():
            op_ref[...] = y

        @pl.when(sb >= nsb_prompt)
        def _():
            os_ref[...] = y


def _moe(x, h, meta, counts, mod, w1, w3, w2, n_prompt_rows, dec_seq):
    r, d = x.shape
    ne, ff, _ = w2.shape
    assert w1.shape == (ne, ff // MOE_FF_TILE, d, MOE_FF_TILE)
    t_super, tb, chunks, tf, pa = MOE_SUPER_BLOCK, MOE_SORT_BLOCK, MOE_CHUNKS, MOE_FF_TILE, PIECE_ALIGN
    ch = max(chunks)
    nsub = t_super // tb
    nsb = r // t_super
    npb = n_prompt_rows // tb
    nf = ff // tf
    n_exp = ne * nf
    loc_rows = 2 * tb + LANES
    assert loc_rows >= 2 * tb + ne * (pa - 1)
    max_rows = 2 * t_super + nsub * ne * (pa - 1)
    sort_rows = LANES * (-(-max_rows // LANES))
    assert all(size % pa == 0 and size <= sort_rows for size in chunks)

    cnt = counts[:, 0, :ne].astype(jnp.int32)
    n16 = (cnt + (pa - 1)) // pa
    loc = pa * (jnp.cumsum(n16, axis=1) - n16)
    n16_sb = n16.reshape(nsb, nsub, ne)
    seg_len = pa * jnp.sum(n16_sb, axis=1)
    seg_off = jnp.cumsum(seg_len, axis=1) - seg_len
    dst = seg_off[:, None, :] + pa * (jnp.cumsum(n16_sb, axis=1) - n16_sb)
    cnum = (seg_len + ch - 1) // ch
    need = pa * ((seg_len // pa + jnp.maximum(cnum, 1) - 1) // jnp.maximum(cnum, 1))
    csel = sum((need > size).astype(jnp.int32) for size in chunks[:-1])
    scalars = [a.reshape(-1).astype(jnp.int32) for a in (n16, loc, dst, seg_off, seg_len, csel, cnum)]

    grp = functools.partial(_group_of_tile, tm=t_super, n_prompt_rows=n_prompt_rows, dec_seq=dec_seq)

    def exp_step(p):
        return jnp.clip(p - nsub, 0, n_exp - 1)

    def tok_blk(sb, s):
        return sb * nsub + jnp.clip(s, 0, nsub - 1)

    grid_spec = pltpu.PrefetchScalarGridSpec(
        num_scalar_prefetch=len(scalars),
        grid=(nsb, nsub + n_exp + nsub),
        in_specs=[
            pl.BlockSpec((tb, d), lambda sb, p, *_: (tok_blk(sb, p), 0)),
            pl.BlockSpec((tb, LANES), lambda sb, p, *_: (sb * nsub + jnp.where(p < nsub, p, jnp.clip(p - nsub - n_exp, 0, nsub - 1)), 0)),
            pl.BlockSpec((tb, d), lambda sb, p, *_: (tok_blk(sb, p - nsub - n_exp), 0)),
            pl.BlockSpec((None, 6, d), lambda sb, p, *_: (grp(sb), 0, 0)),
            pl.BlockSpec((None, None, d, tf), lambda sb, p, *_: (exp_step(p) // nf, exp_step(p) % nf, 0, 0)),
            pl.BlockSpec((None, None, d, tf), lambda sb, p, *_: (exp_step(p) // nf, exp_step(p) % nf, 0, 0)),
            pl.BlockSpec((None, tf, d), lambda sb, p, *_: (exp_step(p) // nf, exp_step(p) % nf, 0)),
        ],
        out_specs=(
            pl.BlockSpec((tb, d), lambda sb, p, *_: (jnp.minimum(tok_blk(sb, p - nsub - n_exp), npb - 1), 0)),
            pl.BlockSpec((tb, d), lambda sb, p, *_: (jnp.maximum(tok_blk(sb, p - nsub - n_exp) - npb, 0), 0)),
        ),
        scratch_shapes=[
            pltpu.VMEM((sort_rows, d), bf16),
            pltpu.VMEM((sort_rows, d), f32),
            pltpu.VMEM((loc_rows, d), bf16),
        ],
    )
    return pl.pallas_call(
        functools.partial(_moe_kernel, nsub=nsub, ne=ne, nf=nf, chunks=chunks,
                          nsb_prompt=n_prompt_rows // t_super),
        out_shape=(jax.ShapeDtypeStruct((n_prompt_rows, d), f32), jax.ShapeDtypeStruct((r - n_prompt_rows, d), f32)),
        grid_spec=grid_spec,
        compiler_params=_cparams("arbitrary", "arbitrary"),
        name="moe_sparse",
    )(*scalars, h, meta, x, mod, w1, w3, w2)


def kernel(x_prompt, x_sample, c, cache_dattn_k, cache_dattn_v, state_mlstm_c, state_mlstm_n, state_mlstm_m,
           c_ctx, w_ada, b_ada, w_in_ab, conv_w, conv_b, gate_b, qn_g, kn_g, lam_q1, lam_k1, lam_q2, lam_k2,
           m_norm_g, a_norm_g, w_out_ab, ff_w1, ff_w3, ff_w2, w_in_c, b_in_c, c_ln_g, c_ln_b, c_ws, c_bs,
           w_out_c, w_router, ex_w1, ex_w3, ex_w2):
    bp, seq, d = x_prompt.shape
    bs, dec_seq, _ = x_sample.shape
    depth = w_ada.shape[0]
    n_prompt_rows = bp * seq
    assert n_prompt_rows % dec_seq == 0 and seq % M_CHUNK == 0 and dec_seq % MOE_SUPER_BLOCK == 0
    nh = M_HEADS

    mods = _ada_table(jnp.concatenate([c_ctx[None], c], axis=0), w_ada, b_ada)
    x = (x_prompt.reshape(n_prompt_rows, d), x_sample.reshape(bs * dec_seq, d))
    rows = (n_prompt_rows, dec_seq)

    def joined(v):
        return jnp.concatenate(v, axis=0) if isinstance(v, tuple) else v

    def parts(v):
        return v if isinstance(v, tuple) else (v[:n_prompt_rows], v[n_prompt_rows:])

    new_k, new_v, new_c, new_n, new_m = [], [], [], [], []
    for l in range(depth):
        j = l // 2
        mod = mods[l]
        if l % 2 == 0:
            lam_init = 0.8 - 0.6 * math.exp(-0.3 * l)
            lam = (jnp.exp(jnp.sum((lam_q1[j] * lam_k1[j]).astype(f32)))
                   - jnp.exp(jnp.sum((lam_q2[j] * lam_k2[j]).astype(f32))) + lam_init).reshape(1)
            x = parts(x)
            nxt = j if l + 1 < depth else None
            z, gates, *ex2 = _inproj(*x, mod, w_in_ab[j], (4 * M_W, 4 * M_W + 4 * nh), *rows,
                                     job=None if nxt is None else functools.partial(_cast_job_rows, ex_w2[nxt]))
            gb = jnp.zeros((1, LANES), f32).at[0, :4 * nh].set(gate_b[j])
            mng = m_norm_g[j].reshape(1, M_W)
            cb = conv_b[j].reshape(1, 2 * M_W)
            hm_p, c_f, n_f, m_f = _mlstm(z, gates, gb, conv_w[j], cb, mng, None, batch=bp, t=seq, row_block0=0)
            state0 = (state_mlstm_c[:, j], state_mlstm_n[:, j].reshape(bs, 2, nh, 1, M_DIM),
                      state_mlstm_m[:, j].reshape(bs, 2, nh, 1, 1))
            hm_s, _, _, _ = _mlstm(z, gates, gb, conv_w[j], cb, mng, state0,
                                   batch=bs, t=dec_seq, row_block0=n_prompt_rows // dec_seq)
            new_c.append(c_f)
            new_n.append(n_f.reshape(bp, 2, nh, M_DIM))
            new_m.append(m_f.reshape(bp, 2, nh))

            qg2 = jnp.tile(qn_g[j], 2).reshape(1, LANES)
            kg2 = jnp.tile(kn_g[j], 2).reshape(1, LANES)
            ag = a_norm_g[j].reshape(1, LANES)
            ha_p, k_ctx, v_ctx = _attn(z, lam, qg2, kg2, ag, batch=bp, t=seq, row_block0=0,
                                       out_scale=1.0 - lam_init)
            kct = cache_dattn_k[:, j].transpose(0, 1, 2, 4, 3).reshape(bs, A_HEADS, LANES, -1)
            (ha_s,) = _attn(z, lam, qg2, kg2, ag, batch=bs, t=dec_seq, row_block0=n_prompt_rows // dec_seq,
                            rope_tabs=_rope_tables(dec_seq), ctx_kv=(kct, cache_dattn_v[:, j]),
                            out_scale=1.0 - lam_init)
            new_k.append(k_ctx)
            new_v.append(v_ctx)
            x, *ex1 = _ffn(*x, hm_p, hm_s, ha_p, ha_s, mod, w_out_ab[j].astype(bf16), ff_w1[j].astype(bf16),
                           ff_w3[j].astype(bf16), ff_w2[j].astype(bf16), *rows,
                           job=None if nxt is None else functools.partial(_cast_job_tiled, ex_w1[nxt], tf=MOE_FF_TILE))
        else:
            e = w_out_c.shape[1]
            bs_full = jnp.repeat(c_bs[j].T, e // C_GROUPS, axis=1)
            x, ex3 = _gmlp(joined(x), mod, w_in_c[j].astype(bf16), b_in_c[j].reshape(1, -1),
                           c_ln_g[j].reshape(1, -1), c_ln_b[j].reshape(1, -1), c_ws[j].astype(bf16), bs_full,
                           w_out_c[j].astype(bf16), *rows,
                           job=functools.partial(_cast_job_tiled, ex_w3[j], tf=MOE_FF_TILE))
            wr = jnp.zeros((d, LANES), f32).at[:, :N_EXPERTS].set(w_router[j])
            wr_hi = wr.astype(bf16)
            wr_lo = (wr - wr_hi.astype(f32)).astype(bf16)
            wr2 = jnp.concatenate([jnp.concatenate([wr_hi, wr_lo], axis=1),
                                   jnp.concatenate([wr_hi, jnp.zeros_like(wr_hi)], axis=1)], axis=0)
            h, meta, counts = _router(x, mod, wr2, *rows)
            x = _moe(x, h, meta, counts, mod, ex1[0], ex3, ex2[0].reshape(ex_w2[j].shape), *rows)

    y_prompt, y_sample = parts(x)
    y_prompt = y_prompt.reshape(bp, seq, d)
    y_sample = y_sample.reshape(bs, dec_seq, d)
    return (y_prompt, y_sample, jnp.stack(new_k, axis=1), jnp.stack(new_v, axis=1),
            jnp.stack(new_c, axis=1), jnp.stack(new_n, axis=1), jnp.stack(new_m, axis=1))
```
